```python
import jax, jax.numpy as jnp
from jax import lax
import numpy as np

D_MODEL = 1024
BATCH = 4
SEQ = 8192
DEPTH = 1

RET_HEADS = 4
RET_QK_DIM = 128
RET_V_DIM = 256
RET_CHUNK = 128
ROPE_BASE = 10000.0
SWA_Q_HEADS = 8
SWA_KV_HEADS = 2
SWA_HEAD_DIM = 64
SWA_WINDOW = 128
MOE_GROUPS = 4
MOE_EXPERTS_PER_GROUP = 8
MOE_TOP_K = 2
MOE_D_FF = 512
MOE_BLOCK = 128
NORM_EPS = 1e-6

RET_QK_W = RET_HEADS * RET_QK_DIM
RET_V_W = RET_HEADS * RET_V_DIM
SWA_Q_W = SWA_Q_HEADS * SWA_HEAD_DIM
SWA_KV_W = SWA_KV_HEADS * SWA_HEAD_DIM
IN_SPLITS = (RET_QK_W, RET_QK_W, RET_V_W, RET_V_W, SWA_Q_W, SWA_KV_W, SWA_KV_W, D_MODEL, D_MODEL)
IN_WIDTH = int(sum(IN_SPLITS))
IN_CUTS = tuple(int(c) for c in np.cumsum(IN_SPLITS)[:-1])
MOE_N_EXPERTS = MOE_GROUPS * MOE_EXPERTS_PER_GROUP

kernel_name = "hybrid_retention_swa_sink_hmoe_block"


def rmsnorm(x, g):
    x32 = x.astype(jnp.float32)
    r = lax.rsqrt(jnp.mean(x32 * x32, axis=-1, keepdims=True) + NORM_EPS)
    return (x32 * r).astype(x.dtype) * g


def rotary(x, pos):
    half = x.shape[-1] // 2
    inv = ROPE_BASE ** (-jnp.arange(half, dtype=jnp.float32) / half)
    ang = pos.astype(jnp.float32)[:, None] * inv[None, :]
    cos = jnp.cos(ang)[None, :, None, :]
    sin = jnp.sin(ang)[None, :, None, :]
    x1, x2 = x[..., :half], x[..., half:]
    return jnp.concatenate([x1 * cos - x2 * sin, x1 * sin + x2 * cos], axis=-1)


def retention_chunkwise(q, k, v):
    B, S, H, dk = q.shape
    dv = v.shape[-1]
    C = RET_CHUNK
    nC = S // C
    q = q.reshape(B, nC, C, H, dk)
    k = k.reshape(B, nC, C, H, dk) * (dk ** -0.5)
    v = v.reshape(B, nC, C, H, dv)
    log_g = jnp.log1p(-jnp.exp2(-5.0 - jnp.arange(H, dtype=jnp.float32)))
    idx = jnp.arange(C, dtype=jnp.float32)
    diff = idx[:, None] - idx[None, :]
    decay = jnp.where(diff >= 0, jnp.exp(log_g[:, None, None] * jnp.maximum(diff, 0.0)), 0.0)
    scores = jnp.einsum('bcqhd,bckhd->bchqk', q, k) * decay[None, None]
    intra = jnp.einsum('bchqk,bckhe->bcqhe', scores, v)
    k_dec = k * jnp.exp(log_g[None, :] * (C - 1 - idx)[:, None])[None, None, :, :, None]
    kv = jnp.einsum('bckhd,bckhe->cbhde', k_dec, v)
    chunk_decay = jnp.exp(log_g * C)[None, :, None, None]

    def step(R, kv_c):
        return chunk_decay * R + kv_c, R

    _, R_prev = lax.scan(step, jnp.zeros_like(kv[0]), kv)
    q_dec = q * jnp.exp(log_g[None, :] * (idx + 1.0)[:, None])[None, None, :, :, None]
    cross = jnp.einsum('bcqhd,cbhde->bcqhe', q_dec, R_prev)
    return (intra + cross).reshape(B, S, H, dv)


def swa_sink_attention(q, k, v, sinks):
    B, S, Hq, d = q.shape
    Hkv = k.shape[2]
    G = Hq // Hkv
    W = SWA_WINDOW
    nB = S // W
    qb = q.reshape(B, nB, W, Hkv, G, d)
    kb = k.reshape(B, nB, W, Hkv, d)
    vb = v.reshape(B, nB, W, Hkv, d)
    pad = jnp.zeros_like(kb[:, :1])
    k_ext = jnp.concatenate([jnp.concatenate([pad, kb[:, :-1]], axis=1), kb], axis=2)
    v_ext = jnp.concatenate([jnp.concatenate([pad, vb[:, :-1]], axis=1), vb], axis=2)
    s = jnp.einsum('bnqkgd,bnskd->bkgnqs', qb, k_ext).astype(jnp.float32) * (d ** -0.5)
    blk = jnp.arange(nB)[:, None, None]
    qpos = blk * W + jnp.arange(W)[None, :, None]
    kpos = (blk - 1) * W + jnp.arange(2 * W)[None, None, :]
    rel = qpos - kpos
    valid = (rel >= 0) & (rel < W) & (kpos >= 0)
    s = jnp.where(valid, s, -jnp.inf)
    sink = sinks.astype(jnp.float32).reshape(Hkv, G)[None, :, :, None, None, None]
    m = jnp.maximum(jnp.max(s, axis=-1, keepdims=True), sink)
    p = jnp.exp(s - m)
    denom = jnp.sum(p, axis=-1, keepdims=True) + jnp.exp(sink - m)
    p = (p / denom).astype(v.dtype)
    o = jnp.einsum('bkgnqs,bnskd->bnqkgd', p, v_ext)
    return o.reshape(B, S, Hq * d)


def hierarchical_moe(h, w_rg, b_rg, w_re, b_re, w_gate, w_up, w_down):
    N, D = h.shape
    E = MOE_EXPERTS_PER_GROUP
    K = MOE_TOP_K
    BLK = MOE_BLOCK
    group_probs = jax.nn.softmax((h @ w_rg).astype(jnp.float32) + b_rg, axis=-1)
    g_prob, g_idx = lax.top_k(group_probs, 1)
    e_logits = ((h @ w_re).astype(jnp.float32) + b_re).reshape(N, MOE_GROUPS, E)
    sel = jnp.take_along_axis(e_logits, g_idx[:, :, None], axis=1)[:, 0]
    top_logit, e_idx = lax.top_k(sel, K)
    e_w = jax.nn.softmax(top_logit, axis=-1) * g_prob
    expert_id = g_idx * E + e_idx
    A = N * K
    flat_e = expert_id.reshape(A)
    flat_tok = jnp.repeat(jnp.arange(N, dtype=jnp.int32), K)
    flat_w = e_w.reshape(A)
    order = jnp.argsort(flat_e)
    se, stok, sw = flat_e[order], flat_tok[order], flat_w[order]
    counts = jnp.zeros((MOE_N_EXPERTS,), jnp.int32).at[flat_e].add(1)
    starts = jnp.cumsum(counts) - counts
    padded = (counts + BLK - 1) // BLK * BLK
    pends = jnp.cumsum(padded)
    pstarts = pends - padded
    dest = pstarts[se] + jnp.arange(A, dtype=jnp.int32) - starts[se]
    P = A + MOE_N_EXPERTS * BLK
    n_blk = P // BLK
    row_tok = jnp.zeros((P,), jnp.int32).at[dest].set(stok)
    row_w = jnp.zeros((P,), h.dtype).at[dest].set(sw.astype(h.dtype))
    blk_expert = jnp.minimum(jnp.searchsorted(pends, jnp.arange(n_blk, dtype=jnp.int32) * BLK, side='right'),
                             MOE_N_EXPERTS - 1)
    xs = h[row_tok].reshape(n_blk, BLK, D)

    def expert_block(args):
        xb, e = args
        return (jax.nn.silu(xb @ w_gate[e]) * (xb @ w_up[e])) @ w_down[e]

    ys = lax.map(expert_block, (xs, blk_expert)).reshape(P, D)
    return jnp.zeros_like(h).at[row_tok].add(ys * row_w[:, None])


def setup_inputs(seed: int = 0) -> dict:
    key = jax.random.key(seed)
    ks = jax.random.split(key, 20)
    L, D = DEPTH, D_MODEL
    f32 = jnp.float32

    def nrm(k, shape, scale):
        return jax.random.normal(k, shape, f32) * scale

    return {
        "x": nrm(ks[0], (BATCH, SEQ, D), 1.0),
        "norm_mix_g": 1.0 + nrm(ks[1], (L, D), 0.02),
        "w_in": nrm(ks[2], (L, D, IN_WIDTH), D ** -0.5),
        "ret_gn_g": 1.0 + nrm(ks[3], (L, RET_HEADS, RET_V_DIM), 0.02),
        "w_ret_o": nrm(ks[4], (L, RET_V_W, D), RET_V_W ** -0.5),
        "q_norm_g": 1.0 + nrm(ks[5], (L, SWA_HEAD_DIM), 0.02),
        "k_norm_g": 1.0 + nrm(ks[6], (L, SWA_HEAD_DIM), 0.02),
        "sinks": nrm(ks[7], (L, SWA_Q_HEADS), 0.5),
        "w_swa_o": nrm(ks[8], (L, SWA_Q_W, D), SWA_Q_W ** -0.5),
        "w_out": nrm(ks[9], (L, D, D), D ** -0.5),
        "norm_ffn_g": 1.0 + nrm(ks[10], (L, D), 0.02),
        "w_router_group": nrm(ks[11], (L, D, MOE_GROUPS), D ** -0.5),
        "b_router_group": nrm(ks[12], (L, MOE_GROUPS), 0.01),
        "w_router_expert": nrm(ks[13], (L, D, MOE_N_EXPERTS), D ** -0.5),
        "b_router_expert": nrm(ks[14], (L, MOE_N_EXPERTS), 0.01),
        "w_gate": nrm(ks[15], (L, MOE_N_EXPERTS, D, MOE_D_FF), D ** -0.5),
        "w_up": nrm(ks[16], (L, MOE_N_EXPERTS, D, MOE_D_FF), D ** -0.5),
        "w_down": nrm(ks[17], (L, MOE_N_EXPERTS, MOE_D_FF, D), MOE_D_FF ** -0.5),
    }


def reference(x, norm_mix_g, w_in, ret_gn_g, w_ret_o, q_norm_g, k_norm_g, sinks, w_swa_o, w_out,
              norm_ffn_g, w_router_group, b_router_group, w_router_expert, b_router_expert,
              w_gate, w_up, w_down):
    B, S, D = x.shape
    pos = jnp.arange(S, dtype=jnp.int32)
    for l in range(DEPTH):
        h = rmsnorm(x, norm_mix_g[l])
        proj = h @ w_in[l]
        rq, rk, rv, rg, sq, sk, sv, gate_a, gate_b = jnp.split(proj, IN_CUTS, axis=-1)
        rq = rotary(rq.astype(jnp.float32).reshape(B, S, RET_HEADS, RET_QK_DIM), pos)
        rk = rotary(rk.astype(jnp.float32).reshape(B, S, RET_HEADS, RET_QK_DIM), pos)
        rv = rv.astype(jnp.float32).reshape(B, S, RET_HEADS, RET_V_DIM)
        ret = retention_chunkwise(rq, rk, rv)
        mu = jnp.mean(ret, axis=-1, keepdims=True)
        var = jnp.mean(jnp.square(ret - mu), axis=-1, keepdims=True)
        ret = ((ret - mu) * lax.rsqrt(var + NORM_EPS)).astype(x.dtype) * ret_gn_g[l]
        ret = jax.nn.silu(rg) * ret.reshape(B, S, RET_V_W)
        y_a = ret @ w_ret_o[l]
        sq = rmsnorm(sq.reshape(B, S, SWA_Q_HEADS, SWA_HEAD_DIM), q_norm_g[l])
        sk = rmsnorm(sk.reshape(B, S, SWA_KV_HEADS, SWA_HEAD_DIM), k_norm_g[l])
        sv = sv.reshape(B, S, SWA_KV_HEADS, SWA_HEAD_DIM)
        y_b = swa_sink_attention(sq, sk, sv, sinks[l]) @ w_swa_o[l]
        merged = jax.nn.sigmoid(gate_a) * y_a + jax.nn.sigmoid(gate_b) * y_b
        x = x + merged @ w_out[l]
        h2 = rmsnorm(x, norm_ffn_g[l]).reshape(B * S, D)
        moe = hierarchical_moe(h2, w_router_group[l], b_router_group[l], w_router_expert[l],
                               b_router_expert[l], w_gate[l], w_up[l], w_down[l])
        x = x + moe.reshape(B, S, D)
    return x
```

```python
import functools
import math

import numpy as np
import jax
import jax.numpy as jnp
from jax import lax
from jax.experimental import pallas as pl
from jax.experimental.pallas import tpu as pltpu

D_MODEL = 1024
RET_HEADS = 4
RET_QK_DIM = 128
RET_V_DIM = 256
RET_CHUNK = 128
ROPE_BASE = 10000.0
SWA_Q_HEADS = 8
SWA_KV_HEADS = 2
SWA_HEAD_DIM = 64
SWA_WINDOW = 128
MOE_GROUPS = 4
MOE_EXPERTS_PER_GROUP = 8
MOE_D_FF = 512
NORM_EPS = 1e-6

RET_QK_W = RET_HEADS * RET_QK_DIM
RET_V_W = RET_HEADS * RET_V_DIM
SWA_Q_W = SWA_Q_HEADS * SWA_HEAD_DIM
SWA_KV_W = SWA_KV_HEADS * SWA_HEAD_DIM
SWA_GROUP = SWA_Q_HEADS // SWA_KV_HEADS
MOE_N_EXPERTS = MOE_GROUPS * MOE_EXPERTS_PER_GROUP

RET_COLS = 2 * RET_QK_W + 2 * RET_V_W
SWA_COLS = SWA_Q_W + 2 * SWA_KV_W
GATE_COLS = 2 * D_MODEL
IN_WIDTH = RET_COLS + SWA_COLS + GATE_COLS

LANES = 128
SUBLANES = 8
ROUTER_GROUP_LANE = 32
NEG_BIG = -1e30

ROW_TILE = 512
SEQ_TILE = 512
EXPERT_TILE = 256
COMBINE_TILE = 256
DISPATCH_TILE = 512
VMEM_LIMIT = 56 * 1024 * 1024

RET_LOG_DECAY = tuple(float(np.log1p(-np.exp2(-5.0 - h))) for h in range(RET_HEADS))

_BF16 = jnp.bfloat16
_F32 = jnp.float32


def _dot(a, b):
    return jnp.dot(a, b, preferred_element_type=_F32)


def _dot_nt(a, b):
    return lax.dot_general(a, b, (((1,), (1,)), ((), ())), preferred_element_type=_F32)


def _sigmoid(v):
    return 1.0 / (1.0 + jnp.exp(-v))


def _inproj_kernel(x_ref, g_ref, w_ref, ret_ref, swa_ref, gate_ref):
    x = x_ref[...]
    r = lax.rsqrt(jnp.mean(x * x, axis=-1, keepdims=True) + NORM_EPS)
    h = ((x * r) * g_ref[...]).astype(_BF16)
    base = 0
    for out_ref, width, step in ((ret_ref, RET_COLS, 512), (swa_ref, SWA_COLS, 256), (gate_ref, GATE_COLS, 512)):
        for c in range(0, width, step):
            out_ref[:, c:c + step] = _dot(h, w_ref[:, base + c:base + c + step]).astype(_BF16)
        base += width


def _inproj(x2d, g, w_in_bf16):
    n = x2d.shape[0]
    grid = (n // ROW_TILE,)
    return pl.pallas_call(
        _inproj_kernel,
        grid=grid,
        in_specs=[
            pl.BlockSpec((ROW_TILE, D_MODEL), lambda i: (i, 0)),
            pl.BlockSpec((1, D_MODEL), lambda i: (0, 0)),
            pl.BlockSpec((D_MODEL, IN_WIDTH), lambda i: (0, 0)),
        ],
        out_specs=[
            pl.BlockSpec((ROW_TILE, RET_COLS), lambda i: (i, 0)),
            pl.BlockSpec((ROW_TILE, SWA_COLS), lambda i: (i, 0)),
            pl.BlockSpec((ROW_TILE, GATE_COLS), lambda i: (i, 0)),
        ],
        out_shape=[
            jax.ShapeDtypeStruct((n, RET_COLS), _BF16),
            jax.ShapeDtypeStruct((n, SWA_COLS), _BF16),
            jax.ShapeDtypeStruct((n, GATE_COLS), _BF16),
        ],
        compiler_params=pltpu.CompilerParams(
            dimension_semantics=("arbitrary",), vmem_limit_bytes=VMEM_LIMIT),
        name="inproj",
    )(x2d, g, w_in_bf16)


def _retention_kernel(q_ref, k_ref, v_ref, g_ref, cos_ref, sin_ref, gn_ref, out_ref, state_ref):
    c = RET_CHUNK

    @pl.when(pl.program_id(1) == 0)
    def _():
        state_ref[...] = jnp.zeros_like(state_ref)

    row = lax.broadcasted_iota(jnp.int32, (c, c), 0)
    col = lax.broadcasted_iota(jnp.int32, (c, c), 1)
    diff = (row - col).astype(_F32)
    pos = lax.broadcasted_iota(jnp.int32, (c, 1), 0).astype(_F32)
    scale = RET_QK_DIM ** -0.5

    for h in range(RET_HEADS):
        lg = RET_LOG_DECAY[h]
        decay = jnp.where(diff >= 0, jnp.exp(lg * jnp.maximum(diff, 0.0)), 0.0) * scale
        q_decay = jnp.exp(lg * (pos + 1.0))
        k_decay = jnp.exp(lg * (c - 1.0 - pos)) * scale
        chunk_decay = math.exp(lg * c)
        qs = slice(h * RET_QK_DIM, (h + 1) * RET_QK_DIM)
        vs = slice(h * RET_V_DIM, (h + 1) * RET_V_DIM)
        gain = gn_ref[h:h + 1, :]
        for ci in range(SEQ_TILE // c):
            rows = slice(ci * c, (ci + 1) * c)
            cos = cos_ref[rows, :]
            sin = sin_ref[rows, :]
            q_raw = q_ref[rows, qs].astype(_F32)
            k_raw = k_ref[rows, qs].astype(_F32)
            q = q_raw * cos + pltpu.roll(q_raw, RET_QK_DIM // 2, 1) * sin
            k = k_raw * cos + pltpu.roll(k_raw, RET_QK_DIM // 2, 1) * sin
            v = v_ref[rows, vs]
            state = state_ref[h]
            scores = _dot_nt(q.astype(_BF16), k.astype(_BF16)) * decay
            ret = _dot(scores.astype(_BF16), v) + _dot((q * q_decay).astype(_BF16), state.astype(_BF16))
            kv = _dot((k * k_decay).T.astype(_BF16), v)
            state_ref[h] = chunk_decay * state + kv
            mu = jnp.mean(ret, axis=-1, keepdims=True)
            dev = ret - mu
            var = jnp.mean(dev * dev, axis=-1, keepdims=True)
            normed = (dev * lax.rsqrt(var + NORM_EPS)) * gain
            gate = g_ref[rows, vs].astype(_F32)
            out_ref[rows, vs] = ((gate * _sigmoid(gate)) * normed).astype(_BF16)


def _retention(ret_proj, cos_tab, sin_tab, gn_g, batch, seq):
    n = ret_proj.shape[0]
    steps = seq // SEQ_TILE
    rowmap = lambda b, s: b * steps + s
    return pl.pallas_call(
        _retention_kernel,
        grid=(batch, steps),
        in_specs=[
            pl.BlockSpec((SEQ_TILE, RET_QK_W), lambda b, s: (rowmap(b, s), 0)),
            pl.BlockSpec((SEQ_TILE, RET_QK_W), lambda b, s: (rowmap(b, s), 1)),
            pl.BlockSpec((SEQ_TILE, RET_V_W), lambda b, s: (rowmap(b, s), 1)),
            pl.BlockSpec((SEQ_TILE, RET_V_W), lambda b, s: (rowmap(b, s), 2)),
            pl.BlockSpec((SEQ_TILE, RET_QK_DIM), lambda b, s: (s, 0)),
            pl.BlockSpec((SEQ_TILE, RET_QK_DIM), lambda b, s: (s, 0)),
            pl.BlockSpec((RET_HEADS, RET_V_DIM), lambda b, s: (0, 0)),
        ],
        out_specs=pl.BlockSpec((SEQ_TILE, RET_V_W), lambda b, s: (rowmap(b, s), 0)),
        out_shape=jax.ShapeDtypeStruct((n, RET_V_W), _BF16),
        scratch_shapes=[pltpu.VMEM((RET_HEADS, RET_QK_DIM, RET_V_DIM), _F32)],
        compiler_params=pltpu.CompilerParams(
            dimension_semantics=("arbitrary", "arbitrary"), vmem_limit_bytes=VMEM_LIMIT),
        name="retention",
    )(ret_proj, ret_proj, ret_proj, ret_proj, cos_tab, sin_tab, gn_g)


def _head_rmsnorm(v, gain):
    r = lax.rsqrt(jnp.mean(v * v, axis=-1, keepdims=True) + NORM_EPS)
    return (v * r) * gain


def _swa_kernel(sinks_ref, q_ref, kc_ref, vc_ref, kp_ref, vp_ref, qg_ref, kg_ref, out_ref):
    w = SWA_WINDOW
    d = SWA_HEAD_DIM
    first = pl.program_id(1) == 0
    rows_q = SWA_GROUP * w
    qi = lax.broadcasted_iota(jnp.int32, (rows_q, 2 * w), 0) % w
    kj = lax.broadcasted_iota(jnp.int32, (rows_q, 2 * w), 1)
    band = (kj > qi) & (kj <= qi + w)
    band_first = band & jnp.logical_or(kj >= w, jnp.logical_not(first))
    q_gain = qg_ref[...]
    k_gain = kg_ref[...]
    for kk in range(SWA_KV_HEADS):
        ks = slice(kk * d, (kk + 1) * d)
        k_all = jnp.concatenate([kp_ref[:, ks], kc_ref[:, ks]], axis=0).astype(_F32)
        k_all = _head_rmsnorm(k_all, k_gain).astype(_BF16)
        v_all = jnp.concatenate([vp_ref[:, ks], vc_ref[:, ks]], axis=0)
        sink_col = jnp.concatenate(
            [jnp.full((w, 1), sinks_ref[kk * SWA_GROUP + g], _F32) for g in range(SWA_GROUP)], axis=0)
        for j in range(SEQ_TILE // w):
            rows = slice(j * w, (j + 1) * w)
            k_ext = k_all[j * w:(j + 2) * w]
            v_ext = v_all[j * w:(j + 2) * w]
            q4 = jnp.concatenate(
                [q_ref[rows, (kk * SWA_GROUP + g) * d:(kk * SWA_GROUP + g + 1) * d] for g in range(SWA_GROUP)],
                axis=0).astype(_F32)
            q4 = _head_rmsnorm(q4, q_gain).astype(_BF16)
            s = _dot_nt(q4, k_ext) * (d ** -0.5)
            s = jnp.where(band_first if j == 0 else band, s, NEG_BIG)
            m = jnp.maximum(jnp.max(s, axis=-1, keepdims=True), sink_col)
            p = jnp.exp(s - m)
            denom = jnp.sum(p, axis=-1, keepdims=True) + jnp.exp(sink_col - m)
            p = (p * (1.0 / denom)).astype(_BF16)
            o = _dot(p, v_ext).astype(_BF16)
            for g in range(SWA_GROUP):
                hq = kk * SWA_GROUP + g
                out_ref[rows, hq * d:(hq + 1) * d] = o[g * w:(g + 1) * w]


def _swa(swa_proj, sinks, q_gain, k_gain, batch, seq):
    n = swa_proj.shape[0]
    steps = seq // SEQ_TILE
    blocks_per_step = SEQ_TILE // SWA_WINDOW
    rowmap = lambda b, s: b * steps + s
    prevmap = lambda b, s: jnp.maximum((b * steps + s) * blocks_per_step - 1, 0)
    k_col = SWA_Q_W // SWA_KV_W
    return pl.pallas_call(
        _swa_kernel,
        grid=(batch, steps),
        in_specs=[
            pl.BlockSpec(memory_space=pltpu.SMEM),
            pl.BlockSpec((SEQ_TILE, SWA_Q_W), lambda b, s: (rowmap(b, s), 0)),
            pl.BlockSpec((SEQ_TILE, SWA_KV_W), lambda b, s: (rowmap(b, s), k_col)),
            pl.BlockSpec((SEQ_TILE, SWA_KV_W), lambda b, s: (rowmap(b, s), k_col + 1)),
            pl.BlockSpec((SWA_WINDOW, SWA_KV_W), lambda b, s: (prevmap(b, s), k_col)),
            pl.BlockSpec((SWA_WINDOW, SWA_KV_W), lambda b, s: (prevmap(b, s), k_col + 1)),
            pl.BlockSpec((1, SWA_HEAD_DIM), lambda b, s: (0, 0)),
            pl.BlockSpec((1, SWA_HEAD_DIM), lambda b, s: (0, 0)),
        ],
        out_specs=pl.BlockSpec((SEQ_TILE, SWA_Q_W), lambda b, s: (rowmap(b, s), 0)),
        out_shape=jax.ShapeDtypeStruct((n, SWA_Q_W), _BF16),
        compiler_params=pltpu.CompilerParams(
            dimension_semantics=("arbitrary", "arbitrary"), vmem_limit_bytes=VMEM_LIMIT),
        name="swa",
    )(sinks, swa_proj, swa_proj, swa_proj, swa_proj, swa_proj, q_gain, k_gain)


def _outproj_kernel(x_ref, retg_ref, attn_ref, ga_ref, gb_ref, wro_ref, wso_ref, wout_ref, g2_ref,
                    wr_ref, br_ref, x1_ref, h2_ref, info_ref, counts_ref):
    tm = ROW_TILE

    @pl.when(pl.program_id(0) == 0)
    def _():
        counts_ref[...] = jnp.zeros_like(counts_ref)

    y_a = _dot(retg_ref[...], wro_ref[...])
    y_b = _dot(attn_ref[...], wso_ref[...])
    merged = _sigmoid(ga_ref[...].astype(_F32)) * y_a + _sigmoid(gb_ref[...].astype(_F32)) * y_b
    x1 = x_ref[...] + _dot(merged.astype(_BF16), wout_ref[...])
    x1_ref[...] = x1
    r = lax.rsqrt(jnp.mean(x1 * x1, axis=-1, keepdims=True) + NORM_EPS)
    h2 = (x1 * r) * g2_ref[...]
    h2_ref[...] = h2

    logits = _dot(h2.astype(_BF16), wr_ref[...]) + br_ref[...]
    lane = lax.broadcasted_iota(jnp.int32, (tm, LANES), 1)
    lane_f = lane.astype(_F32)
    is_group = (lane >= ROUTER_GROUP_LANE) & (lane < ROUTER_GROUP_LANE + MOE_GROUPS)
    gl = jnp.where(is_group, logits, NEG_BIG)
    g_max = jnp.max(gl, axis=-1, keepdims=True)
    g_prob = 1.0 / jnp.sum(jnp.exp(gl - g_max), axis=-1, keepdims=True)
    g_idx = jnp.min(jnp.where(gl == g_max, lane_f - ROUTER_GROUP_LANE, float(LANES)), axis=-1, keepdims=True)
    e_lo = g_idx * MOE_EXPERTS_PER_GROUP
    in_group = (lane_f >= e_lo) & (lane_f < e_lo + MOE_EXPERTS_PER_GROUP)
    el = jnp.where(in_group, logits, NEG_BIG)
    t1 = jnp.max(el, axis=-1, keepdims=True)
    i1 = jnp.min(jnp.where(el == t1, lane_f, float(LANES)), axis=-1, keepdims=True)
    el2 = jnp.where(lane_f == i1, NEG_BIG, el)
    t2 = jnp.max(el2, axis=-1, keepdims=True)
    i2 = jnp.min(jnp.where(el2 == t2, lane_f, float(LANES)), axis=-1, keepdims=True)
    e21 = jnp.exp(t2 - t1)
    w1 = g_prob / (1.0 + e21)
    w2 = g_prob * e21 / (1.0 + e21)

    sel1 = lane_f == i1
    sel2 = lane_f == i2
    onehot = jnp.where(sel1 | sel2, 1.0, 0.0)
    tri = (lax.broadcasted_iota(jnp.int32, (tm, tm), 0) > lax.broadcasted_iota(jnp.int32, (tm, tm), 1))
    before = _dot(tri.astype(_BF16), onehot.astype(_BF16)) + counts_ref[0:1, :]
    rank1 = jnp.sum(jnp.where(sel1, before, 0.0), axis=-1, keepdims=True)
    rank2 = jnp.sum(jnp.where(sel2, before, 0.0), axis=-1, keepdims=True)
    counts_ref[0:1, :] = counts_ref[0:1, :] + jnp.sum(onehot, axis=0, keepdims=True)

    info = jnp.where(lane == 0, i1, 0.0)
    info = jnp.where(lane == 1, i2, info)
    info = jnp.where(lane == 2, w1, info)
    info = jnp.where(lane == 3, w2, info)
    info = jnp.where(lane == 4, rank1, info)
    info = jnp.where(lane == 5, rank2, info)
    info_ref[...] = info


def _outproj(x2d, retg, attn, gates, w_ret_o, w_swa_o, w_out, g2, w_router, b_router):
    n = x2d.shape[0]
    tm = ROW_TILE
    const = lambda i: (0, 0)
    return pl.pallas_call(
        _outproj_kernel,
        grid=(n // tm,),
        in_specs=[
            pl.BlockSpec((tm, D_MODEL), lambda i: (i, 0)),
            pl.BlockSpec((tm, RET_V_W), lambda i: (i, 0)),
            pl.BlockSpec((tm, SWA_Q_W), lambda i: (i, 0)),
            pl.BlockSpec((tm, D_MODEL), lambda i: (i, 0)),
            pl.BlockSpec((tm, D_MODEL), lambda i: (i, 1)),
            pl.BlockSpec((RET_V_W, D_MODEL), const),
            pl.BlockSpec((SWA_Q_W, D_MODEL), const),
            pl.BlockSpec((D_MODEL, D_MODEL), const),
            pl.BlockSpec((1, D_MODEL), const),
            pl.BlockSpec((D_MODEL, LANES), const),
            pl.BlockSpec((1, LANES), const),
        ],
        out_specs=[
            pl.BlockSpec((tm, D_MODEL), lambda i: (i, 0)),
            pl.BlockSpec((tm, D_MODEL), lambda i: (i, 0)),
            pl.BlockSpec((tm, LANES), lambda i: (i, 0)),
            pl.BlockSpec((8, LANES), const),
        ],
        out_shape=[
            jax.ShapeDtypeStruct((n, D_MODEL), _F32),
            jax.ShapeDtypeStruct((n, D_MODEL), _F32),
            jax.ShapeDtypeStruct((n, LANES), _F32),
            jax.ShapeDtypeStruct((8, LANES), _F32),
        ],
        compiler_params=pltpu.CompilerParams(
            dimension_semantics=("arbitrary",), vmem_limit_bytes=VMEM_LIMIT),
        name="outproj",
    )(x2d, retg, attn, gates, gates, w_ret_o, w_swa_o, w_out, g2, w_router, b_router)


def _row_copy(src_hbm, src_row, dst_hbm, dst_row, sem):
    return pltpu.make_async_copy(src_hbm.at[pl.ds(src_row, 1)], dst_hbm.at[pl.ds(dst_row, 1)], sem)


def _zero_fill_padding(pad_start_ref, pad_len_ref, zeros_ref, xs_hbm, sem, wait):
    def per_expert(e, carry):
        start = pad_start_ref[e]
        head = (-start) & (SUBLANES - 1)
        for r in range(SUBLANES - 1):
            copy = pltpu.make_async_copy(zeros_ref.at[pl.ds(0, 1)], xs_hbm.at[pl.ds(start + r, 1)], sem)
            pl.when(r < head)(copy.wait if wait else copy.start)
        pos = start + head
        rest = pad_len_ref[e] - head
        bit = EXPERT_TILE // 2
        while bit >= SUBLANES:
            copy = pltpu.make_async_copy(
                zeros_ref.at[pl.ds(0, bit)], xs_hbm.at[pl.ds(pl.multiple_of(pos, SUBLANES), bit)], sem)
            pl.when((rest & bit) != 0)(copy.wait if wait else copy.start)
            pos = pos + (rest & bit)
            bit //= 2
        return carry

    lax.fori_loop(0, MOE_N_EXPERTS, per_expert, 0)

    tail_start = pad_start_ref[MOE_N_EXPERTS]
    n_rows = xs_hbm.shape[0]
    for j in range(MOE_N_EXPERTS):
        pos = tail_start + j * EXPERT_TILE
        copy = pltpu.make_async_copy(
            zeros_ref,
            xs_hbm.at[pl.ds(pl.multiple_of(jnp.minimum(pos, n_rows - EXPERT_TILE), SUBLANES), EXPERT_TILE)], sem)
        pl.when(pos < n_rows)(copy.wait if wait else copy.start)


def _dispatch_kernel(pad_start_ref, pad_len_ref, dest_ref, h2_hbm, xs_hbm, zeros_ref, sem, zero_sem):
    base = pl.program_id(0) * DISPATCH_TILE

    @pl.when(pl.program_id(0) == 0)
    def _():
        zeros_ref[...] = jnp.zeros_like(zeros_ref)
        _zero_fill_padding(pad_start_ref, pad_len_ref, zeros_ref, xs_hbm, zero_sem, wait=False)
        _zero_fill_padding(pad_start_ref, pad_len_ref, zeros_ref, xs_hbm, zero_sem, wait=True)

    def issue(t, carry):
        for k in range(2):
            _row_copy(h2_hbm, base + t, xs_hbm, dest_ref[2 * t + k], sem).start()
        return carry

    lax.fori_loop(0, DISPATCH_TILE, issue, 0)

    def drain(t, carry):
        for k in range(2):
            _row_copy(h2_hbm, 0, xs_hbm, 0, sem).wait()
        return carry

    lax.fori_loop(0, DISPATCH_TILE, drain, 0)


def _dispatch(pad_start, pad_len, dest_flat, h2, n_rows):
    n, width = h2.shape
    grid_spec = pltpu.PrefetchScalarGridSpec(
        num_scalar_prefetch=2,
        grid=(n // DISPATCH_TILE,),
        in_specs=[
            pl.BlockSpec((2 * DISPATCH_TILE,), lambda i, ps, pn: (i,), memory_space=pltpu.SMEM),
            pl.BlockSpec(memory_space=pl.ANY),
        ],
        out_specs=pl.BlockSpec(memory_space=pl.ANY),
        scratch_shapes=[
            pltpu.VMEM((EXPERT_TILE, width), h2.dtype),
            pltpu.SemaphoreType.DMA(()),
            pltpu.SemaphoreType.DMA(()),
        ],
    )
    return pl.pallas_call(
        _dispatch_kernel,
        grid_spec=grid_spec,
        out_shape=jax.ShapeDtypeStruct((n_rows, width), h2.dtype),
        compiler_params=pltpu.CompilerParams(dimension_semantics=("arbitrary",)),
        name="dispatch",
    )(pad_start, pad_len, dest_flat, h2)


def _experts_kernel(te_ref, tv_ref, xs_ref, wg_ref, wu_ref, wd_ref, ys_ref, wg_s, wu_s, wd_s):
    i = pl.program_id(0)
    valid = tv_ref[i]
    prev = te_ref[jnp.maximum(i - 1, 0)]
    changed = jnp.logical_or(i == 0, te_ref[i] != prev)

    @pl.when(jnp.logical_and(valid > 0, changed))
    def _():
        wg_s[...] = wg_ref[...].astype(_BF16)
        wu_s[...] = wu_ref[...].astype(_BF16)
        wd_s[...] = wd_ref[...].astype(_BF16)

    @pl.when(valid > 0)
    def _():
        xb = xs_ref[...].astype(_BF16)
        gate = _dot(xb, wg_s[...])
        up = _dot(xb, wu_s[...])
        act = ((gate * _sigmoid(gate)) * up).astype(_BF16)
        ys_ref[...] = _dot(act, wd_s[...])

    @pl.when(valid <= 0)
    def _():
        ys_ref[...] = jnp.zeros_like(ys_ref)


def _experts(tile_expert, tile_valid, xs, w_gate, w_up, w_down):
    n_rows = xs.shape[0]
    n_tiles = n_rows // EXPERT_TILE
    grid_spec = pltpu.PrefetchScalarGridSpec(
        num_scalar_prefetch=2,
        grid=(n_tiles,),
        in_specs=[
            pl.BlockSpec((EXPERT_TILE, D_MODEL), lambda i, te, tv: (i, 0)),
            pl.BlockSpec((None, D_MODEL, MOE_D_FF), lambda i, te, tv: (te[i], 0, 0)),
            pl.BlockSpec((None, D_MODEL, MOE_D_FF), lambda i, te, tv: (te[i], 0, 0)),
            pl.BlockSpec((None, MOE_D_FF, D_MODEL), lambda i, te, tv: (te[i], 0, 0)),
        ],
        out_specs=pl.BlockSpec((EXPERT_TILE, D_MODEL), lambda i, te, tv: (i, 0)),
        scratch_shapes=[
            pltpu.VMEM((D_MODEL, MOE_D_FF), _BF16),
            pltpu.VMEM((D_MODEL, MOE_D_FF), _BF16),
            pltpu.VMEM((MOE_D_FF, D_MODEL), _BF16),
        ],
    )
    return pl.pallas_call(
        _experts_kernel,
        grid_spec=grid_spec,
        out_shape=jax.ShapeDtypeStruct((n_rows, D_MODEL), _F32),
        compiler_params=pltpu.CompilerParams(
            dimension_semantics=("arbitrary",), vmem_limit_bytes=VMEM_LIMIT),
        name="experts",
    )(tile_expert, tile_valid, xs, w_gate, w_up, w_down)


def _combine_kernel(dest_ref, x1_ref, info_ref, ys_hbm, out_ref, buf, sem):
    tc = COMBINE_TILE

    def issue(t, carry):
        for k in range(2):
            pltpu.make_async_copy(ys_hbm.at[pl.ds(dest_ref[2 * t + k], 1)], buf.at[k, pl.ds(t, 1)], sem).start()
        return carry

    lax.fori_loop(0, tc, issue, 0)

    def drain(t, carry):
        for k in range(2):
            pltpu.make_async_copy(ys_hbm.at[pl.ds(0, 1)], buf.at[k, pl.ds(0, 1)], sem).wait()
        return carry

    lax.fori_loop(0, tc, drain, 0)
    info = info_ref[...]
    w0 = info[:, 2:3]
    w1 = info[:, 3:4]
    out_ref[...] = x1_ref[...] + w0 * buf[0] + w1 * buf[1]


def _combine(dest_flat, x1, info, ys):
    n = x1.shape[0]
    tc = COMBINE_TILE
    return pl.pallas_call(
        _combine_kernel,
        grid=(n // tc,),
        in_specs=[
            pl.BlockSpec((2 * tc,), lambda i: (i,), memory_space=pltpu.SMEM),
            pl.BlockSpec((tc, D_MODEL), lambda i: (i, 0)),
            pl.BlockSpec((tc, LANES), lambda i: (i, 0)),
            pl.BlockSpec(memory_space=pl.ANY),
        ],
        out_specs=pl.BlockSpec((tc, D_MODEL), lambda i: (i, 0)),
        out_shape=jax.ShapeDtypeStruct((n, D_MODEL), _F32),
        scratch_shapes=[pltpu.VMEM((2, tc, D_MODEL), _F32), pltpu.SemaphoreType.DMA(())],
        compiler_params=pltpu.CompilerParams(
            dimension_semantics=("arbitrary",), vmem_limit_bytes=VMEM_LIMIT),
        name="combine",
    )(dest_flat, x1, info, ys)


def _rotary_tables(seq):
    half = RET_QK_DIM // 2
    inv = ROPE_BASE ** (-jnp.arange(half, dtype=_F32) / half)
    ang = jnp.arange(seq, dtype=jnp.int32).astype(_F32)[:, None] * inv[None, :]
    cos = jnp.cos(ang)
    sin = jnp.sin(ang)
    return jnp.concatenate([cos, cos], axis=1), jnp.concatenate([-sin, sin], axis=1)


def _routing_plan(info, counts_out):
    te = EXPERT_TILE
    n = info.shape[0]
    counts = counts_out[0, :MOE_N_EXPERTS].astype(jnp.int32)
    padded = (counts + te - 1) // te * te
    pends = jnp.cumsum(padded)
    pstarts = pends - padded
    eid = info[:, 0:2].astype(jnp.int32)
    rank = info[:, 4:6].astype(jnp.int32)
    onehot = eid[:, :, None] == jnp.arange(MOE_N_EXPERTS, dtype=jnp.int32)[None, None, :]
    dest = jnp.sum(jnp.where(onehot, pstarts[None, None, :], 0), axis=-1) + rank
    n_tiles = (2 * n + MOE_N_EXPERTS * te) // te
    tile_start = jnp.arange(n_tiles, dtype=jnp.int32) * te
    tile_expert = jnp.sum((tile_start[:, None] >= pends[None, :]).astype(jnp.int32), axis=1)
    tile_expert = jnp.minimum(tile_expert, MOE_N_EXPERTS - 1)
    tile_valid = jnp.clip(counts[tile_expert] - (tile_start - pstarts[tile_expert]), 0, te).astype(jnp.int32)
    pad_start = jnp.concatenate([pstarts + counts, pends[-1:]]).astype(jnp.int32)
    pad_len = (padded - counts).astype(jnp.int32)
    return dest.reshape(-1), tile_expert, tile_valid, pad_start, pad_len, n_tiles * te


def kernel(x, norm_mix_g, w_in, ret_gn_g, w_ret_o, q_norm_g, k_norm_g, sinks, w_swa_o, w_out, norm_ffn_g,
           w_router_group, b_router_group, w_router_expert, b_router_expert, w_gate, w_up, w_down):
    batch, seq, d = x.shape
    n = batch * seq
    assert d == D_MODEL and seq % SEQ_TILE == 0 and n % ROW_TILE == 0
    cos_tab, sin_tab = _rotary_tables(seq)
    for l in range(w_in.shape[0]):
        x2d = x.reshape(n, d)
        ret_proj, swa_proj, gates = _inproj(x2d, norm_mix_g[l][None, :], w_in[l].astype(_BF16))
        retg = _retention(ret_proj, cos_tab, sin_tab, ret_gn_g[l], batch, seq)
        attn = _swa(swa_proj, sinks[l], q_norm_g[l][None, :], k_norm_g[l][None, :], batch, seq)
        pad = LANES - MOE_N_EXPERTS - MOE_GROUPS
        w_router = jnp.concatenate(
            [w_router_expert[l], w_router_group[l], jnp.zeros((d, pad), _F32)], axis=1).astype(_BF16)
        b_router = jnp.concatenate([b_router_expert[l], b_router_group[l], jnp.zeros((pad,), _F32)])[None, :]
        x1, h2, info, counts = _outproj(
            x2d, retg, attn, gates, w_ret_o[l].astype(_BF16), w_swa_o[l].astype(_BF16), w_out[l].astype(_BF16),
            norm_ffn_g[l][None, :], w_router, b_router)
        dest, tile_expert, tile_valid, pad_start, pad_len, n_rows = _routing_plan(info, counts)
        xs = _dispatch(pad_start, pad_len, dest, h2, n_rows)
        ys = _experts(tile_expert, tile_valid, xs, w_gate[l], w_up[l], w_down[l])
        x = _combine(dest, x1, info, ys).reshape(batch, seq, d)
    return x
```

```python
import functools
import math

import numpy as np
import jax
import jax.numpy as jnp
from jax import lax
from jax.experimental import pallas as pl
from jax.experimental.pallas import tpu as pltpu

D_MODEL = 1024
RET_HEADS = 4
RET_QK_DIM = 128
RET_V_DIM = 256
RET_CHUNK = 128
ROPE_BASE = 10000.0
SWA_Q_HEADS = 8
SWA_KV_HEADS = 2
SWA_HEAD_DIM = 64
SWA_WINDOW = 128
MOE_GROUPS = 4
MOE_EXPERTS_PER_GROUP = 8
MOE_D_FF = 512
NORM_EPS = 1e-6

RET_QK_W = RET_HEADS * RET_QK_DIM
RET_V_W = RET_HEADS * RET_V_DIM
SWA_Q_W = SWA_Q_HEADS * SWA_HEAD_DIM
SWA_KV_W = SWA_KV_HEADS * SWA_HEAD_DIM
SWA_GROUP = SWA_Q_HEADS // SWA_KV_HEADS
MOE_N_EXPERTS = MOE_GROUPS * MOE_EXPERTS_PER_GROUP

RET_COLS = 2 * RET_QK_W + 2 * RET_V_W
SWA_COLS = SWA_Q_W + 2 * SWA_KV_W
GATE_COLS = 2 * D_MODEL
IN_WIDTH = RET_COLS + SWA_COLS + GATE_COLS

LANES = 128
SUBLANES = 8
ROW_CHUNKS = D_MODEL // LANES
assert ROW_CHUNKS == SUBLANES
ROUTER_GROUP_LANE = 32
NEG_BIG = -1e30

ROW_TILE = 512
SEQ_TILE = 512
EXPERT_TILE = 256
COMBINE_TILE = 256
DISPATCH_TILE = 512
VMEM_LIMIT = 56 * 1024 * 1024

RET_LOG_DECAY = tuple(float(np.log1p(-np.exp2(-5.0 - h))) for h in range(RET_HEADS))

_BF16 = jnp.bfloat16
_F32 = jnp.float32


def _dot(a, b):
    return jnp.dot(a, b, preferred_element_type=_F32)


def _dot_nt(a, b):
    return lax.dot_general(a, b, (((1,), (1,)), ((), ())), preferred_element_type=_F32)


def _sigmoid(v):
    return 1.0 / (1.0 + jnp.exp(-v))


def _inproj_kernel(x_ref, g_ref, w_ref, ret_ref, swa_ref, gate_ref):
    x = x_ref[...]
    r = lax.rsqrt(jnp.mean(x * x, axis=-1, keepdims=True) + NORM_EPS)
    h = ((x * r) * g_ref[...]).astype(_BF16)
    base = 0
    for out_ref, width, step in ((ret_ref, RET_COLS, 512), (swa_ref, SWA_COLS, 256), (gate_ref, GATE_COLS, 512)):
        for c in range(0, width, step):
            out_ref[:, c:c + step] = _dot(h, w_ref[:, base + c:base + c + step]).astype(_BF16)
        base += width


def _inproj(x2d, g, w_in_bf16):
    n = x2d.shape[0]
    grid = (n // ROW_TILE,)
    return pl.pallas_call(
        _inproj_kernel,
        grid=grid,
        in_specs=[
            pl.BlockSpec((ROW_TILE, D_MODEL), lambda i: (i, 0)),
            pl.BlockSpec((1, D_MODEL), lambda i: (0, 0)),
            pl.BlockSpec((D_MODEL, IN_WIDTH), lambda i: (0, 0)),
        ],
        out_specs=[
            pl.BlockSpec((ROW_TILE, RET_COLS), lambda i: (i, 0)),
            pl.BlockSpec((ROW_TILE, SWA_COLS), lambda i: (i, 0)),
            pl.BlockSpec((ROW_TILE, GATE_COLS), lambda i: (i, 0)),
        ],
        out_shape=[
            jax.ShapeDtypeStruct((n, RET_COLS), _BF16),
            jax.ShapeDtypeStruct((n, SWA_COLS), _BF16),
            jax.ShapeDtypeStruct((n, GATE_COLS), _BF16),
        ],
        compiler_params=pltpu.CompilerParams(
            dimension_semantics=("arbitrary",), vmem_limit_bytes=VMEM_LIMIT),
        name="inproj",
    )(x2d, g, w_in_bf16)


def _retention_kernel(q_ref, k_ref, v_ref, g_ref, cos_ref, sin_ref, gn_ref, out_ref, state_ref):
    c = RET_CHUNK

    @pl.when(pl.program_id(1) == 0)
    def _():
        state_ref[...] = jnp.zeros_like(state_ref)

    row = lax.broadcasted_iota(jnp.int32, (c, c), 0)
    col = lax.broadcasted_iota(jnp.int32, (c, c), 1)
    diff = (row - col).astype(_F32)
    pos = lax.broadcasted_iota(jnp.int32, (c, 1), 0).astype(_F32)
    scale = RET_QK_DIM ** -0.5

    for h in range(RET_HEADS):
        lg = RET_LOG_DECAY[h]
        decay = jnp.where(diff >= 0, jnp.exp(lg * jnp.maximum(diff, 0.0)), 0.0) * scale
        q_decay = jnp.exp(lg * (pos + 1.0))
        k_decay = jnp.exp(lg * (c - 1.0 - pos)) * scale
        chunk_decay = math.exp(lg * c)
        qs = slice(h * RET_QK_DIM, (h + 1) * RET_QK_DIM)
        vs = slice(h * RET_V_DIM, (h + 1) * RET_V_DIM)
        gain = gn_ref[h:h + 1, :]
        for ci in range(SEQ_TILE // c):
            rows = slice(ci * c, (ci + 1) * c)
            cos = cos_ref[rows, :]
            sin = sin_ref[rows, :]
            q_raw = q_ref[rows, qs].astype(_F32)
            k_raw = k_ref[rows, qs].astype(_F32)
            q = q_raw * cos + pltpu.roll(q_raw, RET_QK_DIM // 2, 1) * sin
            k = k_raw * cos + pltpu.roll(k_raw, RET_QK_DIM // 2, 1) * sin
            v = v_ref[rows, vs]
            state = state_ref[h]
            scores = _dot_nt(q.astype(_BF16), k.astype(_BF16)) * decay
            ret = _dot(scores.astype(_BF16), v) + _dot((q * q_decay).astype(_BF16), state.astype(_BF16))
            kv = _dot((k * k_decay).T.astype(_BF16), v)
            state_ref[h] = chunk_decay * state + kv
            mu = jnp.mean(ret, axis=-1, keepdims=True)
            dev = ret - mu
            var = jnp.mean(dev * dev, axis=-1, keepdims=True)
            normed = (dev * lax.rsqrt(var + NORM_EPS)) * gain
            gate = g_ref[rows, vs].astype(_F32)
            out_ref[rows, vs] = ((gate * _sigmoid(gate)) * normed).astype(_BF16)


def _retention(ret_proj, cos_tab, sin_tab, gn_g, batch, seq):
    n = ret_proj.shape[0]
    steps = seq // SEQ_TILE
    rowmap = lambda b, s: b * steps + s
    return pl.pallas_call(
        _retention_kernel,
        grid=(batch, steps),
        in_specs=[
            pl.BlockSpec((SEQ_TILE, RET_QK_W), lambda b, s: (rowmap(b, s), 0)),
            pl.BlockSpec((SEQ_TILE, RET_QK_W), lambda b, s: (rowmap(b, s), 1)),
            pl.BlockSpec((SEQ_TILE, RET_V_W), lambda b, s: (rowmap(b, s), 1)),
            pl.BlockSpec((SEQ_TILE, RET_V_W), lambda b, s: (rowmap(b, s), 2)),
            pl.BlockSpec((SEQ_TILE, RET_QK_DIM), lambda b, s: (s, 0)),
            pl.BlockSpec((SEQ_TILE, RET_QK_DIM), lambda b, s: (s, 0)),
            pl.BlockSpec((RET_HEADS, RET_V_DIM), lambda b, s: (0, 0)),
        ],
        out_specs=pl.BlockSpec((SEQ_TILE, RET_V_W), lambda b, s: (rowmap(b, s), 0)),
        out_shape=jax.ShapeDtypeStruct((n, RET_V_W), _BF16),
        scratch_shapes=[pltpu.VMEM((RET_HEADS, RET_QK_DIM, RET_V_DIM), _F32)],
        compiler_params=pltpu.CompilerParams(
            dimension_semantics=("arbitrary", "arbitrary"), vmem_limit_bytes=VMEM_LIMIT),
        name="retention",
    )(ret_proj, ret_proj, ret_proj, ret_proj, cos_tab, sin_tab, gn_g)


def _head_rmsnorm(v, gain):
    r = lax.rsqrt(jnp.mean(v * v, axis=-1, keepdims=True) + NORM_EPS)
    return (v * r) * gain


def _swa_kernel(sinks_ref, q_ref, kc_ref, vc_ref, kp_ref, vp_ref, qg_ref, kg_ref, out_ref):
    w = SWA_WINDOW
    d = SWA_HEAD_DIM
    first = pl.program_id(1) == 0
    rows_q = SWA_GROUP * w
    qi = lax.broadcasted_iota(jnp.int32, (rows_q, 2 * w), 0) % w
    kj = lax.broadcasted_iota(jnp.int32, (rows_q, 2 * w), 1)
    band = (kj > qi) & (kj <= qi + w)
    band_first = band & jnp.logical_or(kj >= w, jnp.logical_not(first))
    q_gain = qg_ref[...]
    k_gain = kg_ref[...]
    for kk in range(SWA_KV_HEADS):
        ks = slice(kk * d, (kk + 1) * d)
        k_all = jnp.concatenate([kp_ref[:, ks], kc_ref[:, ks]], axis=0).astype(_F32)
        k_all = _head_rmsnorm(k_all, k_gain).astype(_BF16)
        v_all = jnp.concatenate([vp_ref[:, ks], vc_ref[:, ks]], axis=0)
        sink_col = jnp.concatenate(
            [jnp.full((w, 1), sinks_ref[kk * SWA_GROUP + g], _F32) for g in range(SWA_GROUP)], axis=0)
        for j in range(SEQ_TILE // w):
            rows = slice(j * w, (j + 1) * w)
            k_ext = k_all[j * w:(j + 2) * w]
            v_ext = v_all[j * w:(j + 2) * w]
            q4 = jnp.concatenate(
                [q_ref[rows, (kk * SWA_GROUP + g) * d:(kk * SWA_GROUP + g + 1) * d] for g in range(SWA_GROUP)],
                axis=0).astype(_F32)
            q4 = _head_rmsnorm(q4, q_gain).astype(_BF16)
            s = _dot_nt(q4, k_ext) * (d ** -0.5)
            s = jnp.where(band_first if j == 0 else band, s, NEG_BIG)
            m = jnp.maximum(jnp.max(s, axis=-1, keepdims=True), sink_col)
            p = jnp.exp(s - m)
            denom = jnp.sum(p, axis=-1, keepdims=True) + jnp.exp(sink_col - m)
            p = (p * (1.0 / denom)).astype(_BF16)
            o = _dot(p, v_ext).astype(_BF16)
            for g in range(SWA_GROUP):
                hq = kk * SWA_GROUP + g
                out_ref[rows, hq * d:(hq + 1) * d] = o[g * w:(g + 1) * w]


def _swa(swa_proj, sinks, q_gain, k_gain, batch, seq):
    n = swa_proj.shape[0]
    steps = seq // SEQ_TILE
    blocks_per_step = SEQ_TILE // SWA_WINDOW
    rowmap = lambda b, s: b * steps + s
    prevmap = lambda b, s: jnp.maximum((b * steps + s) * blocks_per_step - 1, 0)
    k_col = SWA_Q_W // SWA_KV_W
    return pl.pallas_call(
        _swa_kernel,
        grid=(batch, steps),
        in_specs=[
            pl.BlockSpec(memory_space=pltpu.SMEM),
            pl.BlockSpec((SEQ_TILE, SWA_Q_W), lambda b, s: (rowmap(b, s), 0)),
            pl.BlockSpec((SEQ_TILE, SWA_KV_W), lambda b, s: (rowmap(b, s), k_col)),
            pl.BlockSpec((SEQ_TILE, SWA_KV_W), lambda b, s: (rowmap(b, s), k_col + 1)),
            pl.BlockSpec((SWA_WINDOW, SWA_KV_W), lambda b, s: (prevmap(b, s), k_col)),
            pl.BlockSpec((SWA_WINDOW, SWA_KV_W), lambda b, s: (prevmap(b, s), k_col + 1)),
            pl.BlockSpec((1, SWA_HEAD_DIM), lambda b, s: (0, 0)),
            pl.BlockSpec((1, SWA_HEAD_DIM), lambda b, s: (0, 0)),
        ],
        out_specs=pl.BlockSpec((SEQ_TILE, SWA_Q_W), lambda b, s: (rowmap(b, s), 0)),
        out_shape=jax.ShapeDtypeStruct((n, SWA_Q_W), _BF16),
        compiler_params=pltpu.CompilerParams(
            dimension_semantics=("arbitrary", "arbitrary"), vmem_limit_bytes=VMEM_LIMIT),
        name="swa",
    )(sinks, swa_proj, swa_proj, swa_proj, swa_proj, swa_proj, q_gain, k_gain)


def _store_token_major(ref, v):
    for c in range(v.shape[1] // LANES):
        ref[:, c, :] = v[:, c * LANES:(c + 1) * LANES]


def _load_token_major(ref):
    return jnp.concatenate([ref[:, c, :] for c in range(ref.shape[1])], axis=1)


def _outproj_kernel(x_ref, retg_ref, attn_ref, ga_ref, gb_ref, wro_ref, wso_ref, wout_ref, g2_ref,
                    wr_ref, br_ref, x1_ref, h2_ref, info_ref, counts_ref):
    tm = ROW_TILE

    @pl.when(pl.program_id(0) == 0)
    def _():
        counts_ref[...] = jnp.zeros_like(counts_ref)

    y_a = _dot(retg_ref[...], wro_ref[...])
    y_b = _dot(attn_ref[...], wso_ref[...])
    merged = _sigmoid(ga_ref[...].astype(_F32)) * y_a + _sigmoid(gb_ref[...].astype(_F32)) * y_b
    x1 = x_ref[...] + _dot(merged.astype(_BF16), wout_ref[...])
    x1_ref[...] = x1
    r = lax.rsqrt(jnp.mean(x1 * x1, axis=-1, keepdims=True) + NORM_EPS)
    h2 = (x1 * r) * g2_ref[...]
    _store_token_major(h2_ref, h2)

    logits = _dot(h2.astype(_BF16), wr_ref[...]) + br_ref[...]
    lane = lax.broadcasted_iota(jnp.int32, (tm, LANES), 1)
    lane_f = lane.astype(_F32)
    is_group = (lane >= ROUTER_GROUP_LANE) & (lane < ROUTER_GROUP_LANE + MOE_GROUPS)
    gl = jnp.where(is_group, logits, NEG_BIG)
    g_max = jnp.max(gl, axis=-1, keepdims=True)
    g_prob = 1.0 / jnp.sum(jnp.exp(gl - g_max), axis=-1, keepdims=True)
    g_idx = jnp.min(jnp.where(gl == g_max, lane_f - ROUTER_GROUP_LANE, float(LANES)), axis=-1, keepdims=True)
    e_lo = g_idx * MOE_EXPERTS_PER_GROUP
    in_group = (lane_f >= e_lo) & (lane_f < e_lo + MOE_EXPERTS_PER_GROUP)
    el = jnp.where(in_group, logits, NEG_BIG)
    t1 = jnp.max(el, axis=-1, keepdims=True)
    i1 = jnp.min(jnp.where(el == t1, lane_f, float(LANES)), axis=-1, keepdims=True)
    el2 = jnp.where(lane_f == i1, NEG_BIG, el)
    t2 = jnp.max(el2, axis=-1, keepdims=True)
    i2 = jnp.min(jnp.where(el2 == t2, lane_f, float(LANES)), axis=-1, keepdims=True)
    e21 = jnp.exp(t2 - t1)
    w1 = g_prob / (1.0 + e21)
    w2 = g_prob * e21 / (1.0 + e21)

    sel1 = lane_f == i1
    sel2 = lane_f == i2
    onehot = jnp.where(sel1 | sel2, 1.0, 0.0)
    tri = (lax.broadcasted_iota(jnp.int32, (tm, tm), 0) > lax.broadcasted_iota(jnp.int32, (tm, tm), 1))
    before = _dot(tri.astype(_BF16), onehot.astype(_BF16)) + counts_ref[0:1, :]
    rank1 = jnp.sum(jnp.where(sel1, before, 0.0), axis=-1, keepdims=True)
    rank2 = jnp.sum(jnp.where(sel2, before, 0.0), axis=-1, keepdims=True)
    counts_ref[0:1, :] = counts_ref[0:1, :] + jnp.sum(onehot, axis=0, keepdims=True)

    info = jnp.where(lane == 0, i1, 0.0)
    info = jnp.where(lane == 1, i2, info)
    info = jnp.where(lane == 2, w1, info)
    info = jnp.where(lane == 3, w2, info)
    info = jnp.where(lane == 4, rank1, info)
    info = jnp.where(lane == 5, rank2, info)
    info_ref[...] = info


def _outproj(x2d, retg, attn, gates, w_ret_o, w_swa_o, w_out, g2, w_router, b_router):
    n = x2d.shape[0]
    tm = ROW_TILE
    const = lambda i: (0, 0)
    return pl.pallas_call(
        _outproj_kernel,
        grid=(n // tm,),
        in_specs=[
            pl.BlockSpec((tm, D_MODEL), lambda i: (i, 0)),
            pl.BlockSpec((tm, RET_V_W), lambda i: (i, 0)),
            pl.BlockSpec((tm, SWA_Q_W), lambda i: (i, 0)),
            pl.BlockSpec((tm, D_MODEL), lambda i: (i, 0)),
            pl.BlockSpec((tm, D_MODEL), lambda i: (i, 1)),
            pl.BlockSpec((RET_V_W, D_MODEL), const),
            pl.BlockSpec((SWA_Q_W, D_MODEL), const),
            pl.BlockSpec((D_MODEL, D_MODEL), const),
            pl.BlockSpec((1, D_MODEL), const),
            pl.BlockSpec((D_MODEL, LANES), const),
            pl.BlockSpec((1, LANES), const),
        ],
        out_specs=[
            pl.BlockSpec((tm, D_MODEL), lambda i: (i, 0)),
            pl.BlockSpec((tm, ROW_CHUNKS, LANES), lambda i: (i, 0, 0)),
            pl.BlockSpec((tm, LANES), lambda i: (i, 0)),
            pl.BlockSpec((8, LANES), const),
        ],
        out_shape=[
            jax.ShapeDtypeStruct((n, D_MODEL), _F32),
            jax.ShapeDtypeStruct((n, ROW_CHUNKS, LANES), _F32),
            jax.ShapeDtypeStruct((n, LANES), _F32),
            jax.ShapeDtypeStruct((8, LANES), _F32),
        ],
        compiler_params=pltpu.CompilerParams(
            dimension_semantics=("arbitrary",), vmem_limit_bytes=VMEM_LIMIT),
        name="outproj",
    )(x2d, retg, attn, gates, gates, w_ret_o, w_swa_o, w_out, g2, w_router, b_router)


def _zero_fill_padding(pad_start_ref, pad_len_ref, zeros_ref, xs_hbm, sem, wait):
    def per_expert(e, carry):
        pos = pad_start_ref[e]
        length = pad_len_ref[e]
        bit = EXPERT_TILE // 2
        while bit >= 1:
            copy = pltpu.make_async_copy(zeros_ref.at[pl.ds(0, bit)], xs_hbm.at[pl.ds(pos, bit)], sem)
            pl.when((length & bit) != 0)(copy.wait if wait else copy.start)
            pos = pos + (length & bit)
            bit //= 2
        return carry

    lax.fori_loop(0, MOE_N_EXPERTS, per_expert, 0)

    tail_start = pad_start_ref[MOE_N_EXPERTS]
    n_rows = xs_hbm.shape[0]
    for j in range(MOE_N_EXPERTS):
        pos = tail_start + j * EXPERT_TILE
        for half in range(2):
            off = jnp.minimum(pos, n_rows - EXPERT_TILE) + half * (EXPERT_TILE // 2)
            copy = pltpu.make_async_copy(zeros_ref, xs_hbm.at[pl.ds(off, EXPERT_TILE // 2)], sem)
            pl.when(pos < n_rows)(copy.wait if wait else copy.start)


def _dispatch_kernel(pad_start_ref, pad_len_ref, dest_ref, h2_ref, xs_hbm, zeros_ref, sem, zero_sem):
    @pl.when(pl.program_id(0) == 0)
    def _():
        zeros_ref[...] = jnp.zeros_like(zeros_ref)
        _zero_fill_padding(pad_start_ref, pad_len_ref, zeros_ref, xs_hbm, zero_sem, wait=False)
        _zero_fill_padding(pad_start_ref, pad_len_ref, zeros_ref, xs_hbm, zero_sem, wait=True)

    def row_copy(t, k):
        return pltpu.make_async_copy(h2_ref.at[t], xs_hbm.at[dest_ref[2 * t + k]], sem)

    def issue(t, carry):
        for k in range(2):
            row_copy(t, k).start()
        return carry

    lax.fori_loop(0, DISPATCH_TILE, issue, 0, unroll=8)

    def drain(t, carry):
        for k in range(2):
            row_copy(0, k).wait()
        return carry

    lax.fori_loop(0, DISPATCH_TILE, drain, 0, unroll=8)


def _dispatch(pad_start, pad_len, dest_flat, h2, n_rows):
    n = h2.shape[0]
    grid_spec = pltpu.PrefetchScalarGridSpec(
        num_scalar_prefetch=2,
        grid=(n // DISPATCH_TILE,),
        in_specs=[
            pl.BlockSpec((2 * DISPATCH_TILE,), lambda i, ps, pn: (i,), memory_space=pltpu.SMEM),
            pl.BlockSpec((DISPATCH_TILE, ROW_CHUNKS, LANES), lambda i, ps, pn: (i, 0, 0)),
        ],
        out_specs=pl.BlockSpec(memory_space=pl.ANY),
        scratch_shapes=[
            pltpu.VMEM((EXPERT_TILE // 2, ROW_CHUNKS, LANES), h2.dtype),
            pltpu.SemaphoreType.DMA(()),
            pltpu.SemaphoreType.DMA(()),
        ],
    )
    return pl.pallas_call(
        _dispatch_kernel,
        grid_spec=grid_spec,
        out_shape=jax.ShapeDtypeStruct((n_rows, ROW_CHUNKS, LANES), h2.dtype),
        compiler_params=pltpu.CompilerParams(dimension_semantics=("arbitrary",)),
        name="dispatch",
    )(pad_start, pad_len, dest_flat, h2)


def _experts_kernel(te_ref, tv_ref, xs_ref, wg_ref, wu_ref, wd_ref, ys_ref, wg_s, wu_s, wd_s):
    i = pl.program_id(0)
    valid = tv_ref[i]
    prev = te_ref[jnp.maximum(i - 1, 0)]
    changed = jnp.logical_or(i == 0, te_ref[i] != prev)

    @pl.when(jnp.logical_and(valid > 0, changed))
    def _():
        wg_s[...] = wg_ref[...].astype(_BF16)
        wu_s[...] = wu_ref[...].astype(_BF16)
        wd_s[...] = wd_ref[...].astype(_BF16)

    @pl.when(valid > 0)
    def _():
        xb = _load_token_major(xs_ref).astype(_BF16)
        gate = _dot(xb, wg_s[...])
        up = _dot(xb, wu_s[...])
        act = ((gate * _sigmoid(gate)) * up).astype(_BF16)
        _store_token_major(ys_ref, _dot(act, wd_s[...]))

    @pl.when(valid <= 0)
    def _():
        ys_ref[...] = jnp.zeros_like(ys_ref)


def _experts(tile_expert, tile_valid, xs, w_gate, w_up, w_down):
    n_rows = xs.shape[0]
    n_tiles = n_rows // EXPERT_TILE
    grid_spec = pltpu.PrefetchScalarGridSpec(
        num_scalar_prefetch=2,
        grid=(n_tiles,),
        in_specs=[
            pl.BlockSpec((EXPERT_TILE, ROW_CHUNKS, LANES), lambda i, te, tv: (i, 0, 0)),
            pl.BlockSpec((None, D_MODEL, MOE_D_FF), lambda i, te, tv: (te[i], 0, 0)),
            pl.BlockSpec((None, D_MODEL, MOE_D_FF), lambda i, te, tv: (te[i], 0, 0)),
            pl.BlockSpec((None, MOE_D_FF, D_MODEL), lambda i, te, tv: (te[i], 0, 0)),
        ],
        out_specs=pl.BlockSpec((EXPERT_TILE, ROW_CHUNKS, LANES), lambda i, te, tv: (i, 0, 0)),
        scratch_shapes=[
            pltpu.VMEM((D_MODEL, MOE_D_FF), _BF16),
            pltpu.VMEM((D_MODEL, MOE_D_FF), _BF16),
            pltpu.VMEM((MOE_D_FF, D_MODEL), _BF16),
        ],
    )
    return pl.pallas_call(
        _experts_kernel,
        grid_spec=grid_spec,
        out_shape=jax.ShapeDtypeStruct((n_rows, ROW_CHUNKS, LANES), _F32),
        compiler_params=pltpu.CompilerParams(
            dimension_semantics=("arbitrary",), vmem_limit_bytes=VMEM_LIMIT),
        name="experts",
    )(tile_expert, tile_valid, xs, w_gate, w_up, w_down)


def _combine_kernel(dest_ref, dest_next_ref, x1_ref, info_ref, ys_hbm, out_ref, buf, sems):
    tc = COMBINE_TILE
    i = pl.program_id(0)
    n_steps = pl.num_programs(0)
    slot = i % 2

    def gather(dref, slot_idx, wait):
        def row_copy(t, k):
            return pltpu.make_async_copy(ys_hbm.at[dref[2 * t + k]], buf.at[slot_idx, k, t], sems.at[slot_idx])

        def body(t, carry):
            for k in range(2):
                if wait:
                    row_copy(0, k).wait()
                else:
                    row_copy(t, k).start()
            return carry

        lax.fori_loop(0, tc, body, 0, unroll=8)

    @pl.when(i == 0)
    def _():
        gather(dest_ref, 0, wait=False)

    @pl.when(i + 1 < n_steps)
    def _():
        gather(dest_next_ref, 1 - slot, wait=False)

    gather(dest_ref, slot, wait=True)
    info = info_ref[...]
    w0 = info[:, 2:3]
    w1 = info[:, 3:4]
    for c in range(ROW_CHUNKS):
        cols = slice(c * LANES, (c + 1) * LANES)
        out_ref[:, cols] = x1_ref[:, cols] + w0 * buf[slot, 0, :, c, :] + w1 * buf[slot, 1, :, c, :]


def _combine(dest_flat, x1, info, ys):
    n = x1.shape[0]
    tc = COMBINE_TILE
    steps = n // tc
    return pl.pallas_call(
        _combine_kernel,
        grid=(steps,),
        in_specs=[
            pl.BlockSpec((2 * tc,), lambda i: (i,), memory_space=pltpu.SMEM),
            pl.BlockSpec((2 * tc,), lambda i: (jnp.minimum(i + 1, steps - 1),), memory_space=pltpu.SMEM),
            pl.BlockSpec((tc, D_MODEL), lambda i: (i, 0)),
            pl.BlockSpec((tc, LANES), lambda i: (i, 0)),
            pl.BlockSpec(memory_space=pl.ANY),
        ],
        out_specs=pl.BlockSpec((tc, D_MODEL), lambda i: (i, 0)),
        out_shape=jax.ShapeDtypeStruct((n, D_MODEL), _F32),
        scratch_shapes=[pltpu.VMEM((2, 2, tc, ROW_CHUNKS, LANES), _F32), pltpu.SemaphoreType.DMA((2,))],
        compiler_params=pltpu.CompilerParams(
            dimension_semantics=("arbitrary",), vmem_limit_bytes=VMEM_LIMIT),
        name="combine",
    )(dest_flat, dest_flat, x1, info, ys)


def _rotary_tables(seq):
    half = RET_QK_DIM // 2
    inv = ROPE_BASE ** (-jnp.arange(half, dtype=_F32) / half)
    ang = jnp.arange(seq, dtype=jnp.int32).astype(_F32)[:, None] * inv[None, :]
    cos = jnp.cos(ang)
    sin = jnp.sin(ang)
    return jnp.concatenate([cos, cos], axis=1), jnp.concatenate([-sin, sin], axis=1)


def _routing_plan(info, counts_out):
    te = EXPERT_TILE
    n = info.shape[0]
    counts = counts_out[0, :MOE_N_EXPERTS].astype(jnp.int32)
    padded = (counts + te - 1) // te * te
    pends = jnp.cumsum(padded)
    pstarts = pends - padded
    eid = info[:, 0:2].astype(jnp.int32)
    rank = info[:, 4:6].astype(jnp.int32)
    onehot = eid[:, :, None] == jnp.arange(MOE_N_EXPERTS, dtype=jnp.int32)[None, None, :]
    dest = jnp.sum(jnp.where(onehot, pstarts[None, None, :], 0), axis=-1) + rank
    n_tiles = (2 * n + MOE_N_EXPERTS * te) // te
    tile_start = jnp.arange(n_tiles, dtype=jnp.int32) * te
    tile_expert = jnp.sum((tile_start[:, None] >= pends[None, :]).astype(jnp.int32), axis=1)
    tile_expert = jnp.minimum(tile_expert, MOE_N_EXPERTS - 1)
    tile_valid = jnp.clip(counts[tile_expert] - (tile_start - pstarts[tile_expert]), 0, te).astype(jnp.int32)
    pad_start = jnp.concatenate([pstarts + counts, pends[-1:]]).astype(jnp.int32)
    pad_len = (padded - counts).astype(jnp.int32)
    return dest.reshape(-1), tile_expert, tile_valid, pad_start, pad_len, n_tiles * te


def kernel(x, norm_mix_g, w_in, ret_gn_g, w_ret_o, q_norm_g, k_norm_g, sinks, w_swa_o, w_out, norm_ffn_g,
           w_router_group, b_router_group, w_router_expert, b_router_expert, w_gate, w_up, w_down):
    batch, seq, d = x.shape
    n = batch * seq
    assert d == D_MODEL and seq % SEQ_TILE == 0 and n % ROW_TILE == 0
    cos_tab, sin_tab = _rotary_tables(seq)
    for l in range(w_in.shape[0]):
        x2d = x.reshape(n, d)
        ret_proj, swa_proj, gates = _inproj(x2d, norm_mix_g[l][None, :], w_in[l].astype(_BF16))
        retg = _retention(ret_proj, cos_tab, sin_tab, ret_gn_g[l], batch, seq)
        attn = _swa(swa_proj, sinks[l], q_norm_g[l][None, :], k_norm_g[l][None, :], batch, seq)
        pad = LANES - MOE_N_EXPERTS - MOE_GROUPS
        w_router = jnp.concatenate(
            [w_router_expert[l], w_router_group[l], jnp.zeros((d, pad), _F32)], axis=1).astype(_BF16)
        b_router = jnp.concatenate([b_router_expert[l], b_router_group[l], jnp.zeros((pad,), _F32)])[None, :]
        x1, h2, info, counts = _outproj(
            x2d, retg, attn, gates, w_ret_o[l].astype(_BF16), w_swa_o[l].astype(_BF16), w_out[l].astype(_BF16),
            norm_ffn_g[l][None, :], w_router, b_router)
        dest, tile_expert, tile_valid, pad_start, pad_len, n_rows = _routing_plan(info, counts)
        xs = _dispatch(pad_start, pad_len, dest, h2, n_rows)
        ys = _experts(tile_expert, tile_valid, xs, w_gate[l], w_up[l], w_down[l])
        x = _combine(dest, x1, info, ys).reshape(batch, seq, d)
    return x
```

```python
import math

import numpy as np
import jax
import jax.numpy as jnp
from jax import lax
from jax.experimental import pallas as pl
from jax.experimental.pallas import tpu as pltpu

D_MODEL = 1024
RET_HEADS = 4
RET_QK_DIM = 128
RET_V_DIM = 256
RET_CHUNK = 128
ROPE_BASE = 10000.0
SWA_Q_HEADS = 8
SWA_KV_HEADS = 2
SWA_HEAD_DIM = 64
SWA_WINDOW = 128
MOE_GROUPS = 4
MOE_EXPERTS_PER_GROUP = 8
MOE_D_FF = 512
NORM_EPS = 1e-6

RET_QK_W = RET_HEADS * RET_QK_DIM
RET_V_W = RET_HEADS * RET_V_DIM
SWA_Q_W = SWA_Q_HEADS * SWA_HEAD_DIM
SWA_KV_W = SWA_KV_HEADS * SWA_HEAD_DIM
SWA_GROUP = SWA_Q_HEADS // SWA_KV_HEADS
MOE_N_EXPERTS = MOE_GROUPS * MOE_EXPERTS_PER_GROUP

RET_COLS = 2 * RET_QK_W + 2 * RET_V_W
SWA_COLS = SWA_Q_W + 2 * SWA_KV_W
GATE_COLS = 2 * D_MODEL
IN_WIDTH = RET_COLS + SWA_COLS + GATE_COLS

LANES = 128
SUBLANES = 8
ROUTER_GROUP_LANE = 32
NEG_BIG = -1e30

ROW_TILE = 512
SEQ_TILE = 512
EXPERT_TILE = 512
SORT_ROWS = 2 * ROW_TILE + MOE_N_EXPERTS * SUBLANES
SPLIT = 32.0
VMEM_LIMIT = 56 * 1024 * 1024

RET_LOG_DECAY = tuple(float(np.log1p(-np.exp2(-5.0 - h))) for h in range(RET_HEADS))

_BF16 = jnp.bfloat16
_F32 = jnp.float32


def _dot(a, b):
    return jnp.dot(a, b, preferred_element_type=_F32)


def _dot_nt(a, b):
    return lax.dot_general(a, b, (((1,), (1,)), ((), ())), preferred_element_type=_F32)


def _sigmoid(v):
    return 1.0 / (1.0 + jnp.exp(-v))


def _split_bf16(v):
    hi = v.astype(_BF16)
    return hi, (v - hi.astype(_F32)).astype(_BF16)


def _iota(shape, dim):
    return lax.broadcasted_iota(jnp.int32, shape, dim)


def _inproj_kernel(x_ref, g_ref, w_ref, ret_ref, swa_ref, gate_ref):
    x = x_ref[...]
    r = lax.rsqrt(jnp.mean(x * x, axis=-1, keepdims=True) + NORM_EPS)
    h = ((x * r) * g_ref[...]).astype(_BF16)
    base = 0
    for out_ref, width, step in ((ret_ref, RET_COLS, 512), (swa_ref, SWA_COLS, 256), (gate_ref, GATE_COLS, 512)):
        for c in range(0, width, step):
            out_ref[:, c:c + step] = _dot(h, w_ref[:, base + c:base + c + step]).astype(_BF16)
        base += width


def _inproj(x2d, g, w_in_bf16):
    n = x2d.shape[0]
    grid = (n // ROW_TILE,)
    return pl.pallas_call(
        _inproj_kernel,
        grid=grid,
        in_specs=[
            pl.BlockSpec((ROW_TILE, D_MODEL), lambda i: (i, 0)),
            pl.BlockSpec((1, D_MODEL), lambda i: (0, 0)),
            pl.BlockSpec((D_MODEL, IN_WIDTH), lambda i: (0, 0)),
        ],
        out_specs=[
            pl.BlockSpec((ROW_TILE, RET_COLS), lambda i: (i, 0)),
            pl.BlockSpec((ROW_TILE, SWA_COLS), lambda i: (i, 0)),
            pl.BlockSpec((ROW_TILE, GATE_COLS), lambda i: (i, 0)),
        ],
        out_shape=[
            jax.ShapeDtypeStruct((n, RET_COLS), _BF16),
            jax.ShapeDtypeStruct((n, SWA_COLS), _BF16),
            jax.ShapeDtypeStruct((n, GATE_COLS), _BF16),
        ],
        compiler_params=pltpu.CompilerParams(
            dimension_semantics=("arbitrary",), vmem_limit_bytes=VMEM_LIMIT),
        name="inproj",
    )(x2d, g, w_in_bf16)


def _retention_kernel(q_ref, k_ref, v_ref, g_ref, cos_ref, sin_ref, gn_ref, out_ref, state_ref):
    c = RET_CHUNK

    @pl.when(pl.program_id(1) == 0)
    def _():
        state_ref[...] = jnp.zeros_like(state_ref)

    diff = (_iota((c, c), 0) - _iota((c, c), 1)).astype(_F32)
    pos = _iota((c, 1), 0).astype(_F32)
    scale = RET_QK_DIM ** -0.5

    for h in range(RET_HEADS):
        lg = RET_LOG_DECAY[h]
        decay = jnp.where(diff >= 0, jnp.exp(lg * jnp.maximum(diff, 0.0)), 0.0) * scale
        q_decay = jnp.exp(lg * (pos + 1.0))
        k_decay = jnp.exp(lg * (c - 1.0 - pos)) * scale
        chunk_decay = math.exp(lg * c)
        qs = slice(h * RET_QK_DIM, (h + 1) * RET_QK_DIM)
        vs = slice(h * RET_V_DIM, (h + 1) * RET_V_DIM)
        gain = gn_ref[h:h + 1, :]
        for ci in range(SEQ_TILE // c):
            rows = slice(ci * c, (ci + 1) * c)
            cos = cos_ref[rows, :]
            sin = sin_ref[rows, :]
            q_raw = q_ref[rows, qs].astype(_F32)
            k_raw = k_ref[rows, qs].astype(_F32)
            q = q_raw * cos + pltpu.roll(q_raw, RET_QK_DIM // 2, 1) * sin
            k = k_raw * cos + pltpu.roll(k_raw, RET_QK_DIM // 2, 1) * sin
            v = v_ref[rows, vs]
            state = state_ref[h]
            scores = _dot_nt(q.astype(_BF16), k.astype(_BF16)) * decay
            ret = _dot(scores.astype(_BF16), v) + _dot((q * q_decay).astype(_BF16), state.astype(_BF16))
            kv = _dot((k * k_decay).T.astype(_BF16), v)
            state_ref[h] = chunk_decay * state + kv
            mu = jnp.mean(ret, axis=-1, keepdims=True)
            dev = ret - mu
            var = jnp.mean(dev * dev, axis=-1, keepdims=True)
            normed = (dev * lax.rsqrt(var + NORM_EPS)) * gain
            gate = g_ref[rows, vs].astype(_F32)
            out_ref[rows, vs] = ((gate * _sigmoid(gate)) * normed).astype(_BF16)


def _retention(ret_proj, cos_tab, sin_tab, gn_g, batch, seq):
    n = ret_proj.shape[0]
    steps = seq // SEQ_TILE
    rowmap = lambda b, s: b * steps + s
    return pl.pallas_call(
        _retention_kernel,
        grid=(batch, steps),
        in_specs=[
            pl.BlockSpec((SEQ_TILE, RET_QK_W), lambda b, s: (rowmap(b, s), 0)),
            pl.BlockSpec((SEQ_TILE, RET_QK_W), lambda b, s: (rowmap(b, s), 1)),
            pl.BlockSpec((SEQ_TILE, RET_V_W), lambda b, s: (rowmap(b, s), 1)),
            pl.BlockSpec((SEQ_TILE, RET_V_W), lambda b, s: (rowmap(b, s), 2)),
            pl.BlockSpec((SEQ_TILE, RET_QK_DIM), lambda b, s: (s, 0)),
            pl.BlockSpec((SEQ_TILE, RET_QK_DIM), lambda b, s: (s, 0)),
            pl.BlockSpec((RET_HEADS, RET_V_DIM), lambda b, s: (0, 0)),
        ],
        out_specs=pl.BlockSpec((SEQ_TILE, RET_V_W), lambda b, s: (rowmap(b, s), 0)),
        out_shape=jax.ShapeDtypeStruct((n, RET_V_W), _BF16),
        scratch_shapes=[pltpu.VMEM((RET_HEADS, RET_QK_DIM, RET_V_DIM), _F32)],
        compiler_params=pltpu.CompilerParams(
            dimension_semantics=("arbitrary", "arbitrary"), vmem_limit_bytes=VMEM_LIMIT),
        name="retention",
    )(ret_proj, ret_proj, ret_proj, ret_proj, cos_tab, sin_tab, gn_g)


def _swa_kernel(sinks_ref, q_ref, kc_ref, vc_ref, kp_ref, vp_ref, qg_ref, kg_ref, out_ref):
    w = SWA_WINDOW
    d = SWA_HEAD_DIM
    n_blk = SEQ_TILE // w
    pair = 2 * d
    first = pl.program_id(1) == 0

    k_both = jnp.concatenate([kp_ref[...], kc_ref[...]], axis=0).astype(_F32)
    same_half = (_iota((pair, pair), 0) // d == _iota((pair, pair), 1) // d).astype(_BF16)
    k_hi, k_lo = _split_bf16(k_both * k_both)
    k_ssq = _dot(k_hi, same_half) + _dot(k_lo, same_half)
    k_n = k_both * lax.rsqrt(k_ssq * (1.0 / d) + NORM_EPS) * (kg_ref[...] * d ** -0.5)
    v_both = jnp.concatenate([vp_ref[...], vc_ref[...]], axis=0).astype(_F32)
    v_t = [v_both[b * w:(b + 1) * w].T for b in range(n_blk + 1)]

    key = _iota((2 * w, SWA_GROUP * w), 0)
    qry = _iota((2 * w, SWA_GROUP * w), 1) % w
    band = (key > qry) & (key <= qry + w)
    band_first = band & jnp.logical_or(key >= w, jnp.logical_not(first))
    head_of_lane = _iota((1, SWA_GROUP * w), 1) // w
    half_of_lane = _iota((1, pair), 1) // d
    half_rows = (_iota((SUBLANES, pair), 0) == _iota((SUBLANES, pair), 1) // d).astype(_BF16)
    ones_rows = jnp.ones((2 * SUBLANES, 2 * w), _BF16)
    q_gain = qg_ref[...]

    for kk in range(SWA_KV_HEADS):
        k_native = jnp.where(half_of_lane == kk, k_n, 0.0)
        k_moved = pltpu.roll(k_native, d, 1)
        k_even, k_odd = (k_native, k_moved) if kk == 0 else (k_moved, k_native)
        k_even = k_even.astype(_BF16)
        k_odd = k_odd.astype(_BF16)
        sink_row = jnp.zeros((1, SWA_GROUP * w), _F32)
        for g in range(SWA_GROUP):
            sink_row = jnp.where(head_of_lane == g, sinks_ref[kk * SWA_GROUP + g], sink_row)
        for j in range(n_blk):
            rows = slice(j * w, (j + 1) * w)
            keys = slice(j * w, (j + 2) * w)
            raws = []
            for p in range(SWA_GROUP // 2):
                c0 = (kk * SWA_GROUP + 2 * p) * d
                q_pair = q_ref[rows, c0:c0 + pair].astype(_F32)
                q_hi, q_lo = _split_bf16(q_pair * q_pair)
                q_ssq = _dot_nt(half_rows, q_hi) + _dot_nt(half_rows, q_lo)
                q_r = lax.rsqrt(q_ssq * (1.0 / d) + NORM_EPS)
                q_g = (q_pair * q_gain).astype(_BF16)
                raws.append(_dot_nt(k_even[keys], q_g) * q_r[0:1, :])
                raws.append(_dot_nt(k_odd[keys], q_g) * q_r[1:2, :])
            s_t = jnp.concatenate(raws, axis=1)
            s_t = jnp.where(band_first if j == 0 else band, s_t, NEG_BIG)
            m = jnp.maximum(jnp.max(s_t, axis=0, keepdims=True), sink_row)
            p_t = jnp.exp(s_t - m).astype(_BF16)
            v_ext = jnp.concatenate([v_t[j][kk * d:(kk + 1) * d], v_t[j + 1][kk * d:(kk + 1) * d]], axis=1)
            o1 = _dot(jnp.concatenate([v_ext.astype(_BF16), ones_rows], axis=0), p_t)
            denom = o1[d:d + 1, :] + jnp.exp(sink_row - m)
            o_t = o1[0:d, :] * (1.0 / denom)
            for p in range(SWA_GROUP // 2):
                c0 = (kk * SWA_GROUP + 2 * p) * d
                both = jnp.concatenate([o_t[:, 2 * p * w:(2 * p + 1) * w], o_t[:, (2 * p + 1) * w:(2 * p + 2) * w]], axis=0)
                out_ref[rows, c0:c0 + pair] = both.T.astype(_BF16)


def _swa(swa_proj, sinks, q_gain2, k_gain2, batch, seq):
    n = swa_proj.shape[0]
    steps = seq // SEQ_TILE
    blocks_per_step = SEQ_TILE // SWA_WINDOW
    rowmap = lambda b, s: b * steps + s
    prevmap = lambda b, s: jnp.maximum((b * steps + s) * blocks_per_step - 1, 0)
    k_col = SWA_Q_W // SWA_KV_W
    return pl.pallas_call(
        _swa_kernel,
        grid=(batch, steps),
        in_specs=[
            pl.BlockSpec(memory_space=pltpu.SMEM),
            pl.BlockSpec((SEQ_TILE, SWA_Q_W), lambda b, s: (rowmap(b, s), 0)),
            pl.BlockSpec((SEQ_TILE, SWA_KV_W), lambda b, s: (rowmap(b, s), k_col)),
            pl.BlockSpec((SEQ_TILE, SWA_KV_W), lambda b, s: (rowmap(b, s), k_col + 1)),
            pl.BlockSpec((SWA_WINDOW, SWA_KV_W), lambda b, s: (prevmap(b, s), k_col)),
            pl.BlockSpec((SWA_WINDOW, SWA_KV_W), lambda b, s: (prevmap(b, s), k_col + 1)),
            pl.BlockSpec((1, 2 * SWA_HEAD_DIM), lambda b, s: (0, 0)),
            pl.BlockSpec((1, 2 * SWA_HEAD_DIM), lambda b, s: (0, 0)),
        ],
        out_specs=pl.BlockSpec((SEQ_TILE, SWA_Q_W), lambda b, s: (rowmap(b, s), 0)),
        out_shape=jax.ShapeDtypeStruct((n, SWA_Q_W), _BF16),
        compiler_params=pltpu.CompilerParams(
            dimension_semantics=("arbitrary", "arbitrary"), vmem_limit_bytes=VMEM_LIMIT),
        name="swa",
    )(sinks, swa_proj, swa_proj, swa_proj, swa_proj, swa_proj, q_gain2, k_gain2)


def _outproj_kernel(x_ref, retg_ref, attn_ref, ga_ref, gb_ref, wro_ref, wso_ref, wout_ref, g2_ref,
                    wr_ref, br_ref, x1_ref, h2_ref, info_ref, counts_ref):
    tm = ROW_TILE
    y_a = _dot(retg_ref[...], wro_ref[...])
    y_b = _dot(attn_ref[...], wso_ref[...])
    merged = _sigmoid(ga_ref[...].astype(_F32)) * y_a + _sigmoid(gb_ref[...].astype(_F32)) * y_b
    x1 = x_ref[...] + _dot(merged.astype(_BF16), wout_ref[...])
    x1_ref[...] = x1
    r = lax.rsqrt(jnp.mean(x1 * x1, axis=-1, keepdims=True) + NORM_EPS)
    h2 = ((x1 * r) * g2_ref[...]).astype(_BF16)
    h2_ref[...] = h2

    logits = _dot(h2, wr_ref[...]) + br_ref[...]
    lane = _iota((tm, LANES), 1)
    lane_f = lane.astype(_F32)
    is_group = (lane >= ROUTER_GROUP_LANE) & (lane < ROUTER_GROUP_LANE + MOE_GROUPS)
    gl = jnp.where(is_group, logits, NEG_BIG)
    g_max = jnp.max(gl, axis=-1, keepdims=True)
    g_prob = 1.0 / jnp.sum(jnp.exp(gl - g_max), axis=-1, keepdims=True)
    g_idx = jnp.min(jnp.where(gl == g_max, lane_f - ROUTER_GROUP_LANE, float(LANES)), axis=-1, keepdims=True)
    e_lo = g_idx * MOE_EXPERTS_PER_GROUP
    in_group = (lane_f >= e_lo) & (lane_f < e_lo + MOE_EXPERTS_PER_GROUP)
    el = jnp.where(in_group, logits, NEG_BIG)
    t1 = jnp.max(el, axis=-1, keepdims=True)
    i1 = jnp.min(jnp.where(el == t1, lane_f, float(LANES)), axis=-1, keepdims=True)
    el2 = jnp.where(lane_f == i1, NEG_BIG, el)
    t2 = jnp.max(el2, axis=-1, keepdims=True)
    i2 = jnp.min(jnp.where(el2 == t2, lane_f, float(LANES)), axis=-1, keepdims=True)
    e21 = jnp.exp(t2 - t1)
    w1 = g_prob / (1.0 + e21)
    w2 = g_prob * e21 / (1.0 + e21)

    sel1 = lane_f == i1
    sel2 = lane_f == i2
    onehot = jnp.where(sel1 | sel2, 1.0, 0.0)
    earlier = (_iota((tm, tm), 0) > _iota((tm, tm), 1)).astype(_BF16)
    before = _dot(earlier, onehot.astype(_BF16))
    counts = jnp.sum(onehot, axis=0, keepdims=True)
    run = jnp.floor((counts + (SUBLANES - 1.0)) * (1.0 / SUBLANES)) * SUBLANES
    lower_experts = (_iota((LANES, LANES), 0) < _iota((LANES, LANES), 1)).astype(_BF16)
    run_off = _dot(jnp.broadcast_to(run, (SUBLANES, LANES)).astype(_BF16), lower_experts)[0:1, :]
    place = before + run_off
    pos1 = jnp.sum(jnp.where(sel1, place, 0.0), axis=-1, keepdims=True)
    pos2 = jnp.sum(jnp.where(sel2, place, 0.0), axis=-1, keepdims=True)
    counts_ref[...] = jnp.broadcast_to(counts, counts_ref.shape)

    info = jnp.where(lane == 0, i1, 0.0)
    info = jnp.where(lane == 1, i2, info)
    info = jnp.where(lane == 2, w1, info)
    info = jnp.where(lane == 3, w2, info)
    info = jnp.where(lane == 4, pos1, info)
    info = jnp.where(lane == 5, pos2, info)
    info_ref[...] = info


def _outproj(x2d, retg, attn, gates, w_ret_o, w_swa_o, w_out, g2, w_router, b_router):
    n = x2d.shape[0]
    tm = ROW_TILE
    const = lambda i: (0, 0)
    return pl.pallas_call(
        _outproj_kernel,
        grid=(n // tm,),
        in_specs=[
            pl.BlockSpec((tm, D_MODEL), lambda i: (i, 0)),
            pl.BlockSpec((tm, RET_V_W), lambda i: (i, 0)),
            pl.BlockSpec((tm, SWA_Q_W), lambda i: (i, 0)),
            pl.BlockSpec((tm, D_MODEL), lambda i: (i, 0)),
            pl.BlockSpec((tm, D_MODEL), lambda i: (i, 1)),
            pl.BlockSpec((RET_V_W, D_MODEL), const),
            pl.BlockSpec((SWA_Q_W, D_MODEL), const),
            pl.BlockSpec((D_MODEL, D_MODEL), const),
            pl.BlockSpec((1, D_MODEL), const),
            pl.BlockSpec((D_MODEL, LANES), const),
            pl.BlockSpec((1, LANES), const),
        ],
        out_specs=[
            pl.BlockSpec((tm, D_MODEL), lambda i: (i, 0)),
            pl.BlockSpec((tm, D_MODEL), lambda i: (i, 0)),
            pl.BlockSpec((tm, LANES), lambda i: (i, 0)),
            pl.BlockSpec((SUBLANES, LANES), lambda i: (i, 0)),
        ],
        out_shape=[
            jax.ShapeDtypeStruct((n, D_MODEL), _F32),
            jax.ShapeDtypeStruct((n, D_MODEL), _BF16),
            jax.ShapeDtypeStruct((n, LANES), _F32),
            jax.ShapeDtypeStruct((n // tm * SUBLANES, LANES), _F32),
        ],
        compiler_params=pltpu.CompilerParams(
            dimension_semantics=("arbitrary",), vmem_limit_bytes=VMEM_LIMIT),
        name="outproj",
    )(x2d, retg, attn, gates, gates, w_ret_o, w_swa_o, w_out, g2, w_router, b_router)


def _run_pieces(length, largest):
    piece = largest
    while piece >= SUBLANES:
        yield piece
        piece //= 2


def _for_each_run_piece(off_ref, run_ref, dst_ref, tile, make_copy, wait):
    def per_expert(e, carry):
        idx = tile * MOE_N_EXPERTS + e
        off = off_ref[idx]
        dst = dst_ref[idx]
        run = run_ref[idx]
        for piece in _run_pieces(run, ROW_TILE):
            copy = make_copy(pl.multiple_of(off, SUBLANES), pl.multiple_of(dst, SUBLANES), piece)
            pl.when((run & piece) != 0)(copy.wait if wait else copy.start)
            off = off + (run & piece)
            dst = dst + (run & piece)
        return carry

    lax.fori_loop(0, MOE_N_EXPERTS, per_expert, 0)


def _zero_fill_padding(pad_start_ref, pad_len_ref, zeros_ref, xs_hbm, sem, wait):
    def per_expert(e, carry):
        pos = pad_start_ref[e]
        length = pad_len_ref[e]
        for piece in _run_pieces(length, EXPERT_TILE // 2):
            copy = pltpu.make_async_copy(
                zeros_ref.at[pl.ds(0, piece)], xs_hbm.at[pl.ds(pl.multiple_of(pos, SUBLANES), piece)], sem)
            pl.when((length & piece) != 0)(copy.wait if wait else copy.start)
            pos = pos + (length & piece)
        return carry

    lax.fori_loop(0, MOE_N_EXPERTS, per_expert, 0)

    tail_start = pad_start_ref[MOE_N_EXPERTS]
    n_rows = xs_hbm.shape[0]
    half = EXPERT_TILE // 2

    def per_half_tile(j, carry):
        pos = tail_start + j * half
        copy = pltpu.make_async_copy(
            zeros_ref, xs_hbm.at[pl.ds(pl.multiple_of(jnp.minimum(pos, n_rows - half), SUBLANES), half)], sem)
        pl.when(pos < n_rows)(copy.wait if wait else copy.start)
        return carry

    lax.fori_loop(0, n_rows // half, per_half_tile, 0)


def _dispatch_kernel(off_ref, run_ref, dst_ref, pad_start_ref, pad_len_ref, h2_ref, info_ref, xs_hbm,
                     sorted_ref, zeros_ref, sem, zero_sem):
    tile = pl.program_id(0)
    tm = ROW_TILE

    @pl.when(tile == 0)
    def _():
        zeros_ref[...] = jnp.zeros_like(zeros_ref)
        _zero_fill_padding(pad_start_ref, pad_len_ref, zeros_ref, xs_hbm, zero_sem, wait=False)
        _zero_fill_padding(pad_start_ref, pad_len_ref, zeros_ref, xs_hbm, zero_sem, wait=True)

    info = info_ref[...]
    hi = jnp.floor(info * (1.0 / SPLIT))
    lo = info - hi * SPLIT
    pick = ((_iota((SUBLANES, LANES), 1) == _iota((SUBLANES, LANES), 0) + 4)
            & (_iota((SUBLANES, LANES), 0) < 2)).astype(_BF16)
    pos = (_dot_nt(pick, hi.astype(_BF16)) * SPLIT + _dot_nt(pick, lo.astype(_BF16))).astype(jnp.int32)
    place = _iota((SORT_ROWS, tm), 0)
    onehot = ((place == pos[0:1, :]) | (place == pos[1:2, :])).astype(_BF16)
    for c in range(0, D_MODEL, 256):
        sorted_ref[:, c:c + 256] = _dot(onehot, h2_ref[:, c:c + 256])

    def make_copy(off, dst, piece):
        return pltpu.make_async_copy(sorted_ref.at[pl.ds(off, piece)], xs_hbm.at[pl.ds(dst, piece)], sem)

    _for_each_run_piece(off_ref, run_ref, dst_ref, tile, make_copy, wait=False)
    _for_each_run_piece(off_ref, run_ref, dst_ref, tile, make_copy, wait=True)


def _dispatch(plan, h2, info, n_rows):
    n = h2.shape[0]
    grid_spec = pltpu.PrefetchScalarGridSpec(
        num_scalar_prefetch=5,
        grid=(n // ROW_TILE,),
        in_specs=[
            pl.BlockSpec((ROW_TILE, D_MODEL), lambda i, *_: (i, 0)),
            pl.BlockSpec((ROW_TILE, LANES), lambda i, *_: (i, 0)),
        ],
        out_specs=pl.BlockSpec(memory_space=pl.ANY),
        scratch_shapes=[
            pltpu.VMEM((SORT_ROWS, D_MODEL), _F32),
            pltpu.VMEM((EXPERT_TILE // 2, D_MODEL), _F32),
            pltpu.SemaphoreType.DMA(()),
            pltpu.SemaphoreType.DMA(()),
        ],
    )
    return pl.pallas_call(
        _dispatch_kernel,
        grid_spec=grid_spec,
        out_shape=jax.ShapeDtypeStruct((n_rows, D_MODEL), _F32),
        compiler_params=pltpu.CompilerParams(
            dimension_semantics=("arbitrary",), vmem_limit_bytes=VMEM_LIMIT),
        name="dispatch",
    )(plan["off"], plan["run"], plan["dst"], plan["pad_start"], plan["pad_len"], h2, info)


def _experts_kernel(te_ref, tv_ref, xs_ref, wg_ref, wu_ref, wd_ref, ys_ref, wg_s, wu_s, wd_s):
    i = pl.program_id(0)
    valid = tv_ref[i]
    prev = te_ref[jnp.maximum(i - 1, 0)]
    changed = jnp.logical_or(i == 0, te_ref[i] != prev)

    @pl.when(jnp.logical_and(valid > 0, changed))
    def _():
        wg_s[...] = wg_ref[...].astype(_BF16)
        wu_s[...] = wu_ref[...].astype(_BF16)
        wd_s[...] = wd_ref[...].astype(_BF16)

    @pl.when(valid > 0)
    def _():
        xb = xs_ref[...].astype(_BF16)
        gate = _dot(xb, wg_s[...])
        up = _dot(xb, wu_s[...])
        act = ((gate * _sigmoid(gate)) * up).astype(_BF16)
        ys_ref[...] = _dot(act, wd_s[...])

    @pl.when(valid <= 0)
    def _():
        ys_ref[...] = jnp.zeros_like(ys_ref)


def _experts(plan, xs, w_gate, w_up, w_down):
    n_rows = xs.shape[0]
    n_tiles = n_rows // EXPERT_TILE
    grid_spec = pltpu.PrefetchScalarGridSpec(
        num_scalar_prefetch=2,
        grid=(n_tiles,),
        in_specs=[
            pl.BlockSpec((EXPERT_TILE, D_MODEL), lambda i, te, tv: (i, 0)),
            pl.BlockSpec((None, D_MODEL, MOE_D_FF), lambda i, te, tv: (te[i], 0, 0)),
            pl.BlockSpec((None, D_MODEL, MOE_D_FF), lambda i, te, tv: (te[i], 0, 0)),
            pl.BlockSpec((None, MOE_D_FF, D_MODEL), lambda i, te, tv: (te[i], 0, 0)),
        ],
        out_specs=pl.BlockSpec((EXPERT_TILE, D_MODEL), lambda i, te, tv: (i, 0)),
        scratch_shapes=[
            pltpu.VMEM((D_MODEL, MOE_D_FF), _BF16),
            pltpu.VMEM((D_MODEL, MOE_D_FF), _BF16),
            pltpu.VMEM((MOE_D_FF, D_MODEL), _BF16),
        ],
    )
    return pl.pallas_call(
        _experts_kernel,
        grid_spec=grid_spec,
        out_shape=jax.ShapeDtypeStruct((n_rows, D_MODEL), _F32),
        compiler_params=pltpu.CompilerParams(
            dimension_semantics=("arbitrary",), vmem_limit_bytes=VMEM_LIMIT),
        name="experts",
    )(plan["tile_expert"], plan["tile_valid"], xs, w_gate, w_up, w_down)


def _combine_kernel(off_ref, run_ref, dst_ref, x1_ref, info_ref, ys_hbm, out_ref, block_ref, sem):
    tile = pl.program_id(0)
    tm = ROW_TILE

    @pl.when(tile == 0)
    def _():
        block_ref[...] = jnp.zeros_like(block_ref)

    def make_copy(off, dst, piece):
        return pltpu.make_async_copy(ys_hbm.at[pl.ds(dst, piece)], block_ref.at[pl.ds(off, piece)], sem)

    _for_each_run_piece(off_ref, run_ref, dst_ref, tile, make_copy, wait=False)
    _for_each_run_piece(off_ref, run_ref, dst_ref, tile, make_copy, wait=True)

    info = info_ref[...]
    place = _iota((tm, SORT_ROWS), 1).astype(_F32)
    weights = (jnp.where(place == info[:, 4:5], info[:, 2:3], 0.0)
               + jnp.where(place == info[:, 5:6], info[:, 3:4], 0.0)).astype(_BF16)
    for c in range(0, D_MODEL, 256):
        out_ref[:, c:c + 256] = x1_ref[:, c:c + 256] + _dot(weights, block_ref[:, c:c + 256].astype(_BF16))


def _combine(plan, x1, info, ys):
    n = x1.shape[0]
    grid_spec = pltpu.PrefetchScalarGridSpec(
        num_scalar_prefetch=3,
        grid=(n // ROW_TILE,),
        in_specs=[
            pl.BlockSpec((ROW_TILE, D_MODEL), lambda i, *_: (i, 0)),
            pl.BlockSpec((ROW_TILE, LANES), lambda i, *_: (i, 0)),
            pl.BlockSpec(memory_space=pl.ANY),
        ],
        out_specs=pl.BlockSpec((ROW_TILE, D_MODEL), lambda i, *_: (i, 0)),
        scratch_shapes=[pltpu.VMEM((SORT_ROWS, D_MODEL), _F32), pltpu.SemaphoreType.DMA(())],
    )
    return pl.pallas_call(
        _combine_kernel,
        grid_spec=grid_spec,
        out_shape=jax.ShapeDtypeStruct((n, D_MODEL), _F32),
        compiler_params=pltpu.CompilerParams(
            dimension_semantics=("arbitrary",), vmem_limit_bytes=VMEM_LIMIT),
        name="combine",
    )(plan["off"], plan["run"], plan["dst"], x1, info, ys)


def _rotary_tables(seq):
    half = RET_QK_DIM // 2
    inv = ROPE_BASE ** (-jnp.arange(half, dtype=_F32) / half)
    ang = jnp.arange(seq, dtype=jnp.int32).astype(_F32)[:, None] * inv[None, :]
    cos = jnp.cos(ang)
    sin = jnp.sin(ang)
    return jnp.concatenate([cos, cos], axis=1), jnp.concatenate([-sin, sin], axis=1)


def _expert_row_bound(n_tokens):
    rows = (2 * n_tokens + (n_tokens // ROW_TILE) * MOE_N_EXPERTS * (SUBLANES - 1)
            + MOE_N_EXPERTS * (EXPERT_TILE - SUBLANES))
    return -(-rows // EXPERT_TILE) * EXPERT_TILE


def _routing_plan(counts_out, n_rows):
    te = EXPERT_TILE
    n_tok_tiles = counts_out.shape[0] // SUBLANES
    counts = counts_out.reshape(n_tok_tiles, SUBLANES, LANES)[:, 0, :MOE_N_EXPERTS].astype(jnp.int32)
    run = (counts + SUBLANES - 1) // SUBLANES * SUBLANES
    off = jnp.cumsum(run, axis=1) - run
    total = jnp.sum(run, axis=0)
    padded = (total + te - 1) // te * te
    pends = jnp.cumsum(padded)
    pstarts = pends - padded
    dst = pstarts[None, :] + jnp.cumsum(run, axis=0) - run
    tile_start = jnp.arange(n_rows // te, dtype=jnp.int32) * te
    tile_expert = jnp.sum((tile_start[:, None] >= pends[None, :]).astype(jnp.int32), axis=1)
    tile_expert = jnp.minimum(tile_expert, MOE_N_EXPERTS - 1)
    tile_valid = jnp.clip(total[tile_expert] - (tile_start - pstarts[tile_expert]), 0, te)
    i32 = lambda v: v.astype(jnp.int32)
    return {
        "off": i32(off.reshape(-1)), "run": i32(run.reshape(-1)), "dst": i32(dst.reshape(-1)),
        "pad_start": i32(jnp.concatenate([pstarts + total, pends[-1:]])), "pad_len": i32(padded - total),
        "tile_expert": i32(tile_expert), "tile_valid": i32(tile_valid),
    }


def kernel(x, norm_mix_g, w_in, ret_gn_g, w_ret_o, q_norm_g, k_norm_g, sinks, w_swa_o, w_out, norm_ffn_g,
           w_router_group, b_router_group, w_router_expert, b_router_expert, w_gate, w_up, w_down):
    batch, seq, d = x.shape
    n = batch * seq
    assert d == D_MODEL and seq % SEQ_TILE == 0 and n % ROW_TILE == 0
    cos_tab, sin_tab = _rotary_tables(seq)
    n_rows = _expert_row_bound(n)
    for l in range(w_in.shape[0]):
        x2d = x.reshape(n, d)
        ret_proj, swa_proj, gates = _inproj(x2d, norm_mix_g[l][None, :], w_in[l].astype(_BF16))
        retg = _retention(ret_proj, cos_tab, sin_tab, ret_gn_g[l], batch, seq)
        attn = _swa(swa_proj, sinks[l], jnp.tile(q_norm_g[l], 2)[None, :], jnp.tile(k_norm_g[l], 2)[None, :],
                    batch, seq)
        pad = LANES - MOE_N_EXPERTS - MOE_GROUPS
        w_router = jnp.concatenate(
            [w_router_expert[l], w_router_group[l], jnp.zeros((d, pad), _F32)], axis=1).astype(_BF16)
        b_router = jnp.concatenate([b_router_expert[l], b_router_group[l], jnp.zeros((pad,), _F32)])[None, :]
        x1, h2, info, counts = _outproj(
            x2d, retg, attn, gates, w_ret_o[l].astype(_BF16), w_swa_o[l].astype(_BF16), w_out[l].astype(_BF16),
            norm_ffn_g[l][None, :], w_router, b_router)
        plan = _routing_plan(counts, n_rows)
        xs = _dispatch(plan, h2, info, n_rows)
        ys = _experts(plan, xs, w_gate[l], w_up[l], w_down[l])
        x = _combine(plan, x1, info, ys).reshape(batch, seq, d)
    return x
```

```python
import math

import numpy as np
import jax
import jax.numpy as jnp
from jax import lax
from jax.experimental import pallas as pl
from jax.experimental.pallas import tpu as pltpu

D_MODEL = 1024
RET_HEADS = 4
RET_QK_DIM = 128
RET_V_DIM = 256
RET_CHUNK = 128
ROPE_BASE = 10000.0
SWA_Q_HEADS = 8
SWA_KV_HEADS = 2
SWA_HEAD_DIM = 64
SWA_WINDOW = 128
MOE_GROUPS = 4
MOE_EXPERTS_PER_GROUP = 8
MOE_D_FF = 512
NORM_EPS = 1e-6

RET_QK_W = RET_HEADS * RET_QK_DIM
RET_V_W = RET_HEADS * RET_V_DIM
SWA_Q_W = SWA_Q_HEADS * SWA_HEAD_DIM
SWA_KV_W = SWA_KV_HEADS * SWA_HEAD_DIM
SWA_GROUP = SWA_Q_HEADS // SWA_KV_HEADS
MOE_N_EXPERTS = MOE_GROUPS * MOE_EXPERTS_PER_GROUP

RET_COLS = 2 * RET_QK_W + 2 * RET_V_W
SWA_COLS = SWA_Q_W + 2 * SWA_KV_W
GATE_COLS = 2 * D_MODEL
IN_WIDTH = RET_COLS + SWA_COLS + GATE_COLS

LANES = 128
SUBLANES = 8
ROUTER_GROUP_LANE = 32
NEG_BIG = -1e30

ROW_TILE = 512
SEQ_TILE = 512
EXPERT_TILE = 512
RUN_ALIGN = 16
SORT_ROWS = 2 * ROW_TILE + MOE_N_EXPERTS * RUN_ALIGN
SPLIT = 32.0
VMEM_LIMIT = 56 * 1024 * 1024

RET_LOG_DECAY = tuple(float(np.log1p(-np.exp2(-5.0 - h))) for h in range(RET_HEADS))

_BF16 = jnp.bfloat16
_F32 = jnp.float32


def _dot(a, b):
    return jnp.dot(a, b, preferred_element_type=_F32)


def _dot_nt(a, b):
    return lax.dot_general(a, b, (((1,), (1,)), ((), ())), preferred_element_type=_F32)


def _sigmoid(v):
    return 1.0 / (1.0 + jnp.exp(-v))


def _split_bf16(v):
    hi = v.astype(_BF16)
    return hi, (v - hi.astype(_F32)).astype(_BF16)


def _iota(shape, dim):
    return lax.broadcasted_iota(jnp.int32, shape, dim)


def _inproj_kernel(x_ref, g_ref, w_ref, ret_ref, swa_ref, gate_ref):
    x = x_ref[...]
    r = lax.rsqrt(jnp.mean(x * x, axis=-1, keepdims=True) + NORM_EPS)
    h = ((x * r) * g_ref[...]).astype(_BF16)
    base = 0
    for out_ref, width, step in ((ret_ref, RET_COLS, 512), (swa_ref, SWA_COLS, 256), (gate_ref, GATE_COLS, 512)):
        for c in range(0, width, step):
            out_ref[:, c:c + step] = _dot(h, w_ref[:, base + c:base + c + step]).astype(_BF16)
        base += width


def _inproj(x2d, g, w_in_bf16):
    n = x2d.shape[0]
    grid = (n // ROW_TILE,)
    return pl.pallas_call(
        _inproj_kernel,
        grid=grid,
        in_specs=[
            pl.BlockSpec((ROW_TILE, D_MODEL), lambda i: (i, 0)),
            pl.BlockSpec((1, D_MODEL), lambda i: (0, 0)),
            pl.BlockSpec((D_MODEL, IN_WIDTH), lambda i: (0, 0)),
        ],
        out_specs=[
            pl.BlockSpec((ROW_TILE, RET_COLS), lambda i: (i, 0)),
            pl.BlockSpec((ROW_TILE, SWA_COLS), lambda i: (i, 0)),
            pl.BlockSpec((ROW_TILE, GATE_COLS), lambda i: (i, 0)),
        ],
        out_shape=[
            jax.ShapeDtypeStruct((n, RET_COLS), _BF16),
            jax.ShapeDtypeStruct((n, SWA_COLS), _BF16),
            jax.ShapeDtypeStruct((n, GATE_COLS), _BF16),
        ],
        compiler_params=pltpu.CompilerParams(
            dimension_semantics=("arbitrary",), vmem_limit_bytes=VMEM_LIMIT),
        name="inproj",
    )(x2d, g, w_in_bf16)


def _retention_kernel(q_ref, k_ref, v_ref, g_ref, cos_ref, sin_ref, gn_ref, out_ref, state_ref):
    c = RET_CHUNK

    @pl.when(pl.program_id(1) == 0)
    def _():
        state_ref[...] = jnp.zeros_like(state_ref)

    diff = (_iota((c, c), 0) - _iota((c, c), 1)).astype(_F32)
    pos = _iota((c, 1), 0).astype(_F32)
    scale = RET_QK_DIM ** -0.5

    for h in range(RET_HEADS):
        lg = RET_LOG_DECAY[h]
        decay = jnp.where(diff >= 0, jnp.exp(lg * jnp.maximum(diff, 0.0)), 0.0) * scale
        q_decay = jnp.exp(lg * (pos + 1.0))
        k_decay = jnp.exp(lg * (c - 1.0 - pos)) * scale
        chunk_decay = math.exp(lg * c)
        qs = slice(h * RET_QK_DIM, (h + 1) * RET_QK_DIM)
        vs = slice(h * RET_V_DIM, (h + 1) * RET_V_DIM)
        gain = gn_ref[h:h + 1, :]
        for ci in range(SEQ_TILE // c):
            rows = slice(ci * c, (ci + 1) * c)
            cos = cos_ref[rows, :]
            sin = sin_ref[rows, :]
            q_raw = q_ref[rows, qs].astype(_F32)
            k_raw = k_ref[rows, qs].astype(_F32)
            q = q_raw * cos + pltpu.roll(q_raw, RET_QK_DIM // 2, 1) * sin
            k = k_raw * cos + pltpu.roll(k_raw, RET_QK_DIM // 2, 1) * sin
            v = v_ref[rows, vs]
            state = state_ref[h]
            scores = _dot_nt(q.astype(_BF16), k.astype(_BF16)) * decay
            ret = _dot(scores.astype(_BF16), v) + _dot((q * q_decay).astype(_BF16), state.astype(_BF16))
            kv = _dot((k * k_decay).T.astype(_BF16), v)
            state_ref[h] = chunk_decay * state + kv
            mu = jnp.mean(ret, axis=-1, keepdims=True)
            dev = ret - mu
            var = jnp.mean(dev * dev, axis=-1, keepdims=True)
            normed = (dev * lax.rsqrt(var + NORM_EPS)) * gain
            gate = g_ref[rows, vs].astype(_F32)
            out_ref[rows, vs] = ((gate * _sigmoid(gate)) * normed).astype(_BF16)


def _retention(ret_proj, cos_tab, sin_tab, gn_g, batch, seq):
    n = ret_proj.shape[0]
    steps = seq // SEQ_TILE
    rowmap = lambda b, s: b * steps + s
    return pl.pallas_call(
        _retention_kernel,
        grid=(batch, steps),
        in_specs=[
            pl.BlockSpec((SEQ_TILE, RET_QK_W), lambda b, s: (rowmap(b, s), 0)),
            pl.BlockSpec((SEQ_TILE, RET_QK_W), lambda b, s: (rowmap(b, s), 1)),
            pl.BlockSpec((SEQ_TILE, RET_V_W), lambda b, s: (rowmap(b, s), 1)),
            pl.BlockSpec((SEQ_TILE, RET_V_W), lambda b, s: (rowmap(b, s), 2)),
            pl.BlockSpec((SEQ_TILE, RET_QK_DIM), lambda b, s: (s, 0)),
            pl.BlockSpec((SEQ_TILE, RET_QK_DIM), lambda b, s: (s, 0)),
            pl.BlockSpec((RET_HEADS, RET_V_DIM), lambda b, s: (0, 0)),
        ],
        out_specs=pl.BlockSpec((SEQ_TILE, RET_V_W), lambda b, s: (rowmap(b, s), 0)),
        out_shape=jax.ShapeDtypeStruct((n, RET_V_W), _BF16),
        scratch_shapes=[pltpu.VMEM((RET_HEADS, RET_QK_DIM, RET_V_DIM), _F32)],
        compiler_params=pltpu.CompilerParams(
            dimension_semantics=("arbitrary", "arbitrary"), vmem_limit_bytes=VMEM_LIMIT),
        name="retention",
    )(ret_proj, ret_proj, ret_proj, ret_proj, cos_tab, sin_tab, gn_g)


def _swa_kernel(sinks_ref, q_ref, kc_ref, vc_ref, kp_ref, vp_ref, qg_ref, kg_ref, out_ref):
    w = SWA_WINDOW
    d = SWA_HEAD_DIM
    n_blk = SEQ_TILE // w
    pair = 2 * d
    first = pl.program_id(1) == 0

    k_both = jnp.concatenate([kp_ref[...], kc_ref[...]], axis=0).astype(_F32)
    same_half = (_iota((pair, pair), 0) // d == _iota((pair, pair), 1) // d).astype(_BF16)
    k_hi, k_lo = _split_bf16(k_both * k_both)
    k_ssq = _dot(k_hi, same_half) + _dot(k_lo, same_half)
    k_n = k_both * lax.rsqrt(k_ssq * (1.0 / d) + NORM_EPS) * (kg_ref[...] * d ** -0.5)
    v_both = jnp.concatenate([vp_ref[...], vc_ref[...]], axis=0).astype(_F32)
    v_t = [v_both[b * w:(b + 1) * w].T for b in range(n_blk + 1)]

    key = _iota((2 * w, SWA_GROUP * w), 0)
    qry = _iota((2 * w, SWA_GROUP * w), 1) % w
    band = (key > qry) & (key <= qry + w)
    band_first = band & jnp.logical_or(key >= w, jnp.logical_not(first))
    head_of_lane = _iota((1, SWA_GROUP * w), 1) // w
    half_of_lane = _iota((1, pair), 1) // d
    half_rows = (_iota((SUBLANES, pair), 0) == _iota((SUBLANES, pair), 1) // d).astype(_BF16)
    ones_rows = jnp.ones((2 * SUBLANES, 2 * w), _BF16)
    q_gain = qg_ref[...]

    for kk in range(SWA_KV_HEADS):
        k_native = jnp.where(half_of_lane == kk, k_n, 0.0)
        k_moved = pltpu.roll(k_native, d, 1)
        k_even, k_odd = (k_native, k_moved) if kk == 0 else (k_moved, k_native)
        k_even = k_even.astype(_BF16)
        k_odd = k_odd.astype(_BF16)
        sink_row = jnp.zeros((1, SWA_GROUP * w), _F32)
        for g in range(SWA_GROUP):
            sink_row = jnp.where(head_of_lane == g, sinks_ref[kk * SWA_GROUP + g], sink_row)
        for j in range(n_blk):
            rows = slice(j * w, (j + 1) * w)
            keys = slice(j * w, (j + 2) * w)
            raws = []
            for p in range(SWA_GROUP // 2):
                c0 = (kk * SWA_GROUP + 2 * p) * d
                q_pair = q_ref[rows, c0:c0 + pair].astype(_F32)
                q_hi, q_lo = _split_bf16(q_pair * q_pair)
                q_ssq = _dot_nt(half_rows, q_hi) + _dot_nt(half_rows, q_lo)
                q_r = lax.rsqrt(q_ssq * (1.0 / d) + NORM_EPS)
                q_g = (q_pair * q_gain).astype(_BF16)
                raws.append(_dot_nt(k_even[keys], q_g) * q_r[0:1, :])
                raws.append(_dot_nt(k_odd[keys], q_g) * q_r[1:2, :])
            s_t = jnp.concatenate(raws, axis=1)
            s_t = jnp.where(band_first if j == 0 else band, s_t, NEG_BIG)
            m = jnp.maximum(jnp.max(s_t, axis=0, keepdims=True), sink_row)
            p_t = jnp.exp(s_t - m).astype(_BF16)
            v_ext = jnp.concatenate([v_t[j][kk * d:(kk + 1) * d], v_t[j + 1][kk * d:(kk + 1) * d]], axis=1)
            o1 = _dot(jnp.concatenate([v_ext.astype(_BF16), ones_rows], axis=0), p_t)
            denom = o1[d:d + 1, :] + jnp.exp(sink_row - m)
            o_t = o1[0:d, :] * (1.0 / denom)
            for p in range(SWA_GROUP // 2):
                c0 = (kk * SWA_GROUP + 2 * p) * d
                both = jnp.concatenate([o_t[:, 2 * p * w:(2 * p + 1) * w], o_t[:, (2 * p + 1) * w:(2 * p + 2) * w]], axis=0)
                out_ref[rows, c0:c0 + pair] = both.T.astype(_BF16)


def _swa(swa_proj, sinks, q_gain2, k_gain2, batch, seq):
    n = swa_proj.shape[0]
    steps = seq // SEQ_TILE
    blocks_per_step = SEQ_TILE // SWA_WINDOW
    rowmap = lambda b, s: b * steps + s
    prevmap = lambda b, s: jnp.maximum((b * steps + s) * blocks_per_step - 1, 0)
    k_col = SWA_Q_W // SWA_KV_W
    return pl.pallas_call(
        _swa_kernel,
        grid=(batch, steps),
        in_specs=[
            pl.BlockSpec(memory_space=pltpu.SMEM),
            pl.BlockSpec((SEQ_TILE, SWA_Q_W), lambda b, s: (rowmap(b, s), 0)),
            pl.BlockSpec((SEQ_TILE, SWA_KV_W), lambda b, s: (rowmap(b, s), k_col)),
            pl.BlockSpec((SEQ_TILE, SWA_KV_W), lambda b, s: (rowmap(b, s), k_col + 1)),
            pl.BlockSpec((SWA_WINDOW, SWA_KV_W), lambda b, s: (prevmap(b, s), k_col)),
            pl.BlockSpec((SWA_WINDOW, SWA_KV_W), lambda b, s: (prevmap(b, s), k_col + 1)),
            pl.BlockSpec((1, 2 * SWA_HEAD_DIM), lambda b, s: (0, 0)),
            pl.BlockSpec((1, 2 * SWA_HEAD_DIM), lambda b, s: (0, 0)),
        ],
        out_specs=pl.BlockSpec((SEQ_TILE, SWA_Q_W), lambda b, s: (rowmap(b, s), 0)),
        out_shape=jax.ShapeDtypeStruct((n, SWA_Q_W), _BF16),
        compiler_params=pltpu.CompilerParams(
            dimension_semantics=("arbitrary", "arbitrary"), vmem_limit_bytes=VMEM_LIMIT),
        name="swa",
    )(sinks, swa_proj, swa_proj, swa_proj, swa_proj, swa_proj, q_gain2, k_gain2)


def _outproj_kernel(x_ref, retg_ref, attn_ref, ga_ref, gb_ref, wro_ref, wso_ref, wout_ref, g2_ref,
                    wr_ref, br_ref, x1_ref, h2_ref, info_ref, counts_ref):
    tm = ROW_TILE
    y_a = _dot(retg_ref[...], wro_ref[...])
    y_b = _dot(attn_ref[...], wso_ref[...])
    merged = _sigmoid(ga_ref[...].astype(_F32)) * y_a + _sigmoid(gb_ref[...].astype(_F32)) * y_b
    x1 = x_ref[...] + _dot(merged.astype(_BF16), wout_ref[...])
    x1_ref[...] = x1
    r = lax.rsqrt(jnp.mean(x1 * x1, axis=-1, keepdims=True) + NORM_EPS)
    h2 = ((x1 * r) * g2_ref[...]).astype(_BF16)
    h2_ref[...] = h2

    logits = _dot(h2, wr_ref[...]) + br_ref[...]
    lane = _iota((tm, LANES), 1)
    lane_f = lane.astype(_F32)
    is_group = (lane >= ROUTER_GROUP_LANE) & (lane < ROUTER_GROUP_LANE + MOE_GROUPS)
    gl = jnp.where(is_group, logits, NEG_BIG)
    g_max = jnp.max(gl, axis=-1, keepdims=True)
    g_prob = 1.0 / jnp.sum(jnp.exp(gl - g_max), axis=-1, keepdims=True)
    g_idx = jnp.min(jnp.where(gl == g_max, lane_f - ROUTER_GROUP_LANE, float(LANES)), axis=-1, keepdims=True)
    e_lo = g_idx * MOE_EXPERTS_PER_GROUP
    in_group = (lane_f >= e_lo) & (lane_f < e_lo + MOE_EXPERTS_PER_GROUP)
    el = jnp.where(in_group, logits, NEG_BIG)
    t1 = jnp.max(el, axis=-1, keepdims=True)
    i1 = jnp.min(jnp.where(el == t1, lane_f, float(LANES)), axis=-1, keepdims=True)
    el2 = jnp.where(lane_f == i1, NEG_BIG, el)
    t2 = jnp.max(el2, axis=-1, keepdims=True)
    i2 = jnp.min(jnp.where(el2 == t2, lane_f, float(LANES)), axis=-1, keepdims=True)
    e21 = jnp.exp(t2 - t1)
    w1 = g_prob / (1.0 + e21)
    w2 = g_prob * e21 / (1.0 + e21)

    sel1 = lane_f == i1
    sel2 = lane_f == i2
    onehot = jnp.where(sel1 | sel2, 1.0, 0.0)
    earlier = (_iota((tm, tm), 0) > _iota((tm, tm), 1)).astype(_BF16)
    before = _dot(earlier, onehot.astype(_BF16))
    counts = jnp.sum(onehot, axis=0, keepdims=True)
    run = jnp.floor((counts + (RUN_ALIGN - 1.0)) * (1.0 / RUN_ALIGN)) * RUN_ALIGN
    lower_experts = (_iota((LANES, LANES), 0) < _iota((LANES, LANES), 1)).astype(_BF16)
    run_off = _dot(jnp.broadcast_to(run, (SUBLANES, LANES)).astype(_BF16), lower_experts)[0:1, :]
    place = before + run_off
    pos1 = jnp.sum(jnp.where(sel1, place, 0.0), axis=-1, keepdims=True)
    pos2 = jnp.sum(jnp.where(sel2, place, 0.0), axis=-1, keepdims=True)
    counts_ref[...] = jnp.broadcast_to(counts, counts_ref.shape)

    info = jnp.where(lane == 0, i1, 0.0)
    info = jnp.where(lane == 1, i2, info)
    info = jnp.where(lane == 2, w1, info)
    info = jnp.where(lane == 3, w2, info)
    info = jnp.where(lane == 4, pos1, info)
    info = jnp.where(lane == 5, pos2, info)
    info_ref[...] = info


def _outproj(x2d, retg, attn, gates, w_ret_o, w_swa_o, w_out, g2, w_router, b_router):
    n = x2d.shape[0]
    tm = ROW_TILE
    const = lambda i: (0, 0)
    return pl.pallas_call(
        _outproj_kernel,
        grid=(n // tm,),
        in_specs=[
            pl.BlockSpec((tm, D_MODEL), lambda i: (i, 0)),
            pl.BlockSpec((tm, RET_V_W), lambda i: (i, 0)),
            pl.BlockSpec((tm, SWA_Q_W), lambda i: (i, 0)),
            pl.BlockSpec((tm, D_MODEL), lambda i: (i, 0)),
            pl.BlockSpec((tm, D_MODEL), lambda i: (i, 1)),
            pl.BlockSpec((RET_V_W, D_MODEL), const),
            pl.BlockSpec((SWA_Q_W, D_MODEL), const),
            pl.BlockSpec((D_MODEL, D_MODEL), const),
            pl.BlockSpec((1, D_MODEL), const),
            pl.BlockSpec((D_MODEL, LANES), const),
            pl.BlockSpec((1, LANES), const),
        ],
        out_specs=[
            pl.BlockSpec((tm, D_MODEL), lambda i: (i, 0)),
            pl.BlockSpec((tm, D_MODEL), lambda i: (i, 0)),
            pl.BlockSpec((tm, LANES), lambda i: (i, 0)),
            pl.BlockSpec((SUBLANES, LANES), lambda i: (i, 0)),
        ],
        out_shape=[
            jax.ShapeDtypeStruct((n, D_MODEL), _F32),
            jax.ShapeDtypeStruct((n, D_MODEL), _BF16),
            jax.ShapeDtypeStruct((n, LANES), _F32),
            jax.ShapeDtypeStruct((n // tm * SUBLANES, LANES), _F32),
        ],
        compiler_params=pltpu.CompilerParams(
            dimension_semantics=("arbitrary",), vmem_limit_bytes=VMEM_LIMIT),
        name="outproj",
    )(x2d, retg, attn, gates, gates, w_ret_o, w_swa_o, w_out, g2, w_router, b_router)


def _piece_sizes(largest):
    piece = largest
    while piece >= RUN_ALIGN:
        yield piece
        piece //= 2


def _for_each_run_piece(off_ref, run_ref, dst_ref, tile, make_copy, wait):
    def per_expert(e, carry):
        idx = tile * MOE_N_EXPERTS + e
        off = off_ref[idx]
        dst = dst_ref[idx]
        run = run_ref[idx]
        for piece in _piece_sizes(ROW_TILE):
            copy = make_copy(pl.multiple_of(off, RUN_ALIGN), pl.multiple_of(dst, RUN_ALIGN), piece)
            pl.when((run & piece) != 0)(copy.wait if wait else copy.start)
            off = off + (run & piece)
            dst = dst + (run & piece)
        return carry

    lax.fori_loop(0, MOE_N_EXPERTS, per_expert, 0)


def _zero_fill_padding(pad_start_ref, pad_len_ref, zeros_ref, xs_hbm, sem, wait):
    def per_expert(e, carry):
        pos = pad_start_ref[e]
        length = pad_len_ref[e]
        for piece in _piece_sizes(EXPERT_TILE // 2):
            copy = pltpu.make_async_copy(
                zeros_ref.at[pl.ds(0, piece)], xs_hbm.at[pl.ds(pl.multiple_of(pos, RUN_ALIGN), piece)], sem)
            pl.when((length & piece) != 0)(copy.wait if wait else copy.start)
            pos = pos + (length & piece)
        return carry

    lax.fori_loop(0, MOE_N_EXPERTS, per_expert, 0)

    tail_start = pad_start_ref[MOE_N_EXPERTS]
    n_rows = xs_hbm.shape[0]
    half = EXPERT_TILE // 2

    def per_half_tile(j, carry):
        pos = tail_start + j * half
        copy = pltpu.make_async_copy(
            zeros_ref, xs_hbm.at[pl.ds(pl.multiple_of(jnp.minimum(pos, n_rows - half), RUN_ALIGN), half)], sem)
        pl.when(pos < n_rows)(copy.wait if wait else copy.start)
        return carry

    lax.fori_loop(0, n_rows // half, per_half_tile, 0)


def _dispatch_kernel(off_ref, run_ref, dst_ref, pad_start_ref, pad_len_ref, h2_ref, info_ref, xs_hbm,
                     sorted_ref, zeros_ref, sems, zero_sem):
    tile = pl.program_id(0)
    last = pl.num_programs(0) - 1
    slot = tile % 2
    tm = ROW_TILE

    @pl.when(tile == 0)
    def _():
        zeros_ref[...] = jnp.zeros_like(zeros_ref)
        _zero_fill_padding(pad_start_ref, pad_len_ref, zeros_ref, xs_hbm, zero_sem, wait=False)
        _zero_fill_padding(pad_start_ref, pad_len_ref, zeros_ref, xs_hbm, zero_sem, wait=True)

    info = info_ref[...]
    hi = jnp.floor(info * (1.0 / SPLIT))
    lo = info - hi * SPLIT
    pick = ((_iota((SUBLANES, LANES), 1) == _iota((SUBLANES, LANES), 0) + 4)
            & (_iota((SUBLANES, LANES), 0) < 2)).astype(_BF16)
    pos = (_dot_nt(pick, hi.astype(_BF16)) * SPLIT + _dot_nt(pick, lo.astype(_BF16))).astype(jnp.int32)
    place = _iota((SORT_ROWS, tm), 0)
    onehot = ((place == pos[0:1, :]) | (place == pos[1:2, :])).astype(_BF16)
    for c in range(0, D_MODEL, 256):
        sorted_ref[slot, :, c:c + 256] = _dot(onehot, h2_ref[:, c:c + 256]).astype(_BF16)

    def run_copies(t, wait):
        s = t % 2

        def make_copy(off, dst, piece):
            return pltpu.make_async_copy(
                sorted_ref.at[s, pl.ds(off, piece)], xs_hbm.at[pl.ds(dst, piece)], sems.at[s])

        _for_each_run_piece(off_ref, run_ref, dst_ref, t, make_copy, wait)

    run_copies(tile, wait=False)

    @pl.when(tile > 0)
    def _():
        run_copies(tile - 1, wait=True)

    @pl.when(tile == last)
    def _():
        run_copies(tile, wait=True)


def _dispatch(plan, h2, info, n_rows):
    n = h2.shape[0]
    grid_spec = pltpu.PrefetchScalarGridSpec(
        num_scalar_prefetch=5,
        grid=(n // ROW_TILE,),
        in_specs=[
            pl.BlockSpec((ROW_TILE, D_MODEL), lambda i, *_: (i, 0)),
            pl.BlockSpec((ROW_TILE, LANES), lambda i, *_: (i, 0)),
        ],
        out_specs=pl.BlockSpec(memory_space=pl.ANY),
        scratch_shapes=[
            pltpu.VMEM((2, SORT_ROWS, D_MODEL), _BF16),
            pltpu.VMEM((EXPERT_TILE // 2, D_MODEL), _BF16),
            pltpu.SemaphoreType.DMA((2,)),
            pltpu.SemaphoreType.DMA(()),
        ],
    )
    return pl.pallas_call(
        _dispatch_kernel,
        grid_spec=grid_spec,
        out_shape=jax.ShapeDtypeStruct((n_rows, D_MODEL), _BF16),
        compiler_params=pltpu.CompilerParams(
            dimension_semantics=("arbitrary",), vmem_limit_bytes=VMEM_LIMIT),
        name="dispatch",
    )(plan["off"], plan["run"], plan["dst"], plan["pad_start"], plan["pad_len"], h2, info)


def _experts_kernel(te_ref, tv_ref, xs_ref, wg_ref, wu_ref, wd_ref, ys_ref, wg_s, wu_s, wd_s):
    i = pl.program_id(0)
    valid = tv_ref[i]
    prev = te_ref[jnp.maximum(i - 1, 0)]
    changed = jnp.logical_or(i == 0, te_ref[i] != prev)

    @pl.when(jnp.logical_and(valid > 0, changed))
    def _():
        wg_s[...] = wg_ref[...].astype(_BF16)
        wu_s[...] = wu_ref[...].astype(_BF16)
        wd_s[...] = wd_ref[...].astype(_BF16)

    @pl.when(valid > 0)
    def _():
        xb = xs_ref[...]
        gate = _dot(xb, wg_s[...])
        up = _dot(xb, wu_s[...])
        act = ((gate * _sigmoid(gate)) * up).astype(_BF16)
        ys_ref[...] = _dot(act, wd_s[...]).astype(_BF16)

    @pl.when(valid <= 0)
    def _():
        ys_ref[...] = jnp.zeros_like(ys_ref)


def _experts(plan, xs, w_gate, w_up, w_down):
    n_rows = xs.shape[0]
    n_tiles = n_rows // EXPERT_TILE
    grid_spec = pltpu.PrefetchScalarGridSpec(
        num_scalar_prefetch=2,
        grid=(n_tiles,),
        in_specs=[
            pl.BlockSpec((EXPERT_TILE, D_MODEL), lambda i, te, tv: (i, 0)),
            pl.BlockSpec((None, D_MODEL, MOE_D_FF), lambda i, te, tv: (te[i], 0, 0)),
            pl.BlockSpec((None, D_MODEL, MOE_D_FF), lambda i, te, tv: (te[i], 0, 0)),
            pl.BlockSpec((None, MOE_D_FF, D_MODEL), lambda i, te, tv: (te[i], 0, 0)),
        ],
        out_specs=pl.BlockSpec((EXPERT_TILE, D_MODEL), lambda i, te, tv: (i, 0)),
        scratch_shapes=[
            pltpu.VMEM((D_MODEL, MOE_D_FF), _BF16),
            pltpu.VMEM((D_MODEL, MOE_D_FF), _BF16),
            pltpu.VMEM((MOE_D_FF, D_MODEL), _BF16),
        ],
    )
    return pl.pallas_call(
        _experts_kernel,
        grid_spec=grid_spec,
        out_shape=jax.ShapeDtypeStruct((n_rows, D_MODEL), _BF16),
        compiler_params=pltpu.CompilerParams(
            dimension_semantics=("arbitrary",), vmem_limit_bytes=VMEM_LIMIT),
        name="experts",
    )(plan["tile_expert"], plan["tile_valid"], xs, w_gate, w_up, w_down)


def _combine_kernel(off_ref, run_ref, dst_ref, x1_ref, info_ref, ys_hbm, out_ref, block_ref, sems):
    tile = pl.program_id(0)
    last = pl.num_programs(0) - 1
    slot = tile % 2
    tm = ROW_TILE

    def run_copies(t, wait):
        s = t % 2

        def make_copy(off, dst, piece):
            return pltpu.make_async_copy(
                ys_hbm.at[pl.ds(dst, piece)], block_ref.at[s, pl.ds(off, piece)], sems.at[s])

        _for_each_run_piece(off_ref, run_ref, dst_ref, t, make_copy, wait)

    @pl.when(tile == 0)
    def _():
        block_ref[...] = jnp.zeros_like(block_ref)
        run_copies(tile, wait=False)

    @pl.when(tile < last)
    def _():
        run_copies(tile + 1, wait=False)

    run_copies(tile, wait=True)

    info = info_ref[...]
    place = _iota((tm, SORT_ROWS), 1).astype(_F32)
    weights = (jnp.where(place == info[:, 4:5], info[:, 2:3], 0.0)
               + jnp.where(place == info[:, 5:6], info[:, 3:4], 0.0)).astype(_BF16)
    for c in range(0, D_MODEL, 256):
        out_ref[:, c:c + 256] = x1_ref[:, c:c + 256] + _dot(weights, block_ref[slot, :, c:c + 256])


def _combine(plan, x1, info, ys):
    n = x1.shape[0]
    grid_spec = pltpu.PrefetchScalarGridSpec(
        num_scalar_prefetch=3,
        grid=(n // ROW_TILE,),
        in_specs=[
            pl.BlockSpec((ROW_TILE, D_MODEL), lambda i, *_: (i, 0)),
            pl.BlockSpec((ROW_TILE, LANES), lambda i, *_: (i, 0)),
            pl.BlockSpec(memory_space=pl.ANY),
        ],
        out_specs=pl.BlockSpec((ROW_TILE, D_MODEL), lambda i, *_: (i, 0)),
        scratch_shapes=[pltpu.VMEM((2, SORT_ROWS, D_MODEL), _BF16), pltpu.SemaphoreType.DMA((2,))],
    )
    return pl.pallas_call(
        _combine_kernel,
        grid_spec=grid_spec,
        out_shape=jax.ShapeDtypeStruct((n, D_MODEL), _F32),
        compiler_params=pltpu.CompilerParams(
            dimension_semantics=("arbitrary",), vmem_limit_bytes=VMEM_LIMIT),
        name="combine",
    )(plan["off"], plan["run"], plan["dst"], x1, info, ys)


def _rotary_tables(seq):
    half = RET_QK_DIM // 2
    inv = ROPE_BASE ** (-jnp.arange(half, dtype=_F32) / half)
    ang = jnp.arange(seq, dtype=jnp.int32).astype(_F32)[:, None] * inv[None, :]
    cos = jnp.cos(ang)
    sin = jnp.sin(ang)
    return jnp.concatenate([cos, cos], axis=1), jnp.concatenate([-sin, sin], axis=1)


def _expert_row_bound(n_tokens):
    rows = (2 * n_tokens + (n_tokens // ROW_TILE) * MOE_N_EXPERTS * (RUN_ALIGN - 1)
            + MOE_N_EXPERTS * (EXPERT_TILE - RUN_ALIGN))
    return -(-rows // EXPERT_TILE) * EXPERT_TILE


def _routing_plan(counts_out, n_rows):
    te = EXPERT_TILE
    n_tok_tiles = counts_out.shape[0] // SUBLANES
    counts = counts_out.reshape(n_tok_tiles, SUBLANES, LANES)[:, 0, :MOE_N_EXPERTS].astype(jnp.int32)
    run = (counts + RUN_ALIGN - 1) // RUN_ALIGN * RUN_ALIGN
    off = jnp.cumsum(run, axis=1) - run
    total = jnp.sum(run, axis=0)
    padded = (total + te - 1) // te * te
    pends = jnp.cumsum(padded)
    pstarts = pends - padded
    dst = pstarts[None, :] + jnp.cumsum(run, axis=0) - run
    tile_start = jnp.arange(n_rows // te, dtype=jnp.int32) * te
    tile_expert = jnp.sum((tile_start[:, None] >= pends[None, :]).astype(jnp.int32), axis=1)
    tile_expert = jnp.minimum(tile_expert, MOE_N_EXPERTS - 1)
    tile_valid = jnp.clip(total[tile_expert] - (tile_start - pstarts[tile_expert]), 0, te)
    i32 = lambda v: v.astype(jnp.int32)
    return {
        "off": i32(off.reshape(-1)), "run": i32(run.reshape(-1)), "dst": i32(dst.reshape(-1)),
        "pad_start": i32(jnp.concatenate([pstarts + total, pends[-1:]])), "pad_len": i32(padded - total),
        "tile_expert": i32(tile_expert), "tile_valid": i32(tile_valid),
    }


def kernel(x, norm_mix_g, w_in, ret_gn_g, w_ret_o, q_norm_g, k_norm_g, sinks, w_swa_o, w_out, norm_ffn_g,
           w_router_group, b_router_group, w_router_expert, b_router_expert, w_gate, w_up, w_down):
    batch, seq, d = x.shape
    n = batch * seq
    assert d == D_MODEL and seq % SEQ_TILE == 0 and n % ROW_TILE == 0
    cos_tab, sin_tab = _rotary_tables(seq)
    n_rows = _expert_row_bound(n)
    for l in range(w_in.shape[0]):
        x2d = x.reshape(n, d)
        ret_proj, swa_proj, gates = _inproj(x2d, norm_mix_g[l][None, :], w_in[l].astype(_BF16))
        retg = _retention(ret_proj, cos_tab, sin_tab, ret_gn_g[l], batch, seq)
        attn = _swa(swa_proj, sinks[l], jnp.tile(q_norm_g[l], 2)[None, :], jnp.tile(k_norm_g[l], 2)[None, :],
                    batch, seq)
        pad = LANES - MOE_N_EXPERTS - MOE_GROUPS
        w_router = jnp.concatenate(
            [w_router_expert[l], w_router_group[l], jnp.zeros((d, pad), _F32)], axis=1).astype(_BF16)
        b_router = jnp.concatenate([b_router_expert[l], b_router_group[l], jnp.zeros((pad,), _F32)])[None, :]
        x1, h2, info, counts = _outproj(
            x2d, retg, attn, gates, w_ret_o[l].astype(_BF16), w_swa_o[l].astype(_BF16), w_out[l].astype(_BF16),
            norm_ffn_g[l][None, :], w_router, b_router)
        plan = _routing_plan(counts, n_rows)
        xs = _dispatch(plan, h2, info, n_rows)
        ys = _experts(plan, xs, w_gate[l], w_up[l], w_down[l])
        x = _combine(plan, x1, info, ys).reshape(batch, seq, d)
    return x
```

```python
import math

import numpy as np
import jax
import jax.numpy as jnp
from jax import lax
from jax.experimental import pallas as pl
from jax.experimental.pallas import tpu as pltpu

D_MODEL = 1024
RET_HEADS = 4
RET_QK_DIM = 128
RET_V_DIM = 256
RET_CHUNK = 128
ROPE_BASE = 10000.0
SWA_Q_HEADS = 8
SWA_KV_HEADS = 2
SWA_HEAD_DIM = 64
SWA_WINDOW = 128
MOE_GROUPS = 4
MOE_EXPERTS_PER_GROUP = 8
MOE_D_FF = 512
NORM_EPS = 1e-6

RET_QK_W = RET_HEADS * RET_QK_DIM
RET_V_W = RET_HEADS * RET_V_DIM
SWA_Q_W = SWA_Q_HEADS * SWA_HEAD_DIM
SWA_KV_W = SWA_KV_HEADS * SWA_HEAD_DIM
SWA_GROUP = SWA_Q_HEADS // SWA_KV_HEADS
MOE_N_EXPERTS = MOE_GROUPS * MOE_EXPERTS_PER_GROUP

RET_COLS = 2 * RET_QK_W + 2 * RET_V_W
SWA_COLS = SWA_Q_W + 2 * SWA_KV_W
GATE_COLS = 2 * D_MODEL
IN_WIDTH = RET_COLS + SWA_COLS + GATE_COLS

LANES = 128
SUBLANES = 8
ROUTER_GROUP_LANE = 32
NEG_BIG = -1e30

ROW_TILE = 512
SEQ_TILE = 512
EXPERT_TILE = 512
RUN_ALIGN = 16
SORT_ROWS = 2 * ROW_TILE + MOE_N_EXPERTS * RUN_ALIGN
SORT_ROWS_POW2 = 1 << SORT_ROWS.bit_length()
RUN_PIECE = 2 * RUN_ALIGN
VMEM_LIMIT = 56 * 1024 * 1024

RET_LOG_DECAY = tuple(float(np.log1p(-np.exp2(-5.0 - h))) for h in range(RET_HEADS))

_BF16 = jnp.bfloat16
_F32 = jnp.float32


def _dot(a, b):
    return jnp.dot(a, b, preferred_element_type=_F32)


def _dot_nt(a, b):
    return lax.dot_general(a, b, (((1,), (1,)), ((), ())), preferred_element_type=_F32)


def _sigmoid(v):
    return 1.0 / (1.0 + jnp.exp(-v))


def _split_bf16(v):
    hi = v.astype(_BF16)
    return hi, (v - hi.astype(_F32)).astype(_BF16)


def _iota(shape, dim):
    return lax.broadcasted_iota(jnp.int32, shape, dim)


def _inproj_kernel(x_ref, g_ref, w_ref, cos_ref, sin_ref, gn_ref, ret_ref, swa_ref, gate_ref):
    x = x_ref[...]
    r = lax.rsqrt(jnp.mean(x * x, axis=-1, keepdims=True) + NORM_EPS)
    h = ((x * r) * g_ref[...]).astype(_BF16)
    cos = cos_ref[...]
    sin = sin_ref[...]
    chunk_pos = (_iota((ROW_TILE, RET_QK_DIM), 0) % RET_CHUNK + 1).astype(_F32)

    def rotary_decay(base, sign, scale):
        def epilogue(raw):
            for hd in range(RET_HEADS):
                v = raw[:, hd * RET_QK_DIM:(hd + 1) * RET_QK_DIM]
                rot = v * cos + pltpu.roll(v, RET_QK_DIM // 2, 1) * sin
                decay = jnp.exp((sign * RET_LOG_DECAY[hd]) * chunk_pos) * scale
                ret_ref[:, base + hd * RET_QK_DIM:base + (hd + 1) * RET_QK_DIM] = (rot * decay).astype(_BF16)
        return epilogue

    def store(ref, lo, fn):
        def epilogue(raw):
            ref[:, lo:lo + raw.shape[1]] = fn(raw).astype(_BF16)
        return epilogue

    v_base = 2 * RET_QK_W
    g_base = v_base + RET_V_W
    gate_base = RET_COLS + SWA_COLS
    jobs = [(0, RET_QK_W, rotary_decay(0, 1.0, 1.0)),
            (RET_QK_W, RET_QK_W, rotary_decay(RET_QK_W, -1.0, RET_QK_DIM ** -0.5))]
    jobs += [(v_base + c, 512, store(ret_ref, v_base + c, lambda raw: raw)) for c in range(0, RET_V_W, 512)]
    jobs += [(g_base + c, 512, store(ret_ref, g_base + c,
                                     lambda raw, c=c: (raw * _sigmoid(raw)) * gn_ref[:, c:c + 512]))
             for c in range(0, RET_V_W, 512)]
    jobs += [(RET_COLS + c, 256, store(swa_ref, c, lambda raw: raw)) for c in range(0, SWA_COLS, 256)]
    jobs += [(gate_base + c, 512, store(gate_ref, c, _sigmoid)) for c in range(0, GATE_COLS, 512)]
    pending = None
    for lo, width, epilogue in jobs:
        raw = _dot(h, w_ref[:, lo:lo + width])
        if pending is not None:
            pending[0](pending[1])
        pending = (epilogue, raw)
    pending[0](pending[1])


def _inproj(x2d, g, w_in_bf16, cos_tab, sin_tab, gn_row):
    n = x2d.shape[0]
    grid = (n // ROW_TILE,)
    seq_tiles = cos_tab.shape[0] // ROW_TILE
    return pl.pallas_call(
        _inproj_kernel,
        grid=grid,
        in_specs=[
            pl.BlockSpec((ROW_TILE, D_MODEL), lambda i: (i, 0)),
            pl.BlockSpec((1, D_MODEL), lambda i: (0, 0)),
            pl.BlockSpec((D_MODEL, IN_WIDTH), lambda i: (0, 0)),
            pl.BlockSpec((ROW_TILE, RET_QK_DIM), lambda i: (i % seq_tiles, 0)),
            pl.BlockSpec((ROW_TILE, RET_QK_DIM), lambda i: (i % seq_tiles, 0)),
            pl.BlockSpec((1, RET_V_W), lambda i: (0, 0)),
        ],
        out_specs=[
            pl.BlockSpec((ROW_TILE, RET_COLS), lambda i: (i, 0)),
            pl.BlockSpec((ROW_TILE, SWA_COLS), lambda i: (i, 0)),
            pl.BlockSpec((ROW_TILE, GATE_COLS), lambda i: (i, 0)),
        ],
        out_shape=[
            jax.ShapeDtypeStruct((n, RET_COLS), _BF16),
            jax.ShapeDtypeStruct((n, SWA_COLS), _BF16),
            jax.ShapeDtypeStruct((n, GATE_COLS), _BF16),
        ],
        compiler_params=pltpu.CompilerParams(
            dimension_semantics=("arbitrary",), vmem_limit_bytes=VMEM_LIMIT),
        name="inproj",
    )(x2d, g, w_in_bf16, cos_tab, sin_tab, gn_row)


def _retention_kernel(q_ref, k_ref, v_ref, g_ref, out_ref, state_ref):
    c = RET_CHUNK

    @pl.when(pl.program_id(1) == 0)
    def _():
        state_ref[...] = jnp.zeros_like(state_ref)

    causal = _iota((c, c), 0) >= _iota((c, c), 1)

    for h in range(RET_HEADS):
        chunk_decay = math.exp(RET_LOG_DECAY[h] * c)
        qs = slice(h * RET_QK_DIM, (h + 1) * RET_QK_DIM)
        vs = slice(h * RET_V_DIM, (h + 1) * RET_V_DIM)
        chunks = [slice(ci * c, (ci + 1) * c) for ci in range(SEQ_TILE // c)]
        scores = [jnp.where(causal, _dot_nt(q_ref[rows, qs], k_ref[rows, qs]), 0.0).astype(_BF16) for rows in chunks]
        kvs = [_dot(k_ref[rows, qs].astype(_F32).T.astype(_BF16), v_ref[rows, vs]) for rows in chunks]
        state = state_ref[h]
        states = []
        for kv in kvs:
            states.append(state.astype(_BF16))
            state = chunk_decay * (state + kv)
        state_ref[h] = state
        rets = [_dot(jnp.concatenate([s, q_ref[rows, qs]], axis=1), jnp.concatenate([v_ref[rows, vs], st], axis=0))
                for rows, s, st in zip(chunks, scores, states)]
        for rows, ret in zip(chunks, rets):
            mu = jnp.mean(ret, axis=-1, keepdims=True)
            dev = ret - mu
            var = jnp.mean(dev * dev, axis=-1, keepdims=True)
            out_ref[rows, vs] = ((dev * lax.rsqrt(var + NORM_EPS)) * g_ref[rows, vs].astype(_F32)).astype(_BF16)


def _retention(ret_proj, batch, seq):
    n = ret_proj.shape[0]
    steps = seq // SEQ_TILE
    rowmap = lambda b, s: b * steps + s
    return pl.pallas_call(
        _retention_kernel,
        grid=(batch, steps),
        in_specs=[
            pl.BlockSpec((SEQ_TILE, RET_QK_W), lambda b, s: (rowmap(b, s), 0)),
            pl.BlockSpec((SEQ_TILE, RET_QK_W), lambda b, s: (rowmap(b, s), 1)),
            pl.BlockSpec((SEQ_TILE, RET_V_W), lambda b, s: (rowmap(b, s), 1)),
            pl.BlockSpec((SEQ_TILE, RET_V_W), lambda b, s: (rowmap(b, s), 2)),
        ],
        out_specs=pl.BlockSpec((SEQ_TILE, RET_V_W), lambda b, s: (rowmap(b, s), 0)),
        out_shape=jax.ShapeDtypeStruct((n, RET_V_W), _BF16),
        scratch_shapes=[pltpu.VMEM((RET_HEADS, RET_QK_DIM, RET_V_DIM), _F32)],
        compiler_params=pltpu.CompilerParams(
            dimension_semantics=("arbitrary", "arbitrary"), vmem_limit_bytes=VMEM_LIMIT),
        name="retention",
    )(ret_proj, ret_proj, ret_proj, ret_proj)


def _swa_kernel(sinks_ref, q_ref, kc_ref, vc_ref, kp_ref, vp_ref, qg_ref, kg_ref, out_ref):
    w = SWA_WINDOW
    d = SWA_HEAD_DIM
    n_blk = SEQ_TILE // w
    pair = 2 * d
    first = pl.program_id(1) == 0

    k_both = jnp.concatenate([kp_ref[...], kc_ref[...]], axis=0).astype(_F32)
    same_half = (_iota((pair, pair), 0) // d == _iota((pair, pair), 1) // d).astype(_BF16)
    k_hi, k_lo = _split_bf16(k_both * k_both)
    k_ssq = _dot(k_hi, same_half) + _dot(k_lo, same_half)
    k_n = k_both * lax.rsqrt(k_ssq * (1.0 / d) + NORM_EPS) * (kg_ref[...] * d ** -0.5)
    v_both = jnp.concatenate([vp_ref[...], vc_ref[...]], axis=0).astype(_F32)
    v_t = [v_both[b * w:(b + 1) * w].T for b in range(n_blk + 1)]

    key = _iota((2 * w, SWA_GROUP * w), 0)
    qry = _iota((2 * w, SWA_GROUP * w), 1) % w
    band = (key > qry) & (key <= qry + w)
    band_first = band & jnp.logical_or(key >= w, jnp.logical_not(first))
    head_of_lane = _iota((1, SWA_GROUP * w), 1) // w
    half_of_lane = _iota((1, pair), 1) // d
    half_rows = (_iota((SUBLANES, pair), 0) == _iota((SUBLANES, pair), 1) // d).astype(_BF16)
    ones_rows = jnp.ones((2 * SUBLANES, 2 * w), _BF16)
    q_gain = qg_ref[...]

    for kk in range(SWA_KV_HEADS):
        k_native = jnp.where(half_of_lane == kk, k_n, 0.0)
        k_moved = pltpu.roll(k_native, d, 1)
        k_even, k_odd = (k_native, k_moved) if kk == 0 else (k_moved, k_native)
        k_even = k_even.astype(_BF16)
        k_odd = k_odd.astype(_BF16)
        sink_row = jnp.zeros((1, SWA_GROUP * w), _F32)
        for g in range(SWA_GROUP):
            sink_row = jnp.where(head_of_lane == g, sinks_ref[kk * SWA_GROUP + g], sink_row)
        blocks = range(n_blk)
        pairs = range(SWA_GROUP // 2)
        q_pairs = [[q_ref[j * w:(j + 1) * w, (kk * SWA_GROUP + 2 * p) * d:(kk * SWA_GROUP + 2 * p + 2) * d].astype(_F32)
                    for p in pairs] for j in blocks]
        q_splits = [[_split_bf16(q * q) for q in qs] for qs in q_pairs]
        q_ssq = [[_dot_nt(half_rows, hi) + _dot_nt(half_rows, lo) for hi, lo in sp] for sp in q_splits]
        q_rs = [[lax.rsqrt(s * (1.0 / d) + NORM_EPS) for s in ss] for ss in q_ssq]
        q_gs = [[(q * q_gain).astype(_BF16) for q in qs] for qs in q_pairs]
        raw_even = [[_dot_nt(k_even[j * w:(j + 2) * w], q_gs[j][p]) for p in pairs] for j in blocks]
        raw_odd = [[_dot_nt(k_odd[j * w:(j + 2) * w], q_gs[j][p]) for p in pairs] for j in blocks]
        score_blocks = []
        for j in blocks:
            raws = []
            for p in pairs:
                raws.append(raw_even[j][p] * q_rs[j][p][0:1, :])
                raws.append(raw_odd[j][p] * q_rs[j][p][1:2, :])
            s_t = jnp.concatenate(raws, axis=1)
            score_blocks.append(jnp.where(band_first if j == 0 else band, s_t, NEG_BIG))
        maxes = [jnp.maximum(jnp.max(s_t, axis=0, keepdims=True), sink_row) for s_t in score_blocks]
        probs = [jnp.exp(s_t - m).astype(_BF16) for s_t, m in zip(score_blocks, maxes)]
        outs = []
        for j in blocks:
            v_ext = jnp.concatenate([v_t[j][kk * d:(kk + 1) * d], v_t[j + 1][kk * d:(kk + 1) * d]], axis=1)
            outs.append(_dot(jnp.concatenate([v_ext.astype(_BF16), ones_rows], axis=0), probs[j]))
        for j in blocks:
            rows = slice(j * w, (j + 1) * w)
            denom = outs[j][d:d + 1, :] + jnp.exp(sink_row - maxes[j])
            o_t = outs[j][0:d, :] * (1.0 / denom)
            for p in range(SWA_GROUP // 2):
                c0 = (kk * SWA_GROUP + 2 * p) * d
                both = jnp.concatenate([o_t[:, 2 * p * w:(2 * p + 1) * w], o_t[:, (2 * p + 1) * w:(2 * p + 2) * w]], axis=0)
                out_ref[rows, c0:c0 + pair] = both.T.astype(_BF16)


def _swa(swa_proj, sinks, q_gain2, k_gain2, batch, seq):
    n = swa_proj.shape[0]
    steps = seq // SEQ_TILE
    blocks_per_step = SEQ_TILE // SWA_WINDOW
    rowmap = lambda b, s: b * steps + s
    prevmap = lambda b, s: jnp.maximum((b * steps + s) * blocks_per_step - 1, 0)
    k_col = SWA_Q_W // SWA_KV_W
    return pl.pallas_call(
        _swa_kernel,
        grid=(batch, steps),
        in_specs=[
            pl.BlockSpec(memory_space=pltpu.SMEM),
            pl.BlockSpec((SEQ_TILE, SWA_Q_W), lambda b, s: (rowmap(b, s), 0)),
            pl.BlockSpec((SEQ_TILE, SWA_KV_W), lambda b, s: (rowmap(b, s), k_col)),
            pl.BlockSpec((SEQ_TILE, SWA_KV_W), lambda b, s: (rowmap(b, s), k_col + 1)),
            pl.BlockSpec((SWA_WINDOW, SWA_KV_W), lambda b, s: (prevmap(b, s), k_col)),
            pl.BlockSpec((SWA_WINDOW, SWA_KV_W), lambda b, s: (prevmap(b, s), k_col + 1)),
            pl.BlockSpec((1, 2 * SWA_HEAD_DIM), lambda b, s: (0, 0)),
            pl.BlockSpec((1, 2 * SWA_HEAD_DIM), lambda b, s: (0, 0)),
        ],
        out_specs=pl.BlockSpec((SEQ_TILE, SWA_Q_W), lambda b, s: (rowmap(b, s), 0)),
        out_shape=jax.ShapeDtypeStruct((n, SWA_Q_W), _BF16),
        compiler_params=pltpu.CompilerParams(
            dimension_semantics=("arbitrary", "arbitrary"), vmem_limit_bytes=VMEM_LIMIT),
        name="swa",
    )(sinks, swa_proj, swa_proj, swa_proj, swa_proj, swa_proj, q_gain2, k_gain2)


def _outproj_kernel(x_ref, retg_ref, attn_ref, ga_ref, gb_ref, wro_ref, wso_ref, wout_ref, g2_ref,
                    wrt_ref, brt_ref, x1_ref, h2_ref, info_ref, info_t_ref, counts_ref):
    tm = ROW_TILE
    y_a = _dot(retg_ref[...], wro_ref[...])
    y_b = _dot(attn_ref[...], wso_ref[...])
    merged = ga_ref[...].astype(_F32) * y_a + gb_ref[...].astype(_F32) * y_b
    x1 = x_ref[...] + _dot(merged.astype(_BF16), wout_ref[...])
    x1_ref[...] = x1
    r = lax.rsqrt(jnp.mean(x1 * x1, axis=-1, keepdims=True) + NORM_EPS)
    h2 = ((x1 * r) * g2_ref[...]).astype(_BF16)
    h2_ref[...] = h2

    logits = _dot_nt(wrt_ref[...], h2) + brt_ref[:, 0:1]
    row = _iota((LANES, tm), 0)
    row_f = row.astype(_F32)
    is_group = (row >= ROUTER_GROUP_LANE) & (row < ROUTER_GROUP_LANE + MOE_GROUPS)
    gl = jnp.where(is_group, logits, NEG_BIG)
    g_max = jnp.max(gl, axis=0, keepdims=True)
    g_prob = 1.0 / jnp.sum(jnp.exp(gl - g_max), axis=0, keepdims=True)
    g_idx = jnp.min(jnp.where(gl == g_max, row_f - ROUTER_GROUP_LANE, float(LANES)), axis=0, keepdims=True)
    e_lo = g_idx * MOE_EXPERTS_PER_GROUP
    in_group = (row_f >= e_lo) & (row_f < e_lo + MOE_EXPERTS_PER_GROUP)
    el = jnp.where(in_group, logits, NEG_BIG)
    t1 = jnp.max(el, axis=0, keepdims=True)
    i1 = jnp.min(jnp.where(el == t1, row_f, float(LANES)), axis=0, keepdims=True)
    el2 = jnp.where(row_f == i1, NEG_BIG, el)
    t2 = jnp.max(el2, axis=0, keepdims=True)
    i2 = jnp.min(jnp.where(el2 == t2, row_f, float(LANES)), axis=0, keepdims=True)
    e21 = jnp.exp(t2 - t1)
    w1 = g_prob / (1.0 + e21)
    w2 = g_prob * e21 / (1.0 + e21)

    sel1 = row_f == i1
    sel2 = row_f == i2
    onehot = jnp.where(sel1 | sel2, 1.0, 0.0).astype(_BF16)
    earlier = (_iota((tm, tm), 0) < _iota((tm, tm), 1)).astype(_BF16)
    before = _dot(onehot, earlier)
    counts = _dot(onehot, jnp.ones((tm, LANES), _BF16))
    run = jnp.floor((counts + (RUN_ALIGN - 1.0)) * (1.0 / RUN_ALIGN)) * RUN_ALIGN
    lower_experts = (_iota((LANES, LANES), 0) > _iota((LANES, LANES), 1)).astype(_BF16)
    run_off = _dot(lower_experts, run.astype(_BF16))
    place = before + jnp.concatenate([run_off] * (tm // LANES), axis=1)
    pos1 = jnp.sum(jnp.where(sel1, place, 0.0), axis=0, keepdims=True)
    pos2 = jnp.sum(jnp.where(sel2, place, 0.0), axis=0, keepdims=True)
    counts_ref[...] = counts.T[0:SUBLANES, :]

    info_t = jnp.concatenate([i1, i2, w1, w2, pos1, pos2, jnp.zeros((2, tm), _F32)], axis=0)
    info_t_ref[...] = info_t
    info_ref[...] = jnp.concatenate([info_t, jnp.zeros((LANES - SUBLANES, tm), _F32)], axis=0).T


def _outproj(x2d, retg, attn, gates, w_ret_o, w_swa_o, w_out, g2, w_router_t, b_router_t):
    n = x2d.shape[0]
    tm = ROW_TILE
    const = lambda i: (0, 0)
    return pl.pallas_call(
        _outproj_kernel,
        grid=(n // tm,),
        in_specs=[
            pl.BlockSpec((tm, D_MODEL), lambda i: (i, 0)),
            pl.BlockSpec((tm, RET_V_W), lambda i: (i, 0)),
            pl.BlockSpec((tm, SWA_Q_W), lambda i: (i, 0)),
            pl.BlockSpec((tm, D_MODEL), lambda i: (i, 0)),
            pl.BlockSpec((tm, D_MODEL), lambda i: (i, 1)),
            pl.BlockSpec((RET_V_W, D_MODEL), const),
            pl.BlockSpec((SWA_Q_W, D_MODEL), const),
            pl.BlockSpec((D_MODEL, D_MODEL), const),
            pl.BlockSpec((1, D_MODEL), const),
            pl.BlockSpec((LANES, D_MODEL), const),
            pl.BlockSpec((LANES, LANES), const),
        ],
        out_specs=[
            pl.BlockSpec((tm, D_MODEL), lambda i: (i, 0)),
            pl.BlockSpec((tm, D_MODEL), lambda i: (i, 0)),
            pl.BlockSpec((tm, LANES), lambda i: (i, 0)),
            pl.BlockSpec((SUBLANES, tm), lambda i: (0, i)),
            pl.BlockSpec((SUBLANES, LANES), lambda i: (i, 0)),
        ],
        out_shape=[
            jax.ShapeDtypeStruct((n, D_MODEL), _F32),
            jax.ShapeDtypeStruct((n, D_MODEL), _BF16),
            jax.ShapeDtypeStruct((n, LANES), _F32),
            jax.ShapeDtypeStruct((SUBLANES, n), _F32),
            jax.ShapeDtypeStruct((n // tm * SUBLANES, LANES), _F32),
        ],
        compiler_params=pltpu.CompilerParams(
            dimension_semantics=("arbitrary",), vmem_limit_bytes=VMEM_LIMIT),
        name="outproj",
    )(x2d, retg, attn, gates, gates, w_ret_o, w_swa_o, w_out, g2, w_router_t, b_router_t)


def _piece_sizes(largest):
    piece = largest
    while piece >= RUN_ALIGN:
        yield piece
        piece //= 2


def _start_run_copies(off_ref, run_ref, dst_ref, tile, make_copy):
    def per_expert(e, carry):
        idx = tile * MOE_N_EXPERTS + e
        off = off_ref[idx]
        dst = dst_ref[idx]
        run = run_ref[idx]
        n_pieces = lax.shift_right_logical(run, RUN_PIECE.bit_length() - 1)

        def per_piece(k, c):
            step = k * RUN_PIECE
            make_copy(pl.multiple_of(off + step, RUN_ALIGN), pl.multiple_of(dst + step, RUN_ALIGN), RUN_PIECE).start()
            return c

        lax.fori_loop(0, n_pieces, per_piece, 0)
        done = n_pieces * RUN_PIECE
        rest = make_copy(pl.multiple_of(off + done, RUN_ALIGN), pl.multiple_of(dst + done, RUN_ALIGN), RUN_ALIGN)
        pl.when(run - done != 0)(rest.start)
        return carry

    lax.fori_loop(0, MOE_N_EXPERTS, per_expert, 0)


def _wait_rows(rows, make_copy):
    for piece in _piece_sizes(SORT_ROWS_POW2 // 2):
        pl.when((rows & piece) != 0)(make_copy(0, 0, piece).wait)


def _zero_fill_padding(pad_start_ref, pad_len_ref, zeros_ref, xs_hbm, sem, wait):
    def per_expert(e, carry):
        pos = pad_start_ref[e]
        length = pad_len_ref[e]
        for piece in _piece_sizes(EXPERT_TILE // 2):
            copy = pltpu.make_async_copy(
                zeros_ref.at[pl.ds(0, piece)], xs_hbm.at[pl.ds(pl.multiple_of(pos, RUN_ALIGN), piece)], sem)
            pl.when((length & piece) != 0)(copy.wait if wait else copy.start)
            pos = pos + (length & piece)
        return carry

    lax.fori_loop(0, MOE_N_EXPERTS, per_expert, 0)

    tail_start = pad_start_ref[MOE_N_EXPERTS]
    n_rows = xs_hbm.shape[0]
    half = EXPERT_TILE // 2

    def per_half_tile(j, carry):
        pos = tail_start + j * half
        copy = pltpu.make_async_copy(
            zeros_ref, xs_hbm.at[pl.ds(pl.multiple_of(jnp.minimum(pos, n_rows - half), RUN_ALIGN), half)], sem)
        pl.when(pos < n_rows)(copy.wait if wait else copy.start)
        return carry

    lax.fori_loop(0, n_rows // half, per_half_tile, 0)


def _dispatch_kernel(off_ref, run_ref, dst_ref, rows_ref, pad_start_ref, pad_len_ref, h2_ref, info_t_ref, xs_hbm,
                     sorted_ref, zeros_ref, sems, zero_sem):
    tile = pl.program_id(0)
    last = pl.num_programs(0) - 1
    slot = tile % 2
    tm = ROW_TILE

    @pl.when(tile == 0)
    def _():
        zeros_ref[...] = jnp.zeros_like(zeros_ref)
        _zero_fill_padding(pad_start_ref, pad_len_ref, zeros_ref, xs_hbm, zero_sem, wait=False)
        _zero_fill_padding(pad_start_ref, pad_len_ref, zeros_ref, xs_hbm, zero_sem, wait=True)

    pos = info_t_ref[4:6, :].astype(jnp.int32)
    place = _iota((SORT_ROWS, tm), 0)
    onehot = ((place == pos[0:1, :]) | (place == pos[1:2, :])).astype(_BF16)
    for c in range(0, D_MODEL, 256):
        sorted_ref[slot, :, c:c + 256] = _dot(onehot, h2_ref[:, c:c + 256]).astype(_BF16)

    def copy_maker(t):
        s = t % 2

        def make_copy(off, dst, piece):
            return pltpu.make_async_copy(
                sorted_ref.at[s, pl.ds(off, piece)], xs_hbm.at[pl.ds(dst, piece)], sems.at[s])

        return make_copy

    _start_run_copies(off_ref, run_ref, dst_ref, tile, copy_maker(tile))

    @pl.when(tile > 0)
    def _():
        _wait_rows(rows_ref[tile - 1], copy_maker(tile - 1))

    @pl.when(tile == last)
    def _():
        _wait_rows(rows_ref[tile], copy_maker(tile))


def _dispatch(plan, h2, info_t, n_rows):
    n = h2.shape[0]
    grid_spec = pltpu.PrefetchScalarGridSpec(
        num_scalar_prefetch=6,
        grid=(n // ROW_TILE,),
        in_specs=[
            pl.BlockSpec((ROW_TILE, D_MODEL), lambda i, *_: (i, 0)),
            pl.BlockSpec((SUBLANES, ROW_TILE), lambda i, *_: (0, i)),
        ],
        out_specs=pl.BlockSpec(memory_space=pl.ANY),
        scratch_shapes=[
            pltpu.VMEM((2, SORT_ROWS, D_MODEL), _BF16),
            pltpu.VMEM((EXPERT_TILE // 2, D_MODEL), _BF16),
            pltpu.SemaphoreType.DMA((2,)),
            pltpu.SemaphoreType.DMA(()),
        ],
    )
    return pl.pallas_call(
        _dispatch_kernel,
        grid_spec=grid_spec,
        out_shape=jax.ShapeDtypeStruct((n_rows, D_MODEL), _BF16),
        compiler_params=pltpu.CompilerParams(
            dimension_semantics=("arbitrary",), vmem_limit_bytes=VMEM_LIMIT),
        name="dispatch",
    )(plan["off"], plan["run"], plan["dst"], plan["rows"], plan["pad_start"], plan["pad_len"], h2, info_t)


def _experts_kernel(te_ref, tv_ref, xs_ref, wg_ref, wu_ref, wd_ref, ys_ref, wg_s, wu_s, wd_s):
    i = pl.program_id(0)
    valid = tv_ref[i]
    prev = te_ref[jnp.maximum(i - 1, 0)]
    changed = jnp.logical_or(i == 0, te_ref[i] != prev)

    @pl.when(jnp.logical_and(valid > 0, changed))
    def _():
        wg_s[...] = wg_ref[...].astype(_BF16)
        wu_s[...] = wu_ref[...].astype(_BF16)
        wd_s[...] = wd_ref[...].astype(_BF16)

    @pl.when(valid > 0)
    def _():
        xb = xs_ref[...]
        gate = _dot(xb, wg_s[...])
        up = _dot(xb, wu_s[...])
        act = ((gate * _sigmoid(gate)) * up).astype(_BF16)
        ys_ref[...] = _dot(act, wd_s[...]).astype(_BF16)

    @pl.when(valid <= 0)
    def _():
        ys_ref[...] = jnp.zeros_like(ys_ref)


def _experts(plan, xs, w_gate, w_up, w_down):
    n_rows = xs.shape[0]
    n_tiles = n_rows // EXPERT_TILE
    grid_spec = pltpu.PrefetchScalarGridSpec(
        num_scalar_prefetch=2,
        grid=(n_tiles,),
        in_specs=[
            pl.BlockSpec((EXPERT_TILE, D_MODEL), lambda i, te, tv: (i, 0)),
            pl.BlockSpec((None, D_MODEL, MOE_D_FF), lambda i, te, tv: (te[i], 0, 0)),
            pl.BlockSpec((None, D_MODEL, MOE_D_FF), lambda i, te, tv: (te[i], 0, 0)),
            pl.BlockSpec((None, MOE_D_FF, D_MODEL), lambda i, te, tv: (te[i], 0, 0)),
        ],
        out_specs=pl.BlockSpec((EXPERT_TILE, D_MODEL), lambda i, te, tv: (i, 0)),
        scratch_shapes=[
            pltpu.VMEM((D_MODEL, MOE_D_FF), _BF16),
            pltpu.VMEM((D_MODEL, MOE_D_FF), _BF16),
            pltpu.VMEM((MOE_D_FF, D_MODEL), _BF16),
        ],
    )
    return pl.pallas_call(
        _experts_kernel,
        grid_spec=grid_spec,
        out_shape=jax.ShapeDtypeStruct((n_rows, D_MODEL), _BF16),
        compiler_params=pltpu.CompilerParams(
            dimension_semantics=("arbitrary",), vmem_limit_bytes=VMEM_LIMIT),
        name="experts",
    )(plan["tile_expert"], plan["tile_valid"], xs, w_gate, w_up, w_down)


def _combine_kernel(off_ref, run_ref, dst_ref, rows_ref, x1_ref, info_ref, ys_hbm, out_ref, block_ref, sems):
    tile = pl.program_id(0)
    last = pl.num_programs(0) - 1
    slot = tile % 2
    tm = ROW_TILE

    def copy_maker(t):
        s = t % 2

        def make_copy(off, dst, piece):
            return pltpu.make_async_copy(
                ys_hbm.at[pl.ds(dst, piece)], block_ref.at[s, pl.ds(off, piece)], sems.at[s])

        return make_copy

    @pl.when(tile == 0)
    def _():
        block_ref[...] = jnp.zeros_like(block_ref)
        _start_run_copies(off_ref, run_ref, dst_ref, tile, copy_maker(tile))

    @pl.when(tile < last)
    def _():
        _start_run_copies(off_ref, run_ref, dst_ref, tile + 1, copy_maker(tile + 1))

    _wait_rows(rows_ref[tile], copy_maker(tile))

    info = info_ref[...]
    place = _iota((tm, SORT_ROWS), 1).astype(_F32)
    weights = (jnp.where(place == info[:, 4:5], info[:, 2:3], 0.0)
               + jnp.where(place == info[:, 5:6], info[:, 3:4], 0.0)).astype(_BF16)
    for c in range(0, D_MODEL, 256):
        out_ref[:, c:c + 256] = x1_ref[:, c:c + 256] + _dot(weights, block_ref[slot, :, c:c + 256])


def _combine(plan, x1, info, ys):
    n = x1.shape[0]
    grid_spec = pltpu.PrefetchScalarGridSpec(
        num_scalar_prefetch=4,
        grid=(n // ROW_TILE,),
        in_specs=[
            pl.BlockSpec((ROW_TILE, D_MODEL), lambda i, *_: (i, 0)),
            pl.BlockSpec((ROW_TILE, LANES), lambda i, *_: (i, 0)),
            pl.BlockSpec(memory_space=pl.ANY),
        ],
        out_specs=pl.BlockSpec((ROW_TILE, D_MODEL), lambda i, *_: (i, 0)),
        scratch_shapes=[pltpu.VMEM((2, SORT_ROWS, D_MODEL), _BF16), pltpu.SemaphoreType.DMA((2,))],
    )
    return pl.pallas_call(
        _combine_kernel,
        grid_spec=grid_spec,
        out_shape=jax.ShapeDtypeStruct((n, D_MODEL), _F32),
        compiler_params=pltpu.CompilerParams(
            dimension_semantics=("arbitrary",), vmem_limit_bytes=VMEM_LIMIT),
        name="combine",
    )(plan["off"], plan["run"], plan["dst"], plan["rows"], x1, info, ys)


def _rotary_tables(seq):
    half = RET_QK_DIM // 2
    inv = ROPE_BASE ** (-jnp.arange(half, dtype=_F32) / half)
    ang = jnp.arange(seq, dtype=jnp.int32).astype(_F32)[:, None] * inv[None, :]
    cos = jnp.cos(ang)
    sin = jnp.sin(ang)
    return jnp.concatenate([cos, cos], axis=1), jnp.concatenate([-sin, sin], axis=1)


def _expert_row_bound(n_tokens):
    rows = (2 * n_tokens + (n_tokens // ROW_TILE) * MOE_N_EXPERTS * (RUN_ALIGN - 1)
            + MOE_N_EXPERTS * (EXPERT_TILE - RUN_ALIGN))
    return -(-rows // EXPERT_TILE) * EXPERT_TILE


def _routing_plan(counts_out, n_rows):
    te = EXPERT_TILE
    n_tok_tiles = counts_out.shape[0] // SUBLANES
    counts = counts_out.reshape(n_tok_tiles, SUBLANES, LANES)[:, 0, :MOE_N_EXPERTS].astype(jnp.int32)
    run = (counts + RUN_ALIGN - 1) // RUN_ALIGN * RUN_ALIGN
    off = jnp.cumsum(run, axis=1) - run
    total = jnp.sum(run, axis=0)
    padded = (total + te - 1) // te * te
    pends = jnp.cumsum(padded)
    pstarts = pends - padded
    dst = pstarts[None, :] + jnp.cumsum(run, axis=0) - run
    tile_start = jnp.arange(n_rows // te, dtype=jnp.int32) * te
    tile_expert = jnp.sum((tile_start[:, None] >= pends[None, :]).astype(jnp.int32), axis=1)
    tile_expert = jnp.minimum(tile_expert, MOE_N_EXPERTS - 1)
    tile_valid = jnp.clip(total[tile_expert] - (tile_start - pstarts[tile_expert]), 0, te)
    i32 = lambda v: v.astype(jnp.int32)
    return {
        "off": i32(off.reshape(-1)), "run": i32(run.reshape(-1)), "dst": i32(dst.reshape(-1)),
        "rows": i32(jnp.sum(run, axis=1)),
        "pad_start": i32(jnp.concatenate([pstarts + total, pends[-1:]])), "pad_len": i32(padded - total),
        "tile_expert": i32(tile_expert), "tile_valid": i32(tile_valid),
    }


def kernel(x, norm_mix_g, w_in, ret_gn_g, w_ret_o, q_norm_g, k_norm_g, sinks, w_swa_o, w_out, norm_ffn_g,
           w_router_group, b_router_group, w_router_expert, b_router_expert, w_gate, w_up, w_down):
    batch, seq, d = x.shape
    n = batch * seq
    assert d == D_MODEL and seq % SEQ_TILE == 0 and n % ROW_TILE == 0
    cos_tab, sin_tab = _rotary_tables(seq)
    n_rows = _expert_row_bound(n)
    for l in range(w_in.shape[0]):
        x2d = x.reshape(n, d)
        ret_proj, swa_proj, gates = _inproj(x2d, norm_mix_g[l][None, :], w_in[l].astype(_BF16), cos_tab, sin_tab,
                                            ret_gn_g[l].reshape(1, RET_V_W))
        retg = _retention(ret_proj, batch, seq)
        attn = _swa(swa_proj, sinks[l], jnp.tile(q_norm_g[l], 2)[None, :], jnp.tile(k_norm_g[l], 2)[None, :],
                    batch, seq)
        pad = LANES - MOE_N_EXPERTS - MOE_GROUPS
        w_router_t = jnp.concatenate(
            [w_router_expert[l].T, w_router_group[l].T, jnp.zeros((pad, d), _F32)], axis=0).astype(_BF16)
        b_router = jnp.concatenate([b_router_expert[l], b_router_group[l], jnp.zeros((pad,), _F32)])
        b_router_t = jnp.broadcast_to(b_router[:, None], (LANES, LANES))
        x1, h2, info, info_t, counts = _outproj(
            x2d, retg, attn, gates, w_ret_o[l].astype(_BF16), w_swa_o[l].astype(_BF16), w_out[l].astype(_BF16),
            norm_ffn_g[l][None, :], w_router_t, b_router_t)
        plan = _routing_plan(counts, n_rows)
        xs = _dispatch(plan, h2, info_t, n_rows)
        ys = _experts(plan, xs, w_gate[l], w_up[l], w_down[l])
        x = _combine(plan, x1, info, ys).reshape(batch, seq, d)
    return x
```

```python
import math

import numpy as np
import jax
import jax.numpy as jnp
from jax import lax
from jax.experimental import pallas as pl
from jax.experimental.pallas import tpu as pltpu

D_MODEL = 1024
RET_HEADS = 4
RET_QK_DIM = 128
RET_V_DIM = 256
RET_CHUNK = 128
ROPE_BASE = 10000.0
SWA_Q_HEADS = 8
SWA_KV_HEADS = 2
SWA_HEAD_DIM = 64
SWA_WINDOW = 128
MOE_GROUPS = 4
MOE_EXPERTS_PER_GROUP = 8
MOE_D_FF = 512
NORM_EPS = 1e-6

RET_QK_W = RET_HEADS * RET_QK_DIM
RET_V_W = RET_HEADS * RET_V_DIM
SWA_Q_W = SWA_Q_HEADS * SWA_HEAD_DIM
SWA_KV_W = SWA_KV_HEADS * SWA_HEAD_DIM
SWA_GROUP = SWA_Q_HEADS // SWA_KV_HEADS
MOE_N_EXPERTS = MOE_GROUPS * MOE_EXPERTS_PER_GROUP

RET_COLS = 2 * RET_QK_W + 2 * RET_V_W
SWA_COLS = SWA_Q_W + 2 * SWA_KV_W
GATE_COLS = 2 * D_MODEL
IN_WIDTH = RET_COLS + SWA_COLS + GATE_COLS

LANES = 128
SUBLANES = 8
ROUTER_GROUP_LANE = 32
NEG_BIG = -1e30

ROW_TILE = 512
SEQ_TILE = 512
EXPERT_TILE = 512
OUTPROJ_SUB = 128
RUN_ALIGN = 16
SORT_ROWS = 2 * ROW_TILE + MOE_N_EXPERTS * RUN_ALIGN
SORT_ROWS_POW2 = 1 << SORT_ROWS.bit_length()
RUN_PIECE = 2 * RUN_ALIGN
VMEM_LIMIT = 56 * 1024 * 1024

RET_LOG_DECAY = tuple(float(np.log1p(-np.exp2(-5.0 - h))) for h in range(RET_HEADS))

_BF16 = jnp.bfloat16
_F32 = jnp.float32


def _dot(a, b):
    return jnp.dot(a, b, preferred_element_type=_F32)


def _dot_nt(a, b):
    return lax.dot_general(a, b, (((1,), (1,)), ((), ())), preferred_element_type=_F32)


def _sigmoid(v):
    return 1.0 / (1.0 + jnp.exp(-v))


def _split_bf16(v):
    hi = v.astype(_BF16)
    return hi, (v - hi.astype(_F32)).astype(_BF16)


def _iota(shape, dim):
    return lax.broadcasted_iota(jnp.int32, shape, dim)


def _inproj_kernel(x_ref, g_ref, w_ref, cos_ref, sin_ref, gn_ref, ret_ref, swa_ref, gate_ref):
    x = x_ref[...]
    r = lax.rsqrt(jnp.mean(x * x, axis=-1, keepdims=True) + NORM_EPS)
    h = ((x * r) * g_ref[...]).astype(_BF16)
    cos = cos_ref[...]
    sin = sin_ref[...]
    chunk_pos = (_iota((ROW_TILE, RET_QK_DIM), 0) % RET_CHUNK + 1).astype(_F32)

    def rotary_decay(base, sign, scale):
        def epilogue(raw):
            for hd in range(RET_HEADS):
                v = raw[:, hd * RET_QK_DIM:(hd + 1) * RET_QK_DIM]
                rot = v * cos + pltpu.roll(v, RET_QK_DIM // 2, 1) * sin
                decay = jnp.exp((sign * RET_LOG_DECAY[hd]) * chunk_pos) * scale
                ret_ref[:, base + hd * RET_QK_DIM:base + (hd + 1) * RET_QK_DIM] = (rot * decay).astype(_BF16)
        return epilogue

    def store(ref, lo, fn):
        def epilogue(raw):
            ref[:, lo:lo + raw.shape[1]] = fn(raw).astype(_BF16)
        return epilogue

    v_base = 2 * RET_QK_W
    g_base = v_base + RET_V_W
    gate_base = RET_COLS + SWA_COLS
    jobs = [(0, RET_QK_W, rotary_decay(0, 1.0, 1.0)),
            (RET_QK_W, RET_QK_W, rotary_decay(RET_QK_W, -1.0, RET_QK_DIM ** -0.5))]
    jobs += [(v_base + c, 512, store(ret_ref, v_base + c, lambda raw: raw)) for c in range(0, RET_V_W, 512)]
    jobs += [(g_base + c, 512, store(ret_ref, g_base + c,
                                     lambda raw, c=c: (raw * _sigmoid(raw)) * gn_ref[:, c:c + 512]))
             for c in range(0, RET_V_W, 512)]
    jobs += [(RET_COLS + c, 256, store(swa_ref, c, lambda raw: raw)) for c in range(0, SWA_COLS, 256)]
    jobs += [(gate_base + c, 512, store(gate_ref, c, _sigmoid)) for c in range(0, GATE_COLS, 512)]
    pending = None
    for lo, width, epilogue in jobs:
        raw = _dot(h, w_ref[:, lo:lo + width])
        if pending is not None:
            pending[0](pending[1])
        pending = (epilogue, raw)
    pending[0](pending[1])


def _inproj(x2d, g, w_in_bf16, cos_tab, sin_tab, gn_row):
    n = x2d.shape[0]
    grid = (n // ROW_TILE,)
    seq_tiles = cos_tab.shape[0] // ROW_TILE
    return pl.pallas_call(
        _inproj_kernel,
        grid=grid,
        in_specs=[
            pl.BlockSpec((ROW_TILE, D_MODEL), lambda i: (i, 0)),
            pl.BlockSpec((1, D_MODEL), lambda i: (0, 0)),
            pl.BlockSpec((D_MODEL, IN_WIDTH), lambda i: (0, 0)),
            pl.BlockSpec((ROW_TILE, RET_QK_DIM), lambda i: (i % seq_tiles, 0)),
            pl.BlockSpec((ROW_TILE, RET_QK_DIM), lambda i: (i % seq_tiles, 0)),
            pl.BlockSpec((1, RET_V_W), lambda i: (0, 0)),
        ],
        out_specs=[
            pl.BlockSpec((ROW_TILE, RET_COLS), lambda i: (i, 0)),
            pl.BlockSpec((ROW_TILE, SWA_COLS), lambda i: (i, 0)),
            pl.BlockSpec((ROW_TILE, GATE_COLS), lambda i: (i, 0)),
        ],
        out_shape=[
            jax.ShapeDtypeStruct((n, RET_COLS), _BF16),
            jax.ShapeDtypeStruct((n, SWA_COLS), _BF16),
            jax.ShapeDtypeStruct((n, GATE_COLS), _BF16),
        ],
        compiler_params=pltpu.CompilerParams(
            dimension_semantics=("arbitrary",), vmem_limit_bytes=VMEM_LIMIT),
        name="inproj",
    )(x2d, g, w_in_bf16, cos_tab, sin_tab, gn_row)


def _retention_kernel(q_ref, k_ref, v_ref, g_ref, out_ref, state_ref):
    c = RET_CHUNK

    @pl.when(pl.program_id(1) == 0)
    def _():
        state_ref[...] = jnp.zeros_like(state_ref)

    causal = _iota((c, c), 0) >= _iota((c, c), 1)

    for h in range(RET_HEADS):
        chunk_decay = math.exp(RET_LOG_DECAY[h] * c)
        qs = slice(h * RET_QK_DIM, (h + 1) * RET_QK_DIM)
        vs = slice(h * RET_V_DIM, (h + 1) * RET_V_DIM)
        chunks = [slice(ci * c, (ci + 1) * c) for ci in range(SEQ_TILE // c)]
        scores = [jnp.where(causal, _dot_nt(q_ref[rows, qs], k_ref[rows, qs]), 0.0).astype(_BF16) for rows in chunks]
        kvs = [_dot(k_ref[rows, qs].astype(_F32).T.astype(_BF16), v_ref[rows, vs]) for rows in chunks]
        state = state_ref[h]
        states = []
        for kv in kvs:
            states.append(state.astype(_BF16))
            state = chunk_decay * (state + kv)
        state_ref[h] = state
        rets = [_dot(jnp.concatenate([s, q_ref[rows, qs]], axis=1), jnp.concatenate([v_ref[rows, vs], st], axis=0))
                for rows, s, st in zip(chunks, scores, states)]
        for rows, ret in zip(chunks, rets):
            mu = jnp.mean(ret, axis=-1, keepdims=True)
            dev = ret - mu
            var = jnp.mean(dev * dev, axis=-1, keepdims=True)
            out_ref[rows, vs] = ((dev * lax.rsqrt(var + NORM_EPS)) * g_ref[rows, vs].astype(_F32)).astype(_BF16)


def _retention(ret_proj, batch, seq):
    n = ret_proj.shape[0]
    steps = seq // SEQ_TILE
    rowmap = lambda b, s: b * steps + s
    return pl.pallas_call(
        _retention_kernel,
        grid=(batch, steps),
        in_specs=[
            pl.BlockSpec((SEQ_TILE, RET_QK_W), lambda b, s: (rowmap(b, s), 0)),
            pl.BlockSpec((SEQ_TILE, RET_QK_W), lambda b, s: (rowmap(b, s), 1)),
            pl.BlockSpec((SEQ_TILE, RET_V_W), lambda b, s: (rowmap(b, s), 1)),
            pl.BlockSpec((SEQ_TILE, RET_V_W), lambda b, s: (rowmap(b, s), 2)),
        ],
        out_specs=pl.BlockSpec((SEQ_TILE, RET_V_W), lambda b, s: (rowmap(b, s), 0)),
        out_shape=jax.ShapeDtypeStruct((n, RET_V_W), _BF16),
        scratch_shapes=[pltpu.VMEM((RET_HEADS, RET_QK_DIM, RET_V_DIM), _F32)],
        compiler_params=pltpu.CompilerParams(
            dimension_semantics=("arbitrary", "arbitrary"), vmem_limit_bytes=VMEM_LIMIT),
        name="retention",
    )(ret_proj, ret_proj, ret_proj, ret_proj)


def _swa_kernel(sinks_ref, q_ref, kc_ref, vc_ref, kp_ref, vp_ref, qg_ref, kg_ref, out_ref):
    w = SWA_WINDOW
    d = SWA_HEAD_DIM
    n_blk = SEQ_TILE // w
    pair = 2 * d
    first = pl.program_id(1) == 0

    k_both = jnp.concatenate([kp_ref[...], kc_ref[...]], axis=0).astype(_F32)
    same_half = (_iota((pair, pair), 0) // d == _iota((pair, pair), 1) // d).astype(_BF16)
    k_hi, k_lo = _split_bf16(k_both * k_both)
    k_ssq = _dot(k_hi, same_half) + _dot(k_lo, same_half)
    k_n = k_both * lax.rsqrt(k_ssq * (1.0 / d) + NORM_EPS) * (kg_ref[...] * d ** -0.5)
    v_both = jnp.concatenate([vp_ref[...], vc_ref[...]], axis=0).astype(_F32)
    v_t = [v_both[b * w:(b + 1) * w].T for b in range(n_blk + 1)]

    key = _iota((2 * w, SWA_GROUP * w), 0)
    qry = _iota((2 * w, SWA_GROUP * w), 1) % w
    band = (key > qry) & (key <= qry + w)
    band_first = band & jnp.logical_or(key >= w, jnp.logical_not(first))
    head_of_lane = _iota((1, SWA_GROUP * w), 1) // w
    half_of_lane = _iota((1, pair), 1) // d
    half_rows = (_iota((SUBLANES, pair), 0) == _iota((SUBLANES, pair), 1) // d).astype(_BF16)
    ones_rows = jnp.ones((2 * SUBLANES, 2 * w), _BF16)
    q_gain = qg_ref[...]

    for kk in range(SWA_KV_HEADS):
        k_native = jnp.where(half_of_lane == kk, k_n, 0.0)
        k_moved = pltpu.roll(k_native, d, 1)
        k_even, k_odd = (k_native, k_moved) if kk == 0 else (k_moved, k_native)
        k_even = k_even.astype(_BF16)
        k_odd = k_odd.astype(_BF16)
        sink_row = jnp.zeros((1, SWA_GROUP * w), _F32)
        for g in range(SWA_GROUP):
            sink_row = jnp.where(head_of_lane == g, sinks_ref[kk * SWA_GROUP + g], sink_row)
        blocks = range(n_blk)
        pairs = range(SWA_GROUP // 2)
        q_pairs = [[q_ref[j * w:(j + 1) * w, (kk * SWA_GROUP + 2 * p) * d:(kk * SWA_GROUP + 2 * p + 2) * d].astype(_F32)
                    for p in pairs] for j in blocks]
        q_splits = [[_split_bf16(q * q) for q in qs] for qs in q_pairs]
        q_ssq = [[_dot_nt(half_rows, hi) + _dot_nt(half_rows, lo) for hi, lo in sp] for sp in q_splits]
        q_rs = [[lax.rsqrt(s * (1.0 / d) + NORM_EPS) for s in ss] for ss in q_ssq]
        q_gs = [[(q * q_gain).astype(_BF16) for q in qs] for qs in q_pairs]
        raw_even = [[_dot_nt(k_even[j * w:(j + 2) * w], q_gs[j][p]) for p in pairs] for j in blocks]
        raw_odd = [[_dot_nt(k_odd[j * w:(j + 2) * w], q_gs[j][p]) for p in pairs] for j in blocks]
        score_blocks = []
        for j in blocks:
            raws = []
            for p in pairs:
                raws.append(raw_even[j][p] * q_rs[j][p][0:1, :])
                raws.append(raw_odd[j][p] * q_rs[j][p][1:2, :])
            s_t = jnp.concatenate(raws, axis=1)
            score_blocks.append(jnp.where(band_first if j == 0 else band, s_t, NEG_BIG))
        maxes = [jnp.maximum(jnp.max(s_t, axis=0, keepdims=True), sink_row) for s_t in score_blocks]
        probs = [jnp.exp(s_t - m).astype(_BF16) for s_t, m in zip(score_blocks, maxes)]
        outs = []
        for j in blocks:
            v_ext = jnp.concatenate([v_t[j][kk * d:(kk + 1) * d], v_t[j + 1][kk * d:(kk + 1) * d]], axis=1)
            outs.append(_dot(jnp.concatenate([v_ext.astype(_BF16), ones_rows], axis=0), probs[j]))
        for j in blocks:
            rows = slice(j * w, (j + 1) * w)
            denom = outs[j][d:d + 1, :] + jnp.exp(sink_row - maxes[j])
            o_t = outs[j][0:d, :] * (1.0 / denom)
            for p in range(SWA_GROUP // 2):
                c0 = (kk * SWA_GROUP + 2 * p) * d
                both = jnp.concatenate([o_t[:, 2 * p * w:(2 * p + 1) * w], o_t[:, (2 * p + 1) * w:(2 * p + 2) * w]], axis=0)
                out_ref[rows, c0:c0 + pair] = both.T.astype(_BF16)


def _swa(swa_proj, sinks, q_gain2, k_gain2, batch, seq):
    n = swa_proj.shape[0]
    steps = seq // SEQ_TILE
    blocks_per_step = SEQ_TILE // SWA_WINDOW
    rowmap = lambda b, s: b * steps + s
    prevmap = lambda b, s: jnp.maximum((b * steps + s) * blocks_per_step - 1, 0)
    k_col = SWA_Q_W // SWA_KV_W
    return pl.pallas_call(
        _swa_kernel,
        grid=(batch, steps),
        in_specs=[
            pl.BlockSpec(memory_space=pltpu.SMEM),
            pl.BlockSpec((SEQ_TILE, SWA_Q_W), lambda b, s: (rowmap(b, s), 0)),
            pl.BlockSpec((SEQ_TILE, SWA_KV_W), lambda b, s: (rowmap(b, s), k_col)),
            pl.BlockSpec((SEQ_TILE, SWA_KV_W), lambda b, s: (rowmap(b, s), k_col + 1)),
            pl.BlockSpec((SWA_WINDOW, SWA_KV_W), lambda b, s: (prevmap(b, s), k_col)),
            pl.BlockSpec((SWA_WINDOW, SWA_KV_W), lambda b, s: (prevmap(b, s), k_col + 1)),
            pl.BlockSpec((1, 2 * SWA_HEAD_DIM), lambda b, s: (0, 0)),
            pl.BlockSpec((1, 2 * SWA_HEAD_DIM), lambda b, s: (0, 0)),
        ],
        out_specs=pl.BlockSpec((SEQ_TILE, SWA_Q_W), lambda b, s: (rowmap(b, s), 0)),
        out_shape=jax.ShapeDtypeStruct((n, SWA_Q_W), _BF16),
        compiler_params=pltpu.CompilerParams(
            dimension_semantics=("arbitrary", "arbitrary"), vmem_limit_bytes=VMEM_LIMIT),
        name="swa",
    )(sinks, swa_proj, swa_proj, swa_proj, swa_proj, swa_proj, q_gain2, k_gain2)


def _outproj_kernel(x_ref, retg_ref, attn_ref, ga_ref, gb_ref, wro_ref, wso_ref, wout_ref, g2_ref,
                    wrt_ref, brt_ref, x1_ref, h2_ref, info_ref, info_t_ref, counts_ref, logits_ref):
    tm = ROW_TILE
    step = pl.program_id(0)
    cur = step % 2

    @pl.when(step == 0)
    def _():
        logits_ref[...] = jnp.zeros_like(logits_ref)

    logits = logits_ref[1 - cur]
    row = _iota((LANES, tm), 0)
    row_f = row.astype(_F32)
    is_group = (row >= ROUTER_GROUP_LANE) & (row < ROUTER_GROUP_LANE + MOE_GROUPS)

    subs = [slice(s * OUTPROJ_SUB, (s + 1) * OUTPROJ_SUB) for s in range(tm // OUTPROJ_SUB)]
    y_a = [_dot(retg_ref[rows, :], wro_ref[...]) for rows in subs]

    gl = jnp.where(is_group, logits, NEG_BIG)
    g_max = jnp.max(gl, axis=0, keepdims=True)
    g_prob = 1.0 / jnp.sum(jnp.exp(gl - g_max), axis=0, keepdims=True)
    g_idx = jnp.min(jnp.where(gl == g_max, row_f - ROUTER_GROUP_LANE, float(LANES)), axis=0, keepdims=True)

    y_b = [_dot(attn_ref[rows, :], wso_ref[...]) for rows in subs]

    e_lo = g_idx * MOE_EXPERTS_PER_GROUP
    in_group = (row_f >= e_lo) & (row_f < e_lo + MOE_EXPERTS_PER_GROUP)
    el = jnp.where(in_group, logits, NEG_BIG)
    t1 = jnp.max(el, axis=0, keepdims=True)
    i1 = jnp.min(jnp.where(el == t1, row_f, float(LANES)), axis=0, keepdims=True)

    merged = [(ga_ref[rows, :].astype(_F32) * a + gb_ref[rows, :].astype(_F32) * b).astype(_BF16)
              for rows, a, b in zip(subs, y_a, y_b)]
    mixed = [_dot(m, wout_ref[...]) for m in merged]
    x1 = [x_ref[rows, :] + m for rows, m in zip(subs, mixed)]
    for rows, v in zip(subs, x1):
        x1_ref[rows, :] = v

    el2 = jnp.where(row_f == i1, NEG_BIG, el)
    t2 = jnp.max(el2, axis=0, keepdims=True)
    i2 = jnp.min(jnp.where(el2 == t2, row_f, float(LANES)), axis=0, keepdims=True)
    e21 = jnp.exp(t2 - t1)
    w1 = g_prob / (1.0 + e21)
    w2 = g_prob * e21 / (1.0 + e21)

    h2 = [((v * lax.rsqrt(jnp.mean(v * v, axis=-1, keepdims=True) + NORM_EPS)) * g2_ref[...]).astype(_BF16)
          for v in x1]
    for rows, v in zip(subs, h2):
        h2_ref[rows, :] = v

    sel1 = row_f == i1
    sel2 = row_f == i2
    onehot = jnp.where(sel1 | sel2, 1.0, 0.0).astype(_BF16)
    earlier = (_iota((tm, tm), 0) < _iota((tm, tm), 1)).astype(_BF16)
    before = _dot(onehot, earlier)
    counts = _dot(onehot, jnp.ones((tm, LANES), _BF16))

    for rows, v in zip(subs, h2):
        logits_ref[cur, :, rows] = _dot_nt(wrt_ref[...], v) + brt_ref[:, 0:1]

    run = jnp.floor((counts + (RUN_ALIGN - 1.0)) * (1.0 / RUN_ALIGN)) * RUN_ALIGN
    lower_experts = (_iota((LANES, LANES), 0) > _iota((LANES, LANES), 1)).astype(_BF16)
    run_off = _dot(lower_experts, run.astype(_BF16))
    place = before + jnp.concatenate([run_off] * (tm // LANES), axis=1)
    pos1 = jnp.sum(jnp.where(sel1, place, 0.0), axis=0, keepdims=True)
    pos2 = jnp.sum(jnp.where(sel2, place, 0.0), axis=0, keepdims=True)
    counts_ref[...] = counts.T[0:SUBLANES, :]

    info_t = jnp.concatenate([i1, i2, w1, w2, pos1, pos2, jnp.zeros((2, tm), _F32)], axis=0)
    info_t_ref[...] = info_t
    info_ref[...] = jnp.concatenate([info_t, jnp.zeros((LANES - SUBLANES, tm), _F32)], axis=0).T


def _outproj(x2d, retg, attn, gates, w_ret_o, w_swa_o, w_out, g2, w_router_t, b_router_t):
    n = x2d.shape[0]
    tm = ROW_TILE
    n_tiles = n // tm
    const = lambda s: (0, 0)
    proj = lambda s: jnp.minimum(s, n_tiles - 1)
    routed = lambda s: jnp.maximum(s - 1, 0)
    return pl.pallas_call(
        _outproj_kernel,
        grid=(n_tiles + 1,),
        in_specs=[
            pl.BlockSpec((tm, D_MODEL), lambda s: (proj(s), 0)),
            pl.BlockSpec((tm, RET_V_W), lambda s: (proj(s), 0)),
            pl.BlockSpec((tm, SWA_Q_W), lambda s: (proj(s), 0)),
            pl.BlockSpec((tm, D_MODEL), lambda s: (proj(s), 0)),
            pl.BlockSpec((tm, D_MODEL), lambda s: (proj(s), 1)),
            pl.BlockSpec((RET_V_W, D_MODEL), const),
            pl.BlockSpec((SWA_Q_W, D_MODEL), const),
            pl.BlockSpec((D_MODEL, D_MODEL), const),
            pl.BlockSpec((1, D_MODEL), const),
            pl.BlockSpec((LANES, D_MODEL), const),
            pl.BlockSpec((LANES, LANES), const),
        ],
        out_specs=[
            pl.BlockSpec((tm, D_MODEL), lambda s: (proj(s), 0)),
            pl.BlockSpec((tm, D_MODEL), lambda s: (proj(s), 0)),
            pl.BlockSpec((tm, LANES), lambda s: (routed(s), 0)),
            pl.BlockSpec((SUBLANES, tm), lambda s: (0, routed(s))),
            pl.BlockSpec((SUBLANES, LANES), lambda s: (routed(s), 0)),
        ],
        scratch_shapes=[pltpu.VMEM((2, LANES, tm), _F32)],
        out_shape=[
            jax.ShapeDtypeStruct((n, D_MODEL), _F32),
            jax.ShapeDtypeStruct((n, D_MODEL), _BF16),
            jax.ShapeDtypeStruct((n, LANES), _F32),
            jax.ShapeDtypeStruct((SUBLANES, n), _F32),
            jax.ShapeDtypeStruct((n // tm * SUBLANES, LANES), _F32),
        ],
        compiler_params=pltpu.CompilerParams(
            dimension_semantics=("arbitrary",), vmem_limit_bytes=VMEM_LIMIT),
        name="outproj",
    )(x2d, retg, attn, gates, gates, w_ret_o, w_swa_o, w_out, g2, w_router_t, b_router_t)


def _piece_sizes(largest):
    piece = largest
    while piece >= RUN_ALIGN:
        yield piece
        piece //= 2


BIG_CAP = SORT_ROWS // RUN_PIECE
SMALL_CAP = MOE_N_EXPERTS
PIECE_LIST_KEYS = ("n_big", "big_off", "big_dst", "n_small", "small_off", "small_dst")


def _start_run_copies(pieces, tile, make_copy):
    n_big_ref, big_off_ref, big_dst_ref, n_small_ref, small_off_ref, small_dst_ref = pieces

    def big(k, carry):
        i = tile * BIG_CAP + k
        make_copy(pl.multiple_of(big_off_ref[i], RUN_ALIGN), pl.multiple_of(big_dst_ref[i], RUN_ALIGN),
                  RUN_PIECE).start()
        return carry

    lax.fori_loop(0, n_big_ref[tile], big, 0)

    def small(k, carry):
        i = tile * SMALL_CAP + k
        make_copy(pl.multiple_of(small_off_ref[i], RUN_ALIGN), pl.multiple_of(small_dst_ref[i], RUN_ALIGN),
                  RUN_ALIGN).start()
        return carry

    lax.fori_loop(0, n_small_ref[tile], small, 0)


def _wait_rows(rows, make_copy):
    for piece in _piece_sizes(SORT_ROWS_POW2 // 2):
        pl.when((rows & piece) != 0)(make_copy(0, 0, piece).wait)


def _zero_fill_padding(pad_start_ref, pad_len_ref, zeros_ref, xs_hbm, sem, wait):
    def per_expert(e, carry):
        pos = pad_start_ref[e]
        length = pad_len_ref[e]
        for piece in _piece_sizes(EXPERT_TILE // 2):
            copy = pltpu.make_async_copy(
                zeros_ref.at[pl.ds(0, piece)], xs_hbm.at[pl.ds(pl.multiple_of(pos, RUN_ALIGN), piece)], sem)
            pl.when((length & piece) != 0)(copy.wait if wait else copy.start)
            pos = pos + (length & piece)
        return carry

    lax.fori_loop(0, MOE_N_EXPERTS, per_expert, 0)

    tail_start = pad_start_ref[MOE_N_EXPERTS]
    n_rows = xs_hbm.shape[0]
    half = EXPERT_TILE // 2

    def per_half_tile(j, carry):
        pos = tail_start + j * half
        copy = pltpu.make_async_copy(
            zeros_ref, xs_hbm.at[pl.ds(pl.multiple_of(jnp.minimum(pos, n_rows - half), RUN_ALIGN), half)], sem)
        pl.when(pos < n_rows)(copy.wait if wait else copy.start)
        return carry

    lax.fori_loop(0, n_rows // half, per_half_tile, 0)


def _dispatch_kernel(*refs):
    pieces = refs[:len(PIECE_LIST_KEYS)]
    rows_ref, pad_start_ref, pad_len_ref, h2_ref, info_t_ref, xs_hbm, sorted_ref, zeros_ref, sems, zero_sem = (
        refs[len(PIECE_LIST_KEYS):])
    tile = pl.program_id(0)
    last = pl.num_programs(0) - 1
    slot = tile % 2
    tm = ROW_TILE

    @pl.when(tile == 0)
    def _():
        zeros_ref[...] = jnp.zeros_like(zeros_ref)
        _zero_fill_padding(pad_start_ref, pad_len_ref, zeros_ref, xs_hbm, zero_sem, wait=False)
        _zero_fill_padding(pad_start_ref, pad_len_ref, zeros_ref, xs_hbm, zero_sem, wait=True)

    pos = info_t_ref[4:6, :].astype(jnp.int32)
    place = _iota((SORT_ROWS, tm), 0)
    onehot = ((place == pos[0:1, :]) | (place == pos[1:2, :])).astype(_BF16)
    for c in range(0, D_MODEL, 256):
        sorted_ref[slot, :, c:c + 256] = _dot(onehot, h2_ref[:, c:c + 256]).astype(_BF16)

    def copy_maker(t):
        s = t % 2

        def make_copy(off, dst, piece):
            return pltpu.make_async_copy(
                sorted_ref.at[s, pl.ds(off, piece)], xs_hbm.at[pl.ds(dst, piece)], sems.at[s])

        return make_copy

    _start_run_copies(pieces, tile, copy_maker(tile))

    @pl.when(tile > 0)
    def _():
        _wait_rows(rows_ref[tile - 1], copy_maker(tile - 1))

    @pl.when(tile == last)
    def _():
        _wait_rows(rows_ref[tile], copy_maker(tile))


def _dispatch(plan, h2, info_t, n_rows):
    n = h2.shape[0]
    grid_spec = pltpu.PrefetchScalarGridSpec(
        num_scalar_prefetch=len(PIECE_LIST_KEYS) + 3,
        grid=(n // ROW_TILE,),
        in_specs=[
            pl.BlockSpec((ROW_TILE, D_MODEL), lambda i, *_: (i, 0)),
            pl.BlockSpec((SUBLANES, ROW_TILE), lambda i, *_: (0, i)),
        ],
        out_specs=pl.BlockSpec(memory_space=pl.ANY),
        scratch_shapes=[
            pltpu.VMEM((2, SORT_ROWS, D_MODEL), _BF16),
            pltpu.VMEM((EXPERT_TILE // 2, D_MODEL), _BF16),
            pltpu.SemaphoreType.DMA((2,)),
            pltpu.SemaphoreType.DMA(()),
        ],
    )
    return pl.pallas_call(
        _dispatch_kernel,
        grid_spec=grid_spec,
        out_shape=jax.ShapeDtypeStruct((n_rows, D_MODEL), _BF16),
        compiler_params=pltpu.CompilerParams(
            dimension_semantics=("arbitrary",), vmem_limit_bytes=VMEM_LIMIT),
        name="dispatch",
    )(*[plan[k] for k in PIECE_LIST_KEYS], plan["rows"], plan["pad_start"], plan["pad_len"], h2, info_t)


def _experts_kernel(te_ref, tv_ref, xs_ref, wg_ref, wu_ref, wd_ref, ys_ref, wg_s, wu_s, wd_s):
    i = pl.program_id(0)
    valid = tv_ref[i]
    prev = te_ref[jnp.maximum(i - 1, 0)]
    changed = jnp.logical_or(i == 0, te_ref[i] != prev)

    @pl.when(jnp.logical_and(valid > 0, changed))
    def _():
        wg_s[...] = wg_ref[...].astype(_BF16)
        wu_s[...] = wu_ref[...].astype(_BF16)
        wd_s[...] = wd_ref[...].astype(_BF16)

    @pl.when(valid > 0)
    def _():
        xb = xs_ref[...]
        gate = _dot(xb, wg_s[...])
        up = _dot(xb, wu_s[...])
        act = ((gate * _sigmoid(gate)) * up).astype(_BF16)
        ys_ref[...] = _dot(act, wd_s[...]).astype(_BF16)

    @pl.when(valid <= 0)
    def _():
        ys_ref[...] = jnp.zeros_like(ys_ref)


def _experts(plan, xs, w_gate, w_up, w_down):
    n_rows = xs.shape[0]
    n_tiles = n_rows // EXPERT_TILE
    grid_spec = pltpu.PrefetchScalarGridSpec(
        num_scalar_prefetch=2,
        grid=(n_tiles,),
        in_specs=[
            pl.BlockSpec((EXPERT_TILE, D_MODEL), lambda i, te, tv: (i, 0)),
            pl.BlockSpec((None, D_MODEL, MOE_D_FF), lambda i, te, tv: (te[i], 0, 0)),
            pl.BlockSpec((None, D_MODEL, MOE_D_FF), lambda i, te, tv: (te[i], 0, 0)),
            pl.BlockSpec((None, MOE_D_FF, D_MODEL), lambda i, te, tv: (te[i], 0, 0)),
        ],
        out_specs=pl.BlockSpec((EXPERT_TILE, D_MODEL), lambda i, te, tv: (i, 0)),
        scratch_shapes=[
            pltpu.VMEM((D_MODEL, MOE_D_FF), _BF16),
            pltpu.VMEM((D_MODEL, MOE_D_FF), _BF16),
            pltpu.VMEM((MOE_D_FF, D_MODEL), _BF16),
        ],
    )
    return pl.pallas_call(
        _experts_kernel,
        grid_spec=grid_spec,
        out_shape=jax.ShapeDtypeStruct((n_rows, D_MODEL), _BF16),
        compiler_params=pltpu.CompilerParams(
            dimension_semantics=("arbitrary",), vmem_limit_bytes=VMEM_LIMIT),
        name="experts",
    )(plan["tile_expert"], plan["tile_valid"], xs, w_gate, w_up, w_down)


def _combine_kernel(*refs):
    pieces = refs[:len(PIECE_LIST_KEYS)]
    rows_ref, x1_ref, info_ref, ys_hbm, out_ref, block_ref, sems = refs[len(PIECE_LIST_KEYS):]
    tile = pl.program_id(0)
    last = pl.num_programs(0) - 1
    slot = tile % 2
    tm = ROW_TILE

    def copy_maker(t):
        s = t % 2

        def make_copy(off, dst, piece):
            return pltpu.make_async_copy(
                ys_hbm.at[pl.ds(dst, piece)], block_ref.at[s, pl.ds(off, piece)], sems.at[s])

        return make_copy

    @pl.when(tile == 0)
    def _():
        block_ref[...] = jnp.zeros_like(block_ref)
        _start_run_copies(pieces, tile, copy_maker(tile))

    @pl.when(tile < last)
    def _():
        _start_run_copies(pieces, tile + 1, copy_maker(tile + 1))

    _wait_rows(rows_ref[tile], copy_maker(tile))

    info = info_ref[...]
    place = _iota((tm, SORT_ROWS), 1).astype(_F32)
    weights = (jnp.where(place == info[:, 4:5], info[:, 2:3], 0.0)
               + jnp.where(place == info[:, 5:6], info[:, 3:4], 0.0)).astype(_BF16)
    for c in range(0, D_MODEL, 256):
        out_ref[:, c:c + 256] = x1_ref[:, c:c + 256] + _dot(weights, block_ref[slot, :, c:c + 256])


def _combine(plan, x1, info, ys):
    n = x1.shape[0]
    grid_spec = pltpu.PrefetchScalarGridSpec(
        num_scalar_prefetch=len(PIECE_LIST_KEYS) + 1,
        grid=(n // ROW_TILE,),
        in_specs=[
            pl.BlockSpec((ROW_TILE, D_MODEL), lambda i, *_: (i, 0)),
            pl.BlockSpec((ROW_TILE, LANES), lambda i, *_: (i, 0)),
            pl.BlockSpec(memory_space=pl.ANY),
        ],
        out_specs=pl.BlockSpec((ROW_TILE, D_MODEL), lambda i, *_: (i, 0)),
        scratch_shapes=[pltpu.VMEM((2, SORT_ROWS, D_MODEL), _BF16), pltpu.SemaphoreType.DMA((2,))],
    )
    return pl.pallas_call(
        _combine_kernel,
        grid_spec=grid_spec,
        out_shape=jax.ShapeDtypeStruct((n, D_MODEL), _F32),
        compiler_params=pltpu.CompilerParams(
            dimension_semantics=("arbitrary",), vmem_limit_bytes=VMEM_LIMIT),
        name="combine",
    )(*[plan[k] for k in PIECE_LIST_KEYS], plan["rows"], x1, info, ys)


def _rotary_tables(seq):
    half = RET_QK_DIM // 2
    inv = ROPE_BASE ** (-jnp.arange(half, dtype=_F32) / half)
    hi = (jnp.arange(seq // RET_CHUNK, dtype=jnp.int32) * RET_CHUNK).astype(_F32)[:, None] * inv[None, :]
    lo = jnp.arange(RET_CHUNK, dtype=jnp.int32).astype(_F32)[:, None] * inv[None, :]
    cos_hi, sin_hi = jnp.cos(hi)[:, None, :], jnp.sin(hi)[:, None, :]
    cos_lo, sin_lo = jnp.cos(lo)[None, :, :], jnp.sin(lo)[None, :, :]
    cos = (cos_hi * cos_lo - sin_hi * sin_lo).reshape(seq, half)
    sin = (sin_hi * cos_lo + cos_hi * sin_lo).reshape(seq, half)
    return jnp.concatenate([cos, cos], axis=1), jnp.concatenate([-sin, sin], axis=1)


def _expert_row_bound(n_tokens):
    rows = (2 * n_tokens + (n_tokens // ROW_TILE) * MOE_N_EXPERTS * (RUN_ALIGN - 1)
            + MOE_N_EXPERTS * (EXPERT_TILE - RUN_ALIGN))
    return -(-rows // EXPERT_TILE) * EXPERT_TILE


def _routing_plan(counts_out, n_rows):
    te = EXPERT_TILE
    n_tok_tiles = counts_out.shape[0] // SUBLANES
    counts = counts_out.reshape(n_tok_tiles, SUBLANES, LANES)[:, 0, :MOE_N_EXPERTS].astype(jnp.int32)
    run = (counts + RUN_ALIGN - 1) // RUN_ALIGN * RUN_ALIGN
    off = jnp.cumsum(run, axis=1) - run
    total = jnp.sum(run, axis=0)
    padded = (total + te - 1) // te * te
    pends = jnp.cumsum(padded)
    pstarts = pends - padded
    dst = pstarts[None, :] + jnp.cumsum(run, axis=0) - run
    tile_start = jnp.arange(n_rows // te, dtype=jnp.int32) * te
    tile_expert = jnp.sum((tile_start[:, None] >= pends[None, :]).astype(jnp.int32), axis=1)
    tile_expert = jnp.minimum(tile_expert, MOE_N_EXPERTS - 1)
    tile_valid = jnp.clip(total[tile_expert] - (tile_start - pstarts[tile_expert]), 0, te)
    i32 = lambda v: v.astype(jnp.int32)

    n_big = run // RUN_PIECE
    big_first = jnp.cumsum(n_big, axis=1) - n_big
    k_big = jnp.arange(BIG_CAP, dtype=jnp.int32)[None, :, None]
    owns_big = (k_big >= big_first[:, None, :]) & (k_big < (big_first + n_big)[:, None, :])
    pick_big = lambda v: jnp.sum(jnp.where(owns_big, (v - RUN_PIECE * big_first)[:, None, :], 0), axis=-1)
    has_small = (run // RUN_ALIGN) % 2
    small_index = jnp.cumsum(has_small, axis=1) - has_small
    k_small = jnp.arange(SMALL_CAP, dtype=jnp.int32)[None, :, None]
    owns_small = (has_small[:, None, :] == 1) & (small_index[:, None, :] == k_small)
    pick_small = lambda v: jnp.sum(jnp.where(owns_small, (v + RUN_PIECE * n_big)[:, None, :], 0), axis=-1)
    return {
        "n_big": i32(jnp.sum(n_big, axis=1)),
        "big_off": i32((pick_big(off) + RUN_PIECE * k_big[..., 0]).reshape(-1)),
        "big_dst": i32((pick_big(dst) + RUN_PIECE * k_big[..., 0]).reshape(-1)),
        "n_small": i32(jnp.sum(has_small, axis=1)),
        "small_off": i32(pick_small(off).reshape(-1)),
        "small_dst": i32(pick_small(dst).reshape(-1)),
        "rows": i32(jnp.sum(run, axis=1)),
        "pad_start": i32(jnp.concatenate([pstarts + total, pends[-1:]])), "pad_len": i32(padded - total),
        "tile_expert": i32(tile_expert), "tile_valid": i32(tile_valid),
    }


def kernel(x, norm_mix_g, w_in, ret_gn_g, w_ret_o, q_norm_g, k_norm_g, sinks, w_swa_o, w_out, norm_ffn_g,
           w_router_group, b_router_group, w_router_expert, b_router_expert, w_gate, w_up, w_down):
    batch, seq, d = x.shape
    n = batch * seq
    assert d == D_MODEL and seq % SEQ_TILE == 0 and n % ROW_TILE == 0
    cos_tab, sin_tab = _rotary_tables(seq)
    n_rows = _expert_row_bound(n)
    for l in range(w_in.shape[0]):
        x2d = x.reshape(n, d)
        ret_proj, swa_proj, gates = _inproj(x2d, norm_mix_g[l][None, :], w_in[l].astype(_BF16), cos_tab, sin_tab,
                                            ret_gn_g[l].reshape(1, RET_V_W))
        retg = _retention(ret_proj, batch, seq)
        attn = _swa(swa_proj, sinks[l], jnp.tile(q_norm_g[l], 2)[None, :], jnp.tile(k_norm_g[l], 2)[None, :],
                    batch, seq)
        pad = LANES - MOE_N_EXPERTS - MOE_GROUPS
        w_router_t = jnp.concatenate(
            [w_router_expert[l].T, w_router_group[l].T, jnp.zeros((pad, d), _F32)], axis=0).astype(_BF16)
        b_router = jnp.concatenate([b_router_expert[l], b_router_group[l], jnp.zeros((pad,), _F32)])
        b_router_t = jnp.broadcast_to(b_router[:, None], (LANES, LANES))
        x1, h2, info, info_t, counts = _outproj(
            x2d, retg, attn, gates, w_ret_o[l].astype(_BF16), w_swa_o[l].astype(_BF16), w_out[l].astype(_BF16),
            norm_ffn_g[l][None, :], w_router_t, b_router_t)
        plan = _routing_plan(counts, n_rows)
        xs = _dispatch(plan, h2, info_t, n_rows)
        ys = _experts(plan, xs, w_gate[l], w_up[l], w_down[l])
        x = _combine(plan, x1, info, ys).reshape(batch, seq, d)
    return x
```

```python
import math

import numpy as np
import jax
import jax.numpy as jnp
from jax import lax
from jax.experimental import pallas as pl
from jax.experimental.pallas import tpu as pltpu

D_MODEL = 1024
RET_HEADS = 4
RET_QK_DIM = 128
RET_V_DIM = 256
RET_CHUNK = 128
ROPE_BASE = 10000.0
SWA_Q_HEADS = 8
SWA_KV_HEADS = 2
SWA_HEAD_DIM = 64
SWA_WINDOW = 128
MOE_GROUPS = 4
MOE_EXPERTS_PER_GROUP = 8
MOE_D_FF = 512
NORM_EPS = 1e-6

RET_QK_W = RET_HEADS * RET_QK_DIM
RET_V_W = RET_HEADS * RET_V_DIM
SWA_Q_W = SWA_Q_HEADS * SWA_HEAD_DIM
SWA_KV_W = SWA_KV_HEADS * SWA_HEAD_DIM
SWA_GROUP = SWA_Q_HEADS // SWA_KV_HEADS
MOE_N_EXPERTS = MOE_GROUPS * MOE_EXPERTS_PER_GROUP

RET_COLS = 2 * RET_QK_W + 2 * RET_V_W
SWA_COLS = SWA_Q_W + 2 * SWA_KV_W
GATE_COLS = 2 * D_MODEL
IN_WIDTH = RET_COLS + SWA_COLS + GATE_COLS

LANES = 128
SUBLANES = 8
ROUTER_GROUP_LANE = 32
NEG_BIG = -1e30

ROW_TILE = 512
SEQ_TILE = 512
EXPERT_TILE = 512
OUTPROJ_SUB = 128
INPROJ_SUB = 256
EXPERT_SUB = 256
RUN_ALIGN = 16
SORT_ROWS = 2 * ROW_TILE + MOE_N_EXPERTS * RUN_ALIGN
SORT_ROWS_POW2 = 1 << SORT_ROWS.bit_length()
RUN_PIECE = 2 * RUN_ALIGN
VMEM_LIMIT = 56 * 1024 * 1024

RET_LOG_DECAY = tuple(float(np.log1p(-np.exp2(-5.0 - h))) for h in range(RET_HEADS))

_BF16 = jnp.bfloat16
_F32 = jnp.float32


def _dot(a, b):
    return jnp.dot(a, b, preferred_element_type=_F32)


def _dot_nt(a, b):
    return lax.dot_general(a, b, (((1,), (1,)), ((), ())), preferred_element_type=_F32)


def _sigmoid(v):
    return 1.0 / (1.0 + jnp.exp(-v))


def _split_bf16(v):
    hi = v.astype(_BF16)
    return hi, (v - hi.astype(_F32)).astype(_BF16)


def _iota(shape, dim):
    return lax.broadcasted_iota(jnp.int32, shape, dim)


def _inproj_kernel(x_ref, g_ref, w_ref, cos_ref, sin_ref, gn_ref, ret_ref, swa_ref, gate_ref):
    subs = [slice(s * INPROJ_SUB, (s + 1) * INPROJ_SUB) for s in range(ROW_TILE // INPROJ_SUB)]
    hs = []
    for rows in subs:
        x = x_ref[rows, :]
        r = lax.rsqrt(jnp.mean(x * x, axis=-1, keepdims=True) + NORM_EPS)
        hs.append(((x * r) * g_ref[...]).astype(_BF16))
    chunk_pos = (_iota((INPROJ_SUB, RET_QK_DIM), 0) % RET_CHUNK + 1).astype(_F32)

    def rotary_decay(base, sign, scale):
        def epilogue(rows, raw):
            cos = cos_ref[rows, :]
            sin = sin_ref[rows, :]
            for hd in range(RET_HEADS):
                v = raw[:, hd * RET_QK_DIM:(hd + 1) * RET_QK_DIM]
                rot = v * cos + pltpu.roll(v, RET_QK_DIM // 2, 1) * sin
                decay = jnp.exp((sign * RET_LOG_DECAY[hd]) * chunk_pos) * scale
                ret_ref[rows, base + hd * RET_QK_DIM:base + (hd + 1) * RET_QK_DIM] = (rot * decay).astype(_BF16)
        return epilogue

    def store(ref, lo, fn):
        def epilogue(rows, raw):
            ref[rows, lo:lo + raw.shape[1]] = fn(raw).astype(_BF16)
        return epilogue

    v_base = 2 * RET_QK_W
    g_base = v_base + RET_V_W
    gate_base = RET_COLS + SWA_COLS
    jobs = [(0, RET_QK_W, rotary_decay(0, 1.0, 1.0)),
            (RET_QK_W, RET_QK_W, rotary_decay(RET_QK_W, -1.0, RET_QK_DIM ** -0.5))]
    jobs += [(v_base + c, 512, store(ret_ref, v_base + c, lambda raw: raw)) for c in range(0, RET_V_W, 512)]
    jobs += [(g_base + c, 512, store(ret_ref, g_base + c,
                                     lambda raw, c=c: (raw * _sigmoid(raw)) * gn_ref[:, c:c + 512]))
             for c in range(0, RET_V_W, 512)]
    jobs += [(RET_COLS + c, 256, store(swa_ref, c, lambda raw: raw)) for c in range(0, SWA_COLS, 256)]
    jobs += [(gate_base + c, 512, store(gate_ref, c, _sigmoid)) for c in range(0, GATE_COLS, 512)]
    for lo, width, epilogue in jobs:
        raws = [_dot(h, w_ref[:, lo:lo + width]) for h in hs]
        for rows, raw in zip(subs, raws):
            epilogue(rows, raw)


def _inproj(x2d, g, w_in_bf16, cos_tab, sin_tab, gn_row):
    n = x2d.shape[0]
    grid = (n // ROW_TILE,)
    seq_tiles = cos_tab.shape[0] // ROW_TILE
    return pl.pallas_call(
        _inproj_kernel,
        grid=grid,
        in_specs=[
            pl.BlockSpec((ROW_TILE, D_MODEL), lambda i: (i, 0)),
            pl.BlockSpec((1, D_MODEL), lambda i: (0, 0)),
            pl.BlockSpec((D_MODEL, IN_WIDTH), lambda i: (0, 0)),
            pl.BlockSpec((ROW_TILE, RET_QK_DIM), lambda i: (i % seq_tiles, 0)),
            pl.BlockSpec((ROW_TILE, RET_QK_DIM), lambda i: (i % seq_tiles, 0)),
            pl.BlockSpec((1, RET_V_W), lambda i: (0, 0)),
        ],
        out_specs=[
            pl.BlockSpec((ROW_TILE, RET_COLS), lambda i: (i, 0)),
            pl.BlockSpec((ROW_TILE, SWA_COLS), lambda i: (i, 0)),
            pl.BlockSpec((ROW_TILE, GATE_COLS), lambda i: (i, 0)),
        ],
        out_shape=[
            jax.ShapeDtypeStruct((n, RET_COLS), _BF16),
            jax.ShapeDtypeStruct((n, SWA_COLS), _BF16),
            jax.ShapeDtypeStruct((n, GATE_COLS), _BF16),
        ],
        compiler_params=pltpu.CompilerParams(
            dimension_semantics=("arbitrary",), vmem_limit_bytes=VMEM_LIMIT),
        name="inproj",
    )(x2d, g, w_in_bf16, cos_tab, sin_tab, gn_row)


def _retention_kernel(q_ref, k_ref, v_ref, g_ref, out_ref, state_ref):
    c = RET_CHUNK

    @pl.when(pl.program_id(1) == 0)
    def _():
        state_ref[...] = jnp.zeros_like(state_ref)

    causal = _iota((c, c), 0) >= _iota((c, c), 1)

    for h in range(RET_HEADS):
        chunk_decay = math.exp(RET_LOG_DECAY[h] * c)
        qs = slice(h * RET_QK_DIM, (h + 1) * RET_QK_DIM)
        vs = slice(h * RET_V_DIM, (h + 1) * RET_V_DIM)
        chunks = [slice(ci * c, (ci + 1) * c) for ci in range(SEQ_TILE // c)]
        scores = [jnp.where(causal, _dot_nt(q_ref[rows, qs], k_ref[rows, qs]), 0.0).astype(_BF16) for rows in chunks]
        kvs = [_dot(k_ref[rows, qs].astype(_F32).T.astype(_BF16), v_ref[rows, vs]) for rows in chunks]
        state = state_ref[h]
        states = []
        for kv in kvs:
            states.append(state.astype(_BF16))
            state = chunk_decay * (state + kv)
        state_ref[h] = state
        rets = [_dot(jnp.concatenate([s, q_ref[rows, qs]], axis=1), jnp.concatenate([v_ref[rows, vs], st], axis=0))
                for rows, s, st in zip(chunks, scores, states)]
        for rows, ret in zip(chunks, rets):
            mu = jnp.mean(ret, axis=-1, keepdims=True)
            dev = ret - mu
            var = jnp.mean(dev * dev, axis=-1, keepdims=True)
            out_ref[rows, vs] = ((dev * lax.rsqrt(var + NORM_EPS)) * g_ref[rows, vs].astype(_F32)).astype(_BF16)


def _retention(ret_proj, batch, seq):
    n = ret_proj.shape[0]
    steps = seq // SEQ_TILE
    rowmap = lambda b, s: b * steps + s
    return pl.pallas_call(
        _retention_kernel,
        grid=(batch, steps),
        in_specs=[
            pl.BlockSpec((SEQ_TILE, RET_QK_W), lambda b, s: (rowmap(b, s), 0)),
            pl.BlockSpec((SEQ_TILE, RET_QK_W), lambda b, s: (rowmap(b, s), 1)),
            pl.BlockSpec((SEQ_TILE, RET_V_W), lambda b, s: (rowmap(b, s), 1)),
            pl.BlockSpec((SEQ_TILE, RET_V_W), lambda b, s: (rowmap(b, s), 2)),
        ],
        out_specs=pl.BlockSpec((SEQ_TILE, RET_V_W), lambda b, s: (rowmap(b, s), 0)),
        out_shape=jax.ShapeDtypeStruct((n, RET_V_W), _BF16),
        scratch_shapes=[pltpu.VMEM((RET_HEADS, RET_QK_DIM, RET_V_DIM), _F32)],
        compiler_params=pltpu.CompilerParams(
            dimension_semantics=("arbitrary", "arbitrary"), vmem_limit_bytes=VMEM_LIMIT),
        name="retention",
    )(ret_proj, ret_proj, ret_proj, ret_proj)


def _swa_kernel(sinks_ref, q_ref, kc_ref, vc_ref, kp_ref, vp_ref, qg_ref, kg_ref, out_ref):
    w = SWA_WINDOW
    d = SWA_HEAD_DIM
    n_blk = SEQ_TILE // w
    pair = 2 * d
    first = pl.program_id(1) == 0

    k_both = jnp.concatenate([kp_ref[...], kc_ref[...]], axis=0).astype(_F32)
    same_half = (_iota((pair, pair), 0) // d == _iota((pair, pair), 1) // d).astype(_BF16)
    k_hi, k_lo = _split_bf16(k_both * k_both)
    k_ssq = _dot(k_hi, same_half) + _dot(k_lo, same_half)
    k_n = k_both * lax.rsqrt(k_ssq * (1.0 / d) + NORM_EPS) * (kg_ref[...] * d ** -0.5)
    v_both = jnp.concatenate([vp_ref[...], vc_ref[...]], axis=0).astype(_F32)
    v_t = [v_both[b * w:(b + 1) * w].T for b in range(n_blk + 1)]

    key = _iota((2 * w, SWA_GROUP * w), 0)
    qry = _iota((2 * w, SWA_GROUP * w), 1) % w
    band = (key > qry) & (key <= qry + w)
    band_first = band & jnp.logical_or(key >= w, jnp.logical_not(first))
    head_of_lane = _iota((1, SWA_GROUP * w), 1) // w
    half_of_lane = _iota((1, pair), 1) // d
    half_rows = (_iota((SUBLANES, pair), 0) == _iota((SUBLANES, pair), 1) // d).astype(_BF16)
    ones_rows = jnp.ones((2 * SUBLANES, 2 * w), _BF16)
    q_gain = qg_ref[...]

    for kk in range(SWA_KV_HEADS):
        k_native = jnp.where(half_of_lane == kk, k_n, 0.0)
        k_moved = pltpu.roll(k_native, d, 1)
        k_even, k_odd = (k_native, k_moved) if kk == 0 else (k_moved, k_native)
        k_even = k_even.astype(_BF16)
        k_odd = k_odd.astype(_BF16)
        sink_row = jnp.zeros((1, SWA_GROUP * w), _F32)
        for g in range(SWA_GROUP):
            sink_row = jnp.where(head_of_lane == g, sinks_ref[kk * SWA_GROUP + g], sink_row)
        blocks = range(n_blk)
        pairs = range(SWA_GROUP // 2)
        q_pairs = [[q_ref[j * w:(j + 1) * w, (kk * SWA_GROUP + 2 * p) * d:(kk * SWA_GROUP + 2 * p + 2) * d].astype(_F32)
                    for p in pairs] for j in blocks]
        q_splits = [[_split_bf16(q * q) for q in qs] for qs in q_pairs]
        q_ssq = [[_dot_nt(half_rows, hi) + _dot_nt(half_rows, lo) for hi, lo in sp] for sp in q_splits]
        q_rs = [[lax.rsqrt(s * (1.0 / d) + NORM_EPS) for s in ss] for ss in q_ssq]
        q_gs = [[(q * q_gain).astype(_BF16) for q in qs] for qs in q_pairs]
        raw_even = [[_dot_nt(k_even[j * w:(j + 2) * w], q_gs[j][p]) for p in pairs] for j in blocks]
        raw_odd = [[_dot_nt(k_odd[j * w:(j + 2) * w], q_gs[j][p]) for p in pairs] for j in blocks]
        score_blocks = []
        for j in blocks:
            raws = []
            for p in pairs:
                raws.append(raw_even[j][p] * q_rs[j][p][0:1, :])
                raws.append(raw_odd[j][p] * q_rs[j][p][1:2, :])
            s_t = jnp.concatenate(raws, axis=1)
            score_blocks.append(jnp.where(band_first if j == 0 else band, s_t, NEG_BIG))
        maxes = [jnp.maximum(jnp.max(s_t, axis=0, keepdims=True), sink_row) for s_t in score_blocks]
        probs = [jnp.exp(s_t - m).astype(_BF16) for s_t, m in zip(score_blocks, maxes)]
        outs = []
        for j in blocks:
            v_ext = jnp.concatenate([v_t[j][kk * d:(kk + 1) * d], v_t[j + 1][kk * d:(kk + 1) * d]], axis=1)
            outs.append(_dot(jnp.concatenate([v_ext.astype(_BF16), ones_rows], axis=0), probs[j]))
        for j in blocks:
            rows = slice(j * w, (j + 1) * w)
            denom = outs[j][d:d + 1, :] + jnp.exp(sink_row - maxes[j])
            o_t = outs[j][0:d, :] * (1.0 / denom)
            for p in range(SWA_GROUP // 2):
                c0 = (kk * SWA_GROUP + 2 * p) * d
                both = jnp.concatenate([o_t[:, 2 * p * w:(2 * p + 1) * w], o_t[:, (2 * p + 1) * w:(2 * p + 2) * w]], axis=0)
                out_ref[rows, c0:c0 + pair] = both.T.astype(_BF16)


def _swa(swa_proj, sinks, q_gain2, k_gain2, batch, seq):
    n = swa_proj.shape[0]
    steps = seq // SEQ_TILE
    blocks_per_step = SEQ_TILE // SWA_WINDOW
    rowmap = lambda b, s: b * steps + s
    prevmap = lambda b, s: jnp.maximum((b * steps + s) * blocks_per_step - 1, 0)
    k_col = SWA_Q_W // SWA_KV_W
    return pl.pallas_call(
        _swa_kernel,
        grid=(batch, steps),
        in_specs=[
            pl.BlockSpec(memory_space=pltpu.SMEM),
            pl.BlockSpec((SEQ_TILE, SWA_Q_W), lambda b, s: (rowmap(b, s), 0)),
            pl.BlockSpec((SEQ_TILE, SWA_KV_W), lambda b, s: (rowmap(b, s), k_col)),
            pl.BlockSpec((SEQ_TILE, SWA_KV_W), lambda b, s: (rowmap(b, s), k_col + 1)),
            pl.BlockSpec((SWA_WINDOW, SWA_KV_W), lambda b, s: (prevmap(b, s), k_col)),
            pl.BlockSpec((SWA_WINDOW, SWA_KV_W), lambda b, s: (prevmap(b, s), k_col + 1)),
            pl.BlockSpec((1, 2 * SWA_HEAD_DIM), lambda b, s: (0, 0)),
            pl.BlockSpec((1, 2 * SWA_HEAD_DIM), lambda b, s: (0, 0)),
        ],
        out_specs=pl.BlockSpec((SEQ_TILE, SWA_Q_W), lambda b, s: (rowmap(b, s), 0)),
        out_shape=jax.ShapeDtypeStruct((n, SWA_Q_W), _BF16),
        compiler_params=pltpu.CompilerParams(
            dimension_semantics=("arbitrary", "arbitrary"), vmem_limit_bytes=VMEM_LIMIT),
        name="swa",
    )(sinks, swa_proj, swa_proj, swa_proj, swa_proj, swa_proj, q_gain2, k_gain2)


def _outproj_kernel(x_ref, retg_ref, attn_ref, ga_ref, gb_ref, wro_ref, wso_ref, wout_ref, g2_ref,
                    wrt_ref, brt_ref, x1_ref, h2_ref, info_ref, info_t_ref, counts_ref, logits_ref):
    tm = ROW_TILE
    step = pl.program_id(0)
    cur = step % 2

    @pl.when(step == 0)
    def _():
        logits_ref[...] = jnp.zeros_like(logits_ref)

    logits = logits_ref[1 - cur]
    row = _iota((LANES, tm), 0)
    row_f = row.astype(_F32)
    is_group = (row >= ROUTER_GROUP_LANE) & (row < ROUTER_GROUP_LANE + MOE_GROUPS)

    subs = [slice(s * OUTPROJ_SUB, (s + 1) * OUTPROJ_SUB) for s in range(tm // OUTPROJ_SUB)]
    y_a = [_dot(retg_ref[rows, :], wro_ref[...]) for rows in subs]

    gl = jnp.where(is_group, logits, NEG_BIG)
    g_max = jnp.max(gl, axis=0, keepdims=True)
    g_prob = 1.0 / jnp.sum(jnp.exp(gl - g_max), axis=0, keepdims=True)
    g_idx = jnp.min(jnp.where(gl == g_max, row_f - ROUTER_GROUP_LANE, float(LANES)), axis=0, keepdims=True)

    y_b = [_dot(attn_ref[rows, :], wso_ref[...]) for rows in subs]

    e_lo = g_idx * MOE_EXPERTS_PER_GROUP
    in_group = (row_f >= e_lo) & (row_f < e_lo + MOE_EXPERTS_PER_GROUP)
    el = jnp.where(in_group, logits, NEG_BIG)
    t1 = jnp.max(el, axis=0, keepdims=True)
    i1 = jnp.min(jnp.where(el == t1, row_f, float(LANES)), axis=0, keepdims=True)

    merged = [(ga_ref[rows, :].astype(_F32) * a + gb_ref[rows, :].astype(_F32) * b).astype(_BF16)
              for rows, a, b in zip(subs, y_a, y_b)]
    mixed = [_dot(m, wout_ref[...]) for m in merged]
    x1 = [x_ref[rows, :] + m for rows, m in zip(subs, mixed)]
    for rows, v in zip(subs, x1):
        x1_ref[rows, :] = v

    el2 = jnp.where(row_f == i1, NEG_BIG, el)
    t2 = jnp.max(el2, axis=0, keepdims=True)
    i2 = jnp.min(jnp.where(el2 == t2, row_f, float(LANES)), axis=0, keepdims=True)
    e21 = jnp.exp(t2 - t1)
    w1 = g_prob / (1.0 + e21)
    w2 = g_prob * e21 / (1.0 + e21)

    h2 = [((v * lax.rsqrt(jnp.mean(v * v, axis=-1, keepdims=True) + NORM_EPS)) * g2_ref[...]).astype(_BF16)
          for v in x1]
    for rows, v in zip(subs, h2):
        h2_ref[rows, :] = v

    sel1 = row_f == i1
    sel2 = row_f == i2
    onehot = jnp.where(sel1 | sel2, 1.0, 0.0).astype(_BF16)
    earlier = (_iota((tm, tm), 0) < _iota((tm, tm), 1)).astype(_BF16)
    before = _dot(onehot, earlier)
    counts = _dot(onehot, jnp.ones((tm, LANES), _BF16))

    for rows, v in zip(subs, h2):
        logits_ref[cur, :, rows] = _dot_nt(wrt_ref[...], v) + brt_ref[:, 0:1]

    run = jnp.floor((counts + (RUN_ALIGN - 1.0)) * (1.0 / RUN_ALIGN)) * RUN_ALIGN
    lower_experts = (_iota((LANES, LANES), 0) > _iota((LANES, LANES), 1)).astype(_BF16)
    run_off = _dot(lower_experts, run.astype(_BF16))
    place = before + jnp.concatenate([run_off] * (tm // LANES), axis=1)
    pos1 = jnp.sum(jnp.where(sel1, place, 0.0), axis=0, keepdims=True)
    pos2 = jnp.sum(jnp.where(sel2, place, 0.0), axis=0, keepdims=True)
    counts_ref[...] = counts.T[0:SUBLANES, :]

    info_t = jnp.concatenate([i1, i2, w1, w2, pos1, pos2, jnp.zeros((2, tm), _F32)], axis=0)
    info_t_ref[...] = info_t
    info_ref[...] = jnp.concatenate([info_t, jnp.zeros((LANES - SUBLANES, tm), _F32)], axis=0).T


def _outproj(x2d, retg, attn, gates, w_ret_o, w_swa_o, w_out, g2, w_router_t, b_router_t):
    n = x2d.shape[0]
    tm = ROW_TILE
    n_tiles = n // tm
    const = lambda s: (0, 0)
    proj = lambda s: jnp.minimum(s, n_tiles - 1)
    routed = lambda s: jnp.maximum(s - 1, 0)
    return pl.pallas_call(
        _outproj_kernel,
        grid=(n_tiles + 1,),
        in_specs=[
            pl.BlockSpec((tm, D_MODEL), lambda s: (proj(s), 0)),
            pl.BlockSpec((tm, RET_V_W), lambda s: (proj(s), 0)),
            pl.BlockSpec((tm, SWA_Q_W), lambda s: (proj(s), 0)),
            pl.BlockSpec((tm, D_MODEL), lambda s: (proj(s), 0)),
            pl.BlockSpec((tm, D_MODEL), lambda s: (proj(s), 1)),
            pl.BlockSpec((RET_V_W, D_MODEL), const),
            pl.BlockSpec((SWA_Q_W, D_MODEL), const),
            pl.BlockSpec((D_MODEL, D_MODEL), const),
            pl.BlockSpec((1, D_MODEL), const),
            pl.BlockSpec((LANES, D_MODEL), const),
            pl.BlockSpec((LANES, LANES), const),
        ],
        out_specs=[
            pl.BlockSpec((tm, D_MODEL), lambda s: (proj(s), 0)),
            pl.BlockSpec((tm, D_MODEL), lambda s: (proj(s), 0)),
            pl.BlockSpec((tm, LANES), lambda s: (routed(s), 0)),
            pl.BlockSpec((SUBLANES, tm), lambda s: (0, routed(s))),
            pl.BlockSpec((SUBLANES, LANES), lambda s: (routed(s), 0)),
        ],
        scratch_shapes=[pltpu.VMEM((2, LANES, tm), _F32)],
        out_shape=[
            jax.ShapeDtypeStruct((n, D_MODEL), _F32),
            jax.ShapeDtypeStruct((n, D_MODEL), _BF16),
            jax.ShapeDtypeStruct((n, LANES), _F32),
            jax.ShapeDtypeStruct((SUBLANES, n), _F32),
            jax.ShapeDtypeStruct((n // tm * SUBLANES, LANES), _F32),
        ],
        compiler_params=pltpu.CompilerParams(
            dimension_semantics=("arbitrary",), vmem_limit_bytes=VMEM_LIMIT),
        name="outproj",
    )(x2d, retg, attn, gates, gates, w_ret_o, w_swa_o, w_out, g2, w_router_t, b_router_t)


def _piece_sizes(largest):
    piece = largest
    while piece >= RUN_ALIGN:
        yield piece
        piece //= 2


BIG_CAP = SORT_ROWS // RUN_PIECE
SMALL_CAP = MOE_N_EXPERTS
PIECE_LIST_KEYS = ("n_big", "big_off", "big_dst", "n_small", "small_off", "small_dst")


def _start_run_copies(pieces, tile, make_copy):
    n_big_ref, big_off_ref, big_dst_ref, n_small_ref, small_off_ref, small_dst_ref = pieces

    def big(k, carry):
        i = tile * BIG_CAP + k
        make_copy(pl.multiple_of(big_off_ref[i], RUN_ALIGN), pl.multiple_of(big_dst_ref[i], RUN_ALIGN),
                  RUN_PIECE).start()
        return carry

    lax.fori_loop(0, n_big_ref[tile], big, 0)

    def small(k, carry):
        i = tile * SMALL_CAP + k
        make_copy(pl.multiple_of(small_off_ref[i], RUN_ALIGN), pl.multiple_of(small_dst_ref[i], RUN_ALIGN),
                  RUN_ALIGN).start()
        return carry

    lax.fori_loop(0, n_small_ref[tile], small, 0)


def _wait_rows(rows, make_copy):
    for piece in _piece_sizes(SORT_ROWS_POW2 // 2):
        pl.when((rows & piece) != 0)(make_copy(0, 0, piece).wait)


def _zero_fill_padding(pad_start_ref, pad_len_ref, zeros_ref, xs_hbm, sem, wait):
    def per_expert(e, carry):
        pos = pad_start_ref[e]
        length = pad_len_ref[e]
        for piece in _piece_sizes(EXPERT_TILE // 2):
            copy = pltpu.make_async_copy(
                zeros_ref.at[pl.ds(0, piece)], xs_hbm.at[pl.ds(pl.multiple_of(pos, RUN_ALIGN), piece)], sem)
            pl.when((length & piece) != 0)(copy.wait if wait else copy.start)
            pos = pos + (length & piece)
        return carry

    lax.fori_loop(0, MOE_N_EXPERTS, per_expert, 0)

    tail_start = pad_start_ref[MOE_N_EXPERTS]
    n_rows = xs_hbm.shape[0]
    half = EXPERT_TILE // 2

    def per_half_tile(j, carry):
        pos = tail_start + j * half
        copy = pltpu.make_async_copy(
            zeros_ref, xs_hbm.at[pl.ds(pl.multiple_of(jnp.minimum(pos, n_rows - half), RUN_ALIGN), half)], sem)
        pl.when(pos < n_rows)(copy.wait if wait else copy.start)
        return carry

    lax.fori_loop(0, n_rows // half, per_half_tile, 0)


def _dispatch_kernel(*refs):
    pieces = refs[:len(PIECE_LIST_KEYS)]
    rows_ref, pad_start_ref, pad_len_ref, h2_ref, info_t_ref, xs_hbm, sorted_ref, zeros_ref, sems, zero_sem = (
        refs[len(PIECE_LIST_KEYS):])
    tile = pl.program_id(0)
    last = pl.num_programs(0) - 1
    slot = tile % 2
    tm = ROW_TILE

    @pl.when(tile == 0)
    def _():
        zeros_ref[...] = jnp.zeros_like(zeros_ref)
        _zero_fill_padding(pad_start_ref, pad_len_ref, zeros_ref, xs_hbm, zero_sem, wait=False)
        _zero_fill_padding(pad_start_ref, pad_len_ref, zeros_ref, xs_hbm, zero_sem, wait=True)

    pos = info_t_ref[4:6, :].astype(jnp.int32)
    place = _iota((SORT_ROWS, tm), 0)
    onehot = ((place == pos[0:1, :]) | (place == pos[1:2, :])).astype(_BF16)
    for c in range(0, D_MODEL, 256):
        sorted_ref[slot, :, c:c + 256] = _dot(onehot, h2_ref[:, c:c + 256]).astype(_BF16)

    def copy_maker(t):
        s = t % 2

        def make_copy(off, dst, piece):
            return pltpu.make_async_copy(
                sorted_ref.at[s, pl.ds(off, piece)], xs_hbm.at[pl.ds(dst, piece)], sems.at[s])

        return make_copy

    _start_run_copies(pieces, tile, copy_maker(tile))

    @pl.when(tile > 0)
    def _():
        _wait_rows(rows_ref[tile - 1], copy_maker(tile - 1))

    @pl.when(tile == last)
    def _():
        _wait_rows(rows_ref[tile], copy_maker(tile))


def _dispatch(plan, h2, info_t, n_rows):
    n = h2.shape[0]
    grid_spec = pltpu.PrefetchScalarGridSpec(
        num_scalar_prefetch=len(PIECE_LIST_KEYS) + 3,
        grid=(n // ROW_TILE,),
        in_specs=[
            pl.BlockSpec((ROW_TILE, D_MODEL), lambda i, *_: (i, 0)),
            pl.BlockSpec((SUBLANES, ROW_TILE), lambda i, *_: (0, i)),
        ],
        out_specs=pl.BlockSpec(memory_space=pl.ANY),
        scratch_shapes=[
            pltpu.VMEM((2, SORT_ROWS, D_MODEL), _BF16),
            pltpu.VMEM((EXPERT_TILE // 2, D_MODEL), _BF16),
            pltpu.SemaphoreType.DMA((2,)),
            pltpu.SemaphoreType.DMA(()),
        ],
    )
    return pl.pallas_call(
        _dispatch_kernel,
        grid_spec=grid_spec,
        out_shape=jax.ShapeDtypeStruct((n_rows, D_MODEL), _BF16),
        compiler_params=pltpu.CompilerParams(
            dimension_semantics=("arbitrary",), vmem_limit_bytes=VMEM_LIMIT),
        name="dispatch",
    )(*[plan[k] for k in PIECE_LIST_KEYS], plan["rows"], plan["pad_start"], plan["pad_len"], h2, info_t)


def _experts_kernel(te_ref, tv_ref, xs_ref, wg_ref, wu_ref, wd_ref, ys_ref, wg_s, wu_s, wd_s):
    i = pl.program_id(0)
    valid = tv_ref[i]
    prev = te_ref[jnp.maximum(i - 1, 0)]
    changed = jnp.logical_or(i == 0, te_ref[i] != prev)

    @pl.when(jnp.logical_and(valid > 0, changed))
    def _():
        wg_s[...] = wg_ref[...].astype(_BF16)
        wu_s[...] = wu_ref[...].astype(_BF16)
        wd_s[...] = wd_ref[...].astype(_BF16)

    @pl.when(valid > 0)
    def _():
        subs = [slice(s * EXPERT_SUB, (s + 1) * EXPERT_SUB) for s in range(EXPERT_TILE // EXPERT_SUB)]
        gate = [_dot(xs_ref[rows, :], wg_s[...]) for rows in subs]
        up = [_dot(xs_ref[rows, :], wu_s[...]) for rows in subs]
        act = [((g * _sigmoid(g)) * u).astype(_BF16) for g, u in zip(gate, up)]
        for rows, a in zip(subs, act):
            ys_ref[rows, :] = _dot(a, wd_s[...]).astype(_BF16)

    @pl.when(valid <= 0)
    def _():
        ys_ref[...] = jnp.zeros_like(ys_ref)


def _experts(plan, xs, w_gate, w_up, w_down):
    n_rows = xs.shape[0]
    n_tiles = n_rows // EXPERT_TILE
    grid_spec = pltpu.PrefetchScalarGridSpec(
        num_scalar_prefetch=2,
        grid=(n_tiles,),
        in_specs=[
            pl.BlockSpec((EXPERT_TILE, D_MODEL), lambda i, te, tv: (i, 0)),
            pl.BlockSpec((None, D_MODEL, MOE_D_FF), lambda i, te, tv: (te[i], 0, 0)),
            pl.BlockSpec((None, D_MODEL, MOE_D_FF), lambda i, te, tv: (te[i], 0, 0)),
            pl.BlockSpec((None, MOE_D_FF, D_MODEL), lambda i, te, tv: (te[i], 0, 0)),
        ],
        out_specs=pl.BlockSpec((EXPERT_TILE, D_MODEL), lambda i, te, tv: (i, 0)),
        scratch_shapes=[
            pltpu.VMEM((D_MODEL, MOE_D_FF), _BF16),
            pltpu.VMEM((D_MODEL, MOE_D_FF), _BF16),
            pltpu.VMEM((MOE_D_FF, D_MODEL), _BF16),
        ],
    )
    return pl.pallas_call(
        _experts_kernel,
        grid_spec=grid_spec,
        out_shape=jax.ShapeDtypeStruct((n_rows, D_MODEL), _BF16),
        compiler_params=pltpu.CompilerParams(
            dimension_semantics=("arbitrary",), vmem_limit_bytes=VMEM_LIMIT),
        name="experts",
    )(plan["tile_expert"], plan["tile_valid"], xs, w_gate, w_up, w_down)


def _combine_kernel(*refs):
    pieces = refs[:len(PIECE_LIST_KEYS)]
    rows_ref, x1_ref, info_ref, ys_hbm, out_ref, block_ref, sems = refs[len(PIECE_LIST_KEYS):]
    tile = pl.program_id(0)
    last = pl.num_programs(0) - 1
    slot = tile % 2
    tm = ROW_TILE

    def copy_maker(t):
        s = t % 2

        def make_copy(off, dst, piece):
            return pltpu.make_async_copy(
                ys_hbm.at[pl.ds(dst, piece)], block_ref.at[s, pl.ds(off, piece)], sems.at[s])

        return make_copy

    @pl.when(tile == 0)
    def _():
        block_ref[...] = jnp.zeros_like(block_ref)
        _start_run_copies(pieces, tile, copy_maker(tile))

    @pl.when(tile < last)
    def _():
        _start_run_copies(pieces, tile + 1, copy_maker(tile + 1))

    _wait_rows(rows_ref[tile], copy_maker(tile))

    info = info_ref[...]
    place = _iota((tm, SORT_ROWS), 1).astype(_F32)
    weights = (jnp.where(place == info[:, 4:5], info[:, 2:3], 0.0)
               + jnp.where(place == info[:, 5:6], info[:, 3:4], 0.0)).astype(_BF16)
    for c in range(0, D_MODEL, 256):
        out_ref[:, c:c + 256] = x1_ref[:, c:c + 256] + _dot(weights, block_ref[slot, :, c:c + 256])


def _combine(plan, x1, info, ys):
    n = x1.shape[0]
    grid_spec = pltpu.PrefetchScalarGridSpec(
        num_scalar_prefetch=len(PIECE_LIST_KEYS) + 1,
        grid=(n // ROW_TILE,),
        in_specs=[
            pl.BlockSpec((ROW_TILE, D_MODEL), lambda i, *_: (i, 0)),
            pl.BlockSpec((ROW_TILE, LANES), lambda i, *_: (i, 0)),
            pl.BlockSpec(memory_space=pl.ANY),
        ],
        out_specs=pl.BlockSpec((ROW_TILE, D_MODEL), lambda i, *_: (i, 0)),
        scratch_shapes=[pltpu.VMEM((2, SORT_ROWS, D_MODEL), _BF16), pltpu.SemaphoreType.DMA((2,))],
    )
    return pl.pallas_call(
        _combine_kernel,
        grid_spec=grid_spec,
        out_shape=jax.ShapeDtypeStruct((n, D_MODEL), _F32),
        compiler_params=pltpu.CompilerParams(
            dimension_semantics=("arbitrary",), vmem_limit_bytes=VMEM_LIMIT),
        name="combine",
    )(*[plan[k] for k in PIECE_LIST_KEYS], plan["rows"], x1, info, ys)


def _rotary_tables(seq):
    half = RET_QK_DIM // 2
    inv = ROPE_BASE ** (-jnp.arange(half, dtype=_F32) / half)
    hi = (jnp.arange(seq // RET_CHUNK, dtype=jnp.int32) * RET_CHUNK).astype(_F32)[:, None] * inv[None, :]
    lo = jnp.arange(RET_CHUNK, dtype=jnp.int32).astype(_F32)[:, None] * inv[None, :]
    cos_hi, sin_hi = jnp.cos(hi)[:, None, :], jnp.sin(hi)[:, None, :]
    cos_lo, sin_lo = jnp.cos(lo)[None, :, :], jnp.sin(lo)[None, :, :]
    cos = (cos_hi * cos_lo - sin_hi * sin_lo).reshape(seq, half)
    sin = (sin_hi * cos_lo + cos_hi * sin_lo).reshape(seq, half)
    return jnp.concatenate([cos, cos], axis=1), jnp.concatenate([-sin, sin], axis=1)


def _expert_row_bound(n_tokens):
    rows = (2 * n_tokens + (n_tokens // ROW_TILE) * MOE_N_EXPERTS * (RUN_ALIGN - 1)
            + MOE_N_EXPERTS * (EXPERT_TILE - RUN_ALIGN))
    return -(-rows // EXPERT_TILE) * EXPERT_TILE


def _routing_plan(counts_out, n_rows):
    te = EXPERT_TILE
    n_tok_tiles = counts_out.shape[0] // SUBLANES
    counts = counts_out.reshape(n_tok_tiles, SUBLANES, LANES)[:, 0, :MOE_N_EXPERTS].astype(jnp.int32)
    run = (counts + RUN_ALIGN - 1) // RUN_ALIGN * RUN_ALIGN
    off = jnp.cumsum(run, axis=1) - run
    total = jnp.sum(run, axis=0)
    padded = (total + te - 1) // te * te
    pends = jnp.cumsum(padded)
    pstarts = pends - padded
    dst = pstarts[None, :] + jnp.cumsum(run, axis=0) - run
    tile_start = jnp.arange(n_rows // te, dtype=jnp.int32) * te
    tile_expert = jnp.sum((tile_start[:, None] >= pends[None, :]).astype(jnp.int32), axis=1)
    tile_expert = jnp.minimum(tile_expert, MOE_N_EXPERTS - 1)
    of_tile = tile_expert[:, None] == jnp.arange(MOE_N_EXPERTS, dtype=jnp.int32)[None, :]
    tile_end = jnp.sum(jnp.where(of_tile, (pstarts + total)[None, :], 0), axis=1)
    tile_valid = jnp.clip(tile_end - tile_start, 0, te)
    i32 = lambda v: v.astype(jnp.int32)

    n_big = run // RUN_PIECE
    big_first = jnp.cumsum(n_big, axis=1) - n_big
    k_big = jnp.arange(BIG_CAP, dtype=jnp.int32)[None, :, None]
    owns_big = (k_big >= big_first[:, None, :]) & (k_big < (big_first + n_big)[:, None, :])
    pick_big = lambda v: jnp.sum(jnp.where(owns_big, (v - RUN_PIECE * big_first)[:, None, :], 0), axis=-1)
    has_small = (run // RUN_ALIGN) % 2
    small_index = jnp.cumsum(has_small, axis=1) - has_small
    k_small = jnp.arange(SMALL_CAP, dtype=jnp.int32)[None, :, None]
    owns_small = (has_small[:, None, :] == 1) & (small_index[:, None, :] == k_small)
    pick_small = lambda v: jnp.sum(jnp.where(owns_small, (v + RUN_PIECE * n_big)[:, None, :], 0), axis=-1)
    return {
        "n_big": i32(jnp.sum(n_big, axis=1)),
        "big_off": i32((pick_big(off) + RUN_PIECE * k_big[..., 0]).reshape(-1)),
        "big_dst": i32((pick_big(dst) + RUN_PIECE * k_big[..., 0]).reshape(-1)),
        "n_small": i32(jnp.sum(has_small, axis=1)),
        "small_off": i32(pick_small(off).reshape(-1)),
        "small_dst": i32(pick_small(dst).reshape(-1)),
        "rows": i32(jnp.sum(run, axis=1)),
        "pad_start": i32(jnp.concatenate([pstarts + total, pends[-1:]])), "pad_len": i32(padded - total),
        "tile_expert": i32(tile_expert), "tile_valid": i32(tile_valid),
    }


def kernel(x, norm_mix_g, w_in, ret_gn_g, w_ret_o, q_norm_g, k_norm_g, sinks, w_swa_o, w_out, norm_ffn_g,
           w_router_group, b_router_group, w_router_expert, b_router_expert, w_gate, w_up, w_down):
    batch, seq, d = x.shape
    n = batch * seq
    assert d == D_MODEL and seq % SEQ_TILE == 0 and n % ROW_TILE == 0
    cos_tab, sin_tab = _rotary_tables(seq)
    n_rows = _expert_row_bound(n)
    for l in range(w_in.shape[0]):
        x2d = x.reshape(n, d)
        ret_proj, swa_proj, gates = _inproj(x2d, norm_mix_g[l][None, :], w_in[l].astype(_BF16), cos_tab, sin_tab,
                                            ret_gn_g[l].reshape(1, RET_V_W))
        retg = _retention(ret_proj, batch, seq)
        attn = _swa(swa_proj, sinks[l], jnp.tile(q_norm_g[l], 2)[None, :], jnp.tile(k_norm_g[l], 2)[None, :],
                    batch, seq)
        pad = LANES - MOE_N_EXPERTS - MOE_GROUPS
        w_router_t = jnp.concatenate(
            [w_router_expert[l].T, w_router_group[l].T, jnp.zeros((pad, d), _F32)], axis=0).astype(_BF16)
        b_router = jnp.concatenate([b_router_expert[l], b_router_group[l], jnp.zeros((pad,), _F32)])
        b_router_t = jnp.broadcast_to(b_router[:, None], (LANES, LANES))
        x1, h2, info, info_t, counts = _outproj(
            x2d, retg, attn, gates, w_ret_o[l].astype(_BF16), w_swa_o[l].astype(_BF16), w_out[l].astype(_BF16),
            norm_ffn_g[l][None, :], w_router_t, b_router_t)
        plan = _routing_plan(counts, n_rows)
        xs = _dispatch(plan, h2, info_t, n_rows)
        ys = _experts(plan, xs, w_gate[l], w_up[l], w_down[l])
        x = _combine(plan, x1, info, ys).reshape(batch, seq, d)
    return x
```

```python
import functools
import math

import numpy as np
import jax
import jax.numpy as jnp
from jax import lax
from jax.experimental import pallas as pl
from jax.experimental.pallas import tpu as pltpu

D_MODEL = 1024
RET_HEADS = 4
RET_QK_DIM = 128
RET_V_DIM = 256
RET_CHUNK = 128
ROPE_BASE = 10000.0
SWA_Q_HEADS = 8
SWA_KV_HEADS = 2
SWA_HEAD_DIM = 64
SWA_WINDOW = 128
MOE_GROUPS = 4
MOE_EXPERTS_PER_GROUP = 8
MOE_D_FF = 512
NORM_EPS = 1e-6

RET_QK_W = RET_HEADS * RET_QK_DIM
RET_V_W = RET_HEADS * RET_V_DIM
SWA_Q_W = SWA_Q_HEADS * SWA_HEAD_DIM
SWA_KV_W = SWA_KV_HEADS * SWA_HEAD_DIM
SWA_GROUP = SWA_Q_HEADS // SWA_KV_HEADS
MOE_N_EXPERTS = MOE_GROUPS * MOE_EXPERTS_PER_GROUP

RET_COLS = 2 * RET_QK_W + 2 * RET_V_W
SWA_COLS = SWA_Q_W + 2 * SWA_KV_W
GATE_COLS = 2 * D_MODEL
IN_WIDTH = RET_COLS + SWA_COLS + GATE_COLS

LANES = 128
SUBLANES = 8
ROUTER_GROUP_LANE = 32
NEG_BIG = -1e30

ROW_TILE = 512
SEQ_TILE = 512
EXPERT_TILE = 512
OUTPROJ_SUB = 128
INPROJ_SUB = 256
EXPERT_SUB = 256
RUN_ALIGN = 16
SORT_ROWS = 2 * ROW_TILE + MOE_N_EXPERTS * RUN_ALIGN
SORT_ROWS_POW2 = 1 << SORT_ROWS.bit_length()
RUN_PIECE = 2 * RUN_ALIGN
VMEM_LIMIT = 56 * 1024 * 1024

RET_LOG_DECAY = tuple(float(np.log1p(-np.exp2(-5.0 - h))) for h in range(RET_HEADS))

_BF16 = jnp.bfloat16
_F32 = jnp.float32


def _dot(a, b):
    return jnp.dot(a, b, preferred_element_type=_F32)


def _dot_nt(a, b):
    return lax.dot_general(a, b, (((1,), (1,)), ((), ())), preferred_element_type=_F32)


def _sigmoid(v):
    return 1.0 / (1.0 + jnp.exp(-v))


def _split_bf16(v):
    hi = v.astype(_BF16)
    return hi, (v - hi.astype(_F32)).astype(_BF16)


def _iota(shape, dim):
    return lax.broadcasted_iota(jnp.int32, shape, dim)


def _mixer_kernel(sinks_ref, xp_ref, xm_ref, g1_ref, w_in_ref, cos_ref, sin_ref, gn_ref, qg_ref, kg_ref,
                  wro_ref, wso_ref, wout_ref, g2_ref, wrt_ref, brt_ref,
                  x1_ref, h2_ref, info_ref, info_t_ref, counts_ref,
                  ret_s, swa_s, gate_s, retg_s, attn_s, state_ref, kv_prev_ref, *, tiles_per_seq):
    tm = ROW_TILE
    c = RET_CHUNK
    w = SWA_WINDOW
    d = SWA_HEAD_DIM
    pair = 2 * d
    step = pl.program_id(0)
    cur = step % 2
    ret_new, swa_new, gate_new = ret_s.at[cur], swa_s.at[cur], gate_s.at[cur]
    ret_old, swa_old, gate_old = ret_s.at[1 - cur], swa_s.at[1 - cur], gate_s.at[1 - cur]
    seq_start = jnp.logical_or(step == 0, (step - 1) % tiles_per_seq == 0)

    @pl.when(step == 0)
    def _():
        ret_s[1] = jnp.zeros(ret_s.shape[1:], ret_s.dtype)
        swa_s[1] = jnp.zeros(swa_s.shape[1:], swa_s.dtype)
        gate_s[1] = jnp.zeros(gate_s.shape[1:], gate_s.dtype)
        kv_prev_ref[...] = jnp.zeros_like(kv_prev_ref)

    @pl.when(seq_start)
    def _():
        state_ref[...] = jnp.zeros_like(state_ref)

    subs_p = [slice(s * INPROJ_SUB, (s + 1) * INPROJ_SUB) for s in range(tm // INPROJ_SUB)]
    hs = []
    for rows in subs_p:
        x = xp_ref[rows, :]
        r = lax.rsqrt(jnp.mean(x * x, axis=-1, keepdims=True) + NORM_EPS)
        hs.append(((x * r) * g1_ref[...]).astype(_BF16))
    chunk_pos = (_iota((INPROJ_SUB, RET_QK_DIM), 0) % c + 1).astype(_F32)

    def rotary_decay(base, sign, scale):
        def epilogue(rows, raw):
            cos = cos_ref[rows, :]
            sin = sin_ref[rows, :]
            for hd in range(RET_HEADS):
                v = raw[:, hd * RET_QK_DIM:(hd + 1) * RET_QK_DIM]
                rot = v * cos + pltpu.roll(v, RET_QK_DIM // 2, 1) * sin
                decay = jnp.exp((sign * RET_LOG_DECAY[hd]) * chunk_pos) * scale
                ret_new[rows, base + hd * RET_QK_DIM:base + (hd + 1) * RET_QK_DIM] = (rot * decay).astype(_BF16)
        return epilogue

    def store(ref, lo, fn):
        def epilogue(rows, raw):
            ref[rows, lo:lo + raw.shape[1]] = fn(raw).astype(_BF16)
        return epilogue

    def project(lo, width, epilogue):
        def job():
            raws = [_dot(h, w_in_ref[:, lo:lo + width]) for h in hs]
            for rows, raw in zip(subs_p, raws):
                epilogue(rows, raw)
        return job

    v_base = 2 * RET_QK_W
    g_base = v_base + RET_V_W
    gate_base = RET_COLS + SWA_COLS
    jobs = [project(0, RET_QK_W, rotary_decay(0, 1.0, 1.0)),
            project(RET_QK_W, RET_QK_W, rotary_decay(RET_QK_W, -1.0, RET_QK_DIM ** -0.5))]
    jobs += [project(v_base + col, 512, store(ret_new, v_base + col, lambda raw: raw)) for col in range(0, RET_V_W, 512)]
    jobs += [project(g_base + col, 512, store(ret_new, g_base + col,
                                             lambda raw, col=col: (raw * _sigmoid(raw)) * gn_ref[:, col:col + 512]))
             for col in range(0, RET_V_W, 512)]
    jobs += [project(RET_COLS + col, 256, store(swa_new, col, lambda raw: raw)) for col in range(0, SWA_COLS, 256)]
    jobs += [project(gate_base + col, 512, store(gate_new, col, _sigmoid)) for col in range(0, GATE_COLS, 512)]

    stages = []
    ctx = {}

    causal = _iota((c, c), 0) >= _iota((c, c), 1)
    chunks = [slice(ci * c, (ci + 1) * c) for ci in range(tm // c)]

    def retention_head(hd):
        def stage():
            chunk_decay = math.exp(RET_LOG_DECAY[hd] * c)
            qs = slice(hd * RET_QK_DIM, (hd + 1) * RET_QK_DIM)
            ks = slice(RET_QK_W + hd * RET_QK_DIM, RET_QK_W + (hd + 1) * RET_QK_DIM)
            vs = slice(v_base + hd * RET_V_DIM, v_base + (hd + 1) * RET_V_DIM)
            gs = slice(g_base + hd * RET_V_DIM, g_base + (hd + 1) * RET_V_DIM)
            scores = [jnp.where(causal, _dot_nt(ret_old[rows, qs], ret_old[rows, ks]), 0.0).astype(_BF16)
                      for rows in chunks]
            kvs = [_dot(ret_old[rows, ks].astype(_F32).T.astype(_BF16), ret_old[rows, vs]) for rows in chunks]
            state = state_ref[hd]
            states = []
            for kv in kvs:
                states.append(state.astype(_BF16))
                state = chunk_decay * (state + kv)
            state_ref[hd] = state
            rets = [_dot(jnp.concatenate([s, ret_old[rows, qs]], axis=1),
                         jnp.concatenate([ret_old[rows, vs], st], axis=0))
                    for rows, s, st in zip(chunks, scores, states)]
            for rows, ret in zip(chunks, rets):
                mu = jnp.mean(ret, axis=-1, keepdims=True)
                dev = ret - mu
                var = jnp.mean(dev * dev, axis=-1, keepdims=True)
                retg_s[rows, hd * RET_V_DIM:(hd + 1) * RET_V_DIM] = (
                    (dev * lax.rsqrt(var + NORM_EPS)) * ret_old[rows, gs].astype(_F32)).astype(_BF16)
        return stage

    stages += [retention_head(hd) for hd in range(RET_HEADS)]

    n_blk = tm // w

    def attention_setup():
        k_cols = slice(SWA_Q_W, SWA_Q_W + SWA_KV_W)
        v_cols = slice(SWA_Q_W + SWA_KV_W, SWA_Q_W + 2 * SWA_KV_W)
        k_both = jnp.concatenate([kv_prev_ref[:, 0:SWA_KV_W], swa_old[:, k_cols]], axis=0).astype(_F32)
        same_half = (_iota((pair, pair), 0) // d == _iota((pair, pair), 1) // d).astype(_BF16)
        k_hi, k_lo = _split_bf16(k_both * k_both)
        k_ssq = _dot(k_hi, same_half) + _dot(k_lo, same_half)
        ctx["k_n"] = k_both * lax.rsqrt(k_ssq * (1.0 / d) + NORM_EPS) * (kg_ref[...] * d ** -0.5)
        v_both = jnp.concatenate([kv_prev_ref[:, SWA_KV_W:2 * SWA_KV_W], swa_old[:, v_cols]], axis=0).astype(_F32)
        ctx["v_t"] = [v_both[b * w:(b + 1) * w].T for b in range(n_blk + 1)]
        kv_prev_ref[...] = swa_old[tm - w:tm, SWA_Q_W:SWA_Q_W + 2 * SWA_KV_W]

    stages.append(attention_setup)

    def attention_head(kk):
        def stage():
            key = _iota((2 * w, SWA_GROUP * w), 0)
            qry = _iota((2 * w, SWA_GROUP * w), 1) % w
            band = (key > qry) & (key <= qry + w)
            band_first = band & jnp.logical_or(key >= w, jnp.logical_not(seq_start))
            head_of_lane = _iota((1, SWA_GROUP * w), 1) // w
            half_of_lane = _iota((1, pair), 1) // d
            half_rows = (_iota((SUBLANES, pair), 0) == _iota((SUBLANES, pair), 1) // d).astype(_BF16)
            ones_rows = jnp.ones((2 * SUBLANES, 2 * w), _BF16)
            q_gain = qg_ref[...]
            k_native = jnp.where(half_of_lane == kk, ctx["k_n"], 0.0)
            k_moved = pltpu.roll(k_native, d, 1)
            k_even, k_odd = (k_native, k_moved) if kk == 0 else (k_moved, k_native)
            k_even = k_even.astype(_BF16)
            k_odd = k_odd.astype(_BF16)
            sink_row = jnp.zeros((1, SWA_GROUP * w), _F32)
            for g in range(SWA_GROUP):
                sink_row = jnp.where(head_of_lane == g, sinks_ref[kk * SWA_GROUP + g], sink_row)
            blocks = range(n_blk)
            pairs = range(SWA_GROUP // 2)
            q_pairs = [[swa_old[j * w:(j + 1) * w, (kk * SWA_GROUP + 2 * p) * d:(kk * SWA_GROUP + 2 * p + 2) * d]
                        .astype(_F32) for p in pairs] for j in blocks]
            q_splits = [[_split_bf16(q * q) for q in qs] for qs in q_pairs]
            q_ssq = [[_dot_nt(half_rows, hi) + _dot_nt(half_rows, lo) for hi, lo in sp] for sp in q_splits]
            q_rs = [[lax.rsqrt(s * (1.0 / d) + NORM_EPS) for s in ss] for ss in q_ssq]
            q_gs = [[(q * q_gain).astype(_BF16) for q in qs] for qs in q_pairs]
            raw_even = [[_dot_nt(k_even[j * w:(j + 2) * w], q_gs[j][p]) for p in pairs] for j in blocks]
            raw_odd = [[_dot_nt(k_odd[j * w:(j + 2) * w], q_gs[j][p]) for p in pairs] for j in blocks]
            score_blocks = []
            for j in blocks:
                raws = []
                for p in pairs:
                    raws.append(raw_even[j][p] * q_rs[j][p][0:1, :])
                    raws.append(raw_odd[j][p] * q_rs[j][p][1:2, :])
                s_t = jnp.concatenate(raws, axis=1)
                score_blocks.append(jnp.where(band_first if j == 0 else band, s_t, NEG_BIG))
            maxes = [jnp.maximum(jnp.max(s_t, axis=0, keepdims=True), sink_row) for s_t in score_blocks]
            probs = [jnp.exp(s_t - m).astype(_BF16) for s_t, m in zip(score_blocks, maxes)]
            v_t = ctx["v_t"]
            outs = []
            for j in blocks:
                v_ext = jnp.concatenate([v_t[j][kk * d:(kk + 1) * d], v_t[j + 1][kk * d:(kk + 1) * d]], axis=1)
                outs.append(_dot(jnp.concatenate([v_ext.astype(_BF16), ones_rows], axis=0), probs[j]))
            for j in blocks:
                rows = slice(j * w, (j + 1) * w)
                denom = outs[j][d:d + 1, :] + jnp.exp(sink_row - maxes[j])
                o_t = outs[j][0:d, :] * (1.0 / denom)
                for p in pairs:
                    c0 = (kk * SWA_GROUP + 2 * p) * d
                    both = jnp.concatenate(
                        [o_t[:, 2 * p * w:(2 * p + 1) * w], o_t[:, (2 * p + 1) * w:(2 * p + 2) * w]], axis=0)
                    attn_s[rows, c0:c0 + pair] = both.T.astype(_BF16)
        return stage

    stages += [attention_head(kk) for kk in range(SWA_KV_HEADS)]

    subs_m = [slice(s * OUTPROJ_SUB, (s + 1) * OUTPROJ_SUB) for s in range(tm // OUTPROJ_SUB)]

    def branch_projections():
        ctx["y_a"] = [_dot(retg_s[rows, :], wro_ref[...]) for rows in subs_m]
        ctx["y_b"] = [_dot(attn_s[rows, :], wso_ref[...]) for rows in subs_m]

    def merge_and_residual():
        merged = [(gate_old[rows, 0:D_MODEL].astype(_F32) * a
                   + gate_old[rows, D_MODEL:2 * D_MODEL].astype(_F32) * b).astype(_BF16)
                  for rows, a, b in zip(subs_m, ctx["y_a"], ctx["y_b"])]
        mixed = [_dot(m, wout_ref[...]) for m in merged]
        ctx["x1"] = [xm_ref[rows, :] + m for rows, m in zip(subs_m, mixed)]
        for rows, v in zip(subs_m, ctx["x1"]):
            x1_ref[rows, :] = v

    def norm_and_logits():
        h2 = [((v * lax.rsqrt(jnp.mean(v * v, axis=-1, keepdims=True) + NORM_EPS)) * g2_ref[...]).astype(_BF16)
              for v in ctx["x1"]]
        for rows, v in zip(subs_m, h2):
            h2_ref[rows, :] = v
        ctx["logits"] = jnp.concatenate([_dot_nt(wrt_ref[...], v) for v in h2], axis=1) + brt_ref[:, 0:1]

    def route_group():
        logits = ctx["logits"]
        row_f = _iota((LANES, tm), 0).astype(_F32)
        is_group = (row_f >= ROUTER_GROUP_LANE) & (row_f < ROUTER_GROUP_LANE + MOE_GROUPS)
        gl = jnp.where(is_group, logits, NEG_BIG)
        g_max = jnp.max(gl, axis=0, keepdims=True)
        ctx["g_prob"] = 1.0 / jnp.sum(jnp.exp(gl - g_max), axis=0, keepdims=True)
        g_idx = jnp.min(jnp.where(gl == g_max, row_f - ROUTER_GROUP_LANE, float(LANES)), axis=0, keepdims=True)
        e_lo = g_idx * MOE_EXPERTS_PER_GROUP
        in_group = (row_f >= e_lo) & (row_f < e_lo + MOE_EXPERTS_PER_GROUP)
        ctx["el"] = jnp.where(in_group, logits, NEG_BIG)

    def route_experts():
        el = ctx["el"]
        row_f = _iota((LANES, tm), 0).astype(_F32)
        t1 = jnp.max(el, axis=0, keepdims=True)
        i1 = jnp.min(jnp.where(el == t1, row_f, float(LANES)), axis=0, keepdims=True)
        el2 = jnp.where(row_f == i1, NEG_BIG, el)
        t2 = jnp.max(el2, axis=0, keepdims=True)
        i2 = jnp.min(jnp.where(el2 == t2, row_f, float(LANES)), axis=0, keepdims=True)
        e21 = jnp.exp(t2 - t1)
        ctx["i1"], ctx["i2"] = i1, i2
        ctx["w1"] = ctx["g_prob"] / (1.0 + e21)
        ctx["w2"] = ctx["g_prob"] * e21 / (1.0 + e21)

    def sort_positions():
        row_f = _iota((LANES, tm), 0).astype(_F32)
        i1, i2 = ctx["i1"], ctx["i2"]
        sel1 = row_f == i1
        sel2 = row_f == i2
        onehot = jnp.where(sel1 | sel2, 1.0, 0.0).astype(_BF16)
        earlier = (_iota((tm, tm), 0) < _iota((tm, tm), 1)).astype(_BF16)
        before = _dot(onehot, earlier)
        counts = _dot(onehot, jnp.ones((tm, LANES), _BF16))
        run = jnp.floor((counts + (RUN_ALIGN - 1.0)) * (1.0 / RUN_ALIGN)) * RUN_ALIGN
        lower_experts = (_iota((LANES, LANES), 0) > _iota((LANES, LANES), 1)).astype(_BF16)
        run_off = _dot(lower_experts, run.astype(_BF16))
        place = before + jnp.concatenate([run_off] * (tm // LANES), axis=1)
        pos1 = jnp.sum(jnp.where(sel1, place, 0.0), axis=0, keepdims=True)
        pos2 = jnp.sum(jnp.where(sel2, place, 0.0), axis=0, keepdims=True)
        counts_ref[...] = counts.T[0:SUBLANES, :]
        info_t = jnp.concatenate([i1, i2, ctx["w1"], ctx["w2"], pos1, pos2, jnp.zeros((2, tm), _F32)], axis=0)
        info_t_ref[...] = info_t
        info_ref[...] = jnp.concatenate([info_t, jnp.zeros((LANES - SUBLANES, tm), _F32)], axis=0).T

    stages += [branch_projections, merge_and_residual, norm_and_logits, route_group, route_experts, sort_positions]

    for k in range(max(len(jobs), len(stages))):
        if k < len(jobs):
            jobs[k]()
        if k < len(stages):
            stages[k]()


def _mixer(x2d, sinks, g1, w_in, cos_tab, sin_tab, gn_row, q_gain2, k_gain2, w_ret_o, w_swa_o, w_out, g2,
           w_router_t, b_router_t, tiles_per_seq):
    n = x2d.shape[0]
    tm = ROW_TILE
    n_tiles = n // tm
    const = lambda s: (0, 0)
    proj = lambda s: jnp.minimum(s, n_tiles - 1)
    mixed = lambda s: jnp.maximum(s - 1, 0)
    resident = lambda shape: pl.BlockSpec(shape, const, pipeline_mode=pl.Buffered(1))
    return pl.pallas_call(
        functools.partial(_mixer_kernel, tiles_per_seq=tiles_per_seq),
        grid=(n_tiles + 1,),
        in_specs=[
            pl.BlockSpec(memory_space=pltpu.SMEM),
            pl.BlockSpec((tm, D_MODEL), lambda s: (proj(s), 0)),
            pl.BlockSpec((tm, D_MODEL), lambda s: (mixed(s), 0)),
            pl.BlockSpec((1, D_MODEL), const),
            resident((D_MODEL, IN_WIDTH)),
            pl.BlockSpec((tm, RET_QK_DIM), lambda s: (proj(s) % tiles_per_seq, 0)),
            pl.BlockSpec((tm, RET_QK_DIM), lambda s: (proj(s) % tiles_per_seq, 0)),
            pl.BlockSpec((1, RET_V_W), const),
            pl.BlockSpec((1, 2 * SWA_HEAD_DIM), const),
            pl.BlockSpec((1, 2 * SWA_HEAD_DIM), const),
            resident((RET_V_W, D_MODEL)),
            resident((SWA_Q_W, D_MODEL)),
            resident((D_MODEL, D_MODEL)),
            pl.BlockSpec((1, D_MODEL), const),
            resident((LANES, D_MODEL)),
            pl.BlockSpec((LANES, LANES), const),
        ],
        out_specs=[
            pl.BlockSpec((tm, D_MODEL), lambda s: (mixed(s), 0)),
            pl.BlockSpec((tm, D_MODEL), lambda s: (mixed(s), 0)),
            pl.BlockSpec((tm, LANES), lambda s: (mixed(s), 0)),
            pl.BlockSpec((SUBLANES, tm), lambda s: (0, mixed(s))),
            pl.BlockSpec((SUBLANES, LANES), lambda s: (mixed(s), 0)),
        ],
        out_shape=[
            jax.ShapeDtypeStruct((n, D_MODEL), _F32),
            jax.ShapeDtypeStruct((n, D_MODEL), _BF16),
            jax.ShapeDtypeStruct((n, LANES), _F32),
            jax.ShapeDtypeStruct((SUBLANES, n), _F32),
            jax.ShapeDtypeStruct((n_tiles * SUBLANES, LANES), _F32),
        ],
        scratch_shapes=[
            pltpu.VMEM((2, tm, RET_COLS), _BF16),
            pltpu.VMEM((2, tm, SWA_COLS), _BF16),
            pltpu.VMEM((2, tm, GATE_COLS), _BF16),
            pltpu.VMEM((tm, RET_V_W), _BF16),
            pltpu.VMEM((tm, SWA_Q_W), _BF16),
            pltpu.VMEM((RET_HEADS, RET_QK_DIM, RET_V_DIM), _F32),
            pltpu.VMEM((SWA_WINDOW, 2 * SWA_KV_W), _BF16),
        ],
        compiler_params=pltpu.CompilerParams(
            dimension_semantics=("arbitrary",), vmem_limit_bytes=VMEM_LIMIT),
        name="mixer",
    )(sinks, x2d, x2d, g1, w_in, cos_tab, sin_tab, gn_row, q_gain2, k_gain2, w_ret_o, w_swa_o, w_out, g2,
      w_router_t, b_router_t)


def _inproj_kernel(x_ref, g_ref, w_ref, cos_ref, sin_ref, gn_ref, ret_ref, swa_ref, gate_ref):
    subs = [slice(s * INPROJ_SUB, (s + 1) * INPROJ_SUB) for s in range(ROW_TILE // INPROJ_SUB)]
    hs = []
    for rows in subs:
        x = x_ref[rows, :]
        r = lax.rsqrt(jnp.mean(x * x, axis=-1, keepdims=True) + NORM_EPS)
        hs.append(((x * r) * g_ref[...]).astype(_BF16))
    chunk_pos = (_iota((INPROJ_SUB, RET_QK_DIM), 0) % RET_CHUNK + 1).astype(_F32)

    def rotary_decay(base, sign, scale):
        def epilogue(rows, raw):
            cos = cos_ref[rows, :]
            sin = sin_ref[rows, :]
            for hd in range(RET_HEADS):
                v = raw[:, hd * RET_QK_DIM:(hd + 1) * RET_QK_DIM]
                rot = v * cos + pltpu.roll(v, RET_QK_DIM // 2, 1) * sin
                decay = jnp.exp((sign * RET_LOG_DECAY[hd]) * chunk_pos) * scale
                ret_ref[rows, base + hd * RET_QK_DIM:base + (hd + 1) * RET_QK_DIM] = (rot * decay).astype(_BF16)
        return epilogue

    def store(ref, lo, fn):
        def epilogue(rows, raw):
            ref[rows, lo:lo + raw.shape[1]] = fn(raw).astype(_BF16)
        return epilogue

    v_base = 2 * RET_QK_W
    g_base = v_base + RET_V_W
    gate_base = RET_COLS + SWA_COLS
    jobs = [(0, RET_QK_W, rotary_decay(0, 1.0, 1.0)),
            (RET_QK_W, RET_QK_W, rotary_decay(RET_QK_W, -1.0, RET_QK_DIM ** -0.5))]
    jobs += [(v_base + c, 512, store(ret_ref, v_base + c, lambda raw: raw)) for c in range(0, RET_V_W, 512)]
    jobs += [(g_base + c, 512, store(ret_ref, g_base + c,
                                     lambda raw, c=c: (raw * _sigmoid(raw)) * gn_ref[:, c:c + 512]))
             for c in range(0, RET_V_W, 512)]
    jobs += [(RET_COLS + c, 256, store(swa_ref, c, lambda raw: raw)) for c in range(0, SWA_COLS, 256)]
    jobs += [(gate_base + c, 512, store(gate_ref, c, _sigmoid)) for c in range(0, GATE_COLS, 512)]
    for lo, width, epilogue in jobs:
        raws = [_dot(h, w_ref[:, lo:lo + width]) for h in hs]
        for rows, raw in zip(subs, raws):
            epilogue(rows, raw)


def _inproj(x2d, g, w_in_bf16, cos_tab, sin_tab, gn_row):
    n = x2d.shape[0]
    grid = (n // ROW_TILE,)
    seq_tiles = cos_tab.shape[0] // ROW_TILE
    return pl.pallas_call(
        _inproj_kernel,
        grid=grid,
        in_specs=[
            pl.BlockSpec((ROW_TILE, D_MODEL), lambda i: (i, 0)),
            pl.BlockSpec((1, D_MODEL), lambda i: (0, 0)),
            pl.BlockSpec((D_MODEL, IN_WIDTH), lambda i: (0, 0), pipeline_mode=pl.Buffered(1)),
            pl.BlockSpec((ROW_TILE, RET_QK_DIM), lambda i: (i % seq_tiles, 0)),
            pl.BlockSpec((ROW_TILE, RET_QK_DIM), lambda i: (i % seq_tiles, 0)),
            pl.BlockSpec((1, RET_V_W), lambda i: (0, 0)),
        ],
        out_specs=[
            pl.BlockSpec((ROW_TILE, RET_COLS), lambda i: (i, 0)),
            pl.BlockSpec((ROW_TILE, SWA_COLS), lambda i: (i, 0)),
            pl.BlockSpec((ROW_TILE, GATE_COLS), lambda i: (i, 0)),
        ],
        out_shape=[
            jax.ShapeDtypeStruct((n, RET_COLS), _BF16),
            jax.ShapeDtypeStruct((n, SWA_COLS), _BF16),
            jax.ShapeDtypeStruct((n, GATE_COLS), _BF16),
        ],
        compiler_params=pltpu.CompilerParams(
            dimension_semantics=("arbitrary",), vmem_limit_bytes=VMEM_LIMIT),
        name="inproj",
    )(x2d, g, w_in_bf16, cos_tab, sin_tab, gn_row)


def _retention_kernel(q_ref, k_ref, v_ref, g_ref, out_ref, state_ref):
    c = RET_CHUNK

    @pl.when(pl.program_id(1) == 0)
    def _():
        state_ref[...] = jnp.zeros_like(state_ref)

    causal = _iota((c, c), 0) >= _iota((c, c), 1)

    for h in range(RET_HEADS):
        chunk_decay = math.exp(RET_LOG_DECAY[h] * c)
        qs = slice(h * RET_QK_DIM, (h + 1) * RET_QK_DIM)
        vs = slice(h * RET_V_DIM, (h + 1) * RET_V_DIM)
        chunks = [slice(ci * c, (ci + 1) * c) for ci in range(SEQ_TILE // c)]
        scores = [jnp.where(causal, _dot_nt(q_ref[rows, qs], k_ref[rows, qs]), 0.0).astype(_BF16) for rows in chunks]
        kvs = [_dot(k_ref[rows, qs].astype(_F32).T.astype(_BF16), v_ref[rows, vs]) for rows in chunks]
        state = state_ref[h]
        states = []
        for kv in kvs:
            states.append(state.astype(_BF16))
            state = chunk_decay * (state + kv)
        state_ref[h] = state
        rets = [_dot(jnp.concatenate([s, q_ref[rows, qs]], axis=1), jnp.concatenate([v_ref[rows, vs], st], axis=0))
                for rows, s, st in zip(chunks, scores, states)]
        for rows, ret in zip(chunks, rets):
            mu = jnp.mean(ret, axis=-1, keepdims=True)
            dev = ret - mu
            var = jnp.mean(dev * dev, axis=-1, keepdims=True)
            out_ref[rows, vs] = ((dev * lax.rsqrt(var + NORM_EPS)) * g_ref[rows, vs].astype(_F32)).astype(_BF16)


def _retention(ret_proj, batch, seq):
    n = ret_proj.shape[0]
    steps = seq // SEQ_TILE
    rowmap = lambda b, s: b * steps + s
    return pl.pallas_call(
        _retention_kernel,
        grid=(batch, steps),
        in_specs=[
            pl.BlockSpec((SEQ_TILE, RET_QK_W), lambda b, s: (rowmap(b, s), 0)),
            pl.BlockSpec((SEQ_TILE, RET_QK_W), lambda b, s: (rowmap(b, s), 1)),
            pl.BlockSpec((SEQ_TILE, RET_V_W), lambda b, s: (rowmap(b, s), 1)),
            pl.BlockSpec((SEQ_TILE, RET_V_W), lambda b, s: (rowmap(b, s), 2)),
        ],
        out_specs=pl.BlockSpec((SEQ_TILE, RET_V_W), lambda b, s: (rowmap(b, s), 0)),
        out_shape=jax.ShapeDtypeStruct((n, RET_V_W), _BF16),
        scratch_shapes=[pltpu.VMEM((RET_HEADS, RET_QK_DIM, RET_V_DIM), _F32)],
        compiler_params=pltpu.CompilerParams(
            dimension_semantics=("arbitrary", "arbitrary"), vmem_limit_bytes=VMEM_LIMIT),
        name="retention",
    )(ret_proj, ret_proj, ret_proj, ret_proj)


def _swa_kernel(sinks_ref, q_ref, kc_ref, vc_ref, kp_ref, vp_ref, qg_ref, kg_ref, out_ref):
    w = SWA_WINDOW
    d = SWA_HEAD_DIM
    n_blk = SEQ_TILE // w
    pair = 2 * d
    first = pl.program_id(1) == 0

    k_both = jnp.concatenate([kp_ref[...], kc_ref[...]], axis=0).astype(_F32)
    same_half = (_iota((pair, pair), 0) // d == _iota((pair, pair), 1) // d).astype(_BF16)
    k_hi, k_lo = _split_bf16(k_both * k_both)
    k_ssq = _dot(k_hi, same_half) + _dot(k_lo, same_half)
    k_n = k_both * lax.rsqrt(k_ssq * (1.0 / d) + NORM_EPS) * (kg_ref[...] * d ** -0.5)
    v_both = jnp.concatenate([vp_ref[...], vc_ref[...]], axis=0).astype(_F32)
    v_t = [v_both[b * w:(b + 1) * w].T for b in range(n_blk + 1)]

    key = _iota((2 * w, SWA_GROUP * w), 0)
    qry = _iota((2 * w, SWA_GROUP * w), 1) % w
    band = (key > qry) & (key <= qry + w)
    band_first = band & jnp.logical_or(key >= w, jnp.logical_not(first))
    head_of_lane = _iota((1, SWA_GROUP * w), 1) // w
    half_of_lane = _iota((1, pair), 1) // d
    half_rows = (_iota((SUBLANES, pair), 0) == _iota((SUBLANES, pair), 1) // d).astype(_BF16)
    ones_rows = jnp.ones((2 * SUBLANES, 2 * w), _BF16)
    q_gain = qg_ref[...]

    for kk in range(SWA_KV_HEADS):
        k_native = jnp.where(half_of_lane == kk, k_n, 0.0)
        k_moved = pltpu.roll(k_native, d, 1)
        k_even, k_odd = (k_native, k_moved) if kk == 0 else (k_moved, k_native)
        k_even = k_even.astype(_BF16)
        k_odd = k_odd.astype(_BF16)
        sink_row = jnp.zeros((1, SWA_GROUP * w), _F32)
        for g in range(SWA_GROUP):
            sink_row = jnp.where(head_of_lane == g, sinks_ref[kk * SWA_GROUP + g], sink_row)
        blocks = range(n_blk)
        pairs = range(SWA_GROUP // 2)
        q_pairs = [[q_ref[j * w:(j + 1) * w, (kk * SWA_GROUP + 2 * p) * d:(kk * SWA_GROUP + 2 * p + 2) * d].astype(_F32)
                    for p in pairs] for j in blocks]
        q_splits = [[_split_bf16(q * q) for q in qs] for qs in q_pairs]
        q_ssq = [[_dot_nt(half_rows, hi) + _dot_nt(half_rows, lo) for hi, lo in sp] for sp in q_splits]
        q_rs = [[lax.rsqrt(s * (1.0 / d) + NORM_EPS) for s in ss] for ss in q_ssq]
        q_gs = [[(q * q_gain).astype(_BF16) for q in qs] for qs in q_pairs]
        raw_even = [[_dot_nt(k_even[j * w:(j + 2) * w], q_gs[j][p]) for p in pairs] for j in blocks]
        raw_odd = [[_dot_nt(k_odd[j * w:(j + 2) * w], q_gs[j][p]) for p in pairs] for j in blocks]
        score_blocks = []
        for j in blocks:
            raws = []
            for p in pairs:
                raws.append(raw_even[j][p] * q_rs[j][p][0:1, :])
                raws.append(raw_odd[j][p] * q_rs[j][p][1:2, :])
            s_t = jnp.concatenate(raws, axis=1)
            score_blocks.append(jnp.where(band_first if j == 0 else band, s_t, NEG_BIG))
        maxes = [jnp.maximum(jnp.max(s_t, axis=0, keepdims=True), sink_row) for s_t in score_blocks]
        probs = [jnp.exp(s_t - m).astype(_BF16) for s_t, m in zip(score_blocks, maxes)]
        outs = []
        for j in blocks:
            v_ext = jnp.concatenate([v_t[j][kk * d:(kk + 1) * d], v_t[j + 1][kk * d:(kk + 1) * d]], axis=1)
            outs.append(_dot(jnp.concatenate([v_ext.astype(_BF16), ones_rows], axis=0), probs[j]))
        for j in blocks:
            rows = slice(j * w, (j + 1) * w)
            denom = outs[j][d:d + 1, :] + jnp.exp(sink_row - maxes[j])
            o_t = outs[j][0:d, :] * (1.0 / denom)
            for p in range(SWA_GROUP // 2):
                c0 = (kk * SWA_GROUP + 2 * p) * d
                both = jnp.concatenate([o_t[:, 2 * p * w:(2 * p + 1) * w], o_t[:, (2 * p + 1) * w:(2 * p + 2) * w]], axis=0)
                out_ref[rows, c0:c0 + pair] = both.T.astype(_BF16)


def _swa(swa_proj, sinks, q_gain2, k_gain2, batch, seq):
    n = swa_proj.shape[0]
    steps = seq // SEQ_TILE
    blocks_per_step = SEQ_TILE // SWA_WINDOW
    rowmap = lambda b, s: b * steps + s
    prevmap = lambda b, s: jnp.maximum((b * steps + s) * blocks_per_step - 1, 0)
    k_col = SWA_Q_W // SWA_KV_W
    return pl.pallas_call(
        _swa_kernel,
        grid=(batch, steps),
        in_specs=[
            pl.BlockSpec(memory_space=pltpu.SMEM),
            pl.BlockSpec((SEQ_TILE, SWA_Q_W), lambda b, s: (rowmap(b, s), 0)),
            pl.BlockSpec((SEQ_TILE, SWA_KV_W), lambda b, s: (rowmap(b, s), k_col)),
            pl.BlockSpec((SEQ_TILE, SWA_KV_W), lambda b, s: (rowmap(b, s), k_col + 1)),
            pl.BlockSpec((SWA_WINDOW, SWA_KV_W), lambda b, s: (prevmap(b, s), k_col)),
            pl.BlockSpec((SWA_WINDOW, SWA_KV_W), lambda b, s: (prevmap(b, s), k_col + 1)),
            pl.BlockSpec((1, 2 * SWA_HEAD_DIM), lambda b, s: (0, 0)),
            pl.BlockSpec((1, 2 * SWA_HEAD_DIM), lambda b, s: (0, 0)),
        ],
        out_specs=pl.BlockSpec((SEQ_TILE, SWA_Q_W), lambda b, s: (rowmap(b, s), 0)),
        out_shape=jax.ShapeDtypeStruct((n, SWA_Q_W), _BF16),
        compiler_params=pltpu.CompilerParams(
            dimension_semantics=("arbitrary", "arbitrary"), vmem_limit_bytes=VMEM_LIMIT),
        name="swa",
    )(sinks, swa_proj, swa_proj, swa_proj, swa_proj, swa_proj, q_gain2, k_gain2)


def _outproj_kernel(x_ref, retg_ref, attn_ref, ga_ref, gb_ref, wro_ref, wso_ref, wout_ref, g2_ref,
                    wrt_ref, brt_ref, x1_ref, h2_ref, info_ref, info_t_ref, counts_ref, logits_ref):
    tm = ROW_TILE
    step = pl.program_id(0)
    cur = step % 2

    @pl.when(step == 0)
    def _():
        logits_ref[...] = jnp.zeros_like(logits_ref)

    logits = logits_ref[1 - cur]
    row = _iota((LANES, tm), 0)
    row_f = row.astype(_F32)
    is_group = (row >= ROUTER_GROUP_LANE) & (row < ROUTER_GROUP_LANE + MOE_GROUPS)

    subs = [slice(s * OUTPROJ_SUB, (s + 1) * OUTPROJ_SUB) for s in range(tm // OUTPROJ_SUB)]
    y_a = [_dot(retg_ref[rows, :], wro_ref[...]) for rows in subs]

    gl = jnp.where(is_group, logits, NEG_BIG)
    g_max = jnp.max(gl, axis=0, keepdims=True)
    g_prob = 1.0 / jnp.sum(jnp.exp(gl - g_max), axis=0, keepdims=True)
    g_idx = jnp.min(jnp.where(gl == g_max, row_f - ROUTER_GROUP_LANE, float(LANES)), axis=0, keepdims=True)

    y_b = [_dot(attn_ref[rows, :], wso_ref[...]) for rows in subs]

    e_lo = g_idx * MOE_EXPERTS_PER_GROUP
    in_group = (row_f >= e_lo) & (row_f < e_lo + MOE_EXPERTS_PER_GROUP)
    el = jnp.where(in_group, logits, NEG_BIG)
    t1 = jnp.max(el, axis=0, keepdims=True)
    i1 = jnp.min(jnp.where(el == t1, row_f, float(LANES)), axis=0, keepdims=True)

    merged = [(ga_ref[rows, :].astype(_F32) * a + gb_ref[rows, :].astype(_F32) * b).astype(_BF16)
              for rows, a, b in zip(subs, y_a, y_b)]
    mixed = [_dot(m, wout_ref[...]) for m in merged]
    x1 = [x_ref[rows, :] + m for rows, m in zip(subs, mixed)]
    for rows, v in zip(subs, x1):
        x1_ref[rows, :] = v

    el2 = jnp.where(row_f == i1, NEG_BIG, el)
    t2 = jnp.max(el2, axis=0, keepdims=True)
    i2 = jnp.min(jnp.where(el2 == t2, row_f, float(LANES)), axis=0, keepdims=True)
    e21 = jnp.exp(t2 - t1)
    w1 = g_prob / (1.0 + e21)
    w2 = g_prob * e21 / (1.0 + e21)

    h2 = [((v * lax.rsqrt(jnp.mean(v * v, axis=-1, keepdims=True) + NORM_EPS)) * g2_ref[...]).astype(_BF16)
          for v in x1]
    for rows, v in zip(subs, h2):
        h2_ref[rows, :] = v

    sel1 = row_f == i1
    sel2 = row_f == i2
    onehot = jnp.where(sel1 | sel2, 1.0, 0.0).astype(_BF16)
    earlier = (_iota((tm, tm), 0) < _iota((tm, tm), 1)).astype(_BF16)
    before = _dot(onehot, earlier)
    counts = _dot(onehot, jnp.ones((tm, LANES), _BF16))

    for rows, v in zip(subs, h2):
        logits_ref[cur, :, rows] = _dot_nt(wrt_ref[...], v) + brt_ref[:, 0:1]

    run = jnp.floor((counts + (RUN_ALIGN - 1.0)) * (1.0 / RUN_ALIGN)) * RUN_ALIGN
    lower_experts = (_iota((LANES, LANES), 0) > _iota((LANES, LANES), 1)).astype(_BF16)
    run_off = _dot(lower_experts, run.astype(_BF16))
    place = before + jnp.concatenate([run_off] * (tm // LANES), axis=1)
    pos1 = jnp.sum(jnp.where(sel1, place, 0.0), axis=0, keepdims=True)
    pos2 = jnp.sum(jnp.where(sel2, place, 0.0), axis=0, keepdims=True)
    counts_ref[...] = counts.T[0:SUBLANES, :]

    info_t = jnp.concatenate([i1, i2, w1, w2, pos1, pos2, jnp.zeros((2, tm), _F32)], axis=0)
    info_t_ref[...] = info_t
    info_ref[...] = jnp.concatenate([info_t, jnp.zeros((LANES - SUBLANES, tm), _F32)], axis=0).T


def _outproj(x2d, retg, attn, gates, w_ret_o, w_swa_o, w_out, g2, w_router_t, b_router_t):
    n = x2d.shape[0]
    tm = ROW_TILE
    n_tiles = n // tm
    const = lambda s: (0, 0)
    proj = lambda s: jnp.minimum(s, n_tiles - 1)
    routed = lambda s: jnp.maximum(s - 1, 0)
    return pl.pallas_call(
        _outproj_kernel,
        grid=(n_tiles + 1,),
        in_specs=[
            pl.BlockSpec((tm, D_MODEL), lambda s: (proj(s), 0)),
            pl.BlockSpec((tm, RET_V_W), lambda s: (proj(s), 0)),
            pl.BlockSpec((tm, SWA_Q_W), lambda s: (proj(s), 0)),
            pl.BlockSpec((tm, D_MODEL), lambda s: (proj(s), 0)),
            pl.BlockSpec((tm, D_MODEL), lambda s: (proj(s), 1)),
            pl.BlockSpec((RET_V_W, D_MODEL), const),
            pl.BlockSpec((SWA_Q_W, D_MODEL), const),
            pl.BlockSpec((D_MODEL, D_MODEL), const),
            pl.BlockSpec((1, D_MODEL), const),
            pl.BlockSpec((LANES, D_MODEL), const),
            pl.BlockSpec((LANES, LANES), const),
        ],
        out_specs=[
            pl.BlockSpec((tm, D_MODEL), lambda s: (proj(s), 0)),
            pl.BlockSpec((tm, D_MODEL), lambda s: (proj(s), 0)),
            pl.BlockSpec((tm, LANES), lambda s: (routed(s), 0)),
            pl.BlockSpec((SUBLANES, tm), lambda s: (0, routed(s))),
            pl.BlockSpec((SUBLANES, LANES), lambda s: (routed(s), 0)),
        ],
        scratch_shapes=[pltpu.VMEM((2, LANES, tm), _F32)],
        out_shape=[
            jax.ShapeDtypeStruct((n, D_MODEL), _F32),
            jax.ShapeDtypeStruct((n, D_MODEL), _BF16),
            jax.ShapeDtypeStruct((n, LANES), _F32),
            jax.ShapeDtypeStruct((SUBLANES, n), _F32),
            jax.ShapeDtypeStruct((n // tm * SUBLANES, LANES), _F32),
        ],
        compiler_params=pltpu.CompilerParams(
            dimension_semantics=("arbitrary",), vmem_limit_bytes=VMEM_LIMIT),
        name="outproj",
    )(x2d, retg, attn, gates, gates, w_ret_o, w_swa_o, w_out, g2, w_router_t, b_router_t)


def _piece_sizes(largest):
    piece = largest
    while piece >= RUN_ALIGN:
        yield piece
        piece //= 2


BIG_CAP = SORT_ROWS // RUN_PIECE
SMALL_CAP = MOE_N_EXPERTS
PIECE_LIST_KEYS = ("n_big", "big_off", "big_dst", "n_small", "small_off", "small_dst")


def _start_run_copies(pieces, tile, make_copy):
    n_big_ref, big_off_ref, big_dst_ref, n_small_ref, small_off_ref, small_dst_ref = pieces

    def big(k, carry):
        i = tile * BIG_CAP + k
        make_copy(pl.multiple_of(big_off_ref[i], RUN_ALIGN), pl.multiple_of(big_dst_ref[i], RUN_ALIGN),
                  RUN_PIECE).start()
        return carry

    lax.fori_loop(0, n_big_ref[tile], big, 0)

    def small(k, carry):
        i = tile * SMALL_CAP + k
        make_copy(pl.multiple_of(small_off_ref[i], RUN_ALIGN), pl.multiple_of(small_dst_ref[i], RUN_ALIGN),
                  RUN_ALIGN).start()
        return carry

    lax.fori_loop(0, n_small_ref[tile], small, 0)


def _wait_rows(rows, make_copy):
    for piece in _piece_sizes(SORT_ROWS_POW2 // 2):
        pl.when((rows & piece) != 0)(make_copy(0, 0, piece).wait)


def _zero_fill_padding(pad_start_ref, pad_len_ref, zeros_ref, xs_hbm, sem, wait):
    def per_expert(e, carry):
        pos = pad_start_ref[e]
        length = pad_len_ref[e]
        for piece in _piece_sizes(EXPERT_TILE // 2):
            copy = pltpu.make_async_copy(
                zeros_ref.at[pl.ds(0, piece)], xs_hbm.at[pl.ds(pl.multiple_of(pos, RUN_ALIGN), piece)], sem)
            pl.when((length & piece) != 0)(copy.wait if wait else copy.start)
            pos = pos + (length & piece)
        return carry

    lax.fori_loop(0, MOE_N_EXPERTS, per_expert, 0)

    tail_start = pad_start_ref[MOE_N_EXPERTS]
    n_rows = xs_hbm.shape[0]
    half = EXPERT_TILE // 2

    def per_half_tile(j, carry):
        pos = tail_start + j * half
        copy = pltpu.make_async_copy(
            zeros_ref, xs_hbm.at[pl.ds(pl.multiple_of(jnp.minimum(pos, n_rows - half), RUN_ALIGN), half)], sem)
        pl.when(pos < n_rows)(copy.wait if wait else copy.start)
        return carry

    lax.fori_loop(0, n_rows // half, per_half_tile, 0)


def _dispatch_kernel(*refs):
    pieces = refs[:len(PIECE_LIST_KEYS)]
    rows_ref, pad_start_ref, pad_len_ref, h2_ref, info_t_ref, xs_hbm, sorted_ref, zeros_ref, sems, zero_sem = (
        refs[len(PIECE_LIST_KEYS):])
    tile = pl.program_id(0)
    last = pl.num_programs(0) - 1
    slot = tile % 2
    tm = ROW_TILE

    @pl.when(tile == 0)
    def _():
        zeros_ref[...] = jnp.zeros_like(zeros_ref)
        _zero_fill_padding(pad_start_ref, pad_len_ref, zeros_ref, xs_hbm, zero_sem, wait=False)
        _zero_fill_padding(pad_start_ref, pad_len_ref, zeros_ref, xs_hbm, zero_sem, wait=True)

    pos = info_t_ref[4:6, :].astype(jnp.int32)
    place = _iota((SORT_ROWS, tm), 0)
    onehot = ((place == pos[0:1, :]) | (place == pos[1:2, :])).astype(_BF16)
    for c in range(0, D_MODEL, 256):
        sorted_ref[slot, :, c:c + 256] = _dot(onehot, h2_ref[:, c:c + 256]).astype(_BF16)

    def copy_maker(t):
        s = t % 2

        def make_copy(off, dst, piece):
            return pltpu.make_async_copy(
                sorted_ref.at[s, pl.ds(off, piece)], xs_hbm.at[pl.ds(dst, piece)], sems.at[s])

        return make_copy

    _start_run_copies(pieces, tile, copy_maker(tile))

    @pl.when(tile > 0)
    def _():
        _wait_rows(rows_ref[tile - 1], copy_maker(tile - 1))

    @pl.when(tile == last)
    def _():
        _wait_rows(rows_ref[tile], copy_maker(tile))


def _dispatch(plan, h2, info_t, n_rows):
    n = h2.shape[0]
    grid_spec = pltpu.PrefetchScalarGridSpec(
        num_scalar_prefetch=len(PIECE_LIST_KEYS) + 3,
        grid=(n // ROW_TILE,),
        in_specs=[
            pl.BlockSpec((ROW_TILE, D_MODEL), lambda i, *_: (i, 0)),
            pl.BlockSpec((SUBLANES, ROW_TILE), lambda i, *_: (0, i)),
        ],
        out_specs=pl.BlockSpec(memory_space=pl.ANY),
        scratch_shapes=[
            pltpu.VMEM((2, SORT_ROWS, D_MODEL), _BF16),
            pltpu.VMEM((EXPERT_TILE // 2, D_MODEL), _BF16),
            pltpu.SemaphoreType.DMA((2,)),
            pltpu.SemaphoreType.DMA(()),
        ],
    )
    return pl.pallas_call(
        _dispatch_kernel,
        grid_spec=grid_spec,
        out_shape=jax.ShapeDtypeStruct((n_rows, D_MODEL), _BF16),
        compiler_params=pltpu.CompilerParams(
            dimension_semantics=("arbitrary",), vmem_limit_bytes=VMEM_LIMIT),
        name="dispatch",
    )(*[plan[k] for k in PIECE_LIST_KEYS], plan["rows"], plan["pad_start"], plan["pad_len"], h2, info_t)


def _experts_kernel(te_ref, tv_ref, xs_ref, wg_ref, wu_ref, wd_ref, ys_ref, wg_s, wu_s, wd_s):
    i = pl.program_id(0)
    valid = tv_ref[i]
    prev = te_ref[jnp.maximum(i - 1, 0)]
    changed = jnp.logical_or(i == 0, te_ref[i] != prev)

    @pl.when(jnp.logical_and(valid > 0, changed))
    def _():
        wg_s[...] = wg_ref[...].astype(_BF16)
        wu_s[...] = wu_ref[...].astype(_BF16)
        wd_s[...] = wd_ref[...].astype(_BF16)

    @pl.when(valid > 0)
    def _():
        subs = [slice(s * EXPERT_SUB, (s + 1) * EXPERT_SUB) for s in range(EXPERT_TILE // EXPERT_SUB)]
        gate = [_dot(xs_ref[rows, :], wg_s[...]) for rows in subs]
        up = [_dot(xs_ref[rows, :], wu_s[...]) for rows in subs]
        act = [((g * _sigmoid(g)) * u).astype(_BF16) for g, u in zip(gate, up)]
        for rows, a in zip(subs, act):
            ys_ref[rows, :] = _dot(a, wd_s[...]).astype(_BF16)

    @pl.when(valid <= 0)
    def _():
        ys_ref[...] = jnp.zeros_like(ys_ref)


def _experts(plan, xs, w_gate, w_up, w_down):
    n_rows = xs.shape[0]
    n_tiles = n_rows // EXPERT_TILE
    grid_spec = pltpu.PrefetchScalarGridSpec(
        num_scalar_prefetch=2,
        grid=(n_tiles,),
        in_specs=[
            pl.BlockSpec((EXPERT_TILE, D_MODEL), lambda i, te, tv: (i, 0)),
            pl.BlockSpec((None, D_MODEL, MOE_D_FF), lambda i, te, tv: (te[i], 0, 0)),
            pl.BlockSpec((None, D_MODEL, MOE_D_FF), lambda i, te, tv: (te[i], 0, 0)),
            pl.BlockSpec((None, MOE_D_FF, D_MODEL), lambda i, te, tv: (te[i], 0, 0)),
        ],
        out_specs=pl.BlockSpec((EXPERT_TILE, D_MODEL), lambda i, te, tv: (i, 0)),
        scratch_shapes=[
            pltpu.VMEM((D_MODEL, MOE_D_FF), _BF16),
            pltpu.VMEM((D_MODEL, MOE_D_FF), _BF16),
            pltpu.VMEM((MOE_D_FF, D_MODEL), _BF16),
        ],
    )
    return pl.pallas_call(
        _experts_kernel,
        grid_spec=grid_spec,
        out_shape=jax.ShapeDtypeStruct((n_rows, D_MODEL), _BF16),
        compiler_params=pltpu.CompilerParams(
            dimension_semantics=("arbitrary",), vmem_limit_bytes=VMEM_LIMIT),
        name="experts",
    )(plan["tile_expert"], plan["tile_valid"], xs, w_gate, w_up, w_down)


def _combine_kernel(*refs):
    pieces = refs[:len(PIECE_LIST_KEYS)]
    rows_ref, x1_ref, info_ref, ys_hbm, out_ref, block_ref, sems = refs[len(PIECE_LIST_KEYS):]
    tile = pl.program_id(0)
    last = pl.num_programs(0) - 1
    slot = tile % 2
    tm = ROW_TILE

    def copy_maker(t):
        s = t % 2

        def make_copy(off, dst, piece):
            return pltpu.make_async_copy(
                ys_hbm.at[pl.ds(dst, piece)], block_ref.at[s, pl.ds(off, piece)], sems.at[s])

        return make_copy

    @pl.when(tile == 0)
    def _():
        block_ref[...] = jnp.zeros_like(block_ref)
        _start_run_copies(pieces, tile, copy_maker(tile))

    @pl.when(tile < last)
    def _():
        _start_run_copies(pieces, tile + 1, copy_maker(tile + 1))

    _wait_rows(rows_ref[tile], copy_maker(tile))

    info = info_ref[...]
    place = _iota((tm, SORT_ROWS), 1).astype(_F32)
    weights = (jnp.where(place == info[:, 4:5], info[:, 2:3], 0.0)
               + jnp.where(place == info[:, 5:6], info[:, 3:4], 0.0)).astype(_BF16)
    for c in range(0, D_MODEL, 256):
        out_ref[:, c:c + 256] = x1_ref[:, c:c + 256] + _dot(weights, block_ref[slot, :, c:c + 256])


def _combine(plan, x1, info, ys):
    n = x1.shape[0]
    grid_spec = pltpu.PrefetchScalarGridSpec(
        num_scalar_prefetch=len(PIECE_LIST_KEYS) + 1,
        grid=(n // ROW_TILE,),
        in_specs=[
            pl.BlockSpec((ROW_TILE, D_MODEL), lambda i, *_: (i, 0)),
            pl.BlockSpec((ROW_TILE, LANES), lambda i, *_: (i, 0)),
            pl.BlockSpec(memory_space=pl.ANY),
        ],
        out_specs=pl.BlockSpec((ROW_TILE, D_MODEL), lambda i, *_: (i, 0)),
        scratch_shapes=[pltpu.VMEM((2, SORT_ROWS, D_MODEL), _BF16), pltpu.SemaphoreType.DMA((2,))],
    )
    return pl.pallas_call(
        _combine_kernel,
        grid_spec=grid_spec,
        out_shape=jax.ShapeDtypeStruct((n, D_MODEL), _F32),
        compiler_params=pltpu.CompilerParams(
            dimension_semantics=("arbitrary",), vmem_limit_bytes=VMEM_LIMIT),
        name="combine",
    )(*[plan[k] for k in PIECE_LIST_KEYS], plan["rows"], x1, info, ys)


def _rotary_tables(seq):
    half = RET_QK_DIM // 2
    inv = ROPE_BASE ** (-jnp.arange(half, dtype=_F32) / half)
    hi = (jnp.arange(seq // RET_CHUNK, dtype=jnp.int32) * RET_CHUNK).astype(_F32)[:, None] * inv[None, :]
    lo = jnp.arange(RET_CHUNK, dtype=jnp.int32).astype(_F32)[:, None] * inv[None, :]
    cos_hi, sin_hi = jnp.cos(hi)[:, None, :], jnp.sin(hi)[:, None, :]
    cos_lo, sin_lo = jnp.cos(lo)[None, :, :], jnp.sin(lo)[None, :, :]
    cos = (cos_hi * cos_lo - sin_hi * sin_lo).reshape(seq, half)
    sin = (sin_hi * cos_lo + cos_hi * sin_lo).reshape(seq, half)
    return jnp.concatenate([cos, cos], axis=1), jnp.concatenate([-sin, sin], axis=1)


def _expert_row_bound(n_tokens):
    rows = (2 * n_tokens + (n_tokens // ROW_TILE) * MOE_N_EXPERTS * (RUN_ALIGN - 1)
            + MOE_N_EXPERTS * (EXPERT_TILE - RUN_ALIGN))
    return -(-rows // EXPERT_TILE) * EXPERT_TILE


def _routing_plan(counts_out, n_rows):
    te = EXPERT_TILE
    n_tok_tiles = counts_out.shape[0] // SUBLANES
    counts = counts_out.reshape(n_tok_tiles, SUBLANES, LANES)[:, 0, :MOE_N_EXPERTS].astype(jnp.int32)
    run = (counts + RUN_ALIGN - 1) // RUN_ALIGN * RUN_ALIGN
    off = jnp.cumsum(run, axis=1) - run
    total = jnp.sum(run, axis=0)
    padded = (total + te - 1) // te * te
    pends = jnp.cumsum(padded)
    pstarts = pends - padded
    dst = pstarts[None, :] + jnp.cumsum(run, axis=0) - run
    tile_start = jnp.arange(n_rows // te, dtype=jnp.int32) * te
    tile_expert = jnp.sum((tile_start[:, None] >= pends[None, :]).astype(jnp.int32), axis=1)
    tile_expert = jnp.minimum(tile_expert, MOE_N_EXPERTS - 1)
    of_tile = tile_expert[:, None] == jnp.arange(MOE_N_EXPERTS, dtype=jnp.int32)[None, :]
    tile_end = jnp.sum(jnp.where(of_tile, (pstarts + total)[None, :], 0), axis=1)
    tile_valid = jnp.clip(tile_end - tile_start, 0, te)
    i32 = lambda v: v.astype(jnp.int32)

    n_big = run // RUN_PIECE
    big_first = jnp.cumsum(n_big, axis=1) - n_big
    k_big = jnp.arange(BIG_CAP, dtype=jnp.int32)[None, :, None]
    owns_big = (k_big >= big_first[:, None, :]) & (k_big < (big_first + n_big)[:, None, :])
    pick_big = lambda v: jnp.sum(jnp.where(owns_big, (v - RUN_PIECE * big_first)[:, None, :], 0), axis=-1)
    has_small = (run // RUN_ALIGN) % 2
    small_index = jnp.cumsum(has_small, axis=1) - has_small
    k_small = jnp.arange(SMALL_CAP, dtype=jnp.int32)[None, :, None]
    owns_small = (has_small[:, None, :] == 1) & (small_index[:, None, :] == k_small)
    pick_small = lambda v: jnp.sum(jnp.where(owns_small, (v + RUN_PIECE * n_big)[:, None, :], 0), axis=-1)
    return {
        "n_big": i32(jnp.sum(n_big, axis=1)),
        "big_off": i32((pick_big(off) + RUN_PIECE * k_big[..., 0]).reshape(-1)),
        "big_dst": i32((pick_big(dst) + RUN_PIECE * k_big[..., 0]).reshape(-1)),
        "n_small": i32(jnp.sum(has_small, axis=1)),
        "small_off": i32(pick_small(off).reshape(-1)),
        "small_dst": i32(pick_small(dst).reshape(-1)),
        "rows": i32(jnp.sum(run, axis=1)),
        "pad_start": i32(jnp.concatenate([pstarts + total, pends[-1:]])), "pad_len": i32(padded - total),
        "tile_expert": i32(tile_expert), "tile_valid": i32(tile_valid),
    }


def kernel(x, norm_mix_g, w_in, ret_gn_g, w_ret_o, q_norm_g, k_norm_g, sinks, w_swa_o, w_out, norm_ffn_g,
           w_router_group, b_router_group, w_router_expert, b_router_expert, w_gate, w_up, w_down):
    batch, seq, d = x.shape
    n = batch * seq
    assert d == D_MODEL and seq % SEQ_TILE == 0 and n % ROW_TILE == 0
    cos_tab, sin_tab = _rotary_tables(seq)
    n_rows = _expert_row_bound(n)
    for l in range(w_in.shape[0]):
        x2d = x.reshape(n, d)
        pad = LANES - MOE_N_EXPERTS - MOE_GROUPS
        w_router_t = jnp.concatenate(
            [w_router_expert[l].T, w_router_group[l].T, jnp.zeros((pad, d), _F32)], axis=0).astype(_BF16)
        b_router = jnp.concatenate([b_router_expert[l], b_router_group[l], jnp.zeros((pad,), _F32)])
        b_router_t = jnp.broadcast_to(b_router[:, None], (LANES, LANES))
        x1, h2, info, info_t, counts = _mixer(
            x2d, sinks[l], norm_mix_g[l][None, :], w_in[l].astype(_BF16), cos_tab, sin_tab,
            ret_gn_g[l].reshape(1, RET_V_W), jnp.tile(q_norm_g[l], 2)[None, :], jnp.tile(k_norm_g[l], 2)[None, :],
            w_ret_o[l].astype(_BF16), w_swa_o[l].astype(_BF16), w_out[l].astype(_BF16), norm_ffn_g[l][None, :],
            w_router_t, b_router_t, seq // ROW_TILE)
        plan = _routing_plan(counts, n_rows)
        xs = _dispatch(plan, h2, info_t, n_rows)
        ys = _experts(plan, xs, w_gate[l], w_up[l], w_down[l])
        x = _combine(plan, x1, info, ys).reshape(batch, seq, d)
    return x
```

```python
import math

import numpy as np
import jax
import jax.numpy as jnp
from jax import lax
from jax.experimental import pallas as pl
from jax.experimental.pallas import tpu as pltpu

D_MODEL = 1024
RET_HEADS = 4
RET_QK_DIM = 128
RET_V_DIM = 256
RET_CHUNK = 128
ROPE_BASE = 10000.0
SWA_Q_HEADS = 8
SWA_KV_HEADS = 2
SWA_HEAD_DIM = 64
SWA_WINDOW = 128
MOE_GROUPS = 4
MOE_EXPERTS_PER_GROUP = 8
MOE_D_FF = 512
NORM_EPS = 1e-6

RET_QK_W = RET_HEADS * RET_QK_DIM
RET_V_W = RET_HEADS * RET_V_DIM
SWA_Q_W = SWA_Q_HEADS * SWA_HEAD_DIM
SWA_KV_W = SWA_KV_HEADS * SWA_HEAD_DIM
SWA_GROUP = SWA_Q_HEADS // SWA_KV_HEADS
MOE_N_EXPERTS = MOE_GROUPS * MOE_EXPERTS_PER_GROUP

RET_COLS = 2 * RET_QK_W + 2 * RET_V_W
SWA_COLS = SWA_Q_W + 2 * SWA_KV_W
GATE_COLS = 2 * D_MODEL
IN_WIDTH = RET_COLS + SWA_COLS + GATE_COLS

LANES = 128
SUBLANES = 8
ROUTER_GROUP_LANE = 32
NEG_BIG = -1e30

ROW_TILE = 512
SEQ_TILE = 512
EXPERT_TILE = 512
OUTPROJ_SUB = 128
INPROJ_SUB = 256
EXPERT_SUB = 256
RUN_ALIGN = 16
SORT_ROWS = 2 * ROW_TILE + MOE_N_EXPERTS * RUN_ALIGN
SORT_ROWS_POW2 = 1 << SORT_ROWS.bit_length()
RUN_PIECE = 2 * RUN_ALIGN
VMEM_LIMIT = 56 * 1024 * 1024

RET_LOG_DECAY = tuple(float(np.log1p(-np.exp2(-5.0 - h))) for h in range(RET_HEADS))

_BF16 = jnp.bfloat16
_F32 = jnp.float32


def _dot(a, b):
    return jnp.dot(a, b, preferred_element_type=_F32)


def _dot_nt(a, b):
    return lax.dot_general(a, b, (((1,), (1,)), ((), ())), preferred_element_type=_F32)


def _sigmoid(v):
    return 1.0 / (1.0 + jnp.exp(-v))


def _split_bf16(v):
    hi = v.astype(_BF16)
    return hi, (v - hi.astype(_F32)).astype(_BF16)


def _iota(shape, dim):
    return lax.broadcasted_iota(jnp.int32, shape, dim)


def _inproj_kernel(x_ref, g_ref, w_ref, cos_ref, sin_ref, gn_ref, ret_ref, swa_ref, gate_ref):
    subs = [slice(s * INPROJ_SUB, (s + 1) * INPROJ_SUB) for s in range(ROW_TILE // INPROJ_SUB)]

    def normed(rows):
        x = x_ref[rows, :]
        r = lax.rsqrt(jnp.mean(x * x, axis=-1, keepdims=True) + NORM_EPS)
        return ((x * r) * g_ref[...]).astype(_BF16)
    chunk_pos = (_iota((INPROJ_SUB, RET_QK_DIM), 0) % RET_CHUNK + 1).astype(_F32)

    def rotary_decay(base, sign, scale):
        def epilogue(rows, raw):
            cos = cos_ref[rows, :]
            sin = sin_ref[rows, :]
            for hd in range(RET_HEADS):
                v = raw[:, hd * RET_QK_DIM:(hd + 1) * RET_QK_DIM]
                rot = v * cos + pltpu.roll(v, RET_QK_DIM // 2, 1) * sin
                decay = jnp.exp((sign * RET_LOG_DECAY[hd]) * chunk_pos) * scale
                ret_ref[rows, base + hd * RET_QK_DIM:base + (hd + 1) * RET_QK_DIM] = (rot * decay).astype(_BF16)
        return epilogue

    def store(ref, lo, fn):
        def epilogue(rows, raw):
            ref[rows, lo:lo + raw.shape[1]] = fn(raw).astype(_BF16)
        return epilogue

    v_base = 2 * RET_QK_W
    g_base = v_base + RET_V_W
    gate_base = RET_COLS + SWA_COLS
    jobs = [(0, RET_QK_W, rotary_decay(0, 1.0, 1.0)),
            (RET_QK_W, RET_QK_W, rotary_decay(RET_QK_W, -1.0, RET_QK_DIM ** -0.5))]
    jobs += [(v_base + c, 512, store(ret_ref, v_base + c, lambda raw: raw)) for c in range(0, RET_V_W, 512)]
    jobs += [(g_base + c, 512, store(ret_ref, g_base + c,
                                     lambda raw, c=c: (raw * _sigmoid(raw)) * gn_ref[:, c:c + 512]))
             for c in range(0, RET_V_W, 512)]
    jobs += [(RET_COLS + c, 256, store(swa_ref, c, lambda raw: raw)) for c in range(0, SWA_COLS, 256)]
    jobs += [(gate_base + c, 512, store(gate_ref, c, _sigmoid)) for c in range(0, GATE_COLS, 512)]
    hs = []
    for lo, width, epilogue in jobs:
        raws = []
        for s, rows in enumerate(subs):
            if len(hs) <= s:
                hs.append(normed(rows))
            raws.append(_dot(hs[s], w_ref[:, lo:lo + width]))
        for rows, raw in zip(subs, raws):
            epilogue(rows, raw)


def _inproj(x2d, g, w_in_bf16, cos_tab, sin_tab, gn_row):
    n = x2d.shape[0]
    grid = (n // ROW_TILE,)
    seq_tiles = cos_tab.shape[0] // ROW_TILE
    return pl.pallas_call(
        _inproj_kernel,
        grid=grid,
        in_specs=[
            pl.BlockSpec((ROW_TILE, D_MODEL), lambda i: (i, 0)),
            pl.BlockSpec((1, D_MODEL), lambda i: (0, 0)),
            pl.BlockSpec((D_MODEL, IN_WIDTH), lambda i: (0, 0), pipeline_mode=pl.Buffered(1)),
            pl.BlockSpec((ROW_TILE, RET_QK_DIM), lambda i: (i % seq_tiles, 0)),
            pl.BlockSpec((ROW_TILE, RET_QK_DIM), lambda i: (i % seq_tiles, 0)),
            pl.BlockSpec((1, RET_V_W), lambda i: (0, 0)),
        ],
        out_specs=[
            pl.BlockSpec((ROW_TILE, RET_COLS), lambda i: (i, 0)),
            pl.BlockSpec((ROW_TILE, SWA_COLS), lambda i: (i, 0)),
            pl.BlockSpec((ROW_TILE, GATE_COLS), lambda i: (i, 0)),
        ],
        out_shape=[
            jax.ShapeDtypeStruct((n, RET_COLS), _BF16),
            jax.ShapeDtypeStruct((n, SWA_COLS), _BF16),
            jax.ShapeDtypeStruct((n, GATE_COLS), _BF16),
        ],
        compiler_params=pltpu.CompilerParams(
            dimension_semantics=("arbitrary",), vmem_limit_bytes=VMEM_LIMIT),
        name="inproj",
    )(x2d, g, w_in_bf16, cos_tab, sin_tab, gn_row)


def _retention_stages(q_ref, k_ref, v_ref, g_ref, out_ref, state_ref):
    c = RET_CHUNK
    causal = _iota((c, c), 0) >= _iota((c, c), 1)

    for h in range(RET_HEADS):
        chunk_decay = math.exp(RET_LOG_DECAY[h] * c)
        qs = slice(h * RET_QK_DIM, (h + 1) * RET_QK_DIM)
        vs = slice(h * RET_V_DIM, (h + 1) * RET_V_DIM)
        chunks = [slice(ci * c, (ci + 1) * c) for ci in range(SEQ_TILE // c)]
        scores = [jnp.where(causal, _dot_nt(q_ref[rows, qs], k_ref[rows, qs]), 0.0).astype(_BF16) for rows in chunks]
        kvs = [_dot(k_ref[rows, qs].astype(_F32).T.astype(_BF16), v_ref[rows, vs]) for rows in chunks]
        yield
        state = state_ref[h]
        states = []
        for kv in kvs:
            states.append(state.astype(_BF16))
            state = chunk_decay * (state + kv)
        state_ref[h] = state
        rets = [_dot(jnp.concatenate([s, q_ref[rows, qs]], axis=1), jnp.concatenate([v_ref[rows, vs], st], axis=0))
                for rows, s, st in zip(chunks, scores, states)]
        yield
        for rows, ret in zip(chunks, rets):
            mu = jnp.mean(ret, axis=-1, keepdims=True)
            dev = ret - mu
            var = jnp.mean(dev * dev, axis=-1, keepdims=True)
            out_ref[rows, vs] = ((dev * lax.rsqrt(var + NORM_EPS)) * g_ref[rows, vs].astype(_F32)).astype(_BF16)
        yield


def _swa_stages(first, sinks_ref, q_ref, kc_ref, vc_ref, kp_ref, vp_ref, qg_ref, kg_ref, out_ref):
    w = SWA_WINDOW
    d = SWA_HEAD_DIM
    n_blk = SEQ_TILE // w
    pair = 2 * d

    k_both = jnp.concatenate([kp_ref[...], kc_ref[...]], axis=0).astype(_F32)
    same_half = (_iota((pair, pair), 0) // d == _iota((pair, pair), 1) // d).astype(_BF16)
    k_hi, k_lo = _split_bf16(k_both * k_both)
    k_ssq = _dot(k_hi, same_half) + _dot(k_lo, same_half)
    k_n = k_both * lax.rsqrt(k_ssq * (1.0 / d) + NORM_EPS) * (kg_ref[...] * d ** -0.5)
    v_both = jnp.concatenate([vp_ref[...], vc_ref[...]], axis=0).astype(_F32)
    v_t = [v_both[b * w:(b + 1) * w].T for b in range(n_blk + 1)]

    key = _iota((2 * w, SWA_GROUP * w), 0)
    qry = _iota((2 * w, SWA_GROUP * w), 1) % w
    band = (key > qry) & (key <= qry + w)
    band_first = band & jnp.logical_or(key >= w, jnp.logical_not(first))
    head_of_lane = _iota((1, SWA_GROUP * w), 1) // w
    half_of_lane = _iota((1, pair), 1) // d
    half_rows = (_iota((SUBLANES, pair), 0) == _iota((SUBLANES, pair), 1) // d).astype(_BF16)
    ones_rows = jnp.ones((2 * SUBLANES, 2 * w), _BF16)
    q_gain = qg_ref[...]
    yield

    for kk in range(SWA_KV_HEADS):
        k_native = jnp.where(half_of_lane == kk, k_n, 0.0)
        k_moved = pltpu.roll(k_native, d, 1)
        k_even, k_odd = (k_native, k_moved) if kk == 0 else (k_moved, k_native)
        k_even = k_even.astype(_BF16)
        k_odd = k_odd.astype(_BF16)
        sink_row = jnp.zeros((1, SWA_GROUP * w), _F32)
        for g in range(SWA_GROUP):
            sink_row = jnp.where(head_of_lane == g, sinks_ref[kk * SWA_GROUP + g], sink_row)
        blocks = range(n_blk)
        pairs = range(SWA_GROUP // 2)
        q_pairs = [[q_ref[j * w:(j + 1) * w, (kk * SWA_GROUP + 2 * p) * d:(kk * SWA_GROUP + 2 * p + 2) * d].astype(_F32)
                    for p in pairs] for j in blocks]
        q_splits = [[_split_bf16(q * q) for q in qs] for qs in q_pairs]
        q_ssq = [[_dot_nt(half_rows, hi) + _dot_nt(half_rows, lo) for hi, lo in sp] for sp in q_splits]
        q_rs = [[lax.rsqrt(s * (1.0 / d) + NORM_EPS) for s in ss] for ss in q_ssq]
        q_gs = [[(q * q_gain).astype(_BF16) for q in qs] for qs in q_pairs]
        yield
        raw_even = [[_dot_nt(k_even[j * w:(j + 2) * w], q_gs[j][p]) for p in pairs] for j in blocks]
        raw_odd = [[_dot_nt(k_odd[j * w:(j + 2) * w], q_gs[j][p]) for p in pairs] for j in blocks]
        yield
        score_blocks = []
        for j in blocks:
            raws = []
            for p in pairs:
                raws.append(raw_even[j][p] * q_rs[j][p][0:1, :])
                raws.append(raw_odd[j][p] * q_rs[j][p][1:2, :])
            s_t = jnp.concatenate(raws, axis=1)
            score_blocks.append(jnp.where(band_first if j == 0 else band, s_t, NEG_BIG))
        yield
        maxes = [jnp.maximum(jnp.max(s_t, axis=0, keepdims=True), sink_row) for s_t in score_blocks]
        probs = [jnp.exp(s_t - m).astype(_BF16) for s_t, m in zip(score_blocks, maxes)]
        yield
        outs = []
        for j in blocks:
            v_ext = jnp.concatenate([v_t[j][kk * d:(kk + 1) * d], v_t[j + 1][kk * d:(kk + 1) * d]], axis=1)
            outs.append(_dot(jnp.concatenate([v_ext.astype(_BF16), ones_rows], axis=0), probs[j]))
        yield
        for j in blocks:
            rows = slice(j * w, (j + 1) * w)
            denom = outs[j][d:d + 1, :] + jnp.exp(sink_row - maxes[j])
            o_t = outs[j][0:d, :] * (1.0 / denom)
            for p in range(SWA_GROUP // 2):
                c0 = (kk * SWA_GROUP + 2 * p) * d
                both = jnp.concatenate([o_t[:, 2 * p * w:(2 * p + 1) * w], o_t[:, (2 * p + 1) * w:(2 * p + 2) * w]], axis=0)
                out_ref[rows, c0:c0 + pair] = both.T.astype(_BF16)
        yield


def _mixers_kernel(sinks_ref, rq_ref, rk_ref, rv_ref, rg_ref, sq_ref, kc_ref, vc_ref, kp_ref, vp_ref, qg_ref, kg_ref,
                   retg_ref, attn_ref, state_ref):
    first = pl.program_id(1) == 0

    @pl.when(first)
    def _():
        state_ref[...] = jnp.zeros_like(state_ref)

    streams = [_retention_stages(rq_ref, rk_ref, rv_ref, rg_ref, retg_ref, state_ref),
               _swa_stages(first, sinks_ref, sq_ref, kc_ref, vc_ref, kp_ref, vp_ref, qg_ref, kg_ref, attn_ref)]
    while streams:
        streams = [s for s in streams if next(s, StopIteration) is not StopIteration]


def _mixers(ret_proj, swa_proj, sinks, q_gain2, k_gain2, batch, seq):
    n = ret_proj.shape[0]
    steps = seq // SEQ_TILE
    blocks_per_step = SEQ_TILE // SWA_WINDOW
    rowmap = lambda b, s: b * steps + s
    prevmap = lambda b, s: jnp.maximum((b * steps + s) * blocks_per_step - 1, 0)
    k_col = SWA_Q_W // SWA_KV_W
    return pl.pallas_call(
        _mixers_kernel,
        grid=(batch, steps),
        in_specs=[
            pl.BlockSpec(memory_space=pltpu.SMEM),
            pl.BlockSpec((SEQ_TILE, RET_QK_W), lambda b, s: (rowmap(b, s), 0)),
            pl.BlockSpec((SEQ_TILE, RET_QK_W), lambda b, s: (rowmap(b, s), 1)),
            pl.BlockSpec((SEQ_TILE, RET_V_W), lambda b, s: (rowmap(b, s), 1)),
            pl.BlockSpec((SEQ_TILE, RET_V_W), lambda b, s: (rowmap(b, s), 2)),
            pl.BlockSpec((SEQ_TILE, SWA_Q_W), lambda b, s: (rowmap(b, s), 0)),
            pl.BlockSpec((SEQ_TILE, SWA_KV_W), lambda b, s: (rowmap(b, s), k_col)),
            pl.BlockSpec((SEQ_TILE, SWA_KV_W), lambda b, s: (rowmap(b, s), k_col + 1)),
            pl.BlockSpec((SWA_WINDOW, SWA_KV_W), lambda b, s: (prevmap(b, s), k_col)),
            pl.BlockSpec((SWA_WINDOW, SWA_KV_W), lambda b, s: (prevmap(b, s), k_col + 1)),
            pl.BlockSpec((1, 2 * SWA_HEAD_DIM), lambda b, s: (0, 0)),
            pl.BlockSpec((1, 2 * SWA_HEAD_DIM), lambda b, s: (0, 0)),
        ],
        out_specs=[
            pl.BlockSpec((SEQ_TILE, RET_V_W), lambda b, s: (rowmap(b, s), 0)),
            pl.BlockSpec((SEQ_TILE, SWA_Q_W), lambda b, s: (rowmap(b, s), 0)),
        ],
        out_shape=[
            jax.ShapeDtypeStruct((n, RET_V_W), _BF16),
            jax.ShapeDtypeStruct((n, SWA_Q_W), _BF16),
        ],
        scratch_shapes=[pltpu.VMEM((RET_HEADS, RET_QK_DIM, RET_V_DIM), _F32)],
        compiler_params=pltpu.CompilerParams(
            dimension_semantics=("arbitrary", "arbitrary"), vmem_limit_bytes=VMEM_LIMIT),
        name="mixers",
    )(sinks, ret_proj, ret_proj, ret_proj, ret_proj, swa_proj, swa_proj, swa_proj, swa_proj, swa_proj,
      q_gain2, k_gain2)


def _outproj_kernel(x_ref, retg_ref, attn_ref, ga_ref, gb_ref, wro_ref, wso_ref, wout_ref, g2_ref,
                    wrt_ref, brt_ref, x1_ref, h2_ref, info_ref, info_t_ref, counts_ref, logits_ref):
    tm = ROW_TILE
    step = pl.program_id(0)
    cur = step % 2

    @pl.when(step == 0)
    def _():
        logits_ref[...] = jnp.zeros_like(logits_ref)

    logits = logits_ref[1 - cur]
    row = _iota((LANES, tm), 0)
    row_f = row.astype(_F32)
    is_group = (row >= ROUTER_GROUP_LANE) & (row < ROUTER_GROUP_LANE + MOE_GROUPS)

    subs = [slice(s * OUTPROJ_SUB, (s + 1) * OUTPROJ_SUB) for s in range(tm // OUTPROJ_SUB)]
    y_a = [_dot(retg_ref[rows, :], wro_ref[...]) for rows in subs]

    gl = jnp.where(is_group, logits, NEG_BIG)
    g_max = jnp.max(gl, axis=0, keepdims=True)
    g_prob = 1.0 / jnp.sum(jnp.exp(gl - g_max), axis=0, keepdims=True)
    g_idx = jnp.min(jnp.where(gl == g_max, row_f - ROUTER_GROUP_LANE, float(LANES)), axis=0, keepdims=True)

    y_b = [_dot(attn_ref[rows, :], wso_ref[...]) for rows in subs]

    e_lo = g_idx * MOE_EXPERTS_PER_GROUP
    in_group = (row_f >= e_lo) & (row_f < e_lo + MOE_EXPERTS_PER_GROUP)
    el = jnp.where(in_group, logits, NEG_BIG)
    t1 = jnp.max(el, axis=0, keepdims=True)
    i1 = jnp.min(jnp.where(el == t1, row_f, float(LANES)), axis=0, keepdims=True)

    merged = [(ga_ref[rows, :].astype(_F32) * a + gb_ref[rows, :].astype(_F32) * b).astype(_BF16)
              for rows, a, b in zip(subs, y_a, y_b)]
    mixed = [_dot(m, wout_ref[...]) for m in merged]
    x1 = [x_ref[rows, :] + m for rows, m in zip(subs, mixed)]
    for rows, v in zip(subs, x1):
        x1_ref[rows, :] = v

    el2 = jnp.where(row_f == i1, NEG_BIG, el)
    t2 = jnp.max(el2, axis=0, keepdims=True)
    i2 = jnp.min(jnp.where(el2 == t2, row_f, float(LANES)), axis=0, keepdims=True)
    e21 = jnp.exp(t2 - t1)
    w1 = g_prob / (1.0 + e21)
    w2 = g_prob * e21 / (1.0 + e21)

    h2 = [((v * lax.rsqrt(jnp.mean(v * v, axis=-1, keepdims=True) + NORM_EPS)) * g2_ref[...]).astype(_BF16)
          for v in x1]
    for rows, v in zip(subs, h2):
        h2_ref[rows, :] = v

    sel1 = row_f == i1
    sel2 = row_f == i2
    onehot = jnp.where(sel1 | sel2, 1.0, 0.0).astype(_BF16)
    earlier = (_iota((tm, tm), 0) < _iota((tm, tm), 1)).astype(_BF16)
    before = _dot(onehot, earlier)
    counts = _dot(onehot, jnp.ones((tm, LANES), _BF16))

    for rows, v in zip(subs, h2):
        logits_ref[cur, :, rows] = _dot_nt(wrt_ref[...], v) + brt_ref[:, 0:1]

    run = jnp.floor((counts + (RUN_ALIGN - 1.0)) * (1.0 / RUN_ALIGN)) * RUN_ALIGN
    lower_experts = (_iota((LANES, LANES), 0) > _iota((LANES, LANES), 1)).astype(_BF16)
    run_off = _dot(lower_experts, run.astype(_BF16))
    place = before + jnp.concatenate([run_off] * (tm // LANES), axis=1)
    pos1 = jnp.sum(jnp.where(sel1, place, 0.0), axis=0, keepdims=True)
    pos2 = jnp.sum(jnp.where(sel2, place, 0.0), axis=0, keepdims=True)
    counts_ref[...] = counts.T[0:SUBLANES, :]

    info_t = jnp.concatenate([i1, i2, w1, w2, pos1, pos2, jnp.zeros((2, tm), _F32)], axis=0)
    info_t_ref[...] = info_t
    info_ref[...] = jnp.concatenate([info_t, jnp.zeros((LANES - SUBLANES, tm), _F32)], axis=0).T


def _outproj(x2d, retg, attn, gates, w_ret_o, w_swa_o, w_out, g2, w_router_t, b_router_t):
    n = x2d.shape[0]
    tm = ROW_TILE
    n_tiles = n // tm
    const = lambda s: (0, 0)
    proj = lambda s: jnp.minimum(s, n_tiles - 1)
    routed = lambda s: jnp.maximum(s - 1, 0)
    return pl.pallas_call(
        _outproj_kernel,
        grid=(n_tiles + 1,),
        in_specs=[
            pl.BlockSpec((tm, D_MODEL), lambda s: (proj(s), 0)),
            pl.BlockSpec((tm, RET_V_W), lambda s: (proj(s), 0)),
            pl.BlockSpec((tm, SWA_Q_W), lambda s: (proj(s), 0)),
            pl.BlockSpec((tm, D_MODEL), lambda s: (proj(s), 0)),
            pl.BlockSpec((tm, D_MODEL), lambda s: (proj(s), 1)),
            pl.BlockSpec((RET_V_W, D_MODEL), const),
            pl.BlockSpec((SWA_Q_W, D_MODEL), const),
            pl.BlockSpec((D_MODEL, D_MODEL), const),
            pl.BlockSpec((1, D_MODEL), const),
            pl.BlockSpec((LANES, D_MODEL), const),
            pl.BlockSpec((LANES, LANES), const),
        ],
        out_specs=[
            pl.BlockSpec((tm, D_MODEL), lambda s: (proj(s), 0)),
            pl.BlockSpec((tm, D_MODEL), lambda s: (proj(s), 0)),
            pl.BlockSpec((tm, LANES), lambda s: (routed(s), 0)),
            pl.BlockSpec((SUBLANES, tm), lambda s: (0, routed(s))),
            pl.BlockSpec((SUBLANES, LANES), lambda s: (routed(s), 0)),
        ],
        scratch_shapes=[pltpu.VMEM((2, LANES, tm), _F32)],
        out_shape=[
            jax.ShapeDtypeStruct((n, D_MODEL), _F32),
            jax.ShapeDtypeStruct((n, D_MODEL), _BF16),
            jax.ShapeDtypeStruct((n, LANES), _F32),
            jax.ShapeDtypeStruct((SUBLANES, n), _F32),
            jax.ShapeDtypeStruct((n // tm * SUBLANES, LANES), _F32),
        ],
        compiler_params=pltpu.CompilerParams(
            dimension_semantics=("arbitrary",), vmem_limit_bytes=VMEM_LIMIT),
        name="outproj",
    )(x2d, retg, attn, gates, gates, w_ret_o, w_swa_o, w_out, g2, w_router_t, b_router_t)


def _piece_sizes(largest):
    piece = largest
    while piece >= RUN_ALIGN:
        yield piece
        piece //= 2


BIG_CAP = SORT_ROWS // RUN_PIECE
SMALL_CAP = MOE_N_EXPERTS
PIECE_LIST_KEYS = ("n_big", "big_off", "big_dst", "n_small", "small_off", "small_dst")


def _start_run_copies(pieces, tile, make_copy):
    n_big_ref, big_off_ref, big_dst_ref, n_small_ref, small_off_ref, small_dst_ref = pieces

    def big(k, carry):
        i = tile * BIG_CAP + k
        make_copy(pl.multiple_of(big_off_ref[i], RUN_ALIGN), pl.multiple_of(big_dst_ref[i], RUN_ALIGN),
                  RUN_PIECE).start()
        return carry

    lax.fori_loop(0, n_big_ref[tile], big, 0)

    def small(k, carry):
        i = tile * SMALL_CAP + k
        make_copy(pl.multiple_of(small_off_ref[i], RUN_ALIGN), pl.multiple_of(small_dst_ref[i], RUN_ALIGN),
                  RUN_ALIGN).start()
        return carry

    lax.fori_loop(0, n_small_ref[tile], small, 0)


def _wait_rows(rows, make_copy):
    for piece in _piece_sizes(SORT_ROWS_POW2 // 2):
        pl.when((rows & piece) != 0)(make_copy(0, 0, piece).wait)


def _zero_fill_padding(pad_start_ref, pad_len_ref, zeros_ref, xs_hbm, sem, wait):
    def per_expert(e, carry):
        pos = pad_start_ref[e]
        length = pad_len_ref[e]
        for piece in _piece_sizes(EXPERT_TILE // 2):
            copy = pltpu.make_async_copy(
                zeros_ref.at[pl.ds(0, piece)], xs_hbm.at[pl.ds(pl.multiple_of(pos, RUN_ALIGN), piece)], sem)
            pl.when((length & piece) != 0)(copy.wait if wait else copy.start)
            pos = pos + (length & piece)
        return carry

    lax.fori_loop(0, MOE_N_EXPERTS, per_expert, 0)

    tail_start = pad_start_ref[MOE_N_EXPERTS]
    n_rows = xs_hbm.shape[0]
    half = EXPERT_TILE // 2

    def per_half_tile(j, carry):
        pos = tail_start + j * half
        copy = pltpu.make_async_copy(
            zeros_ref, xs_hbm.at[pl.ds(pl.multiple_of(jnp.minimum(pos, n_rows - half), RUN_ALIGN), half)], sem)
        pl.when(pos < n_rows)(copy.wait if wait else copy.start)
        return carry

    lax.fori_loop(0, n_rows // half, per_half_tile, 0)


def _dispatch_kernel(*refs):
    pieces = refs[:len(PIECE_LIST_KEYS)]
    rows_ref, pad_start_ref, pad_len_ref, h2_ref, info_t_ref, xs_hbm, sorted_ref, zeros_ref, sems, zero_sem = (
        refs[len(PIECE_LIST_KEYS):])
    tile = pl.program_id(0)
    last = pl.num_programs(0) - 1
    slot = tile % 2
    tm = ROW_TILE

    @pl.when(tile == 0)
    def _():
        zeros_ref[...] = jnp.zeros_like(zeros_ref)
        _zero_fill_padding(pad_start_ref, pad_len_ref, zeros_ref, xs_hbm, zero_sem, wait=False)
        _zero_fill_padding(pad_start_ref, pad_len_ref, zeros_ref, xs_hbm, zero_sem, wait=True)

    pos = info_t_ref[4:6, :].astype(jnp.int32)
    place = _iota((SORT_ROWS, tm), 0)
    onehot = ((place == pos[0:1, :]) | (place == pos[1:2, :])).astype(_BF16)
    for c in range(0, D_MODEL, 256):
        sorted_ref[slot, :, c:c + 256] = _dot(onehot, h2_ref[:, c:c + 256]).astype(_BF16)

    def copy_maker(t):
        s = t % 2

        def make_copy(off, dst, piece):
            return pltpu.make_async_copy(
                sorted_ref.at[s, pl.ds(off, piece)], xs_hbm.at[pl.ds(dst, piece)], sems.at[s])

        return make_copy

    _start_run_copies(pieces, tile, copy_maker(tile))

    @pl.when(tile > 0)
    def _():
        _wait_rows(rows_ref[tile - 1], copy_maker(tile - 1))

    @pl.when(tile == last)
    def _():
        _wait_rows(rows_ref[tile], copy_maker(tile))


def _dispatch(plan, h2, info_t, n_rows):
    n = h2.shape[0]
    grid_spec = pltpu.PrefetchScalarGridSpec(
        num_scalar_prefetch=len(PIECE_LIST_KEYS) + 3,
        grid=(n // ROW_TILE,),
        in_specs=[
            pl.BlockSpec((ROW_TILE, D_MODEL), lambda i, *_: (i, 0)),
            pl.BlockSpec((SUBLANES, ROW_TILE), lambda i, *_: (0, i)),
        ],
        out_specs=pl.BlockSpec(memory_space=pl.ANY),
        scratch_shapes=[
            pltpu.VMEM((2, SORT_ROWS, D_MODEL), _BF16),
            pltpu.VMEM((EXPERT_TILE // 2, D_MODEL), _BF16),
            pltpu.SemaphoreType.DMA((2,)),
            pltpu.SemaphoreType.DMA(()),
        ],
    )
    return pl.pallas_call(
        _dispatch_kernel,
        grid_spec=grid_spec,
        out_shape=jax.ShapeDtypeStruct((n_rows, D_MODEL), _BF16),
        compiler_params=pltpu.CompilerParams(
            dimension_semantics=("arbitrary",), vmem_limit_bytes=VMEM_LIMIT),
        name="dispatch",
    )(*[plan[k] for k in PIECE_LIST_KEYS], plan["rows"], plan["pad_start"], plan["pad_len"], h2, info_t)


def _experts_kernel(te_ref, tv_ref, xs_ref, wg_ref, wu_ref, wd_ref, ys_ref, wg_s, wu_s, wd_s):
    i = pl.program_id(0)
    valid = tv_ref[i]
    prev = te_ref[jnp.maximum(i - 1, 0)]
    changed = jnp.logical_or(i == 0, te_ref[i] != prev)

    @pl.when(jnp.logical_and(valid > 0, changed))
    def _():
        wg_s[...] = wg_ref[...].astype(_BF16)
        wu_s[...] = wu_ref[...].astype(_BF16)
        wd_s[...] = wd_ref[...].astype(_BF16)

    @pl.when(valid > 0)
    def _():
        subs = [slice(s * EXPERT_SUB, (s + 1) * EXPERT_SUB) for s in range(EXPERT_TILE // EXPERT_SUB)]
        gate = [_dot(xs_ref[rows, :], wg_s[...]) for rows in subs]
        up = [_dot(xs_ref[rows, :], wu_s[...]) for rows in subs]
        act = [((g * _sigmoid(g)) * u).astype(_BF16) for g, u in zip(gate, up)]
        for rows, a in zip(subs, act):
            ys_ref[rows, :] = _dot(a, wd_s[...]).astype(_BF16)

    @pl.when(valid <= 0)
    def _():
        ys_ref[...] = jnp.zeros_like(ys_ref)


def _experts(plan, xs, w_gate, w_up, w_down):
    n_rows = xs.shape[0]
    n_tiles = n_rows // EXPERT_TILE
    grid_spec = pltpu.PrefetchScalarGridSpec(
        num_scalar_prefetch=2,
        grid=(n_tiles,),
        in_specs=[
            pl.BlockSpec((EXPERT_TILE, D_MODEL), lambda i, te, tv: (i, 0)),
            pl.BlockSpec((None, D_MODEL, MOE_D_FF), lambda i, te, tv: (te[i], 0, 0)),
            pl.BlockSpec((None, D_MODEL, MOE_D_FF), lambda i, te, tv: (te[i], 0, 0)),
            pl.BlockSpec((None, MOE_D_FF, D_MODEL), lambda i, te, tv: (te[i], 0, 0)),
        ],
        out_specs=pl.BlockSpec((EXPERT_TILE, D_MODEL), lambda i, te, tv: (i, 0)),
        scratch_shapes=[
            pltpu.VMEM((D_MODEL, MOE_D_FF), _BF16),
            pltpu.VMEM((D_MODEL, MOE_D_FF), _BF16),
            pltpu.VMEM((MOE_D_FF, D_MODEL), _BF16),
        ],
    )
    return pl.pallas_call(
        _experts_kernel,
        grid_spec=grid_spec,
        out_shape=jax.ShapeDtypeStruct((n_rows, D_MODEL), _BF16),
        compiler_params=pltpu.CompilerParams(
            dimension_semantics=("arbitrary",), vmem_limit_bytes=VMEM_LIMIT),
        name="experts",
    )(plan["tile_expert"], plan["tile_valid"], xs, w_gate, w_up, w_down)


def _combine_kernel(*refs):
    pieces = refs[:len(PIECE_LIST_KEYS)]
    rows_ref, x1_ref, info_ref, ys_hbm, out_ref, block_ref, sems = refs[len(PIECE_LIST_KEYS):]
    tile = pl.program_id(0)
    last = pl.num_programs(0) - 1
    slot = tile % 2
    tm = ROW_TILE

    def copy_maker(t):
        s = t % 2

        def make_copy(off, dst, piece):
            return pltpu.make_async_copy(
                ys_hbm.at[pl.ds(dst, piece)], block_ref.at[s, pl.ds(off, piece)], sems.at[s])

        return make_copy

    @pl.when(tile == 0)
    def _():
        block_ref[...] = jnp.zeros_like(block_ref)
        _start_run_copies(pieces, tile, copy_maker(tile))

    @pl.when(tile < last)
    def _():
        _start_run_copies(pieces, tile + 1, copy_maker(tile + 1))

    _wait_rows(rows_ref[tile], copy_maker(tile))

    info = info_ref[...]
    place = _iota((tm, SORT_ROWS), 1).astype(_F32)
    weights = (jnp.where(place == info[:, 4:5], info[:, 2:3], 0.0)
               + jnp.where(place == info[:, 5:6], info[:, 3:4], 0.0)).astype(_BF16)
    for c in range(0, D_MODEL, 256):
        out_ref[:, c:c + 256] = x1_ref[:, c:c + 256] + _dot(weights, block_ref[slot, :, c:c + 256])


def _combine(plan, x1, info, ys):
    n = x1.shape[0]
    grid_spec = pltpu.PrefetchScalarGridSpec(
        num_scalar_prefetch=len(PIECE_LIST_KEYS) + 1,
        grid=(n // ROW_TILE,),
        in_specs=[
            pl.BlockSpec((ROW_TILE, D_MODEL), lambda i, *_: (i, 0)),
            pl.BlockSpec((ROW_TILE, LANES), lambda i, *_: (i, 0)),
            pl.BlockSpec(memory_space=pl.ANY),
        ],
        out_specs=pl.BlockSpec((ROW_TILE, D_MODEL), lambda i, *_: (i, 0)),
        scratch_shapes=[pltpu.VMEM((2, SORT_ROWS, D_MODEL), _BF16), pltpu.SemaphoreType.DMA((2,))],
    )
    return pl.pallas_call(
        _combine_kernel,
        grid_spec=grid_spec,
        out_shape=jax.ShapeDtypeStruct((n, D_MODEL), _F32),
        compiler_params=pltpu.CompilerParams(
            dimension_semantics=("arbitrary",), vmem_limit_bytes=VMEM_LIMIT),
        name="combine",
    )(*[plan[k] for k in PIECE_LIST_KEYS], plan["rows"], x1, info, ys)


def _rotary_tables(seq):
    half = RET_QK_DIM // 2
    inv = ROPE_BASE ** (-jnp.arange(half, dtype=_F32) / half)
    hi = (jnp.arange(seq // RET_CHUNK, dtype=jnp.int32) * RET_CHUNK).astype(_F32)[:, None] * inv[None, :]
    lo = jnp.arange(RET_CHUNK, dtype=jnp.int32).astype(_F32)[:, None] * inv[None, :]
    cos_hi, sin_hi = jnp.cos(hi)[:, None, :], jnp.sin(hi)[:, None, :]
    cos_lo, sin_lo = jnp.cos(lo)[None, :, :], jnp.sin(lo)[None, :, :]
    cos = (cos_hi * cos_lo - sin_hi * sin_lo).reshape(seq, half)
    sin = (sin_hi * cos_lo + cos_hi * sin_lo).reshape(seq, half)
    return jnp.concatenate([cos, cos], axis=1), jnp.concatenate([-sin, sin], axis=1)


def _expert_row_bound(n_tokens):
    rows = (2 * n_tokens + (n_tokens // ROW_TILE) * MOE_N_EXPERTS * (RUN_ALIGN - 1)
            + MOE_N_EXPERTS * (EXPERT_TILE - RUN_ALIGN))
    return -(-rows // EXPERT_TILE) * EXPERT_TILE


def _routing_plan(counts_out, n_rows):
    te = EXPERT_TILE
    n_tok_tiles = counts_out.shape[0] // SUBLANES
    counts = counts_out.reshape(n_tok_tiles, SUBLANES, LANES)[:, 0, :MOE_N_EXPERTS].astype(jnp.int32)
    run = (counts + RUN_ALIGN - 1) // RUN_ALIGN * RUN_ALIGN
    off = jnp.cumsum(run, axis=1) - run
    total = jnp.sum(run, axis=0)
    padded = (total + te - 1) // te * te
    pends = jnp.cumsum(padded)
    pstarts = pends - padded
    dst = pstarts[None, :] + jnp.cumsum(run, axis=0) - run
    tile_start = jnp.arange(n_rows // te, dtype=jnp.int32) * te
    tile_expert = jnp.sum((tile_start[:, None] >= pends[None, :]).astype(jnp.int32), axis=1)
    tile_expert = jnp.minimum(tile_expert, MOE_N_EXPERTS - 1)
    of_tile = tile_expert[:, None] == jnp.arange(MOE_N_EXPERTS, dtype=jnp.int32)[None, :]
    tile_end = jnp.sum(jnp.where(of_tile, (pstarts + total)[None, :], 0), axis=1)
    tile_valid = jnp.clip(tile_end - tile_start, 0, te)
    i32 = lambda v: v.astype(jnp.int32)

    n_big = run // RUN_PIECE
    big_first = jnp.cumsum(n_big, axis=1) - n_big
    k_big = jnp.arange(BIG_CAP, dtype=jnp.int32)[None, :, None]
    owns_big = (k_big >= big_first[:, None, :]) & (k_big < (big_first + n_big)[:, None, :])
    pick_big = lambda v: jnp.sum(jnp.where(owns_big, (v - RUN_PIECE * big_first)[:, None, :], 0), axis=-1)
    has_small = (run // RUN_ALIGN) % 2
    small_index = jnp.cumsum(has_small, axis=1) - has_small
    k_small = jnp.arange(SMALL_CAP, dtype=jnp.int32)[None, :, None]
    owns_small = (has_small[:, None, :] == 1) & (small_index[:, None, :] == k_small)
    pick_small = lambda v: jnp.sum(jnp.where(owns_small, (v + RUN_PIECE * n_big)[:, None, :], 0), axis=-1)
    return {
        "n_big": i32(jnp.sum(n_big, axis=1)),
        "big_off": i32((pick_big(off) + RUN_PIECE * k_big[..., 0]).reshape(-1)),
        "big_dst": i32((pick_big(dst) + RUN_PIECE * k_big[..., 0]).reshape(-1)),
        "n_small": i32(jnp.sum(has_small, axis=1)),
        "small_off": i32(pick_small(off).reshape(-1)),
        "small_dst": i32(pick_small(dst).reshape(-1)),
        "rows": i32(jnp.sum(run, axis=1)),
        "pad_start": i32(jnp.concatenate([pstarts + total, pends[-1:]])), "pad_len": i32(padded - total),
        "tile_expert": i32(tile_expert), "tile_valid": i32(tile_valid),
    }


def kernel(x, norm_mix_g, w_in, ret_gn_g, w_ret_o, q_norm_g, k_norm_g, sinks, w_swa_o, w_out, norm_ffn_g,
           w_router_group, b_router_group, w_router_expert, b_router_expert, w_gate, w_up, w_down):
    batch, seq, d = x.shape
    n = batch * seq
    assert d == D_MODEL and seq % SEQ_TILE == 0 and n % ROW_TILE == 0
    cos_tab, sin_tab = _rotary_tables(seq)
    n_rows = _expert_row_bound(n)
    for l in range(w_in.shape[0]):
        x2d = x.reshape(n, d)
        ret_proj, swa_proj, gates = _inproj(x2d, norm_mix_g[l][None, :], w_in[l].astype(_BF16), cos_tab, sin_tab,
                                            ret_gn_g[l].reshape(1, RET_V_W))
        retg, attn = _mixers(ret_proj, swa_proj, sinks[l], jnp.tile(q_norm_g[l], 2)[None, :],
                             jnp.tile(k_norm_g[l], 2)[None, :], batch, seq)
        pad = LANES - MOE_N_EXPERTS - MOE_GROUPS
        w_router_t = jnp.concatenate(
            [w_router_expert[l].T, w_router_group[l].T, jnp.zeros((pad, d), _F32)], axis=0).astype(_BF16)
        b_router = jnp.concatenate([b_router_expert[l], b_router_group[l], jnp.zeros((pad,), _F32)])
        b_router_t = jnp.broadcast_to(b_router[:, None], (LANES, LANES))
        x1, h2, info, info_t, counts = _outproj(
            x2d, retg, attn, gates, w_ret_o[l].astype(_BF16), w_swa_o[l].astype(_BF16), w_out[l].astype(_BF16),
            norm_ffn_g[l][None, :], w_router_t, b_router_t)
        plan = _routing_plan(counts, n_rows)
        xs = _dispatch(plan, h2, info_t, n_rows)
        ys = _experts(plan, xs, w_gate[l], w_up[l], w_down[l])
        x = _combine(plan, x1, info, ys).reshape(batch, seq, d)
    return x
```

```python
import math

import numpy as np
import jax
import jax.numpy as jnp
from jax import lax
from jax.experimental import pallas as pl
from jax.experimental.pallas import tpu as pltpu

D_MODEL = 1024
RET_HEADS = 4
RET_QK_DIM = 128
RET_V_DIM = 256
RET_CHUNK = 128
ROPE_BASE = 10000.0
SWA_Q_HEADS = 8
SWA_KV_HEADS = 2
SWA_HEAD_DIM = 64
SWA_WINDOW = 128
MOE_GROUPS = 4
MOE_EXPERTS_PER_GROUP = 8
MOE_D_FF = 512
NORM_EPS = 1e-6

RET_QK_W = RET_HEADS * RET_QK_DIM
RET_V_W = RET_HEADS * RET_V_DIM
SWA_Q_W = SWA_Q_HEADS * SWA_HEAD_DIM
SWA_KV_W = SWA_KV_HEADS * SWA_HEAD_DIM
SWA_GROUP = SWA_Q_HEADS // SWA_KV_HEADS
MOE_N_EXPERTS = MOE_GROUPS * MOE_EXPERTS_PER_GROUP

RET_COLS = 2 * RET_QK_W + 2 * RET_V_W
SWA_COLS = SWA_Q_W + 2 * SWA_KV_W
GATE_COLS = 2 * D_MODEL
IN_WIDTH = RET_COLS + SWA_COLS + GATE_COLS

LANES = 128
SUBLANES = 8
ROUTER_GROUP_LANE = 32
NEG_BIG = -1e30

ROW_TILE = 512
SEQ_TILE = 512
EXPERT_TILE = 512
OUTPROJ_SUB = 128
INPROJ_SUB = 256
EXPERT_SUB = 256
RUN_ALIGN = 16
SORT_ROWS = 2 * ROW_TILE + MOE_N_EXPERTS * RUN_ALIGN
SORT_ROWS_POW2 = 1 << SORT_ROWS.bit_length()
SORT_COMMON = 2 * ROW_TILE + MOE_N_EXPERTS * RUN_ALIGN * 9 // 16
RUN_PIECE = 2 * RUN_ALIGN
VMEM_LIMIT = 56 * 1024 * 1024

RET_LOG_DECAY = tuple(float(np.log1p(-np.exp2(-5.0 - h))) for h in range(RET_HEADS))

_BF16 = jnp.bfloat16
_F32 = jnp.float32


def _dot(a, b):
    return jnp.dot(a, b, preferred_element_type=_F32)


def _dot_nt(a, b):
    return lax.dot_general(a, b, (((1,), (1,)), ((), ())), preferred_element_type=_F32)


def _sigmoid(v):
    return 1.0 / (1.0 + jnp.exp(-v))


def _split_bf16(v):
    hi = v.astype(_BF16)
    return hi, (v - hi.astype(_F32)).astype(_BF16)


def _iota(shape, dim):
    return lax.broadcasted_iota(jnp.int32, shape, dim)


def _inproj_kernel(x_ref, g_ref, w_ref, cos_ref, sin_ref, gn_ref, ret_ref, swa_ref, gate_ref):
    subs = [slice(s * INPROJ_SUB, (s + 1) * INPROJ_SUB) for s in range(ROW_TILE // INPROJ_SUB)]

    def normed(rows):
        x = x_ref[rows, :]
        r = lax.rsqrt(jnp.mean(x * x, axis=-1, keepdims=True) + NORM_EPS)
        return ((x * r) * g_ref[...]).astype(_BF16)
    chunk_pos = (_iota((INPROJ_SUB, RET_QK_DIM), 0) % RET_CHUNK + 1).astype(_F32)

    def rotary_decay(base, sign, scale):
        def epilogue(rows, raw):
            cos = cos_ref[rows, :]
            sin = sin_ref[rows, :]
            for hd in range(RET_HEADS):
                v = raw[:, hd * RET_QK_DIM:(hd + 1) * RET_QK_DIM]
                rot = v * cos + pltpu.roll(v, RET_QK_DIM // 2, 1) * sin
                decay = jnp.exp((sign * RET_LOG_DECAY[hd]) * chunk_pos) * scale
                ret_ref[rows, base + hd * RET_QK_DIM:base + (hd + 1) * RET_QK_DIM] = (rot * decay).astype(_BF16)
        return epilogue

    def store(ref, lo, fn):
        def epilogue(rows, raw):
            ref[rows, lo:lo + raw.shape[1]] = fn(raw).astype(_BF16)
        return epilogue

    v_base = 2 * RET_QK_W
    g_base = v_base + RET_V_W
    gate_base = RET_COLS + SWA_COLS
    jobs = [(0, RET_QK_W, rotary_decay(0, 1.0, 1.0)),
            (RET_QK_W, RET_QK_W, rotary_decay(RET_QK_W, -1.0, RET_QK_DIM ** -0.5))]
    jobs += [(v_base + c, 512, store(ret_ref, v_base + c, lambda raw: raw)) for c in range(0, RET_V_W, 512)]
    jobs += [(g_base + c, 512, store(ret_ref, g_base + c,
                                     lambda raw, c=c: (raw * _sigmoid(raw)) * gn_ref[:, c:c + 512]))
             for c in range(0, RET_V_W, 512)]
    jobs += [(RET_COLS + c, 256, store(swa_ref, c, lambda raw: raw)) for c in range(0, SWA_COLS, 256)]
    jobs += [(gate_base + c, 512, store(gate_ref, c, _sigmoid)) for c in range(0, GATE_COLS, 512)]
    hs = []
    for lo, width, epilogue in jobs:
        raws = []
        for s, rows in enumerate(subs):
            if len(hs) <= s:
                hs.append(normed(rows))
            raws.append(_dot(hs[s], w_ref[:, lo:lo + width]))
        for rows, raw in zip(subs, raws):
            epilogue(rows, raw)


def _inproj(x2d, g, w_in_bf16, cos_tab, sin_tab, gn_row):
    n = x2d.shape[0]
    grid = (n // ROW_TILE,)
    seq_tiles = cos_tab.shape[0] // ROW_TILE
    return pl.pallas_call(
        _inproj_kernel,
        grid=grid,
        in_specs=[
            pl.BlockSpec((ROW_TILE, D_MODEL), lambda i: (i, 0)),
            pl.BlockSpec((1, D_MODEL), lambda i: (0, 0)),
            pl.BlockSpec((D_MODEL, IN_WIDTH), lambda i: (0, 0), pipeline_mode=pl.Buffered(1)),
            pl.BlockSpec((ROW_TILE, RET_QK_DIM), lambda i: (i % seq_tiles, 0)),
            pl.BlockSpec((ROW_TILE, RET_QK_DIM), lambda i: (i % seq_tiles, 0)),
            pl.BlockSpec((1, RET_V_W), lambda i: (0, 0)),
        ],
        out_specs=[
            pl.BlockSpec((ROW_TILE, RET_COLS), lambda i: (i, 0)),
            pl.BlockSpec((ROW_TILE, SWA_COLS), lambda i: (i, 0)),
            pl.BlockSpec((ROW_TILE, GATE_COLS), lambda i: (i, 0)),
        ],
        out_shape=[
            jax.ShapeDtypeStruct((n, RET_COLS), _BF16),
            jax.ShapeDtypeStruct((n, SWA_COLS), _BF16),
            jax.ShapeDtypeStruct((n, GATE_COLS), _BF16),
        ],
        compiler_params=pltpu.CompilerParams(
            dimension_semantics=("arbitrary",), vmem_limit_bytes=VMEM_LIMIT),
        name="inproj",
    )(x2d, g, w_in_bf16, cos_tab, sin_tab, gn_row)


def _retention_stages(q_ref, k_ref, v_ref, g_ref, out_ref, state_ref):
    c = RET_CHUNK
    causal = _iota((c, c), 0) >= _iota((c, c), 1)

    for h in range(RET_HEADS):
        chunk_decay = math.exp(RET_LOG_DECAY[h] * c)
        qs = slice(h * RET_QK_DIM, (h + 1) * RET_QK_DIM)
        vs = slice(h * RET_V_DIM, (h + 1) * RET_V_DIM)
        chunks = [slice(ci * c, (ci + 1) * c) for ci in range(SEQ_TILE // c)]
        scores = [jnp.where(causal, _dot_nt(q_ref[rows, qs], k_ref[rows, qs]), 0.0).astype(_BF16) for rows in chunks]
        kvs = [_dot(k_ref[rows, qs].astype(_F32).T.astype(_BF16), v_ref[rows, vs]) for rows in chunks]
        yield
        state = state_ref[h]
        states = []
        for kv in kvs:
            states.append(state.astype(_BF16))
            state = chunk_decay * (state + kv)
        state_ref[h] = state
        rets = [_dot(jnp.concatenate([s, q_ref[rows, qs]], axis=1), jnp.concatenate([v_ref[rows, vs], st], axis=0))
                for rows, s, st in zip(chunks, scores, states)]
        yield
        for rows, ret in zip(chunks, rets):
            mu = jnp.mean(ret, axis=-1, keepdims=True)
            dev = ret - mu
            var = jnp.mean(dev * dev, axis=-1, keepdims=True)
            out_ref[rows, vs] = ((dev * lax.rsqrt(var + NORM_EPS)) * g_ref[rows, vs].astype(_F32)).astype(_BF16)
        yield


def _swa_stages(first, sinks_ref, q_ref, kc_ref, vc_ref, kp_ref, vp_ref, qg_ref, kg_ref, out_ref):
    w = SWA_WINDOW
    d = SWA_HEAD_DIM
    n_blk = SEQ_TILE // w
    pair = 2 * d

    k_both = jnp.concatenate([kp_ref[...], kc_ref[...]], axis=0).astype(_F32)
    same_half = (_iota((pair, pair), 0) // d == _iota((pair, pair), 1) // d).astype(_BF16)
    k_hi, k_lo = _split_bf16(k_both * k_both)
    k_ssq = _dot(k_hi, same_half) + _dot(k_lo, same_half)
    k_n = k_both * lax.rsqrt(k_ssq * (1.0 / d) + NORM_EPS) * (kg_ref[...] * d ** -0.5)
    v_both = jnp.concatenate([vp_ref[...], vc_ref[...]], axis=0).astype(_F32)
    v_t = [v_both[b * w:(b + 1) * w].T for b in range(n_blk + 1)]

    key = _iota((2 * w, SWA_GROUP * w), 0)
    qry = _iota((2 * w, SWA_GROUP * w), 1) % w
    band = (key > qry) & (key <= qry + w)
    band_first = band & jnp.logical_or(key >= w, jnp.logical_not(first))
    head_of_lane = _iota((1, SWA_GROUP * w), 1) // w
    half_of_lane = _iota((1, pair), 1) // d
    half_rows = (_iota((SUBLANES, pair), 0) == _iota((SUBLANES, pair), 1) // d).astype(_BF16)
    ones_rows = jnp.ones((2 * SUBLANES, 2 * w), _BF16)
    q_gain = qg_ref[...]
    yield

    for kk in range(SWA_KV_HEADS):
        k_native = jnp.where(half_of_lane == kk, k_n, 0.0)
        k_moved = pltpu.roll(k_native, d, 1)
        k_even, k_odd = (k_native, k_moved) if kk == 0 else (k_moved, k_native)
        k_even = k_even.astype(_BF16)
        k_odd = k_odd.astype(_BF16)
        sink_row = jnp.zeros((1, SWA_GROUP * w), _F32)
        for g in range(SWA_GROUP):
            sink_row = jnp.where(head_of_lane == g, sinks_ref[kk * SWA_GROUP + g], sink_row)
        blocks = range(n_blk)
        pairs = range(SWA_GROUP // 2)
        q_pairs = [[q_ref[j * w:(j + 1) * w, (kk * SWA_GROUP + 2 * p) * d:(kk * SWA_GROUP + 2 * p + 2) * d].astype(_F32)
                    for p in pairs] for j in blocks]
        q_splits = [[_split_bf16(q * q) for q in qs] for qs in q_pairs]
        q_ssq = [[_dot_nt(half_rows, hi) + _dot_nt(half_rows, lo) for hi, lo in sp] for sp in q_splits]
        q_rs = [[lax.rsqrt(s * (1.0 / d) + NORM_EPS) for s in ss] for ss in q_ssq]
        q_gs = [[(q * q_gain).astype(_BF16) for q in qs] for qs in q_pairs]
        yield
        raw_even = [[_dot_nt(k_even[j * w:(j + 2) * w], q_gs[j][p]) for p in pairs] for j in blocks]
        raw_odd = [[_dot_nt(k_odd[j * w:(j + 2) * w], q_gs[j][p]) for p in pairs] for j in blocks]
        yield
        score_blocks = []
        for j in blocks:
            raws = []
            for p in pairs:
                raws.append(raw_even[j][p] * q_rs[j][p][0:1, :])
                raws.append(raw_odd[j][p] * q_rs[j][p][1:2, :])
            s_t = jnp.concatenate(raws, axis=1)
            score_blocks.append(jnp.where(band_first if j == 0 else band, s_t, NEG_BIG))
        yield
        maxes = [jnp.maximum(jnp.max(s_t, axis=0, keepdims=True), sink_row) for s_t in score_blocks]
        probs = [jnp.exp(s_t - m).astype(_BF16) for s_t, m in zip(score_blocks, maxes)]
        yield
        outs = []
        for j in blocks:
            v_ext = jnp.concatenate([v_t[j][kk * d:(kk + 1) * d], v_t[j + 1][kk * d:(kk + 1) * d]], axis=1)
            outs.append(_dot(jnp.concatenate([v_ext.astype(_BF16), ones_rows], axis=0), probs[j]))
        yield
        for j in blocks:
            rows = slice(j * w, (j + 1) * w)
            denom = outs[j][d:d + 1, :] + jnp.exp(sink_row - maxes[j])
            o_t = outs[j][0:d, :] * (1.0 / denom)
            for p in range(SWA_GROUP // 2):
                c0 = (kk * SWA_GROUP + 2 * p) * d
                both = jnp.concatenate([o_t[:, 2 * p * w:(2 * p + 1) * w], o_t[:, (2 * p + 1) * w:(2 * p + 2) * w]], axis=0)
                out_ref[rows, c0:c0 + pair] = both.T.astype(_BF16)
        yield


def _mixers_kernel(sinks_ref, rq_ref, rk_ref, rv_ref, rg_ref, sq_ref, kc_ref, vc_ref, kp_ref, vp_ref, qg_ref, kg_ref,
                   retg_ref, attn_ref, state_ref):
    first = pl.program_id(1) == 0

    @pl.when(first)
    def _():
        state_ref[...] = jnp.zeros_like(state_ref)

    streams = [_retention_stages(rq_ref, rk_ref, rv_ref, rg_ref, retg_ref, state_ref),
               _swa_stages(first, sinks_ref, sq_ref, kc_ref, vc_ref, kp_ref, vp_ref, qg_ref, kg_ref, attn_ref)]
    while streams:
        streams = [s for s in streams if next(s, StopIteration) is not StopIteration]


def _mixers(ret_proj, swa_proj, sinks, q_gain2, k_gain2, batch, seq):
    n = ret_proj.shape[0]
    steps = seq // SEQ_TILE
    blocks_per_step = SEQ_TILE // SWA_WINDOW
    rowmap = lambda b, s: b * steps + s
    prevmap = lambda b, s: jnp.maximum((b * steps + s) * blocks_per_step - 1, 0)
    k_col = SWA_Q_W // SWA_KV_W
    return pl.pallas_call(
        _mixers_kernel,
        grid=(batch, steps),
        in_specs=[
            pl.BlockSpec(memory_space=pltpu.SMEM),
            pl.BlockSpec((SEQ_TILE, RET_QK_W), lambda b, s: (rowmap(b, s), 0)),
            pl.BlockSpec((SEQ_TILE, RET_QK_W), lambda b, s: (rowmap(b, s), 1)),
            pl.BlockSpec((SEQ_TILE, RET_V_W), lambda b, s: (rowmap(b, s), 1)),
            pl.BlockSpec((SEQ_TILE, RET_V_W), lambda b, s: (rowmap(b, s), 2)),
            pl.BlockSpec((SEQ_TILE, SWA_Q_W), lambda b, s: (rowmap(b, s), 0)),
            pl.BlockSpec((SEQ_TILE, SWA_KV_W), lambda b, s: (rowmap(b, s), k_col)),
            pl.BlockSpec((SEQ_TILE, SWA_KV_W), lambda b, s: (rowmap(b, s), k_col + 1)),
            pl.BlockSpec((SWA_WINDOW, SWA_KV_W), lambda b, s: (prevmap(b, s), k_col)),
            pl.BlockSpec((SWA_WINDOW, SWA_KV_W), lambda b, s: (prevmap(b, s), k_col + 1)),
            pl.BlockSpec((1, 2 * SWA_HEAD_DIM), lambda b, s: (0, 0)),
            pl.BlockSpec((1, 2 * SWA_HEAD_DIM), lambda b, s: (0, 0)),
        ],
        out_specs=[
            pl.BlockSpec((SEQ_TILE, RET_V_W), lambda b, s: (rowmap(b, s), 0)),
            pl.BlockSpec((SEQ_TILE, SWA_Q_W), lambda b, s: (rowmap(b, s), 0)),
        ],
        out_shape=[
            jax.ShapeDtypeStruct((n, RET_V_W), _BF16),
            jax.ShapeDtypeStruct((n, SWA_Q_W), _BF16),
        ],
        scratch_shapes=[pltpu.VMEM((RET_HEADS, RET_QK_DIM, RET_V_DIM), _F32)],
        compiler_params=pltpu.CompilerParams(
            dimension_semantics=("arbitrary", "arbitrary"), vmem_limit_bytes=VMEM_LIMIT),
        name="mixers",
    )(sinks, ret_proj, ret_proj, ret_proj, ret_proj, swa_proj, swa_proj, swa_proj, swa_proj, swa_proj,
      q_gain2, k_gain2)


def _outproj_kernel(x_ref, retg_ref, attn_ref, ga_ref, gb_ref, wro_ref, wso_ref, wout_ref, g2_ref,
                    wrt_ref, brt_ref, x1_ref, h2_ref, info_ref, info_t_ref, counts_ref, logits_ref):
    tm = ROW_TILE
    step = pl.program_id(0)
    cur = step % 2

    @pl.when(step == 0)
    def _():
        logits_ref[...] = jnp.zeros_like(logits_ref)

    logits = logits_ref[1 - cur]
    row = _iota((LANES, tm), 0)
    row_f = row.astype(_F32)
    is_group = (row >= ROUTER_GROUP_LANE) & (row < ROUTER_GROUP_LANE + MOE_GROUPS)

    subs = [slice(s * OUTPROJ_SUB, (s + 1) * OUTPROJ_SUB) for s in range(tm // OUTPROJ_SUB)]
    y_a = [_dot(retg_ref[rows, :], wro_ref[...]) for rows in subs]

    gl = jnp.where(is_group, logits, NEG_BIG)
    g_max = jnp.max(gl, axis=0, keepdims=True)
    g_prob = 1.0 / jnp.sum(jnp.exp(gl - g_max), axis=0, keepdims=True)
    g_idx = jnp.min(jnp.where(gl == g_max, row_f - ROUTER_GROUP_LANE, float(LANES)), axis=0, keepdims=True)

    y_b = [_dot(attn_ref[rows, :], wso_ref[...]) for rows in subs]

    e_lo = g_idx * MOE_EXPERTS_PER_GROUP
    in_group = (row_f >= e_lo) & (row_f < e_lo + MOE_EXPERTS_PER_GROUP)
    el = jnp.where(in_group, logits, NEG_BIG)
    t1 = jnp.max(el, axis=0, keepdims=True)
    i1 = jnp.min(jnp.where(el == t1, row_f, float(LANES)), axis=0, keepdims=True)

    merged = [(ga_ref[rows, :].astype(_F32) * a + gb_ref[rows, :].astype(_F32) * b).astype(_BF16)
              for rows, a, b in zip(subs, y_a, y_b)]
    mixed = [_dot(m, wout_ref[...]) for m in merged]
    x1 = [x_ref[rows, :] + m for rows, m in zip(subs, mixed)]
    for rows, v in zip(subs, x1):
        x1_ref[rows, :] = v

    el2 = jnp.where(row_f == i1, NEG_BIG, el)
    t2 = jnp.max(el2, axis=0, keepdims=True)
    i2 = jnp.min(jnp.where(el2 == t2, row_f, float(LANES)), axis=0, keepdims=True)
    e21 = jnp.exp(t2 - t1)
    w1 = g_prob / (1.0 + e21)
    w2 = g_prob * e21 / (1.0 + e21)

    h2 = [((v * lax.rsqrt(jnp.mean(v * v, axis=-1, keepdims=True) + NORM_EPS)) * g2_ref[...]).astype(_BF16)
          for v in x1]
    for rows, v in zip(subs, h2):
        h2_ref[rows, :] = v

    sel1 = row_f == i1
    sel2 = row_f == i2
    onehot = jnp.where(sel1 | sel2, 1.0, 0.0).astype(_BF16)
    earlier = (_iota((tm, tm), 0) < _iota((tm, tm), 1)).astype(_BF16)
    before = _dot(onehot, earlier)
    counts = _dot(onehot, jnp.ones((tm, LANES), _BF16))

    for rows, v in zip(subs, h2):
        logits_ref[cur, :, rows] = _dot_nt(wrt_ref[...], v) + brt_ref[:, 0:1]

    run = jnp.floor((counts + (RUN_ALIGN - 1.0)) * (1.0 / RUN_ALIGN)) * RUN_ALIGN
    lower_experts = (_iota((LANES, LANES), 0) > _iota((LANES, LANES), 1)).astype(_BF16)
    run_off = _dot(lower_experts, run.astype(_BF16))
    place = before + jnp.concatenate([run_off] * (tm // LANES), axis=1)
    pos1 = jnp.sum(jnp.where(sel1, place, 0.0), axis=0, keepdims=True)
    pos2 = jnp.sum(jnp.where(sel2, place, 0.0), axis=0, keepdims=True)
    counts_ref[...] = counts.T[0:SUBLANES, :]

    info_t = jnp.concatenate([i1, i2, w1, w2, pos1, pos2, jnp.zeros((2, tm), _F32)], axis=0)
    info_t_ref[...] = info_t
    info_ref[...] = jnp.concatenate([info_t, jnp.zeros((LANES - SUBLANES, tm), _F32)], axis=0).T


def _outproj(x2d, retg, attn, gates, w_ret_o, w_swa_o, w_out, g2, w_router_t, b_router_t):
    n = x2d.shape[0]
    tm = ROW_TILE
    n_tiles = n // tm
    const = lambda s: (0, 0)
    proj = lambda s: jnp.minimum(s, n_tiles - 1)
    routed = lambda s: jnp.maximum(s - 1, 0)
    return pl.pallas_call(
        _outproj_kernel,
        grid=(n_tiles + 1,),
        in_specs=[
            pl.BlockSpec((tm, D_MODEL), lambda s: (proj(s), 0)),
            pl.BlockSpec((tm, RET_V_W), lambda s: (proj(s), 0)),
            pl.BlockSpec((tm, SWA_Q_W), lambda s: (proj(s), 0)),
            pl.BlockSpec((tm, D_MODEL), lambda s: (proj(s), 0)),
            pl.BlockSpec((tm, D_MODEL), lambda s: (proj(s), 1)),
            pl.BlockSpec((RET_V_W, D_MODEL), const),
            pl.BlockSpec((SWA_Q_W, D_MODEL), const),
            pl.BlockSpec((D_MODEL, D_MODEL), const),
            pl.BlockSpec((1, D_MODEL), const),
            pl.BlockSpec((LANES, D_MODEL), const),
            pl.BlockSpec((LANES, LANES), const),
        ],
        out_specs=[
            pl.BlockSpec((tm, D_MODEL), lambda s: (proj(s), 0)),
            pl.BlockSpec((tm, D_MODEL), lambda s: (proj(s), 0)),
            pl.BlockSpec((tm, LANES), lambda s: (routed(s), 0)),
            pl.BlockSpec((SUBLANES, tm), lambda s: (0, routed(s))),
            pl.BlockSpec((SUBLANES, LANES), lambda s: (routed(s), 0)),
        ],
        scratch_shapes=[pltpu.VMEM((2, LANES, tm), _F32)],
        out_shape=[
            jax.ShapeDtypeStruct((n, D_MODEL), _F32),
            jax.ShapeDtypeStruct((n, D_MODEL), _BF16),
            jax.ShapeDtypeStruct((n, LANES), _F32),
            jax.ShapeDtypeStruct((SUBLANES, n), _F32),
            jax.ShapeDtypeStruct((n // tm * SUBLANES, LANES), _F32),
        ],
        compiler_params=pltpu.CompilerParams(
            dimension_semantics=("arbitrary",), vmem_limit_bytes=VMEM_LIMIT),
        name="outproj",
    )(x2d, retg, attn, gates, gates, w_ret_o, w_swa_o, w_out, g2, w_router_t, b_router_t)


def _piece_sizes(largest):
    piece = largest
    while piece >= RUN_ALIGN:
        yield piece
        piece //= 2


BIG_CAP = SORT_ROWS // RUN_PIECE
SMALL_CAP = MOE_N_EXPERTS
PIECE_LIST_KEYS = ("n_big", "big_off", "big_dst", "n_small", "small_off", "small_dst")


def _start_run_copies(pieces, tile, make_copy):
    n_big_ref, big_off_ref, big_dst_ref, n_small_ref, small_off_ref, small_dst_ref = pieces

    def big(k, carry):
        i = tile * BIG_CAP + k
        make_copy(pl.multiple_of(big_off_ref[i], RUN_ALIGN), pl.multiple_of(big_dst_ref[i], RUN_ALIGN),
                  RUN_PIECE).start()
        return carry

    lax.fori_loop(0, n_big_ref[tile], big, 0)

    def small(k, carry):
        i = tile * SMALL_CAP + k
        make_copy(pl.multiple_of(small_off_ref[i], RUN_ALIGN), pl.multiple_of(small_dst_ref[i], RUN_ALIGN),
                  RUN_ALIGN).start()
        return carry

    lax.fori_loop(0, n_small_ref[tile], small, 0)


def _wait_rows(rows, make_copy):
    for piece in _piece_sizes(SORT_ROWS_POW2 // 2):
        pl.when((rows & piece) != 0)(make_copy(0, 0, piece).wait)


def _zero_fill_padding(pad_start_ref, pad_len_ref, zeros_ref, xs_hbm, sem, wait):
    def per_expert(e, carry):
        pos = pad_start_ref[e]
        length = pad_len_ref[e]
        for piece in _piece_sizes(EXPERT_TILE // 2):
            copy = pltpu.make_async_copy(
                zeros_ref.at[pl.ds(0, piece)], xs_hbm.at[pl.ds(pl.multiple_of(pos, RUN_ALIGN), piece)], sem)
            pl.when((length & piece) != 0)(copy.wait if wait else copy.start)
            pos = pos + (length & piece)
        return carry

    lax.fori_loop(0, MOE_N_EXPERTS, per_expert, 0)

    tail_start = pad_start_ref[MOE_N_EXPERTS]
    n_rows = xs_hbm.shape[0]
    half = EXPERT_TILE // 2

    def per_half_tile(j, carry):
        pos = tail_start + j * half
        copy = pltpu.make_async_copy(
            zeros_ref, xs_hbm.at[pl.ds(pl.multiple_of(jnp.minimum(pos, n_rows - half), RUN_ALIGN), half)], sem)
        pl.when(pos < n_rows)(copy.wait if wait else copy.start)
        return carry

    lax.fori_loop(0, n_rows // half, per_half_tile, 0)


def _dispatch_kernel(*refs):
    pieces = refs[:len(PIECE_LIST_KEYS)]
    rows_ref, pad_start_ref, pad_len_ref, h2_ref, info_t_ref, xs_hbm, sorted_ref, zeros_ref, sems, zero_sem = (
        refs[len(PIECE_LIST_KEYS):])
    tile = pl.program_id(0)
    last = pl.num_programs(0) - 1
    slot = tile % 2
    tm = ROW_TILE

    @pl.when(tile == 0)
    def _():
        zeros_ref[...] = jnp.zeros_like(zeros_ref)
        _zero_fill_padding(pad_start_ref, pad_len_ref, zeros_ref, xs_hbm, zero_sem, wait=False)
        _zero_fill_padding(pad_start_ref, pad_len_ref, zeros_ref, xs_hbm, zero_sem, wait=True)

    pos = info_t_ref[4:6, :].astype(jnp.int32)

    def sort_rows(lo, hi):
        place = _iota((hi - lo, tm), 0) + lo
        onehot = ((place == pos[0:1, :]) | (place == pos[1:2, :])).astype(_BF16)
        for c in range(0, D_MODEL, 256):
            sorted_ref[slot, lo:hi, c:c + 256] = _dot(onehot, h2_ref[:, c:c + 256]).astype(_BF16)

    sort_rows(0, SORT_COMMON)
    pl.when(rows_ref[tile] > SORT_COMMON)(lambda: sort_rows(SORT_COMMON, SORT_ROWS))

    def copy_maker(t):
        s = t % 2

        def make_copy(off, dst, piece):
            return pltpu.make_async_copy(
                sorted_ref.at[s, pl.ds(off, piece)], xs_hbm.at[pl.ds(dst, piece)], sems.at[s])

        return make_copy

    _start_run_copies(pieces, tile, copy_maker(tile))

    @pl.when(tile > 0)
    def _():
        _wait_rows(rows_ref[tile - 1], copy_maker(tile - 1))

    @pl.when(tile == last)
    def _():
        _wait_rows(rows_ref[tile], copy_maker(tile))


def _dispatch(plan, h2, info_t, n_rows):
    n = h2.shape[0]
    grid_spec = pltpu.PrefetchScalarGridSpec(
        num_scalar_prefetch=len(PIECE_LIST_KEYS) + 3,
        grid=(n // ROW_TILE,),
        in_specs=[
            pl.BlockSpec((ROW_TILE, D_MODEL), lambda i, *_: (i, 0)),
            pl.BlockSpec((SUBLANES, ROW_TILE), lambda i, *_: (0, i)),
        ],
        out_specs=pl.BlockSpec(memory_space=pl.ANY),
        scratch_shapes=[
            pltpu.VMEM((2, SORT_ROWS, D_MODEL), _BF16),
            pltpu.VMEM((EXPERT_TILE // 2, D_MODEL), _BF16),
            pltpu.SemaphoreType.DMA((2,)),
            pltpu.SemaphoreType.DMA(()),
        ],
    )
    return pl.pallas_call(
        _dispatch_kernel,
        grid_spec=grid_spec,
        out_shape=jax.ShapeDtypeStruct((n_rows, D_MODEL), _BF16),
        compiler_params=pltpu.CompilerParams(
            dimension_semantics=("arbitrary",), vmem_limit_bytes=VMEM_LIMIT),
        name="dispatch",
    )(*[plan[k] for k in PIECE_LIST_KEYS], plan["rows"], plan["pad_start"], plan["pad_len"], h2, info_t)


def _experts_kernel(te_ref, nt_ref, xs_ref, wg_ref, wu_ref, wd_ref, ys_ref, wg_s, wu_s, wd_s):
    i = pl.program_id(0)
    prev = te_ref[jnp.maximum(i - 1, 0)]
    changed = jnp.logical_or(i == 0, te_ref[i] != prev)

    @pl.when(changed)
    def _():
        wg_s[...] = wg_ref[...].astype(_BF16)
        wu_s[...] = wu_ref[...].astype(_BF16)
        wd_s[...] = wd_ref[...].astype(_BF16)

    subs = [slice(s * EXPERT_SUB, (s + 1) * EXPERT_SUB) for s in range(EXPERT_TILE // EXPERT_SUB)]
    gate = [_dot(xs_ref[rows, :], wg_s[...]) for rows in subs]
    up = [_dot(xs_ref[rows, :], wu_s[...]) for rows in subs]
    act = [((g * _sigmoid(g)) * u).astype(_BF16) for g, u in zip(gate, up)]
    for rows, a in zip(subs, act):
        ys_ref[rows, :] = _dot(a, wd_s[...]).astype(_BF16)


def _experts(plan, xs, w_gate, w_up, w_down):
    n_rows = xs.shape[0]
    grid_spec = pltpu.PrefetchScalarGridSpec(
        num_scalar_prefetch=2,
        grid=(plan["tiles_used"][0],),
        in_specs=[
            pl.BlockSpec((EXPERT_TILE, D_MODEL), lambda i, te, nt: (i, 0)),
            pl.BlockSpec((None, D_MODEL, MOE_D_FF), lambda i, te, nt: (te[i], 0, 0)),
            pl.BlockSpec((None, D_MODEL, MOE_D_FF), lambda i, te, nt: (te[i], 0, 0)),
            pl.BlockSpec((None, MOE_D_FF, D_MODEL), lambda i, te, nt: (te[i], 0, 0)),
        ],
        out_specs=pl.BlockSpec((EXPERT_TILE, D_MODEL), lambda i, te, nt: (i, 0)),
        scratch_shapes=[
            pltpu.VMEM((D_MODEL, MOE_D_FF), _BF16),
            pltpu.VMEM((D_MODEL, MOE_D_FF), _BF16),
            pltpu.VMEM((MOE_D_FF, D_MODEL), _BF16),
        ],
    )
    return pl.pallas_call(
        _experts_kernel,
        grid_spec=grid_spec,
        out_shape=jax.ShapeDtypeStruct((n_rows, D_MODEL), _BF16),
        input_output_aliases={2: 0},
        compiler_params=pltpu.CompilerParams(
            dimension_semantics=("arbitrary",), vmem_limit_bytes=VMEM_LIMIT),
        name="experts",
    )(plan["tile_expert"], plan["tiles_used"], xs, w_gate, w_up, w_down)


def _combine_kernel(*refs):
    pieces = refs[:len(PIECE_LIST_KEYS)]
    rows_ref, x1_ref, info_ref, ys_hbm, out_ref, block_ref, sems = refs[len(PIECE_LIST_KEYS):]
    tile = pl.program_id(0)
    last = pl.num_programs(0) - 1
    slot = tile % 2
    tm = ROW_TILE

    def copy_maker(t):
        s = t % 2

        def make_copy(off, dst, piece):
            return pltpu.make_async_copy(
                ys_hbm.at[pl.ds(dst, piece)], block_ref.at[s, pl.ds(off, piece)], sems.at[s])

        return make_copy

    @pl.when(tile == 0)
    def _():
        block_ref[...] = jnp.zeros_like(block_ref)
        _start_run_copies(pieces, tile, copy_maker(tile))

    @pl.when(tile < last)
    def _():
        _start_run_copies(pieces, tile + 1, copy_maker(tile + 1))

    _wait_rows(rows_ref[tile], copy_maker(tile))

    info = info_ref[...]

    def weights_of(lo, hi):
        place = (_iota((tm, hi - lo), 1) + lo).astype(_F32)
        return (jnp.where(place == info[:, 4:5], info[:, 2:3], 0.0)
                + jnp.where(place == info[:, 5:6], info[:, 3:4], 0.0)).astype(_BF16)

    weights = weights_of(0, SORT_COMMON)
    for c in range(0, D_MODEL, 256):
        out_ref[:, c:c + 256] = x1_ref[:, c:c + 256] + _dot(weights, block_ref[slot, 0:SORT_COMMON, c:c + 256])

    @pl.when(rows_ref[tile] > SORT_COMMON)
    def _():
        tail = weights_of(SORT_COMMON, SORT_ROWS)
        for c in range(0, D_MODEL, 256):
            out_ref[:, c:c + 256] += _dot(tail, block_ref[slot, SORT_COMMON:SORT_ROWS, c:c + 256])


def _combine(plan, x1, info, ys):
    n = x1.shape[0]
    grid_spec = pltpu.PrefetchScalarGridSpec(
        num_scalar_prefetch=len(PIECE_LIST_KEYS) + 1,
        grid=(n // ROW_TILE,),
        in_specs=[
            pl.BlockSpec((ROW_TILE, D_MODEL), lambda i, *_: (i, 0)),
            pl.BlockSpec((ROW_TILE, LANES), lambda i, *_: (i, 0)),
            pl.BlockSpec(memory_space=pl.ANY),
        ],
        out_specs=pl.BlockSpec((ROW_TILE, D_MODEL), lambda i, *_: (i, 0)),
        scratch_shapes=[pltpu.VMEM((2, SORT_ROWS, D_MODEL), _BF16), pltpu.SemaphoreType.DMA((2,))],
    )
    return pl.pallas_call(
        _combine_kernel,
        grid_spec=grid_spec,
        out_shape=jax.ShapeDtypeStruct((n, D_MODEL), _F32),
        compiler_params=pltpu.CompilerParams(
            dimension_semantics=("arbitrary",), vmem_limit_bytes=VMEM_LIMIT),
        name="combine",
    )(*[plan[k] for k in PIECE_LIST_KEYS], plan["rows"], x1, info, ys)


def _rotary_tables(seq):
    half = RET_QK_DIM // 2
    inv = ROPE_BASE ** (-jnp.arange(half, dtype=_F32) / half)
    hi = (jnp.arange(seq // RET_CHUNK, dtype=jnp.int32) * RET_CHUNK).astype(_F32)[:, None] * inv[None, :]
    lo = jnp.arange(RET_CHUNK, dtype=jnp.int32).astype(_F32)[:, None] * inv[None, :]
    cos_hi, sin_hi = jnp.cos(hi)[:, None, :], jnp.sin(hi)[:, None, :]
    cos_lo, sin_lo = jnp.cos(lo)[None, :, :], jnp.sin(lo)[None, :, :]
    cos = (cos_hi * cos_lo - sin_hi * sin_lo).reshape(seq, half)
    sin = (sin_hi * cos_lo + cos_hi * sin_lo).reshape(seq, half)
    return jnp.concatenate([cos, cos], axis=1), jnp.concatenate([-sin, sin], axis=1)


def _expert_row_bound(n_tokens):
    rows = (2 * n_tokens + (n_tokens // ROW_TILE) * MOE_N_EXPERTS * (RUN_ALIGN - 1)
            + MOE_N_EXPERTS * (EXPERT_TILE - RUN_ALIGN))
    return -(-rows // EXPERT_TILE) * EXPERT_TILE


def _routing_plan(counts_out, n_rows):
    te = EXPERT_TILE
    n_tok_tiles = counts_out.shape[0] // SUBLANES
    counts = counts_out.reshape(n_tok_tiles, SUBLANES, LANES)[:, 0, :MOE_N_EXPERTS].astype(jnp.int32)
    run = (counts + RUN_ALIGN - 1) // RUN_ALIGN * RUN_ALIGN
    off = jnp.cumsum(run, axis=1) - run
    total = jnp.sum(run, axis=0)
    padded = (total + te - 1) // te * te
    pends = jnp.cumsum(padded)
    pstarts = pends - padded
    dst = pstarts[None, :] + jnp.cumsum(run, axis=0) - run
    tile_start = jnp.arange(n_rows // te, dtype=jnp.int32) * te
    tile_expert = jnp.sum((tile_start[:, None] >= pends[None, :]).astype(jnp.int32), axis=1)
    tile_expert = jnp.minimum(tile_expert, MOE_N_EXPERTS - 1)
    i32 = lambda v: v.astype(jnp.int32)

    n_big = run // RUN_PIECE
    big_first = jnp.cumsum(n_big, axis=1) - n_big
    k_big = jnp.arange(BIG_CAP, dtype=jnp.int32)[None, :, None]
    owns_big = (k_big >= big_first[:, None, :]) & (k_big < (big_first + n_big)[:, None, :])
    pick_big = lambda v: jnp.sum(jnp.where(owns_big, (v - RUN_PIECE * big_first)[:, None, :], 0), axis=-1)
    has_small = (run // RUN_ALIGN) % 2
    small_index = jnp.cumsum(has_small, axis=1) - has_small
    k_small = jnp.arange(SMALL_CAP, dtype=jnp.int32)[None, :, None]
    owns_small = (has_small[:, None, :] == 1) & (small_index[:, None, :] == k_small)
    pick_small = lambda v: jnp.sum(jnp.where(owns_small, (v + RUN_PIECE * n_big)[:, None, :], 0), axis=-1)
    return {
        "n_big": i32(jnp.sum(n_big, axis=1)),
        "big_off": i32((pick_big(off) + RUN_PIECE * k_big[..., 0]).reshape(-1)),
        "big_dst": i32((pick_big(dst) + RUN_PIECE * k_big[..., 0]).reshape(-1)),
        "n_small": i32(jnp.sum(has_small, axis=1)),
        "small_off": i32(pick_small(off).reshape(-1)),
        "small_dst": i32(pick_small(dst).reshape(-1)),
        "rows": i32(jnp.sum(run, axis=1)),
        "pad_start": i32(jnp.concatenate([pstarts + total, pends[-1:]])), "pad_len": i32(padded - total),
        "tile_expert": i32(tile_expert), "tiles_used": i32(pends[-1:] // te),
    }


def kernel(x, norm_mix_g, w_in, ret_gn_g, w_ret_o, q_norm_g, k_norm_g, sinks, w_swa_o, w_out, norm_ffn_g,
           w_router_group, b_router_group, w_router_expert, b_router_expert, w_gate, w_up, w_down):
    batch, seq, d = x.shape
    n = batch * seq
    assert d == D_MODEL and seq % SEQ_TILE == 0 and n % ROW_TILE == 0
    cos_tab, sin_tab = _rotary_tables(seq)
    n_rows = _expert_row_bound(n)
    for l in range(w_in.shape[0]):
        x2d = x.reshape(n, d)
        ret_proj, swa_proj, gates = _inproj(x2d, norm_mix_g[l][None, :], w_in[l].astype(_BF16), cos_tab, sin_tab,
                                            ret_gn_g[l].reshape(1, RET_V_W))
        retg, attn = _mixers(ret_proj, swa_proj, sinks[l], jnp.tile(q_norm_g[l], 2)[None, :],
                             jnp.tile(k_norm_g[l], 2)[None, :], batch, seq)
        pad = LANES - MOE_N_EXPERTS - MOE_GROUPS
        w_router_t = jnp.concatenate(
            [w_router_expert[l].T, w_router_group[l].T, jnp.zeros((pad, d), _F32)], axis=0).astype(_BF16)
        b_router = jnp.concatenate([b_router_expert[l], b_router_group[l], jnp.zeros((pad,), _F32)])
        b_router_t = jnp.broadcast_to(b_router[:, None], (LANES, LANES))
        x1, h2, info, info_t, counts = _outproj(
            x2d, retg, attn, gates, w_ret_o[l].astype(_BF16), w_swa_o[l].astype(_BF16), w_out[l].astype(_BF16),
            norm_ffn_g[l][None, :], w_router_t, b_router_t)
        plan = _routing_plan(counts, n_rows)
        xs = _dispatch(plan, h2, info_t, n_rows)
        ys = _experts(plan, xs, w_gate[l], w_up[l], w_down[l])
        x = _combine(plan, x1, info, ys).reshape(batch, seq, d)
    return x
```

```python
import math

import numpy as np
import jax
import jax.numpy as jnp
from jax import lax
from jax.experimental import pallas as pl
from jax.experimental.pallas import tpu as pltpu

D_MODEL = 1024
RET_HEADS = 4
RET_QK_DIM = 128
RET_V_DIM = 256
RET_CHUNK = 128
ROPE_BASE = 10000.0
SWA_Q_HEADS = 8
SWA_KV_HEADS = 2
SWA_HEAD_DIM = 64
SWA_WINDOW = 128
MOE_GROUPS = 4
MOE_EXPERTS_PER_GROUP = 8
MOE_D_FF = 512
NORM_EPS = 1e-6

RET_QK_W = RET_HEADS * RET_QK_DIM
RET_V_W = RET_HEADS * RET_V_DIM
SWA_Q_W = SWA_Q_HEADS * SWA_HEAD_DIM
SWA_KV_W = SWA_KV_HEADS * SWA_HEAD_DIM
SWA_GROUP = SWA_Q_HEADS // SWA_KV_HEADS
MOE_N_EXPERTS = MOE_GROUPS * MOE_EXPERTS_PER_GROUP

RET_COLS = 2 * RET_QK_W + 2 * RET_V_W
SWA_COLS = SWA_Q_W + 2 * SWA_KV_W
GATE_COLS = 2 * D_MODEL
IN_WIDTH = RET_COLS + SWA_COLS + GATE_COLS

LANES = 128
SUBLANES = 8
ROUTER_GROUP_LANE = 32
NEG_BIG = -1e30

ROW_TILE = 512
SEQ_TILE = 512
EXPERT_TILE = 512
OUTPROJ_SUB = 128
INPROJ_SUB = 256
EXPERT_SUB = 256
RUN_ALIGN = 16
SORT_ROWS = 2 * ROW_TILE + MOE_N_EXPERTS * RUN_ALIGN
SORT_ROWS_POW2 = 1 << SORT_ROWS.bit_length()
SORT_COMMON = 2 * ROW_TILE + MOE_N_EXPERTS * RUN_ALIGN * 9 // 16
RUN_PIECE = 2 * RUN_ALIGN
VMEM_LIMIT = 56 * 1024 * 1024

RET_LOG_DECAY = tuple(float(np.log1p(-np.exp2(-5.0 - h))) for h in range(RET_HEADS))

_BF16 = jnp.bfloat16
_F32 = jnp.float32


def _dot(a, b):
    return jnp.dot(a, b, preferred_element_type=_F32)


def _dot_nt(a, b):
    return lax.dot_general(a, b, (((1,), (1,)), ((), ())), preferred_element_type=_F32)


def _sigmoid(v):
    return 1.0 / (1.0 + jnp.exp(-v))


def _split_bf16(v):
    hi = v.astype(_BF16)
    return hi, (v - hi.astype(_F32)).astype(_BF16)


def _iota(shape, dim):
    return lax.broadcasted_iota(jnp.int32, shape, dim)


def _inproj_kernel(x_ref, g_ref, w_ref, cos_ref, sin_ref, gn_ref, ret_ref, swa_ref, gate_ref):
    subs = [slice(s * INPROJ_SUB, (s + 1) * INPROJ_SUB) for s in range(ROW_TILE // INPROJ_SUB)]

    def normed(rows):
        x = x_ref[rows, :]
        r = lax.rsqrt(jnp.mean(x * x, axis=-1, keepdims=True) + NORM_EPS)
        return ((x * r) * g_ref[...]).astype(_BF16)
    chunk_pos = (_iota((INPROJ_SUB, RET_QK_DIM), 0) % RET_CHUNK + 1).astype(_F32)

    def rotary_decay(base, sign, scale):
        def epilogue(rows, raw):
            cos = cos_ref[rows, :]
            sin = sin_ref[rows, :]
            for hd in range(RET_HEADS):
                v = raw[:, hd * RET_QK_DIM:(hd + 1) * RET_QK_DIM]
                rot = v * cos + pltpu.roll(v, RET_QK_DIM // 2, 1) * sin
                decay = jnp.exp((sign * RET_LOG_DECAY[hd]) * chunk_pos) * scale
                ret_ref[rows, base + hd * RET_QK_DIM:base + (hd + 1) * RET_QK_DIM] = (rot * decay).astype(_BF16)
        return epilogue

    def store(ref, lo, fn):
        def epilogue(rows, raw):
            ref[rows, lo:lo + raw.shape[1]] = fn(raw).astype(_BF16)
        return epilogue

    v_base = 2 * RET_QK_W
    g_base = v_base + RET_V_W
    gate_base = RET_COLS + SWA_COLS
    jobs = [(0, RET_QK_W, rotary_decay(0, 1.0, 1.0)),
            (RET_QK_W, RET_QK_W, rotary_decay(RET_QK_W, -1.0, RET_QK_DIM ** -0.5))]
    jobs += [(v_base + c, 512, store(ret_ref, v_base + c, lambda raw: raw)) for c in range(0, RET_V_W, 512)]
    jobs += [(g_base + c, 512, store(ret_ref, g_base + c,
                                     lambda raw, c=c: (raw * _sigmoid(raw)) * gn_ref[:, c:c + 512]))
             for c in range(0, RET_V_W, 512)]
    jobs += [(RET_COLS + c, 256, store(swa_ref, c, lambda raw: raw)) for c in range(0, SWA_COLS, 256)]
    jobs += [(gate_base + c, 512, store(gate_ref, c, _sigmoid)) for c in range(0, GATE_COLS, 512)]
    hs = []
    for lo, width, epilogue in jobs:
        raws = []
        for s, rows in enumerate(subs):
            if len(hs) <= s:
                hs.append(normed(rows))
            raws.append(_dot(hs[s], w_ref[:, lo:lo + width]))
        for rows, raw in zip(subs, raws):
            epilogue(rows, raw)


def _inproj(x2d, g, w_in_bf16, cos_tab, sin_tab, gn_row):
    n = x2d.shape[0]
    grid = (n // ROW_TILE,)
    seq_tiles = cos_tab.shape[0] // ROW_TILE
    return pl.pallas_call(
        _inproj_kernel,
        grid=grid,
        in_specs=[
            pl.BlockSpec((ROW_TILE, D_MODEL), lambda i: (i, 0)),
            pl.BlockSpec((1, D_MODEL), lambda i: (0, 0)),
            pl.BlockSpec((D_MODEL, IN_WIDTH), lambda i: (0, 0), pipeline_mode=pl.Buffered(1)),
            pl.BlockSpec((ROW_TILE, RET_QK_DIM), lambda i: (i % seq_tiles, 0)),
            pl.BlockSpec((ROW_TILE, RET_QK_DIM), lambda i: (i % seq_tiles, 0)),
            pl.BlockSpec((1, RET_V_W), lambda i: (0, 0)),
        ],
        out_specs=[
            pl.BlockSpec((ROW_TILE, RET_COLS), lambda i: (i, 0)),
            pl.BlockSpec((ROW_TILE, SWA_COLS), lambda i: (i, 0)),
            pl.BlockSpec((ROW_TILE, GATE_COLS), lambda i: (i, 0)),
        ],
        out_shape=[
            jax.ShapeDtypeStruct((n, RET_COLS), _BF16),
            jax.ShapeDtypeStruct((n, SWA_COLS), _BF16),
            jax.ShapeDtypeStruct((n, GATE_COLS), _BF16),
        ],
        compiler_params=pltpu.CompilerParams(
            dimension_semantics=("arbitrary",), vmem_limit_bytes=VMEM_LIMIT),
        name="inproj",
    )(x2d, g, w_in_bf16, cos_tab, sin_tab, gn_row)


def _retention_stages(q_ref, k_ref, v_ref, g_ref, out_ref, state_ref):
    c = RET_CHUNK
    causal = _iota((c, c), 0) >= _iota((c, c), 1)

    for h in range(RET_HEADS):
        chunk_decay = math.exp(RET_LOG_DECAY[h] * c)
        qs = slice(h * RET_QK_DIM, (h + 1) * RET_QK_DIM)
        vs = slice(h * RET_V_DIM, (h + 1) * RET_V_DIM)
        chunks = [slice(ci * c, (ci + 1) * c) for ci in range(SEQ_TILE // c)]
        scores = [jnp.where(causal, _dot_nt(q_ref[rows, qs], k_ref[rows, qs]), 0.0).astype(_BF16) for rows in chunks]
        kvs = [_dot(k_ref[rows, qs].astype(_F32).T.astype(_BF16), v_ref[rows, vs]) for rows in chunks]
        yield
        state = state_ref[h]
        states = []
        for kv in kvs:
            states.append(state.astype(_BF16))
            state = chunk_decay * (state + kv)
        state_ref[h] = state
        rets = [_dot(jnp.concatenate([s, q_ref[rows, qs]], axis=1), jnp.concatenate([v_ref[rows, vs], st], axis=0))
                for rows, s, st in zip(chunks, scores, states)]
        yield
        for rows, ret in zip(chunks, rets):
            mu = jnp.mean(ret, axis=-1, keepdims=True)
            dev = ret - mu
            var = jnp.mean(dev * dev, axis=-1, keepdims=True)
            out_ref[rows, vs] = ((dev * lax.rsqrt(var + NORM_EPS)) * g_ref[rows, vs].astype(_F32)).astype(_BF16)
        yield


def _swa_stages(first, sinks_ref, q_ref, kc_ref, vc_ref, kp_ref, vp_ref, qg_ref, kg_ref, out_ref):
    w = SWA_WINDOW
    d = SWA_HEAD_DIM
    n_blk = SEQ_TILE // w
    pair = 2 * d

    k_both = jnp.concatenate([kp_ref[...], kc_ref[...]], axis=0).astype(_F32)
    same_half = (_iota((pair, pair), 0) // d == _iota((pair, pair), 1) // d).astype(_BF16)
    k_hi, k_lo = _split_bf16(k_both * k_both)
    k_ssq = _dot(k_hi, same_half) + _dot(k_lo, same_half)
    k_n = k_both * lax.rsqrt(k_ssq * (1.0 / d) + NORM_EPS) * (kg_ref[...] * d ** -0.5)
    v_both = jnp.concatenate([vp_ref[...], vc_ref[...]], axis=0).astype(_F32)
    v_t = [v_both[b * w:(b + 1) * w].T for b in range(n_blk + 1)]

    key = _iota((2 * w, SWA_GROUP * w), 0)
    qry = _iota((2 * w, SWA_GROUP * w), 1) % w
    band = (key > qry) & (key <= qry + w)
    band_first = band & jnp.logical_or(key >= w, jnp.logical_not(first))
    head_of_lane = _iota((1, SWA_GROUP * w), 1) // w
    half_of_lane = _iota((1, pair), 1) // d
    half_rows = (_iota((SUBLANES, pair), 0) == _iota((SUBLANES, pair), 1) // d).astype(_BF16)
    ones_rows = jnp.ones((2 * SUBLANES, 2 * w), _BF16)
    q_gain = qg_ref[...]
    yield

    for kk in range(SWA_KV_HEADS):
        k_native = jnp.where(half_of_lane == kk, k_n, 0.0)
        k_moved = pltpu.roll(k_native, d, 1)
        k_even, k_odd = (k_native, k_moved) if kk == 0 else (k_moved, k_native)
        k_even = k_even.astype(_BF16)
        k_odd = k_odd.astype(_BF16)
        sink_row = jnp.zeros((1, SWA_GROUP * w), _F32)
        for g in range(SWA_GROUP):
            sink_row = jnp.where(head_of_lane == g, sinks_ref[kk * SWA_GROUP + g], sink_row)
        blocks = range(n_blk)
        pairs = range(SWA_GROUP // 2)
        q_pairs = [[q_ref[j * w:(j + 1) * w, (kk * SWA_GROUP + 2 * p) * d:(kk * SWA_GROUP + 2 * p + 2) * d].astype(_F32)
                    for p in pairs] for j in blocks]
        q_splits = [[_split_bf16(q * q) for q in qs] for qs in q_pairs]
        q_ssq = [[_dot_nt(half_rows, hi) + _dot_nt(half_rows, lo) for hi, lo in sp] for sp in q_splits]
        q_rs = [[lax.rsqrt(s * (1.0 / d) + NORM_EPS) for s in ss] for ss in q_ssq]
        q_gs = [[(q * q_gain).astype(_BF16) for q in qs] for qs in q_pairs]
        yield
        raw_even = [[_dot_nt(k_even[j * w:(j + 2) * w], q_gs[j][p]) for p in pairs] for j in blocks]
        raw_odd = [[_dot_nt(k_odd[j * w:(j + 2) * w], q_gs[j][p]) for p in pairs] for j in blocks]
        yield
        score_blocks = []
        for j in blocks:
            raws = []
            for p in pairs:
                raws.append(raw_even[j][p] * q_rs[j][p][0:1, :])
                raws.append(raw_odd[j][p] * q_rs[j][p][1:2, :])
            s_t = jnp.concatenate(raws, axis=1)
            score_blocks.append(jnp.where(band_first if j == 0 else band, s_t, NEG_BIG))
        yield
        maxes = [jnp.maximum(jnp.max(s_t, axis=0, keepdims=True), sink_row) for s_t in score_blocks]
        probs = [jnp.exp(s_t - m).astype(_BF16) for s_t, m in zip(score_blocks, maxes)]
        yield
        outs = []
        for j in blocks:
            v_ext = jnp.concatenate([v_t[j][kk * d:(kk + 1) * d], v_t[j + 1][kk * d:(kk + 1) * d]], axis=1)
            outs.append(_dot(jnp.concatenate([v_ext.astype(_BF16), ones_rows], axis=0), probs[j]))
        yield
        for j in blocks:
            rows = slice(j * w, (j + 1) * w)
            denom = outs[j][d:d + 1, :] + jnp.exp(sink_row - maxes[j])
            o_t = outs[j][0:d, :] * (1.0 / denom)
            for p in range(SWA_GROUP // 2):
                c0 = (kk * SWA_GROUP + 2 * p) * d
                both = jnp.concatenate([o_t[:, 2 * p * w:(2 * p + 1) * w], o_t[:, (2 * p + 1) * w:(2 * p + 2) * w]], axis=0)
                out_ref[rows, c0:c0 + pair] = both.T.astype(_BF16)
        yield


def _mixers_kernel(sinks_ref, rq_ref, rk_ref, rv_ref, rg_ref, sq_ref, kc_ref, vc_ref, kp_ref, vp_ref, qg_ref, kg_ref,
                   retg_ref, attn_ref, state_ref):
    first = pl.program_id(1) == 0

    @pl.when(first)
    def _():
        state_ref[...] = jnp.zeros_like(state_ref)

    streams = [_retention_stages(rq_ref, rk_ref, rv_ref, rg_ref, retg_ref, state_ref),
               _swa_stages(first, sinks_ref, sq_ref, kc_ref, vc_ref, kp_ref, vp_ref, qg_ref, kg_ref, attn_ref)]
    while streams:
        streams = [s for s in streams if next(s, StopIteration) is not StopIteration]


def _mixers(ret_proj, swa_proj, sinks, q_gain2, k_gain2, batch, seq):
    n = ret_proj.shape[0]
    steps = seq // SEQ_TILE
    blocks_per_step = SEQ_TILE // SWA_WINDOW
    rowmap = lambda b, s: b * steps + s
    prevmap = lambda b, s: jnp.maximum((b * steps + s) * blocks_per_step - 1, 0)
    k_col = SWA_Q_W // SWA_KV_W
    return pl.pallas_call(
        _mixers_kernel,
        grid=(batch, steps),
        in_specs=[
            pl.BlockSpec(memory_space=pltpu.SMEM),
            pl.BlockSpec((SEQ_TILE, RET_QK_W), lambda b, s: (rowmap(b, s), 0)),
            pl.BlockSpec((SEQ_TILE, RET_QK_W), lambda b, s: (rowmap(b, s), 1)),
            pl.BlockSpec((SEQ_TILE, RET_V_W), lambda b, s: (rowmap(b, s), 1)),
            pl.BlockSpec((SEQ_TILE, RET_V_W), lambda b, s: (rowmap(b, s), 2)),
            pl.BlockSpec((SEQ_TILE, SWA_Q_W), lambda b, s: (rowmap(b, s), 0)),
            pl.BlockSpec((SEQ_TILE, SWA_KV_W), lambda b, s: (rowmap(b, s), k_col)),
            pl.BlockSpec((SEQ_TILE, SWA_KV_W), lambda b, s: (rowmap(b, s), k_col + 1)),
            pl.BlockSpec((SWA_WINDOW, SWA_KV_W), lambda b, s: (prevmap(b, s), k_col)),
            pl.BlockSpec((SWA_WINDOW, SWA_KV_W), lambda b, s: (prevmap(b, s), k_col + 1)),
            pl.BlockSpec((1, 2 * SWA_HEAD_DIM), lambda b, s: (0, 0)),
            pl.BlockSpec((1, 2 * SWA_HEAD_DIM), lambda b, s: (0, 0)),
        ],
        out_specs=[
            pl.BlockSpec((SEQ_TILE, RET_V_W), lambda b, s: (rowmap(b, s), 0)),
            pl.BlockSpec((SEQ_TILE, SWA_Q_W), lambda b, s: (rowmap(b, s), 0)),
        ],
        out_shape=[
            jax.ShapeDtypeStruct((n, RET_V_W), _BF16),
            jax.ShapeDtypeStruct((n, SWA_Q_W), _BF16),
        ],
        scratch_shapes=[pltpu.VMEM((RET_HEADS, RET_QK_DIM, RET_V_DIM), _F32)],
        compiler_params=pltpu.CompilerParams(
            dimension_semantics=("arbitrary", "arbitrary"), vmem_limit_bytes=VMEM_LIMIT),
        name="mixers",
    )(sinks, ret_proj, ret_proj, ret_proj, ret_proj, swa_proj, swa_proj, swa_proj, swa_proj, swa_proj,
      q_gain2, k_gain2)


def _outproj_kernel(x_ref, retg_ref, attn_ref, ga_ref, gb_ref, wro_ref, wso_ref, wout_ref, g2_ref,
                    wrt_ref, brt_ref, x1_ref, h2_ref, info_ref, info_t_ref, counts_ref, logits_ref):
    tm = ROW_TILE
    step = pl.program_id(0)
    cur = step % 2

    @pl.when(step == 0)
    def _():
        logits_ref[...] = jnp.zeros_like(logits_ref)

    logits = logits_ref[1 - cur]
    row = _iota((LANES, tm), 0)
    row_f = row.astype(_F32)
    is_group = (row >= ROUTER_GROUP_LANE) & (row < ROUTER_GROUP_LANE + MOE_GROUPS)

    subs = [slice(s * OUTPROJ_SUB, (s + 1) * OUTPROJ_SUB) for s in range(tm // OUTPROJ_SUB)]
    y_a = [_dot(retg_ref[rows, :], wro_ref[...]) for rows in subs]

    gl = jnp.where(is_group, logits, NEG_BIG)
    g_max = jnp.max(gl, axis=0, keepdims=True)
    g_prob = 1.0 / jnp.sum(jnp.exp(gl - g_max), axis=0, keepdims=True)
    g_idx = jnp.min(jnp.where(gl == g_max, row_f - ROUTER_GROUP_LANE, float(LANES)), axis=0, keepdims=True)

    y_b = [_dot(attn_ref[rows, :], wso_ref[...]) for rows in subs]

    e_lo = g_idx * MOE_EXPERTS_PER_GROUP
    in_group = (row_f >= e_lo) & (row_f < e_lo + MOE_EXPERTS_PER_GROUP)
    el = jnp.where(in_group, logits, NEG_BIG)
    t1 = jnp.max(el, axis=0, keepdims=True)
    i1 = jnp.min(jnp.where(el == t1, row_f, float(LANES)), axis=0, keepdims=True)

    merged = [(ga_ref[rows, :].astype(_F32) * a + gb_ref[rows, :].astype(_F32) * b).astype(_BF16)
              for rows, a, b in zip(subs, y_a, y_b)]
    mixed = [_dot(m, wout_ref[...]) for m in merged]
    x1 = [x_ref[rows, :] + m for rows, m in zip(subs, mixed)]
    for rows, v in zip(subs, x1):
        x1_ref[rows, :] = v

    el2 = jnp.where(row_f == i1, NEG_BIG, el)
    t2 = jnp.max(el2, axis=0, keepdims=True)
    i2 = jnp.min(jnp.where(el2 == t2, row_f, float(LANES)), axis=0, keepdims=True)
    e21 = jnp.exp(t2 - t1)
    w1 = g_prob / (1.0 + e21)
    w2 = g_prob * e21 / (1.0 + e21)

    h2 = [((v * lax.rsqrt(jnp.mean(v * v, axis=-1, keepdims=True) + NORM_EPS)) * g2_ref[...]).astype(_BF16)
          for v in x1]
    for rows, v in zip(subs, h2):
        h2_ref[rows, :] = v

    sel1 = row_f == i1
    sel2 = row_f == i2
    onehot = jnp.where(sel1 | sel2, 1.0, 0.0).astype(_BF16)
    earlier = (_iota((tm, tm), 0) < _iota((tm, tm), 1)).astype(_BF16)
    before = _dot(onehot, earlier)
    counts = _dot(onehot, jnp.ones((tm, LANES), _BF16))

    for rows, v in zip(subs, h2):
        logits_ref[cur, :, rows] = _dot_nt(wrt_ref[...], v) + brt_ref[:, 0:1]

    run = jnp.floor((counts + (RUN_ALIGN - 1.0)) * (1.0 / RUN_ALIGN)) * RUN_ALIGN
    lower_experts = (_iota((LANES, LANES), 0) > _iota((LANES, LANES), 1)).astype(_BF16)
    run_off = _dot(lower_experts, run.astype(_BF16))
    place = before + jnp.concatenate([run_off] * (tm // LANES), axis=1)
    pos1 = jnp.sum(jnp.where(sel1, place, 0.0), axis=0, keepdims=True)
    pos2 = jnp.sum(jnp.where(sel2, place, 0.0), axis=0, keepdims=True)
    counts_ref[...] = counts.T[0:SUBLANES, :]

    info_t = jnp.concatenate([i1, i2, w1, w2, pos1, pos2, jnp.zeros((2, tm), _F32)], axis=0)
    info_t_ref[...] = info_t
    info_ref[...] = jnp.concatenate([info_t, jnp.zeros((LANES - SUBLANES, tm), _F32)], axis=0).T


def _outproj(x2d, retg, attn, gates, w_ret_o, w_swa_o, w_out, g2, w_router_t, b_router_t):
    n = x2d.shape[0]
    tm = ROW_TILE
    n_tiles = n // tm
    const = lambda s: (0, 0)
    proj = lambda s: jnp.minimum(s, n_tiles - 1)
    routed = lambda s: jnp.maximum(s - 1, 0)
    return pl.pallas_call(
        _outproj_kernel,
        grid=(n_tiles + 1,),
        in_specs=[
            pl.BlockSpec((tm, D_MODEL), lambda s: (proj(s), 0)),
            pl.BlockSpec((tm, RET_V_W), lambda s: (proj(s), 0)),
            pl.BlockSpec((tm, SWA_Q_W), lambda s: (proj(s), 0)),
            pl.BlockSpec((tm, D_MODEL), lambda s: (proj(s), 0)),
            pl.BlockSpec((tm, D_MODEL), lambda s: (proj(s), 1)),
            pl.BlockSpec((RET_V_W, D_MODEL), const),
            pl.BlockSpec((SWA_Q_W, D_MODEL), const),
            pl.BlockSpec((D_MODEL, D_MODEL), const),
            pl.BlockSpec((1, D_MODEL), const),
            pl.BlockSpec((LANES, D_MODEL), const),
            pl.BlockSpec((LANES, LANES), const),
        ],
        out_specs=[
            pl.BlockSpec((tm, D_MODEL), lambda s: (proj(s), 0)),
            pl.BlockSpec((tm, D_MODEL), lambda s: (proj(s), 0)),
            pl.BlockSpec((tm, LANES), lambda s: (routed(s), 0)),
            pl.BlockSpec((SUBLANES, tm), lambda s: (0, routed(s))),
            pl.BlockSpec((SUBLANES, LANES), lambda s: (routed(s), 0)),
        ],
        scratch_shapes=[pltpu.VMEM((2, LANES, tm), _F32)],
        out_shape=[
            jax.ShapeDtypeStruct((n, D_MODEL), _F32),
            jax.ShapeDtypeStruct((n, D_MODEL), _BF16),
            jax.ShapeDtypeStruct((n, LANES), _F32),
            jax.ShapeDtypeStruct((SUBLANES, n), _F32),
            jax.ShapeDtypeStruct((n // tm * SUBLANES, LANES), _F32),
        ],
        compiler_params=pltpu.CompilerParams(
            dimension_semantics=("arbitrary",), vmem_limit_bytes=VMEM_LIMIT),
        name="outproj",
    )(x2d, retg, attn, gates, gates, w_ret_o, w_swa_o, w_out, g2, w_router_t, b_router_t)


def _piece_sizes(largest):
    piece = largest
    while piece >= RUN_ALIGN:
        yield piece
        piece //= 2


BIG_CAP = SORT_ROWS // RUN_PIECE
SMALL_CAP = MOE_N_EXPERTS
PIECE_LIST_KEYS = ("n_big", "big_off", "big_dst", "n_small", "small_off", "small_dst")


def _start_run_copies(pieces, tile, make_copy):
    n_big_ref, big_off_ref, big_dst_ref, n_small_ref, small_off_ref, small_dst_ref = pieces

    def big(k, carry):
        i = tile * BIG_CAP + k
        make_copy(pl.multiple_of(big_off_ref[i], RUN_ALIGN), pl.multiple_of(big_dst_ref[i], RUN_ALIGN),
                  RUN_PIECE).start()
        return carry

    lax.fori_loop(0, n_big_ref[tile], big, 0)

    def small(k, carry):
        i = tile * SMALL_CAP + k
        make_copy(pl.multiple_of(small_off_ref[i], RUN_ALIGN), pl.multiple_of(small_dst_ref[i], RUN_ALIGN),
                  RUN_ALIGN).start()
        return carry

    lax.fori_loop(0, n_small_ref[tile], small, 0)


def _wait_rows(rows, make_copy):
    for piece in _piece_sizes(SORT_ROWS_POW2 // 2):
        pl.when((rows & piece) != 0)(make_copy(0, 0, piece).wait)


def _zero_fill_padding(pad_start_ref, pad_len_ref, zeros_ref, xs_hbm, sem, wait):
    def per_expert(e, carry):
        pos = pad_start_ref[e]
        length = pad_len_ref[e]
        for piece in _piece_sizes(EXPERT_TILE // 2):
            copy = pltpu.make_async_copy(
                zeros_ref.at[pl.ds(0, piece)], xs_hbm.at[pl.ds(pl.multiple_of(pos, RUN_ALIGN), piece)], sem)
            pl.when((length & piece) != 0)(copy.wait if wait else copy.start)
            pos = pos + (length & piece)
        return carry

    lax.fori_loop(0, MOE_N_EXPERTS, per_expert, 0)

    tail_start = pad_start_ref[MOE_N_EXPERTS]
    n_rows = xs_hbm.shape[0]
    half = EXPERT_TILE // 2

    def per_half_tile(j, carry):
        pos = tail_start + j * half
        copy = pltpu.make_async_copy(
            zeros_ref, xs_hbm.at[pl.ds(pl.multiple_of(jnp.minimum(pos, n_rows - half), RUN_ALIGN), half)], sem)
        pl.when(pos < n_rows)(copy.wait if wait else copy.start)
        return carry

    lax.fori_loop(0, n_rows // half, per_half_tile, 0)


def _dispatch_kernel(*refs):
    pieces = refs[:len(PIECE_LIST_KEYS)]
    rows_ref, pad_start_ref, pad_len_ref, h2_ref, info_t_ref, xs_hbm, sorted_ref, zeros_ref, sems, zero_sem = (
        refs[len(PIECE_LIST_KEYS):])
    tile = pl.program_id(0)
    last = pl.num_programs(0) - 1
    slot = tile % 2
    tm = ROW_TILE

    @pl.when(tile == 0)
    def _():
        zeros_ref[...] = jnp.zeros_like(zeros_ref)
        _zero_fill_padding(pad_start_ref, pad_len_ref, zeros_ref, xs_hbm, zero_sem, wait=False)
        _zero_fill_padding(pad_start_ref, pad_len_ref, zeros_ref, xs_hbm, zero_sem, wait=True)

    pos = info_t_ref[4:6, :].astype(jnp.int32)

    def sort_rows(lo, hi):
        place = _iota((hi - lo, tm), 0) + lo
        onehot = ((place == pos[0:1, :]) | (place == pos[1:2, :])).astype(_BF16)
        for c in range(0, D_MODEL, 256):
            sorted_ref[slot, lo:hi, c:c + 256] = _dot(onehot, h2_ref[:, c:c + 256]).astype(_BF16)

    sort_rows(0, SORT_COMMON)
    pl.when(rows_ref[tile] > SORT_COMMON)(lambda: sort_rows(SORT_COMMON, SORT_ROWS))

    def copy_maker(t):
        s = t % 2

        def make_copy(off, dst, piece):
            return pltpu.make_async_copy(
                sorted_ref.at[s, pl.ds(off, piece)], xs_hbm.at[pl.ds(dst, piece)], sems.at[s])

        return make_copy

    _start_run_copies(pieces, tile, copy_maker(tile))

    @pl.when(tile > 0)
    def _():
        _wait_rows(rows_ref[tile - 1], copy_maker(tile - 1))

    @pl.when(tile == last)
    def _():
        _wait_rows(rows_ref[tile], copy_maker(tile))


def _dispatch(plan, h2, info_t, n_rows):
    n = h2.shape[0]
    grid_spec = pltpu.PrefetchScalarGridSpec(
        num_scalar_prefetch=len(PIECE_LIST_KEYS) + 3,
        grid=(n // ROW_TILE,),
        in_specs=[
            pl.BlockSpec((ROW_TILE, D_MODEL), lambda i, *_: (i, 0)),
            pl.BlockSpec((SUBLANES, ROW_TILE), lambda i, *_: (0, i)),
        ],
        out_specs=pl.BlockSpec(memory_space=pl.ANY),
        scratch_shapes=[
            pltpu.VMEM((2, SORT_ROWS, D_MODEL), _BF16),
            pltpu.VMEM((EXPERT_TILE // 2, D_MODEL), _BF16),
            pltpu.SemaphoreType.DMA((2,)),
            pltpu.SemaphoreType.DMA(()),
        ],
    )
    return pl.pallas_call(
        _dispatch_kernel,
        grid_spec=grid_spec,
        out_shape=jax.ShapeDtypeStruct((n_rows, D_MODEL), _BF16),
        compiler_params=pltpu.CompilerParams(
            dimension_semantics=("arbitrary",), vmem_limit_bytes=VMEM_LIMIT),
        name="dispatch",
    )(*[plan[k] for k in PIECE_LIST_KEYS], plan["rows"], plan["pad_start"], plan["pad_len"], h2, info_t)


def _experts_kernel(te_ref, nt_ref, xs_ref, wg_ref, wu_ref, wd_ref, ys_ref, wg_s, wu_s, wd_s):
    i = pl.program_id(0)
    in_use = i < nt_ref[0]
    prev = te_ref[jnp.maximum(i - 1, 0)]
    changed = jnp.logical_or(i == 0, te_ref[i] != prev)

    @pl.when(jnp.logical_and(in_use, changed))
    def _():
        wg_s[...] = wg_ref[...].astype(_BF16)
        wu_s[...] = wu_ref[...].astype(_BF16)
        wd_s[...] = wd_ref[...].astype(_BF16)

    @pl.when(in_use)
    def _():
        subs = [slice(s * EXPERT_SUB, (s + 1) * EXPERT_SUB) for s in range(EXPERT_TILE // EXPERT_SUB)]
        gate = [_dot(xs_ref[rows, :], wg_s[...]) for rows in subs]
        up = [_dot(xs_ref[rows, :], wu_s[...]) for rows in subs]
        act = [((g * _sigmoid(g)) * u).astype(_BF16) for g, u in zip(gate, up)]
        for rows, a in zip(subs, act):
            ys_ref[rows, :] = _dot(a, wd_s[...]).astype(_BF16)


def _experts(plan, xs, w_gate, w_up, w_down):
    n_rows = xs.shape[0]
    last_used = lambda i, nt: jnp.minimum(i, nt[0] - 1)
    grid_spec = pltpu.PrefetchScalarGridSpec(
        num_scalar_prefetch=2,
        grid=(n_rows // EXPERT_TILE,),
        in_specs=[
            pl.BlockSpec((EXPERT_TILE, D_MODEL), lambda i, te, nt: (last_used(i, nt), 0)),
            pl.BlockSpec((None, D_MODEL, MOE_D_FF), lambda i, te, nt: (te[last_used(i, nt)], 0, 0)),
            pl.BlockSpec((None, D_MODEL, MOE_D_FF), lambda i, te, nt: (te[last_used(i, nt)], 0, 0)),
            pl.BlockSpec((None, MOE_D_FF, D_MODEL), lambda i, te, nt: (te[last_used(i, nt)], 0, 0)),
        ],
        out_specs=pl.BlockSpec((EXPERT_TILE, D_MODEL), lambda i, te, nt: (last_used(i, nt), 0)),
        scratch_shapes=[
            pltpu.VMEM((D_MODEL, MOE_D_FF), _BF16),
            pltpu.VMEM((D_MODEL, MOE_D_FF), _BF16),
            pltpu.VMEM((MOE_D_FF, D_MODEL), _BF16),
        ],
    )
    return pl.pallas_call(
        _experts_kernel,
        grid_spec=grid_spec,
        out_shape=jax.ShapeDtypeStruct((n_rows, D_MODEL), _BF16),
        input_output_aliases={2: 0},
        compiler_params=pltpu.CompilerParams(
            dimension_semantics=("arbitrary",), vmem_limit_bytes=VMEM_LIMIT),
        name="experts",
    )(plan["tile_expert"], plan["tiles_used"], xs, w_gate, w_up, w_down)


def _combine_kernel(*refs):
    pieces = refs[:len(PIECE_LIST_KEYS)]
    rows_ref, x1_ref, info_ref, ys_hbm, out_ref, block_ref, sems = refs[len(PIECE_LIST_KEYS):]
    tile = pl.program_id(0)
    last = pl.num_programs(0) - 1
    slot = tile % 2
    tm = ROW_TILE

    def copy_maker(t):
        s = t % 2

        def make_copy(off, dst, piece):
            return pltpu.make_async_copy(
                ys_hbm.at[pl.ds(dst, piece)], block_ref.at[s, pl.ds(off, piece)], sems.at[s])

        return make_copy

    @pl.when(tile == 0)
    def _():
        block_ref[...] = jnp.zeros_like(block_ref)
        _start_run_copies(pieces, tile, copy_maker(tile))

    @pl.when(tile < last)
    def _():
        _start_run_copies(pieces, tile + 1, copy_maker(tile + 1))

    _wait_rows(rows_ref[tile], copy_maker(tile))

    info = info_ref[...]

    def weights_of(lo, hi):
        place = (_iota((tm, hi - lo), 1) + lo).astype(_F32)
        return (jnp.where(place == info[:, 4:5], info[:, 2:3], 0.0)
                + jnp.where(place == info[:, 5:6], info[:, 3:4], 0.0)).astype(_BF16)

    weights = weights_of(0, SORT_COMMON)
    for c in range(0, D_MODEL, 256):
        out_ref[:, c:c + 256] = x1_ref[:, c:c + 256] + _dot(weights, block_ref[slot, 0:SORT_COMMON, c:c + 256])

    @pl.when(rows_ref[tile] > SORT_COMMON)
    def _():
        tail = weights_of(SORT_COMMON, SORT_ROWS)
        for c in range(0, D_MODEL, 256):
            out_ref[:, c:c + 256] += _dot(tail, block_ref[slot, SORT_COMMON:SORT_ROWS, c:c + 256])


def _combine(plan, x1, info, ys):
    n = x1.shape[0]
    grid_spec = pltpu.PrefetchScalarGridSpec(
        num_scalar_prefetch=len(PIECE_LIST_KEYS) + 1,
        grid=(n // ROW_TILE,),
        in_specs=[
            pl.BlockSpec((ROW_TILE, D_MODEL), lambda i, *_: (i, 0)),
            pl.BlockSpec((ROW_TILE, LANES), lambda i, *_: (i, 0)),
            pl.BlockSpec(memory_space=pl.ANY),
        ],
        out_specs=pl.BlockSpec((ROW_TILE, D_MODEL), lambda i, *_: (i, 0)),
        scratch_shapes=[pltpu.VMEM((2, SORT_ROWS, D_MODEL), _BF16), pltpu.SemaphoreType.DMA((2,))],
    )
    return pl.pallas_call(
        _combine_kernel,
        grid_spec=grid_spec,
        out_shape=jax.ShapeDtypeStruct((n, D_MODEL), _F32),
        compiler_params=pltpu.CompilerParams(
            dimension_semantics=("arbitrary",), vmem_limit_bytes=VMEM_LIMIT),
        name="combine",
    )(*[plan[k] for k in PIECE_LIST_KEYS], plan["rows"], x1, info, ys)


def _rotary_tables(seq):
    half = RET_QK_DIM // 2
    inv = ROPE_BASE ** (-jnp.arange(half, dtype=_F32) / half)
    hi = (jnp.arange(seq // RET_CHUNK, dtype=jnp.int32) * RET_CHUNK).astype(_F32)[:, None] * inv[None, :]
    lo = jnp.arange(RET_CHUNK, dtype=jnp.int32).astype(_F32)[:, None] * inv[None, :]
    cos_hi, sin_hi = jnp.cos(hi)[:, None, :], jnp.sin(hi)[:, None, :]
    cos_lo, sin_lo = jnp.cos(lo)[None, :, :], jnp.sin(lo)[None, :, :]
    cos = (cos_hi * cos_lo - sin_hi * sin_lo).reshape(seq, half)
    sin = (sin_hi * cos_lo + cos_hi * sin_lo).reshape(seq, half)
    return jnp.concatenate([cos, cos], axis=1), jnp.concatenate([-sin, sin], axis=1)


def _expert_row_bound(n_tokens):
    rows = (2 * n_tokens + (n_tokens // ROW_TILE) * MOE_N_EXPERTS * (RUN_ALIGN - 1)
            + MOE_N_EXPERTS * (EXPERT_TILE - RUN_ALIGN))
    return -(-rows // EXPERT_TILE) * EXPERT_TILE


def _routing_plan(counts_out, n_rows):
    te = EXPERT_TILE
    n_tok_tiles = counts_out.shape[0] // SUBLANES
    counts = counts_out.reshape(n_tok_tiles, SUBLANES, LANES)[:, 0, :MOE_N_EXPERTS].astype(jnp.int32)
    run = (counts + RUN_ALIGN - 1) // RUN_ALIGN * RUN_ALIGN
    off = jnp.cumsum(run, axis=1) - run
    total = jnp.sum(run, axis=0)
    padded = (total + te - 1) // te * te
    pends = jnp.cumsum(padded)
    pstarts = pends - padded
    dst = pstarts[None, :] + jnp.cumsum(run, axis=0) - run
    tile_start = jnp.arange(n_rows // te, dtype=jnp.int32) * te
    tile_expert = jnp.sum((tile_start[:, None] >= pends[None, :]).astype(jnp.int32), axis=1)
    tile_expert = jnp.minimum(tile_expert, MOE_N_EXPERTS - 1)
    i32 = lambda v: v.astype(jnp.int32)

    n_big = run // RUN_PIECE
    big_first = jnp.cumsum(n_big, axis=1) - n_big
    k_big = jnp.arange(BIG_CAP, dtype=jnp.int32)[None, :, None]
    owns_big = (k_big >= big_first[:, None, :]) & (k_big < (big_first + n_big)[:, None, :])
    pick_big = lambda v: jnp.sum(jnp.where(owns_big, (v - RUN_PIECE * big_first)[:, None, :], 0), axis=-1)
    has_small = (run // RUN_ALIGN) % 2
    small_index = jnp.cumsum(has_small, axis=1) - has_small
    k_small = jnp.arange(SMALL_CAP, dtype=jnp.int32)[None, :, None]
    owns_small = (has_small[:, None, :] == 1) & (small_index[:, None, :] == k_small)
    pick_small = lambda v: jnp.sum(jnp.where(owns_small, (v + RUN_PIECE * n_big)[:, None, :], 0), axis=-1)
    return {
        "n_big": i32(jnp.sum(n_big, axis=1)),
        "big_off": i32((pick_big(off) + RUN_PIECE * k_big[..., 0]).reshape(-1)),
        "big_dst": i32((pick_big(dst) + RUN_PIECE * k_big[..., 0]).reshape(-1)),
        "n_small": i32(jnp.sum(has_small, axis=1)),
        "small_off": i32(pick_small(off).reshape(-1)),
        "small_dst": i32(pick_small(dst).reshape(-1)),
        "rows": i32(jnp.sum(run, axis=1)),
        "pad_start": i32(jnp.concatenate([pstarts + total, pends[-1:]])), "pad_len": i32(padded - total),
        "tile_expert": i32(tile_expert), "tiles_used": i32(pends[-1:] // te),
    }


def kernel(x, norm_mix_g, w_in, ret_gn_g, w_ret_o, q_norm_g, k_norm_g, sinks, w_swa_o, w_out, norm_ffn_g,
           w_router_group, b_router_group, w_router_expert, b_router_expert, w_gate, w_up, w_down):
    batch, seq, d = x.shape
    n = batch * seq
    assert d == D_MODEL and seq % SEQ_TILE == 0 and n % ROW_TILE == 0
    cos_tab, sin_tab = _rotary_tables(seq)
    n_rows = _expert_row_bound(n)
    for l in range(w_in.shape[0]):
        x2d = x.reshape(n, d)
        ret_proj, swa_proj, gates = _inproj(x2d, norm_mix_g[l][None, :], w_in[l].astype(_BF16), cos_tab, sin_tab,
                                            ret_gn_g[l].reshape(1, RET_V_W))
        retg, attn = _mixers(ret_proj, swa_proj, sinks[l], jnp.tile(q_norm_g[l], 2)[None, :],
                             jnp.tile(k_norm_g[l], 2)[None, :], batch, seq)
        pad = LANES - MOE_N_EXPERTS - MOE_GROUPS
        w_router_t = jnp.concatenate(
            [w_router_expert[l].T, w_router_group[l].T, jnp.zeros((pad, d), _F32)], axis=0).astype(_BF16)
        b_router = jnp.concatenate([b_router_expert[l], b_router_group[l], jnp.zeros((pad,), _F32)])
        b_router_t = jnp.broadcast_to(b_router[:, None], (LANES, LANES))
        x1, h2, info, info_t, counts = _outproj(
            x2d, retg, attn, gates, w_ret_o[l].astype(_BF16), w_swa_o[l].astype(_BF16), w_out[l].astype(_BF16),
            norm_ffn_g[l][None, :], w_router_t, b_router_t)
        plan = _routing_plan(counts, n_rows)
        xs = _dispatch(plan, h2, info_t, n_rows)
        ys = _experts(plan, xs, w_gate[l], w_up[l], w_down[l])
        x = _combine(plan, x1, info, ys).reshape(batch, seq, d)
    return x
```

```python
import math

import numpy as np
import jax
import jax.numpy as jnp
from jax import lax
from jax.experimental import pallas as pl
from jax.experimental.pallas import tpu as pltpu

D_MODEL = 1024
RET_HEADS = 4
RET_QK_DIM = 128
RET_V_DIM = 256
RET_CHUNK = 128
ROPE_BASE = 10000.0
SWA_Q_HEADS = 8
SWA_KV_HEADS = 2
SWA_HEAD_DIM = 64
SWA_WINDOW = 128
MOE_GROUPS = 4
MOE_EXPERTS_PER_GROUP = 8
MOE_D_FF = 512
NORM_EPS = 1e-6

RET_QK_W = RET_HEADS * RET_QK_DIM
RET_V_W = RET_HEADS * RET_V_DIM
SWA_Q_W = SWA_Q_HEADS * SWA_HEAD_DIM
SWA_KV_W = SWA_KV_HEADS * SWA_HEAD_DIM
SWA_GROUP = SWA_Q_HEADS // SWA_KV_HEADS
MOE_N_EXPERTS = MOE_GROUPS * MOE_EXPERTS_PER_GROUP

RET_COLS = 2 * RET_QK_W + 2 * RET_V_W
SWA_COLS = SWA_Q_W + 2 * SWA_KV_W
GATE_COLS = 2 * D_MODEL
IN_WIDTH = RET_COLS + SWA_COLS + GATE_COLS

LANES = 128
SUBLANES = 8
ROUTER_GROUP_LANE = 32
NEG_BIG = -1e30

ROW_TILE = 512
SEQ_TILE = 512
EXPERT_TILE = 1024
EXPERT_PART = 512
OUTPROJ_SUB = 128
INPROJ_SUB = 256
EXPERT_SUB = 256
RUN_ALIGN = 16
SORT_ROWS = 2 * ROW_TILE + MOE_N_EXPERTS * RUN_ALIGN
SORT_ROWS_POW2 = 1 << SORT_ROWS.bit_length()
SORT_COMMON = 2 * ROW_TILE + MOE_N_EXPERTS * RUN_ALIGN * 9 // 16
RUN_PIECE = 2 * RUN_ALIGN
VMEM_LIMIT = 56 * 1024 * 1024

RET_LOG_DECAY = tuple(float(np.log1p(-np.exp2(-5.0 - h))) for h in range(RET_HEADS))

_BF16 = jnp.bfloat16
_F32 = jnp.float32


def _dot(a, b):
    return jnp.dot(a, b, preferred_element_type=_F32)


def _dot_nt(a, b):
    return lax.dot_general(a, b, (((1,), (1,)), ((), ())), preferred_element_type=_F32)


def _sigmoid(v):
    return 1.0 / (1.0 + jnp.exp(-v))


def _split_bf16(v):
    hi = v.astype(_BF16)
    return hi, (v - hi.astype(_F32)).astype(_BF16)


def _iota(shape, dim):
    return lax.broadcasted_iota(jnp.int32, shape, dim)


def _inproj_kernel(x_ref, g_ref, w_ref, cos_ref, sin_ref, gn_ref, ret_ref, swa_ref, gate_ref):
    subs = [slice(s * INPROJ_SUB, (s + 1) * INPROJ_SUB) for s in range(ROW_TILE // INPROJ_SUB)]

    def normed(rows):
        x = x_ref[rows, :]
        r = lax.rsqrt(jnp.mean(x * x, axis=-1, keepdims=True) + NORM_EPS)
        return ((x * r) * g_ref[...]).astype(_BF16)
    chunk_pos = (_iota((INPROJ_SUB, RET_QK_DIM), 0) % RET_CHUNK + 1).astype(_F32)

    def rotary_decay(base, sign, scale):
        def epilogue(rows, raw):
            cos = cos_ref[rows, :]
            sin = sin_ref[rows, :]
            for hd in range(RET_HEADS):
                v = raw[:, hd * RET_QK_DIM:(hd + 1) * RET_QK_DIM]
                rot = v * cos + pltpu.roll(v, RET_QK_DIM // 2, 1) * sin
                decay = jnp.exp((sign * RET_LOG_DECAY[hd]) * chunk_pos) * scale
                ret_ref[rows, base + hd * RET_QK_DIM:base + (hd + 1) * RET_QK_DIM] = (rot * decay).astype(_BF16)
        return epilogue

    def store(ref, lo, fn):
        def epilogue(rows, raw):
            ref[rows, lo:lo + raw.shape[1]] = fn(raw).astype(_BF16)
        return epilogue

    v_base = 2 * RET_QK_W
    g_base = v_base + RET_V_W
    gate_base = RET_COLS + SWA_COLS
    jobs = [(0, RET_QK_W, rotary_decay(0, 1.0, 1.0)),
            (RET_QK_W, RET_QK_W, rotary_decay(RET_QK_W, -1.0, RET_QK_DIM ** -0.5))]
    jobs += [(v_base + c, 512, store(ret_ref, v_base + c, lambda raw: raw)) for c in range(0, RET_V_W, 512)]
    jobs += [(g_base + c, 512, store(ret_ref, g_base + c,
                                     lambda raw, c=c: (raw * _sigmoid(raw)) * gn_ref[:, c:c + 512]))
             for c in range(0, RET_V_W, 512)]
    jobs += [(RET_COLS + c, 256, store(swa_ref, c, lambda raw: raw)) for c in range(0, SWA_COLS, 256)]
    jobs += [(gate_base + c, 512, store(gate_ref, c, _sigmoid)) for c in range(0, GATE_COLS, 512)]
    hs = []
    for lo, width, epilogue in jobs:
        raws = []
        for s, rows in enumerate(subs):
            if len(hs) <= s:
                hs.append(normed(rows))
            raws.append(_dot(hs[s], w_ref[:, lo:lo + width]))
        for rows, raw in zip(subs, raws):
            epilogue(rows, raw)


def _inproj(x2d, g, w_in_bf16, cos_tab, sin_tab, gn_row):
    n = x2d.shape[0]
    grid = (n // ROW_TILE,)
    seq_tiles = cos_tab.shape[0] // ROW_TILE
    return pl.pallas_call(
        _inproj_kernel,
        grid=grid,
        in_specs=[
            pl.BlockSpec((ROW_TILE, D_MODEL), lambda i: (i, 0)),
            pl.BlockSpec((1, D_MODEL), lambda i: (0, 0)),
            pl.BlockSpec((D_MODEL, IN_WIDTH), lambda i: (0, 0), pipeline_mode=pl.Buffered(1)),
            pl.BlockSpec((ROW_TILE, RET_QK_DIM), lambda i: (i % seq_tiles, 0)),
            pl.BlockSpec((ROW_TILE, RET_QK_DIM), lambda i: (i % seq_tiles, 0)),
            pl.BlockSpec((1, RET_V_W), lambda i: (0, 0)),
        ],
        out_specs=[
            pl.BlockSpec((ROW_TILE, RET_COLS), lambda i: (i, 0)),
            pl.BlockSpec((ROW_TILE, SWA_COLS), lambda i: (i, 0)),
            pl.BlockSpec((ROW_TILE, GATE_COLS), lambda i: (i, 0)),
        ],
        out_shape=[
            jax.ShapeDtypeStruct((n, RET_COLS), _BF16),
            jax.ShapeDtypeStruct((n, SWA_COLS), _BF16),
            jax.ShapeDtypeStruct((n, GATE_COLS), _BF16),
        ],
        compiler_params=pltpu.CompilerParams(
            dimension_semantics=("arbitrary",), vmem_limit_bytes=VMEM_LIMIT),
        name="inproj",
    )(x2d, g, w_in_bf16, cos_tab, sin_tab, gn_row)


def _retention_stages(q_ref, k_ref, v_ref, g_ref, out_ref, state_ref):
    c = RET_CHUNK
    causal = _iota((c, c), 0) >= _iota((c, c), 1)

    for h in range(RET_HEADS):
        chunk_decay = math.exp(RET_LOG_DECAY[h] * c)
        qs = slice(h * RET_QK_DIM, (h + 1) * RET_QK_DIM)
        vs = slice(h * RET_V_DIM, (h + 1) * RET_V_DIM)
        chunks = [slice(ci * c, (ci + 1) * c) for ci in range(SEQ_TILE // c)]
        scores = [jnp.where(causal, _dot_nt(q_ref[rows, qs], k_ref[rows, qs]), 0.0).astype(_BF16) for rows in chunks]
        kvs = [_dot(k_ref[rows, qs].astype(_F32).T.astype(_BF16), v_ref[rows, vs]) for rows in chunks]
        yield
        state = state_ref[h]
        states = []
        for kv in kvs:
            states.append(state.astype(_BF16))
            state = chunk_decay * (state + kv)
        state_ref[h] = state
        rets = [_dot(jnp.concatenate([s, q_ref[rows, qs]], axis=1), jnp.concatenate([v_ref[rows, vs], st], axis=0))
                for rows, s, st in zip(chunks, scores, states)]
        yield
        for rows, ret in zip(chunks, rets):
            mu = jnp.mean(ret, axis=-1, keepdims=True)
            dev = ret - mu
            var = jnp.mean(dev * dev, axis=-1, keepdims=True)
            out_ref[rows, vs] = ((dev * lax.rsqrt(var + NORM_EPS)) * g_ref[rows, vs].astype(_F32)).astype(_BF16)
        yield


def _swa_stages(first, sinks_ref, q_ref, kc_ref, vc_ref, kp_ref, vp_ref, qg_ref, kg_ref, out_ref):
    w = SWA_WINDOW
    d = SWA_HEAD_DIM
    n_blk = SEQ_TILE // w
    pair = 2 * d

    k_both = jnp.concatenate([kp_ref[...], kc_ref[...]], axis=0).astype(_F32)
    same_half = (_iota((pair, pair), 0) // d == _iota((pair, pair), 1) // d).astype(_BF16)
    k_hi, k_lo = _split_bf16(k_both * k_both)
    k_ssq = _dot(k_hi, same_half) + _dot(k_lo, same_half)
    k_n = k_both * lax.rsqrt(k_ssq * (1.0 / d) + NORM_EPS) * (kg_ref[...] * d ** -0.5)
    v_both = jnp.concatenate([vp_ref[...], vc_ref[...]], axis=0).astype(_F32)
    v_t = [v_both[b * w:(b + 1) * w].T for b in range(n_blk + 1)]

    key = _iota((2 * w, SWA_GROUP * w), 0)
    qry = _iota((2 * w, SWA_GROUP * w), 1) % w
    band = (key > qry) & (key <= qry + w)
    band_first = band & jnp.logical_or(key >= w, jnp.logical_not(first))
    head_of_lane = _iota((1, SWA_GROUP * w), 1) // w
    half_of_lane = _iota((1, pair), 1) // d
    half_rows = (_iota((SUBLANES, pair), 0) == _iota((SUBLANES, pair), 1) // d).astype(_BF16)
    ones_rows = jnp.ones((2 * SUBLANES, 2 * w), _BF16)
    q_gain = qg_ref[...]
    yield

    for kk in range(SWA_KV_HEADS):
        k_native = jnp.where(half_of_lane == kk, k_n, 0.0)
        k_moved = pltpu.roll(k_native, d, 1)
        k_even, k_odd = (k_native, k_moved) if kk == 0 else (k_moved, k_native)
        k_even = k_even.astype(_BF16)
        k_odd = k_odd.astype(_BF16)
        sink_row = jnp.zeros((1, SWA_GROUP * w), _F32)
        for g in range(SWA_GROUP):
            sink_row = jnp.where(head_of_lane == g, sinks_ref[kk * SWA_GROUP + g], sink_row)
        blocks = range(n_blk)
        pairs = range(SWA_GROUP // 2)
        q_pairs = [[q_ref[j * w:(j + 1) * w, (kk * SWA_GROUP + 2 * p) * d:(kk * SWA_GROUP + 2 * p + 2) * d].astype(_F32)
                    for p in pairs] for j in blocks]
        q_splits = [[_split_bf16(q * q) for q in qs] for qs in q_pairs]
        q_ssq = [[_dot_nt(half_rows, hi) + _dot_nt(half_rows, lo) for hi, lo in sp] for sp in q_splits]
        q_rs = [[lax.rsqrt(s * (1.0 / d) + NORM_EPS) for s in ss] for ss in q_ssq]
        q_gs = [[(q * q_gain).astype(_BF16) for q in qs] for qs in q_pairs]
        yield
        raw_even = [[_dot_nt(k_even[j * w:(j + 2) * w], q_gs[j][p]) for p in pairs] for j in blocks]
        raw_odd = [[_dot_nt(k_odd[j * w:(j + 2) * w], q_gs[j][p]) for p in pairs] for j in blocks]
        yield
        score_blocks = []
        for j in blocks:
            raws = []
            for p in pairs:
                raws.append(raw_even[j][p] * q_rs[j][p][0:1, :])
                raws.append(raw_odd[j][p] * q_rs[j][p][1:2, :])
            s_t = jnp.concatenate(raws, axis=1)
            score_blocks.append(jnp.where(band_first if j == 0 else band, s_t, NEG_BIG))
        yield
        maxes = [jnp.maximum(jnp.max(s_t, axis=0, keepdims=True), sink_row) for s_t in score_blocks]
        probs = [jnp.exp(s_t - m).astype(_BF16) for s_t, m in zip(score_blocks, maxes)]
        yield
        outs = []
        for j in blocks:
            v_ext = jnp.concatenate([v_t[j][kk * d:(kk + 1) * d], v_t[j + 1][kk * d:(kk + 1) * d]], axis=1)
            outs.append(_dot(jnp.concatenate([v_ext.astype(_BF16), ones_rows], axis=0), probs[j]))
        yield
        for j in blocks:
            rows = slice(j * w, (j + 1) * w)
            denom = outs[j][d:d + 1, :] + jnp.exp(sink_row - maxes[j])
            o_t = outs[j][0:d, :] * (1.0 / denom)
            for p in range(SWA_GROUP // 2):
                c0 = (kk * SWA_GROUP + 2 * p) * d
                both = jnp.concatenate([o_t[:, 2 * p * w:(2 * p + 1) * w], o_t[:, (2 * p + 1) * w:(2 * p + 2) * w]], axis=0)
                out_ref[rows, c0:c0 + pair] = both.T.astype(_BF16)
        yield


def _mixers_kernel(sinks_ref, rq_ref, rk_ref, rv_ref, rg_ref, sq_ref, kc_ref, vc_ref, kp_ref, vp_ref, qg_ref, kg_ref,
                   retg_ref, attn_ref, state_ref):
    first = pl.program_id(1) == 0

    @pl.when(first)
    def _():
        state_ref[...] = jnp.zeros_like(state_ref)

    streams = [_retention_stages(rq_ref, rk_ref, rv_ref, rg_ref, retg_ref, state_ref),
               _swa_stages(first, sinks_ref, sq_ref, kc_ref, vc_ref, kp_ref, vp_ref, qg_ref, kg_ref, attn_ref)]
    while streams:
        streams = [s for s in streams if next(s, StopIteration) is not StopIteration]


def _mixers(ret_proj, swa_proj, sinks, q_gain2, k_gain2, batch, seq):
    n = ret_proj.shape[0]
    steps = seq // SEQ_TILE
    blocks_per_step = SEQ_TILE // SWA_WINDOW
    rowmap = lambda b, s: b * steps + s
    prevmap = lambda b, s: jnp.maximum((b * steps + s) * blocks_per_step - 1, 0)
    k_col = SWA_Q_W // SWA_KV_W
    return pl.pallas_call(
        _mixers_kernel,
        grid=(batch, steps),
        in_specs=[
            pl.BlockSpec(memory_space=pltpu.SMEM),
            pl.BlockSpec((SEQ_TILE, RET_QK_W), lambda b, s: (rowmap(b, s), 0)),
            pl.BlockSpec((SEQ_TILE, RET_QK_W), lambda b, s: (rowmap(b, s), 1)),
            pl.BlockSpec((SEQ_TILE, RET_V_W), lambda b, s: (rowmap(b, s), 1)),
            pl.BlockSpec((SEQ_TILE, RET_V_W), lambda b, s: (rowmap(b, s), 2)),
            pl.BlockSpec((SEQ_TILE, SWA_Q_W), lambda b, s: (rowmap(b, s), 0)),
            pl.BlockSpec((SEQ_TILE, SWA_KV_W), lambda b, s: (rowmap(b, s), k_col)),
            pl.BlockSpec((SEQ_TILE, SWA_KV_W), lambda b, s: (rowmap(b, s), k_col + 1)),
            pl.BlockSpec((SWA_WINDOW, SWA_KV_W), lambda b, s: (prevmap(b, s), k_col)),
            pl.BlockSpec((SWA_WINDOW, SWA_KV_W), lambda b, s: (prevmap(b, s), k_col + 1)),
            pl.BlockSpec((1, 2 * SWA_HEAD_DIM), lambda b, s: (0, 0)),
            pl.BlockSpec((1, 2 * SWA_HEAD_DIM), lambda b, s: (0, 0)),
        ],
        out_specs=[
            pl.BlockSpec((SEQ_TILE, RET_V_W), lambda b, s: (rowmap(b, s), 0)),
            pl.BlockSpec((SEQ_TILE, SWA_Q_W), lambda b, s: (rowmap(b, s), 0)),
        ],
        out_shape=[
            jax.ShapeDtypeStruct((n, RET_V_W), _BF16),
            jax.ShapeDtypeStruct((n, SWA_Q_W), _BF16),
        ],
        scratch_shapes=[pltpu.VMEM((RET_HEADS, RET_QK_DIM, RET_V_DIM), _F32)],
        compiler_params=pltpu.CompilerParams(
            dimension_semantics=("arbitrary", "arbitrary"), vmem_limit_bytes=VMEM_LIMIT),
        name="mixers",
    )(sinks, ret_proj, ret_proj, ret_proj, ret_proj, swa_proj, swa_proj, swa_proj, swa_proj, swa_proj,
      q_gain2, k_gain2)


def _outproj_kernel(x_ref, retg_ref, attn_ref, ga_ref, gb_ref, wro_ref, wso_ref, wout_ref, g2_ref,
                    wrt_ref, brt_ref, x1_ref, h2_ref, info_ref, info_t_ref, counts_ref, logits_ref):
    tm = ROW_TILE
    step = pl.program_id(0)
    cur = step % 2

    @pl.when(step == 0)
    def _():
        logits_ref[...] = jnp.zeros_like(logits_ref)

    logits = logits_ref[1 - cur]
    row = _iota((LANES, tm), 0)
    row_f = row.astype(_F32)
    is_group = (row >= ROUTER_GROUP_LANE) & (row < ROUTER_GROUP_LANE + MOE_GROUPS)

    subs = [slice(s * OUTPROJ_SUB, (s + 1) * OUTPROJ_SUB) for s in range(tm // OUTPROJ_SUB)]
    y_a = [_dot(retg_ref[rows, :], wro_ref[...]) for rows in subs]

    gl = jnp.where(is_group, logits, NEG_BIG)
    g_max = jnp.max(gl, axis=0, keepdims=True)
    g_prob = 1.0 / jnp.sum(jnp.exp(gl - g_max), axis=0, keepdims=True)
    g_idx = jnp.min(jnp.where(gl == g_max, row_f - ROUTER_GROUP_LANE, float(LANES)), axis=0, keepdims=True)

    y_b = [_dot(attn_ref[rows, :], wso_ref[...]) for rows in subs]

    e_lo = g_idx * MOE_EXPERTS_PER_GROUP
    in_group = (row_f >= e_lo) & (row_f < e_lo + MOE_EXPERTS_PER_GROUP)
    el = jnp.where(in_group, logits, NEG_BIG)
    t1 = jnp.max(el, axis=0, keepdims=True)
    i1 = jnp.min(jnp.where(el == t1, row_f, float(LANES)), axis=0, keepdims=True)

    merged = [(ga_ref[rows, :].astype(_F32) * a + gb_ref[rows, :].astype(_F32) * b).astype(_BF16)
              for rows, a, b in zip(subs, y_a, y_b)]
    mixed = [_dot(m, wout_ref[...]) for m in merged]
    x1 = [x_ref[rows, :] + m for rows, m in zip(subs, mixed)]
    for rows, v in zip(subs, x1):
        x1_ref[rows, :] = v

    el2 = jnp.where(row_f == i1, NEG_BIG, el)
    t2 = jnp.max(el2, axis=0, keepdims=True)
    i2 = jnp.min(jnp.where(el2 == t2, row_f, float(LANES)), axis=0, keepdims=True)
    e21 = jnp.exp(t2 - t1)
    w1 = g_prob / (1.0 + e21)
    w2 = g_prob * e21 / (1.0 + e21)

    h2 = [((v * lax.rsqrt(jnp.mean(v * v, axis=-1, keepdims=True) + NORM_EPS)) * g2_ref[...]).astype(_BF16)
          for v in x1]
    for rows, v in zip(subs, h2):
        h2_ref[rows, :] = v

    sel1 = row_f == i1
    sel2 = row_f == i2
    onehot = jnp.where(sel1 | sel2, 1.0, 0.0).astype(_BF16)
    earlier = (_iota((tm, tm), 0) < _iota((tm, tm), 1)).astype(_BF16)
    before = _dot(onehot, earlier)
    counts = _dot(onehot, jnp.ones((tm, LANES), _BF16))

    for rows, v in zip(subs, h2):
        logits_ref[cur, :, rows] = _dot_nt(wrt_ref[...], v) + brt_ref[:, 0:1]

    run = jnp.floor((counts + (RUN_ALIGN - 1.0)) * (1.0 / RUN_ALIGN)) * RUN_ALIGN
    lower_experts = (_iota((LANES, LANES), 0) > _iota((LANES, LANES), 1)).astype(_BF16)
    run_off = _dot(lower_experts, run.astype(_BF16))
    place = before + jnp.concatenate([run_off] * (tm // LANES), axis=1)
    pos1 = jnp.sum(jnp.where(sel1, place, 0.0), axis=0, keepdims=True)
    pos2 = jnp.sum(jnp.where(sel2, place, 0.0), axis=0, keepdims=True)
    counts_ref[...] = counts.T[0:SUBLANES, :]

    info_t = jnp.concatenate([i1, i2, w1, w2, pos1, pos2, jnp.zeros((2, tm), _F32)], axis=0)
    info_t_ref[...] = info_t
    info_ref[...] = jnp.concatenate([info_t, jnp.zeros((LANES - SUBLANES, tm), _F32)], axis=0).T


def _outproj(x2d, retg, attn, gates, w_ret_o, w_swa_o, w_out, g2, w_router_t, b_router_t):
    n = x2d.shape[0]
    tm = ROW_TILE
    n_tiles = n // tm
    const = lambda s: (0, 0)
    proj = lambda s: jnp.minimum(s, n_tiles - 1)
    routed = lambda s: jnp.maximum(s - 1, 0)
    return pl.pallas_call(
        _outproj_kernel,
        grid=(n_tiles + 1,),
        in_specs=[
            pl.BlockSpec((tm, D_MODEL), lambda s: (proj(s), 0)),
            pl.BlockSpec((tm, RET_V_W), lambda s: (proj(s), 0)),
            pl.BlockSpec((tm, SWA_Q_W), lambda s: (proj(s), 0)),
            pl.BlockSpec((tm, D_MODEL), lambda s: (proj(s), 0)),
            pl.BlockSpec((tm, D_MODEL), lambda s: (proj(s), 1)),
            pl.BlockSpec((RET_V_W, D_MODEL), const),
            pl.BlockSpec((SWA_Q_W, D_MODEL), const),
            pl.BlockSpec((D_MODEL, D_MODEL), const),
            pl.BlockSpec((1, D_MODEL), const),
            pl.BlockSpec((LANES, D_MODEL), const),
            pl.BlockSpec((LANES, LANES), const),
        ],
        out_specs=[
            pl.BlockSpec((tm, D_MODEL), lambda s: (proj(s), 0)),
            pl.BlockSpec((tm, D_MODEL), lambda s: (proj(s), 0)),
            pl.BlockSpec((tm, LANES), lambda s: (routed(s), 0)),
            pl.BlockSpec((SUBLANES, tm), lambda s: (0, routed(s))),
            pl.BlockSpec((SUBLANES, LANES), lambda s: (routed(s), 0)),
        ],
        scratch_shapes=[pltpu.VMEM((2, LANES, tm), _F32)],
        out_shape=[
            jax.ShapeDtypeStruct((n, D_MODEL), _F32),
            jax.ShapeDtypeStruct((n, D_MODEL), _BF16),
            jax.ShapeDtypeStruct((n, LANES), _F32),
            jax.ShapeDtypeStruct((SUBLANES, n), _F32),
            jax.ShapeDtypeStruct((n // tm * SUBLANES, LANES), _F32),
        ],
        compiler_params=pltpu.CompilerParams(
            dimension_semantics=("arbitrary",), vmem_limit_bytes=VMEM_LIMIT),
        name="outproj",
    )(x2d, retg, attn, gates, gates, w_ret_o, w_swa_o, w_out, g2, w_router_t, b_router_t)


def _piece_sizes(largest):
    piece = largest
    while piece >= RUN_ALIGN:
        yield piece
        piece //= 2


BIG_CAP = SORT_ROWS // RUN_PIECE
SMALL_CAP = MOE_N_EXPERTS
PIECE_LIST_KEYS = ("n_big", "big_off", "big_dst", "n_small", "small_off", "small_dst")


def _start_run_copies(pieces, tile, make_copy):
    n_big_ref, big_off_ref, big_dst_ref, n_small_ref, small_off_ref, small_dst_ref = pieces

    def big(k, carry):
        i = tile * BIG_CAP + k
        make_copy(pl.multiple_of(big_off_ref[i], RUN_ALIGN), pl.multiple_of(big_dst_ref[i], RUN_ALIGN),
                  RUN_PIECE).start()
        return carry

    lax.fori_loop(0, n_big_ref[tile], big, 0)

    def small(k, carry):
        i = tile * SMALL_CAP + k
        make_copy(pl.multiple_of(small_off_ref[i], RUN_ALIGN), pl.multiple_of(small_dst_ref[i], RUN_ALIGN),
                  RUN_ALIGN).start()
        return carry

    lax.fori_loop(0, n_small_ref[tile], small, 0)


def _wait_rows(rows, make_copy):
    for piece in _piece_sizes(SORT_ROWS_POW2 // 2):
        pl.when((rows & piece) != 0)(make_copy(0, 0, piece).wait)


def _zero_fill_padding(pad_start_ref, pad_len_ref, zeros_ref, xs_hbm, sem, wait):
    def per_expert(e, carry):
        pos = pad_start_ref[e]
        length = pad_len_ref[e]
        for piece in _piece_sizes(EXPERT_TILE // 2):
            copy = pltpu.make_async_copy(
                zeros_ref.at[pl.ds(0, piece)], xs_hbm.at[pl.ds(pl.multiple_of(pos, RUN_ALIGN), piece)], sem)
            pl.when((length & piece) != 0)(copy.wait if wait else copy.start)
            pos = pos + (length & piece)
        return carry

    lax.fori_loop(0, MOE_N_EXPERTS, per_expert, 0)

    tail_start = pad_start_ref[MOE_N_EXPERTS]
    n_rows = xs_hbm.shape[0]
    half = EXPERT_TILE // 2

    def per_half_tile(j, carry):
        pos = tail_start + j * half
        copy = pltpu.make_async_copy(
            zeros_ref, xs_hbm.at[pl.ds(pl.multiple_of(jnp.minimum(pos, n_rows - half), RUN_ALIGN), half)], sem)
        pl.when(pos < n_rows)(copy.wait if wait else copy.start)
        return carry

    lax.fori_loop(0, n_rows // half, per_half_tile, 0)


def _dispatch_kernel(*refs):
    pieces = refs[:len(PIECE_LIST_KEYS)]
    rows_ref, pad_start_ref, pad_len_ref, h2_ref, info_t_ref, xs_hbm, sorted_ref, zeros_ref, sems, zero_sem = (
        refs[len(PIECE_LIST_KEYS):])
    tile = pl.program_id(0)
    last = pl.num_programs(0) - 1
    slot = tile % 2
    tm = ROW_TILE

    @pl.when(tile == 0)
    def _():
        zeros_ref[...] = jnp.zeros_like(zeros_ref)
        _zero_fill_padding(pad_start_ref, pad_len_ref, zeros_ref, xs_hbm, zero_sem, wait=False)
        _zero_fill_padding(pad_start_ref, pad_len_ref, zeros_ref, xs_hbm, zero_sem, wait=True)

    pos = info_t_ref[4:6, :].astype(jnp.int32)

    def sort_rows(lo, hi):
        place = _iota((hi - lo, tm), 0) + lo
        onehot = ((place == pos[0:1, :]) | (place == pos[1:2, :])).astype(_BF16)
        for c in range(0, D_MODEL, 256):
            sorted_ref[slot, lo:hi, c:c + 256] = _dot(onehot, h2_ref[:, c:c + 256]).astype(_BF16)

    sort_rows(0, SORT_COMMON)
    pl.when(rows_ref[tile] > SORT_COMMON)(lambda: sort_rows(SORT_COMMON, SORT_ROWS))

    def copy_maker(t):
        s = t % 2

        def make_copy(off, dst, piece):
            return pltpu.make_async_copy(
                sorted_ref.at[s, pl.ds(off, piece)], xs_hbm.at[pl.ds(dst, piece)], sems.at[s])

        return make_copy

    _start_run_copies(pieces, tile, copy_maker(tile))

    @pl.when(tile > 0)
    def _():
        _wait_rows(rows_ref[tile - 1], copy_maker(tile - 1))

    @pl.when(tile == last)
    def _():
        _wait_rows(rows_ref[tile], copy_maker(tile))


def _dispatch(plan, h2, info_t, n_rows):
    n = h2.shape[0]
    grid_spec = pltpu.PrefetchScalarGridSpec(
        num_scalar_prefetch=len(PIECE_LIST_KEYS) + 3,
        grid=(n // ROW_TILE,),
        in_specs=[
            pl.BlockSpec((ROW_TILE, D_MODEL), lambda i, *_: (i, 0)),
            pl.BlockSpec((SUBLANES, ROW_TILE), lambda i, *_: (0, i)),
        ],
        out_specs=pl.BlockSpec(memory_space=pl.ANY),
        scratch_shapes=[
            pltpu.VMEM((2, SORT_ROWS, D_MODEL), _BF16),
            pltpu.VMEM((EXPERT_TILE // 2, D_MODEL), _BF16),
            pltpu.SemaphoreType.DMA((2,)),
            pltpu.SemaphoreType.DMA(()),
        ],
    )
    return pl.pallas_call(
        _dispatch_kernel,
        grid_spec=grid_spec,
        out_shape=jax.ShapeDtypeStruct((n_rows, D_MODEL), _BF16),
        compiler_params=pltpu.CompilerParams(
            dimension_semantics=("arbitrary",), vmem_limit_bytes=VMEM_LIMIT),
        name="dispatch",
    )(*[plan[k] for k in PIECE_LIST_KEYS], plan["rows"], plan["pad_start"], plan["pad_len"], h2, info_t)


def _experts_kernel(te_ref, nt_ref, tv_ref, xs_ref, wg_ref, wu_ref, wd_ref, ys_ref, wg_s, wu_s, wd_s):
    i = pl.program_id(0)
    in_use = i < nt_ref[0]
    prev = te_ref[jnp.maximum(i - 1, 0)]
    changed = jnp.logical_or(i == 0, te_ref[i] != prev)

    @pl.when(jnp.logical_and(in_use, changed))
    def _():
        wg_s[...] = wg_ref[...].astype(_BF16)
        wu_s[...] = wu_ref[...].astype(_BF16)
        wd_s[...] = wd_ref[...].astype(_BF16)

    def work(part):
        lo = part * EXPERT_PART
        subs = [slice(lo + s * EXPERT_SUB, lo + (s + 1) * EXPERT_SUB) for s in range(EXPERT_PART // EXPERT_SUB)]
        gate = [_dot(xs_ref[rows, :], wg_s[...]) for rows in subs]
        up = [_dot(xs_ref[rows, :], wu_s[...]) for rows in subs]
        act = [((g * _sigmoid(g)) * u).astype(_BF16) for g, u in zip(gate, up)]
        for rows, a in zip(subs, act):
            ys_ref[rows, :] = _dot(a, wd_s[...]).astype(_BF16)

    for part in range(EXPERT_TILE // EXPERT_PART):
        holds_rows = jnp.logical_and(in_use, tv_ref[i] > part * EXPERT_PART)
        pl.when(holds_rows)(lambda part=part: work(part))
        if part > 0:
            @pl.when(jnp.logical_and(in_use, jnp.logical_not(holds_rows)))
            def _(part=part):
                ys_ref[part * EXPERT_PART:(part + 1) * EXPERT_PART, :] = jnp.zeros((EXPERT_PART, D_MODEL), _BF16)


def _experts(plan, xs, w_gate, w_up, w_down):
    n_rows = xs.shape[0]
    last_used = lambda i, nt: jnp.minimum(i, nt[0] - 1)
    grid_spec = pltpu.PrefetchScalarGridSpec(
        num_scalar_prefetch=3,
        grid=(n_rows // EXPERT_TILE,),
        in_specs=[
            pl.BlockSpec((EXPERT_TILE, D_MODEL), lambda i, te, nt, tv: (last_used(i, nt), 0)),
            pl.BlockSpec((None, D_MODEL, MOE_D_FF), lambda i, te, nt, tv: (te[last_used(i, nt)], 0, 0)),
            pl.BlockSpec((None, D_MODEL, MOE_D_FF), lambda i, te, nt, tv: (te[last_used(i, nt)], 0, 0)),
            pl.BlockSpec((None, MOE_D_FF, D_MODEL), lambda i, te, nt, tv: (te[last_used(i, nt)], 0, 0)),
        ],
        out_specs=pl.BlockSpec((EXPERT_TILE, D_MODEL), lambda i, te, nt, tv: (last_used(i, nt), 0)),
        scratch_shapes=[
            pltpu.VMEM((D_MODEL, MOE_D_FF), _BF16),
            pltpu.VMEM((D_MODEL, MOE_D_FF), _BF16),
            pltpu.VMEM((MOE_D_FF, D_MODEL), _BF16),
        ],
    )
    return pl.pallas_call(
        _experts_kernel,
        grid_spec=grid_spec,
        out_shape=jax.ShapeDtypeStruct((n_rows, D_MODEL), _BF16),
        input_output_aliases={3: 0},
        compiler_params=pltpu.CompilerParams(
            dimension_semantics=("arbitrary",), vmem_limit_bytes=VMEM_LIMIT),
        name="experts",
    )(plan["tile_expert"], plan["tiles_used"], plan["tile_valid"], xs, w_gate, w_up, w_down)


def _combine_kernel(*refs):
    pieces = refs[:len(PIECE_LIST_KEYS)]
    rows_ref, x1_ref, info_ref, ys_hbm, out_ref, block_ref, sems = refs[len(PIECE_LIST_KEYS):]
    tile = pl.program_id(0)
    last = pl.num_programs(0) - 1
    slot = tile % 2
    tm = ROW_TILE

    def copy_maker(t):
        s = t % 2

        def make_copy(off, dst, piece):
            return pltpu.make_async_copy(
                ys_hbm.at[pl.ds(dst, piece)], block_ref.at[s, pl.ds(off, piece)], sems.at[s])

        return make_copy

    @pl.when(tile == 0)
    def _():
        block_ref[...] = jnp.zeros_like(block_ref)
        _start_run_copies(pieces, tile, copy_maker(tile))

    @pl.when(tile < last)
    def _():
        _start_run_copies(pieces, tile + 1, copy_maker(tile + 1))

    _wait_rows(rows_ref[tile], copy_maker(tile))

    info = info_ref[...]

    def weights_of(lo, hi):
        place = (_iota((tm, hi - lo), 1) + lo).astype(_F32)
        return (jnp.where(place == info[:, 4:5], info[:, 2:3], 0.0)
                + jnp.where(place == info[:, 5:6], info[:, 3:4], 0.0)).astype(_BF16)

    weights = weights_of(0, SORT_COMMON)
    for c in range(0, D_MODEL, 256):
        out_ref[:, c:c + 256] = x1_ref[:, c:c + 256] + _dot(weights, block_ref[slot, 0:SORT_COMMON, c:c + 256])

    @pl.when(rows_ref[tile] > SORT_COMMON)
    def _():
        tail = weights_of(SORT_COMMON, SORT_ROWS)
        for c in range(0, D_MODEL, 256):
            out_ref[:, c:c + 256] += _dot(tail, block_ref[slot, SORT_COMMON:SORT_ROWS, c:c + 256])


def _combine(plan, x1, info, ys):
    n = x1.shape[0]
    grid_spec = pltpu.PrefetchScalarGridSpec(
        num_scalar_prefetch=len(PIECE_LIST_KEYS) + 1,
        grid=(n // ROW_TILE,),
        in_specs=[
            pl.BlockSpec((ROW_TILE, D_MODEL), lambda i, *_: (i, 0)),
            pl.BlockSpec((ROW_TILE, LANES), lambda i, *_: (i, 0)),
            pl.BlockSpec(memory_space=pl.ANY),
        ],
        out_specs=pl.BlockSpec((ROW_TILE, D_MODEL), lambda i, *_: (i, 0)),
        scratch_shapes=[pltpu.VMEM((2, SORT_ROWS, D_MODEL), _BF16), pltpu.SemaphoreType.DMA((2,))],
    )
    return pl.pallas_call(
        _combine_kernel,
        grid_spec=grid_spec,
        out_shape=jax.ShapeDtypeStruct((n, D_MODEL), _F32),
        compiler_params=pltpu.CompilerParams(
            dimension_semantics=("arbitrary",), vmem_limit_bytes=VMEM_LIMIT),
        name="combine",
    )(*[plan[k] for k in PIECE_LIST_KEYS], plan["rows"], x1, info, ys)


def _rotary_tables(seq):
    half = RET_QK_DIM // 2
    inv = ROPE_BASE ** (-jnp.arange(half, dtype=_F32) / half)
    hi = (jnp.arange(seq // RET_CHUNK, dtype=jnp.int32) * RET_CHUNK).astype(_F32)[:, None] * inv[None, :]
    lo = jnp.arange(RET_CHUNK, dtype=jnp.int32).astype(_F32)[:, None] * inv[None, :]
    cos_hi, sin_hi = jnp.cos(hi)[:, None, :], jnp.sin(hi)[:, None, :]
    cos_lo, sin_lo = jnp.cos(lo)[None, :, :], jnp.sin(lo)[None, :, :]
    cos = (cos_hi * cos_lo - sin_hi * sin_lo).reshape(seq, half)
    sin = (sin_hi * cos_lo + cos_hi * sin_lo).reshape(seq, half)
    return jnp.concatenate([cos, cos], axis=1), jnp.concatenate([-sin, sin], axis=1)


def _expert_row_bound(n_tokens):
    rows = (2 * n_tokens + (n_tokens // ROW_TILE) * MOE_N_EXPERTS * (RUN_ALIGN - 1)
            + MOE_N_EXPERTS * (EXPERT_TILE - RUN_ALIGN))
    return -(-rows // EXPERT_TILE) * EXPERT_TILE


def _routing_plan(counts_out, n_rows):
    te = EXPERT_TILE
    n_tok_tiles = counts_out.shape[0] // SUBLANES
    counts = counts_out.reshape(n_tok_tiles, SUBLANES, LANES)[:, 0, :MOE_N_EXPERTS].astype(jnp.int32)
    run = (counts + RUN_ALIGN - 1) // RUN_ALIGN * RUN_ALIGN
    off = jnp.cumsum(run, axis=1) - run
    total = jnp.sum(run, axis=0)
    padded = (total + te - 1) // te * te
    pends = jnp.cumsum(padded)
    pstarts = pends - padded
    dst = pstarts[None, :] + jnp.cumsum(run, axis=0) - run
    tile_start = jnp.arange(n_rows // te, dtype=jnp.int32) * te
    tile_expert = jnp.sum((tile_start[:, None] >= pends[None, :]).astype(jnp.int32), axis=1)
    tile_expert = jnp.minimum(tile_expert, MOE_N_EXPERTS - 1)
    of_tile = tile_expert[:, None] == jnp.arange(MOE_N_EXPERTS, dtype=jnp.int32)[None, :]
    tile_end = jnp.sum(jnp.where(of_tile, (pstarts + total)[None, :], 0), axis=1)
    tile_valid = jnp.clip(tile_end - tile_start, 0, te)
    i32 = lambda v: v.astype(jnp.int32)

    n_big = run // RUN_PIECE
    big_first = jnp.cumsum(n_big, axis=1) - n_big
    k_big = jnp.arange(BIG_CAP, dtype=jnp.int32)[None, :, None]
    owns_big = (k_big >= big_first[:, None, :]) & (k_big < (big_first + n_big)[:, None, :])
    pick_big = lambda v: jnp.sum(jnp.where(owns_big, (v - RUN_PIECE * big_first)[:, None, :], 0), axis=-1)
    has_small = (run // RUN_ALIGN) % 2
    small_index = jnp.cumsum(has_small, axis=1) - has_small
    k_small = jnp.arange(SMALL_CAP, dtype=jnp.int32)[None, :, None]
    owns_small = (has_small[:, None, :] == 1) & (small_index[:, None, :] == k_small)
    pick_small = lambda v: jnp.sum(jnp.where(owns_small, (v + RUN_PIECE * n_big)[:, None, :], 0), axis=-1)
    return {
        "n_big": i32(jnp.sum(n_big, axis=1)),
        "big_off": i32((pick_big(off) + RUN_PIECE * k_big[..., 0]).reshape(-1)),
        "big_dst": i32((pick_big(dst) + RUN_PIECE * k_big[..., 0]).reshape(-1)),
        "n_small": i32(jnp.sum(has_small, axis=1)),
        "small_off": i32(pick_small(off).reshape(-1)),
        "small_dst": i32(pick_small(dst).reshape(-1)),
        "rows": i32(jnp.sum(run, axis=1)),
        "pad_start": i32(jnp.concatenate([pstarts + total, pends[-1:]])), "pad_len": i32(padded - total),
        "tile_expert": i32(tile_expert), "tiles_used": i32(pends[-1:] // te), "tile_valid": i32(tile_valid),
    }


def kernel(x, norm_mix_g, w_in, ret_gn_g, w_ret_o, q_norm_g, k_norm_g, sinks, w_swa_o, w_out, norm_ffn_g,
           w_router_group, b_router_group, w_router_expert, b_router_expert, w_gate, w_up, w_down):
    batch, seq, d = x.shape
    n = batch * seq
    assert d == D_MODEL and seq % SEQ_TILE == 0 and n % ROW_TILE == 0
    cos_tab, sin_tab = _rotary_tables(seq)
    n_rows = _expert_row_bound(n)
    for l in range(w_in.shape[0]):
        x2d = x.reshape(n, d)
        ret_proj, swa_proj, gates = _inproj(x2d, norm_mix_g[l][None, :], w_in[l].astype(_BF16), cos_tab, sin_tab,
                                            ret_gn_g[l].reshape(1, RET_V_W))
        retg, attn = _mixers(ret_proj, swa_proj, sinks[l], jnp.tile(q_norm_g[l], 2)[None, :],
                             jnp.tile(k_norm_g[l], 2)[None, :], batch, seq)
        pad = LANES - MOE_N_EXPERTS - MOE_GROUPS
        w_router_t = jnp.concatenate(
            [w_router_expert[l].T, w_router_group[l].T, jnp.zeros((pad, d), _F32)], axis=0).astype(_BF16)
        b_router = jnp.concatenate([b_router_expert[l], b_router_group[l], jnp.zeros((pad,), _F32)])
        b_router_t = jnp.broadcast_to(b_router[:, None], (LANES, LANES))
        x1, h2, info, info_t, counts = _outproj(
            x2d, retg, attn, gates, w_ret_o[l].astype(_BF16), w_swa_o[l].astype(_BF16), w_out[l].astype(_BF16),
            norm_ffn_g[l][None, :], w_router_t, b_router_t)
        plan = _routing_plan(counts, n_rows)
        xs = _dispatch(plan, h2, info_t, n_rows)
        ys = _experts(plan, xs, w_gate[l], w_up[l], w_down[l])
        x = _combine(plan, x1, info, ys).reshape(batch, seq, d)
    return x
```

```python
import math

import numpy as np
import jax
import jax.numpy as jnp
from jax import lax
from jax.experimental import pallas as pl
from jax.experimental.pallas import tpu as pltpu

D_MODEL = 1024
RET_HEADS = 4
RET_QK_DIM = 128
RET_V_DIM = 256
RET_CHUNK = 128
ROPE_BASE = 10000.0
SWA_Q_HEADS = 8
SWA_KV_HEADS = 2
SWA_HEAD_DIM = 64
SWA_WINDOW = 128
MOE_GROUPS = 4
MOE_EXPERTS_PER_GROUP = 8
MOE_D_FF = 512
NORM_EPS = 1e-6

RET_QK_W = RET_HEADS * RET_QK_DIM
RET_V_W = RET_HEADS * RET_V_DIM
SWA_Q_W = SWA_Q_HEADS * SWA_HEAD_DIM
SWA_KV_W = SWA_KV_HEADS * SWA_HEAD_DIM
SWA_GROUP = SWA_Q_HEADS // SWA_KV_HEADS
MOE_N_EXPERTS = MOE_GROUPS * MOE_EXPERTS_PER_GROUP

RET_COLS = 2 * RET_QK_W + 2 * RET_V_W
SWA_COLS = SWA_Q_W + 2 * SWA_KV_W
GATE_COLS = 2 * D_MODEL
IN_WIDTH = RET_COLS + SWA_COLS + GATE_COLS

LANES = 128
SUBLANES = 8
ROUTER_GROUP_LANE = 32
NEG_BIG = -1e30

ROW_TILE = 512
SEQ_TILE = 512
EXPERT_TILE = 512
OUTPROJ_SUB = 128
INPROJ_SUB = 256
EXPERT_SUB = 256
RUN_ALIGN = 16
SORT_ROWS = 2 * ROW_TILE + MOE_N_EXPERTS * RUN_ALIGN
SORT_ROWS_POW2 = 1 << SORT_ROWS.bit_length()
SORT_COMMON = 2 * ROW_TILE + MOE_N_EXPERTS * RUN_ALIGN * 9 // 16
RUN_PIECE = 2 * RUN_ALIGN
VMEM_LIMIT = 56 * 1024 * 1024

RET_LOG_DECAY = tuple(float(np.log1p(-np.exp2(-5.0 - h))) for h in range(RET_HEADS))

_BF16 = jnp.bfloat16
_F32 = jnp.float32


def _dot(a, b):
    return jnp.dot(a, b, preferred_element_type=_F32)


def _dot_nt(a, b):
    return lax.dot_general(a, b, (((1,), (1,)), ((), ())), preferred_element_type=_F32)


def _sigmoid(v):
    return 1.0 / (1.0 + jnp.exp(-v))


def _split_bf16(v):
    hi = v.astype(_BF16)
    return hi, (v - hi.astype(_F32)).astype(_BF16)


def _iota(shape, dim):
    return lax.broadcasted_iota(jnp.int32, shape, dim)


def _inproj_kernel(x_ref, g_ref, w_ref, cos_ref, sin_ref, gn_ref, ret_ref, swa_ref, gate_ref):
    subs = [slice(s * INPROJ_SUB, (s + 1) * INPROJ_SUB) for s in range(ROW_TILE // INPROJ_SUB)]

    def normed(rows):
        x = x_ref[rows, :]
        r = lax.rsqrt(jnp.mean(x * x, axis=-1, keepdims=True) + NORM_EPS)
        return ((x * r) * g_ref[...]).astype(_BF16)
    chunk_pos = (_iota((INPROJ_SUB, RET_QK_DIM), 0) % RET_CHUNK + 1).astype(_F32)

    def rotary_decay(base, sign, scale):
        def epilogue(rows, raw):
            cos = cos_ref[rows, :]
            sin = sin_ref[rows, :]
            for hd in range(RET_HEADS):
                v = raw[:, hd * RET_QK_DIM:(hd + 1) * RET_QK_DIM]
                rot = v * cos + pltpu.roll(v, RET_QK_DIM // 2, 1) * sin
                decay = jnp.exp((sign * RET_LOG_DECAY[hd]) * chunk_pos) * scale
                ret_ref[rows, base + hd * RET_QK_DIM:base + (hd + 1) * RET_QK_DIM] = (rot * decay).astype(_BF16)
        return epilogue

    def store(ref, lo, fn):
        def epilogue(rows, raw):
            ref[rows, lo:lo + raw.shape[1]] = fn(raw).astype(_BF16)
        return epilogue

    v_base = 2 * RET_QK_W
    g_base = v_base + RET_V_W
    gate_base = RET_COLS + SWA_COLS
    jobs = [(0, RET_QK_W, rotary_decay(0, 1.0, 1.0)),
            (RET_QK_W, RET_QK_W, rotary_decay(RET_QK_W, -1.0, RET_QK_DIM ** -0.5))]
    jobs += [(v_base + c, 512, store(ret_ref, v_base + c, lambda raw: raw)) for c in range(0, RET_V_W, 512)]
    jobs += [(g_base + c, 512, store(ret_ref, g_base + c,
                                     lambda raw, c=c: (raw * _sigmoid(raw)) * gn_ref[:, c:c + 512]))
             for c in range(0, RET_V_W, 512)]
    jobs += [(RET_COLS + c, 256, store(swa_ref, c, lambda raw: raw)) for c in range(0, SWA_COLS, 256)]
    jobs += [(gate_base + c, 512, store(gate_ref, c, _sigmoid)) for c in range(0, GATE_COLS, 512)]
    hs = []
    for lo, width, epilogue in jobs:
        raws = []
        for s, rows in enumerate(subs):
            if len(hs) <= s:
                hs.append(normed(rows))
            raws.append(_dot(hs[s], w_ref[:, lo:lo + width]))
        for rows, raw in zip(subs, raws):
            epilogue(rows, raw)


def _inproj(x2d, g, w_in_bf16, cos_tab, sin_tab, gn_row):
    n = x2d.shape[0]
    grid = (n // ROW_TILE,)
    seq_tiles = cos_tab.shape[0] // ROW_TILE
    return pl.pallas_call(
        _inproj_kernel,
        grid=grid,
        in_specs=[
            pl.BlockSpec((ROW_TILE, D_MODEL), lambda i: (i, 0)),
            pl.BlockSpec((1, D_MODEL), lambda i: (0, 0)),
            pl.BlockSpec((D_MODEL, IN_WIDTH), lambda i: (0, 0), pipeline_mode=pl.Buffered(1)),
            pl.BlockSpec((ROW_TILE, RET_QK_DIM), lambda i: (i % seq_tiles, 0)),
            pl.BlockSpec((ROW_TILE, RET_QK_DIM), lambda i: (i % seq_tiles, 0)),
            pl.BlockSpec((1, RET_V_W), lambda i: (0, 0)),
        ],
        out_specs=[
            pl.BlockSpec((ROW_TILE, RET_COLS), lambda i: (i, 0)),
            pl.BlockSpec((ROW_TILE, SWA_COLS), lambda i: (i, 0)),
            pl.BlockSpec((ROW_TILE, GATE_COLS), lambda i: (i, 0)),
        ],
        out_shape=[
            jax.ShapeDtypeStruct((n, RET_COLS), _BF16),
            jax.ShapeDtypeStruct((n, SWA_COLS), _BF16),
            jax.ShapeDtypeStruct((n, GATE_COLS), _BF16),
        ],
        compiler_params=pltpu.CompilerParams(
            dimension_semantics=("arbitrary",), vmem_limit_bytes=VMEM_LIMIT),
        name="inproj",
    )(x2d, g, w_in_bf16, cos_tab, sin_tab, gn_row)


def _retention_stages(q_ref, k_ref, v_ref, g_ref, out_ref, state_ref):
    c = RET_CHUNK
    causal = _iota((c, c), 0) >= _iota((c, c), 1)

    for h in range(RET_HEADS):
        chunk_decay = math.exp(RET_LOG_DECAY[h] * c)
        qs = slice(h * RET_QK_DIM, (h + 1) * RET_QK_DIM)
        vs = slice(h * RET_V_DIM, (h + 1) * RET_V_DIM)
        chunks = [slice(ci * c, (ci + 1) * c) for ci in range(SEQ_TILE // c)]
        scores = [jnp.where(causal, _dot_nt(q_ref[rows, qs], k_ref[rows, qs]), 0.0).astype(_BF16) for rows in chunks]
        kvs = [_dot(k_ref[rows, qs].astype(_F32).T.astype(_BF16), v_ref[rows, vs]) for rows in chunks]
        yield
        state = state_ref[h]
        states = []
        for kv in kvs:
            states.append(state.astype(_BF16))
            state = chunk_decay * (state + kv)
        state_ref[h] = state
        rets = [_dot(jnp.concatenate([s, q_ref[rows, qs]], axis=1), jnp.concatenate([v_ref[rows, vs], st], axis=0))
                for rows, s, st in zip(chunks, scores, states)]
        yield
        for rows, ret in zip(chunks, rets):
            mu = jnp.mean(ret, axis=-1, keepdims=True)
            dev = ret - mu
            var = jnp.mean(dev * dev, axis=-1, keepdims=True)
            out_ref[rows, vs] = ((dev * lax.rsqrt(var + NORM_EPS)) * g_ref[rows, vs].astype(_F32)).astype(_BF16)
        yield


def _swa_stages(first, sinks_ref, q_ref, kc_ref, vc_ref, kp_ref, vp_ref, qg_ref, kg_ref, out_ref):
    w = SWA_WINDOW
    d = SWA_HEAD_DIM
    n_blk = SEQ_TILE // w
    pair = 2 * d

    k_both = jnp.concatenate([kp_ref[...], kc_ref[...]], axis=0).astype(_F32)
    same_half = (_iota((pair, pair), 0) // d == _iota((pair, pair), 1) // d).astype(_BF16)
    k_hi, k_lo = _split_bf16(k_both * k_both)
    k_ssq = _dot(k_hi, same_half) + _dot(k_lo, same_half)
    k_n = k_both * lax.rsqrt(k_ssq * (1.0 / d) + NORM_EPS) * (kg_ref[...] * d ** -0.5)
    v_both = jnp.concatenate([vp_ref[...], vc_ref[...]], axis=0).astype(_F32)
    v_t = [v_both[b * w:(b + 1) * w].T for b in range(n_blk + 1)]

    key = _iota((2 * w, SWA_GROUP * w), 0)
    qry = _iota((2 * w, SWA_GROUP * w), 1) % w
    band = (key > qry) & (key <= qry + w)
    band_first = band & jnp.logical_or(key >= w, jnp.logical_not(first))
    head_of_lane = _iota((1, SWA_GROUP * w), 1) // w
    half_of_lane = _iota((1, pair), 1) // d
    half_rows = (_iota((SUBLANES, pair), 0) == _iota((SUBLANES, pair), 1) // d).astype(_BF16)
    ones_rows = jnp.ones((2 * SUBLANES, 2 * w), _BF16)
    q_gain = qg_ref[...]
    yield

    for kk in range(SWA_KV_HEADS):
        k_native = jnp.where(half_of_lane == kk, k_n, 0.0)
        k_moved = pltpu.roll(k_native, d, 1)
        k_even, k_odd = (k_native, k_moved) if kk == 0 else (k_moved, k_native)
        k_even = k_even.astype(_BF16)
        k_odd = k_odd.astype(_BF16)
        sink_row = jnp.zeros((1, SWA_GROUP * w), _F32)
        for g in range(SWA_GROUP):
            sink_row = jnp.where(head_of_lane == g, sinks_ref[kk * SWA_GROUP + g], sink_row)
        blocks = range(n_blk)
        pairs = range(SWA_GROUP // 2)
        q_pairs = [[q_ref[j * w:(j + 1) * w, (kk * SWA_GROUP + 2 * p) * d:(kk * SWA_GROUP + 2 * p + 2) * d].astype(_F32)
                    for p in pairs] for j in blocks]
        q_splits = [[_split_bf16(q * q) for q in qs] for qs in q_pairs]
        q_ssq = [[_dot_nt(half_rows, hi) + _dot_nt(half_rows, lo) for hi, lo in sp] for sp in q_splits]
        q_rs = [[lax.rsqrt(s * (1.0 / d) + NORM_EPS) for s in ss] for ss in q_ssq]
        q_gs = [[(q * q_gain).astype(_BF16) for q in qs] for qs in q_pairs]
        yield
        raw_even = [[_dot_nt(k_even[j * w:(j + 2) * w], q_gs[j][p]) for p in pairs] for j in blocks]
        raw_odd = [[_dot_nt(k_odd[j * w:(j + 2) * w], q_gs[j][p]) for p in pairs] for j in blocks]
        yield
        score_blocks = []
        for j in blocks:
            raws = []
            for p in pairs:
                raws.append(raw_even[j][p] * q_rs[j][p][0:1, :])
                raws.append(raw_odd[j][p] * q_rs[j][p][1:2, :])
            s_t = jnp.concatenate(raws, axis=1)
            score_blocks.append(jnp.where(band_first if j == 0 else band, s_t, NEG_BIG))
        yield
        maxes = [jnp.maximum(jnp.max(s_t, axis=0, keepdims=True), sink_row) for s_t in score_blocks]
        probs = [jnp.exp(s_t - m).astype(_BF16) for s_t, m in zip(score_blocks, maxes)]
        yield
        outs = []
        for j in blocks:
            v_ext = jnp.concatenate([v_t[j][kk * d:(kk + 1) * d], v_t[j + 1][kk * d:(kk + 1) * d]], axis=1)
            outs.append(_dot(jnp.concatenate([v_ext.astype(_BF16), ones_rows], axis=0), probs[j]))
        yield
        for j in blocks:
            rows = slice(j * w, (j + 1) * w)
            denom = outs[j][d:d + 1, :] + jnp.exp(sink_row - maxes[j])
            o_t = outs[j][0:d, :] * (1.0 / denom)
            for p in range(SWA_GROUP // 2):
                c0 = (kk * SWA_GROUP + 2 * p) * d
                both = jnp.concatenate([o_t[:, 2 * p * w:(2 * p + 1) * w], o_t[:, (2 * p + 1) * w:(2 * p + 2) * w]], axis=0)
                out_ref[rows, c0:c0 + pair] = both.T.astype(_BF16)
        yield


def _mixers_kernel(sinks_ref, rq_ref, rk_ref, rv_ref, rg_ref, sq_ref, kc_ref, vc_ref, kp_ref, vp_ref, qg_ref, kg_ref,
                   retg_ref, attn_ref, state_ref):
    first = pl.program_id(1) == 0

    @pl.when(first)
    def _():
        state_ref[...] = jnp.zeros_like(state_ref)

    streams = [_retention_stages(rq_ref, rk_ref, rv_ref, rg_ref, retg_ref, state_ref),
               _swa_stages(first, sinks_ref, sq_ref, kc_ref, vc_ref, kp_ref, vp_ref, qg_ref, kg_ref, attn_ref)]
    while streams:
        streams = [s for s in streams if next(s, StopIteration) is not StopIteration]


def _mixers(ret_proj, swa_proj, sinks, q_gain2, k_gain2, batch, seq):
    n = ret_proj.shape[0]
    steps = seq // SEQ_TILE
    blocks_per_step = SEQ_TILE // SWA_WINDOW
    rowmap = lambda b, s: b * steps + s
    prevmap = lambda b, s: jnp.maximum((b * steps + s) * blocks_per_step - 1, 0)
    k_col = SWA_Q_W // SWA_KV_W
    return pl.pallas_call(
        _mixers_kernel,
        grid=(batch, steps),
        in_specs=[
            pl.BlockSpec(memory_space=pltpu.SMEM),
            pl.BlockSpec((SEQ_TILE, RET_QK_W), lambda b, s: (rowmap(b, s), 0)),
            pl.BlockSpec((SEQ_TILE, RET_QK_W), lambda b, s: (rowmap(b, s), 1)),
            pl.BlockSpec((SEQ_TILE, RET_V_W), lambda b, s: (rowmap(b, s), 1)),
            pl.BlockSpec((SEQ_TILE, RET_V_W), lambda b, s: (rowmap(b, s), 2)),
            pl.BlockSpec((SEQ_TILE, SWA_Q_W), lambda b, s: (rowmap(b, s), 0)),
            pl.BlockSpec((SEQ_TILE, SWA_KV_W), lambda b, s: (rowmap(b, s), k_col)),
            pl.BlockSpec((SEQ_TILE, SWA_KV_W), lambda b, s: (rowmap(b, s), k_col + 1)),
            pl.BlockSpec((SWA_WINDOW, SWA_KV_W), lambda b, s: (prevmap(b, s), k_col)),
            pl.BlockSpec((SWA_WINDOW, SWA_KV_W), lambda b, s: (prevmap(b, s), k_col + 1)),
            pl.BlockSpec((1, 2 * SWA_HEAD_DIM), lambda b, s: (0, 0)),
            pl.BlockSpec((1, 2 * SWA_HEAD_DIM), lambda b, s: (0, 0)),
        ],
        out_specs=[
            pl.BlockSpec((SEQ_TILE, RET_V_W), lambda b, s: (rowmap(b, s), 0)),
            pl.BlockSpec((SEQ_TILE, SWA_Q_W), lambda b, s: (rowmap(b, s), 0)),
        ],
        out_shape=[
            jax.ShapeDtypeStruct((n, RET_V_W), _BF16),
            jax.ShapeDtypeStruct((n, SWA_Q_W), _BF16),
        ],
        scratch_shapes=[pltpu.VMEM((RET_HEADS, RET_QK_DIM, RET_V_DIM), _F32)],
        compiler_params=pltpu.CompilerParams(
            dimension_semantics=("arbitrary", "arbitrary"), vmem_limit_bytes=VMEM_LIMIT),
        name="mixers",
    )(sinks, ret_proj, ret_proj, ret_proj, ret_proj, swa_proj, swa_proj, swa_proj, swa_proj, swa_proj,
      q_gain2, k_gain2)


def _outproj_kernel(x_ref, retg_ref, attn_ref, ga_ref, gb_ref, wro_ref, wso_ref, wout_ref, g2_ref,
                    wrt_ref, brt_ref, x1_ref, h2_ref, info_ref, info_t_ref, counts_ref, logits_ref):
    tm = ROW_TILE
    step = pl.program_id(0)
    cur = step % 2

    @pl.when(step == 0)
    def _():
        logits_ref[...] = jnp.zeros_like(logits_ref)

    logits = logits_ref[1 - cur]
    row = _iota((LANES, tm), 0)
    row_f = row.astype(_F32)
    is_group = (row >= ROUTER_GROUP_LANE) & (row < ROUTER_GROUP_LANE + MOE_GROUPS)

    subs = [slice(s * OUTPROJ_SUB, (s + 1) * OUTPROJ_SUB) for s in range(tm // OUTPROJ_SUB)]
    y_a = [_dot(retg_ref[rows, :], wro_ref[...]) for rows in subs]

    gl = jnp.where(is_group, logits, NEG_BIG)
    g_max = jnp.max(gl, axis=0, keepdims=True)
    g_prob = 1.0 / jnp.sum(jnp.exp(gl - g_max), axis=0, keepdims=True)
    g_idx = jnp.min(jnp.where(gl == g_max, row_f - ROUTER_GROUP_LANE, float(LANES)), axis=0, keepdims=True)

    y_b = [_dot(attn_ref[rows, :], wso_ref[...]) for rows in subs]

    e_lo = g_idx * MOE_EXPERTS_PER_GROUP
    in_group = (row_f >= e_lo) & (row_f < e_lo + MOE_EXPERTS_PER_GROUP)
    el = jnp.where(in_group, logits, NEG_BIG)
    t1 = jnp.max(el, axis=0, keepdims=True)
    i1 = jnp.min(jnp.where(el == t1, row_f, float(LANES)), axis=0, keepdims=True)

    merged = [(ga_ref[rows, :].astype(_F32) * a + gb_ref[rows, :].astype(_F32) * b).astype(_BF16)
              for rows, a, b in zip(subs, y_a, y_b)]
    mixed = [_dot(m, wout_ref[...]) for m in merged]
    x1 = [x_ref[rows, :] + m for rows, m in zip(subs, mixed)]
    for rows, v in zip(subs, x1):
        x1_ref[rows, :] = v

    el2 = jnp.where(row_f == i1, NEG_BIG, el)
    t2 = jnp.max(el2, axis=0, keepdims=True)
    i2 = jnp.min(jnp.where(el2 == t2, row_f, float(LANES)), axis=0, keepdims=True)
    e21 = jnp.exp(t2 - t1)
    w1 = g_prob / (1.0 + e21)
    w2 = g_prob * e21 / (1.0 + e21)

    h2 = [((v * lax.rsqrt(jnp.mean(v * v, axis=-1, keepdims=True) + NORM_EPS)) * g2_ref[...]).astype(_BF16)
          for v in x1]
    for rows, v in zip(subs, h2):
        h2_ref[rows, :] = v

    sel1 = row_f == i1
    sel2 = row_f == i2
    onehot = jnp.where(sel1 | sel2, 1.0, 0.0).astype(_BF16)
    earlier = (_iota((tm, tm), 0) < _iota((tm, tm), 1)).astype(_BF16)
    before = _dot(onehot, earlier)
    counts = _dot(onehot, jnp.ones((tm, LANES), _BF16))

    for rows, v in zip(subs, h2):
        logits_ref[cur, :, rows] = _dot_nt(wrt_ref[...], v) + brt_ref[:, 0:1]

    run = jnp.floor((counts + (RUN_ALIGN - 1.0)) * (1.0 / RUN_ALIGN)) * RUN_ALIGN
    lower_experts = (_iota((LANES, LANES), 0) > _iota((LANES, LANES), 1)).astype(_BF16)
    run_off = _dot(lower_experts, run.astype(_BF16))
    place = before + jnp.concatenate([run_off] * (tm // LANES), axis=1)
    pos1 = jnp.sum(jnp.where(sel1, place, 0.0), axis=0, keepdims=True)
    pos2 = jnp.sum(jnp.where(sel2, place, 0.0), axis=0, keepdims=True)
    counts_ref[...] = counts.T[0:SUBLANES, :]

    info_t = jnp.concatenate([i1, i2, w1, w2, pos1, pos2, jnp.zeros((2, tm), _F32)], axis=0)
    info_t_ref[...] = info_t
    info_ref[...] = jnp.concatenate([info_t, jnp.zeros((LANES - SUBLANES, tm), _F32)], axis=0).T


def _outproj(x2d, retg, attn, gates, w_ret_o, w_swa_o, w_out, g2, w_router_t, b_router_t):
    n = x2d.shape[0]
    tm = ROW_TILE
    n_tiles = n // tm
    const = lambda s: (0, 0)
    proj = lambda s: jnp.minimum(s, n_tiles - 1)
    routed = lambda s: jnp.maximum(s - 1, 0)
    return pl.pallas_call(
        _outproj_kernel,
        grid=(n_tiles + 1,),
        in_specs=[
            pl.BlockSpec((tm, D_MODEL), lambda s: (proj(s), 0)),
            pl.BlockSpec((tm, RET_V_W), lambda s: (proj(s), 0)),
            pl.BlockSpec((tm, SWA_Q_W), lambda s: (proj(s), 0)),
            pl.BlockSpec((tm, D_MODEL), lambda s: (proj(s), 0)),
            pl.BlockSpec((tm, D_MODEL), lambda s: (proj(s), 1)),
            pl.BlockSpec((RET_V_W, D_MODEL), const),
            pl.BlockSpec((SWA_Q_W, D_MODEL), const),
            pl.BlockSpec((D_MODEL, D_MODEL), const),
            pl.BlockSpec((1, D_MODEL), const),
            pl.BlockSpec((LANES, D_MODEL), const),
            pl.BlockSpec((LANES, LANES), const),
        ],
        out_specs=[
            pl.BlockSpec((tm, D_MODEL), lambda s: (proj(s), 0)),
            pl.BlockSpec((tm, D_MODEL), lambda s: (proj(s), 0)),
            pl.BlockSpec((tm, LANES), lambda s: (routed(s), 0)),
            pl.BlockSpec((SUBLANES, tm), lambda s: (0, routed(s))),
            pl.BlockSpec((SUBLANES, LANES), lambda s: (routed(s), 0)),
        ],
        scratch_shapes=[pltpu.VMEM((2, LANES, tm), _F32)],
        out_shape=[
            jax.ShapeDtypeStruct((n, D_MODEL), _F32),
            jax.ShapeDtypeStruct((n, D_MODEL), _BF16),
            jax.ShapeDtypeStruct((n, LANES), _F32),
            jax.ShapeDtypeStruct((SUBLANES, n), _F32),
            jax.ShapeDtypeStruct((n // tm * SUBLANES, LANES), _F32),
        ],
        compiler_params=pltpu.CompilerParams(
            dimension_semantics=("arbitrary",), vmem_limit_bytes=VMEM_LIMIT),
        name="outproj",
    )(x2d, retg, attn, gates, gates, w_ret_o, w_swa_o, w_out, g2, w_router_t, b_router_t)


def _piece_sizes(largest):
    piece = largest
    while piece >= RUN_ALIGN:
        yield piece
        piece //= 2


BIG_CAP = SORT_ROWS // RUN_PIECE
SMALL_CAP = MOE_N_EXPERTS
PIECE_LIST_KEYS = ("n_big", "big_off", "big_dst", "n_small", "small_off", "small_dst")


def _start_run_copies(pieces, tile, make_copy):
    n_big_ref, big_off_ref, big_dst_ref, n_small_ref, small_off_ref, small_dst_ref = pieces

    def big(k, carry):
        i = tile * BIG_CAP + k
        make_copy(pl.multiple_of(big_off_ref[i], RUN_ALIGN), pl.multiple_of(big_dst_ref[i], RUN_ALIGN),
                  RUN_PIECE).start()
        return carry

    lax.fori_loop(0, n_big_ref[tile], big, 0)

    def small(k, carry):
        i = tile * SMALL_CAP + k
        make_copy(pl.multiple_of(small_off_ref[i], RUN_ALIGN), pl.multiple_of(small_dst_ref[i], RUN_ALIGN),
                  RUN_ALIGN).start()
        return carry

    lax.fori_loop(0, n_small_ref[tile], small, 0)


def _wait_rows(rows, make_copy):
    for piece in _piece_sizes(SORT_ROWS_POW2 // 2):
        pl.when((rows & piece) != 0)(make_copy(0, 0, piece).wait)


def _zero_fill_padding(pad_start_ref, pad_len_ref, zeros_ref, xs_hbm, sem, wait):
    def per_expert(e, carry):
        pos = pad_start_ref[e]
        length = pad_len_ref[e]
        for piece in _piece_sizes(EXPERT_TILE // 2):
            copy = pltpu.make_async_copy(
                zeros_ref.at[pl.ds(0, piece)], xs_hbm.at[pl.ds(pl.multiple_of(pos, RUN_ALIGN), piece)], sem)
            pl.when((length & piece) != 0)(copy.wait if wait else copy.start)
            pos = pos + (length & piece)
        return carry

    lax.fori_loop(0, MOE_N_EXPERTS, per_expert, 0)

    tail_start = pad_start_ref[MOE_N_EXPERTS]
    n_rows = xs_hbm.shape[0]
    half = EXPERT_TILE // 2

    def per_half_tile(j, carry):
        pos = tail_start + j * half
        copy = pltpu.make_async_copy(
            zeros_ref, xs_hbm.at[pl.ds(pl.multiple_of(jnp.minimum(pos, n_rows - half), RUN_ALIGN), half)], sem)
        pl.when(pos < n_rows)(copy.wait if wait else copy.start)
        return carry

    lax.fori_loop(0, n_rows // half, per_half_tile, 0)


def _dispatch_kernel(*refs):
    pieces = refs[:len(PIECE_LIST_KEYS)]
    rows_ref, pad_start_ref, pad_len_ref, h2_ref, info_t_ref, xs_hbm, sorted_ref, zeros_ref, sems, zero_sem = (
        refs[len(PIECE_LIST_KEYS):])
    tile = pl.program_id(0)
    last = pl.num_programs(0) - 1
    slot = tile % 2
    tm = ROW_TILE

    @pl.when(tile == 0)
    def _():
        zeros_ref[...] = jnp.zeros_like(zeros_ref)
        _zero_fill_padding(pad_start_ref, pad_len_ref, zeros_ref, xs_hbm, zero_sem, wait=False)
        _zero_fill_padding(pad_start_ref, pad_len_ref, zeros_ref, xs_hbm, zero_sem, wait=True)

    pos = info_t_ref[4:6, :].astype(jnp.int32)

    def sort_rows(lo, hi):
        place = _iota((hi - lo, tm), 0) + lo
        onehot = ((place == pos[0:1, :]) | (place == pos[1:2, :])).astype(_BF16)
        for c in range(0, D_MODEL, 256):
            sorted_ref[slot, lo:hi, c:c + 256] = _dot(onehot, h2_ref[:, c:c + 256]).astype(_BF16)

    sort_rows(0, SORT_COMMON)
    pl.when(rows_ref[tile] > SORT_COMMON)(lambda: sort_rows(SORT_COMMON, SORT_ROWS))

    def copy_maker(t):
        s = t % 2

        def make_copy(off, dst, piece):
            return pltpu.make_async_copy(
                sorted_ref.at[s, pl.ds(off, piece)], xs_hbm.at[pl.ds(dst, piece)], sems.at[s])

        return make_copy

    _start_run_copies(pieces, tile, copy_maker(tile))

    @pl.when(tile > 0)
    def _():
        _wait_rows(rows_ref[tile - 1], copy_maker(tile - 1))

    @pl.when(tile == last)
    def _():
        _wait_rows(rows_ref[tile], copy_maker(tile))


def _dispatch(plan, h2, info_t, n_rows):
    n = h2.shape[0]
    grid_spec = pltpu.PrefetchScalarGridSpec(
        num_scalar_prefetch=len(PIECE_LIST_KEYS) + 3,
        grid=(n // ROW_TILE,),
        in_specs=[
            pl.BlockSpec((ROW_TILE, D_MODEL), lambda i, *_: (i, 0)),
            pl.BlockSpec((SUBLANES, ROW_TILE), lambda i, *_: (0, i)),
        ],
        out_specs=pl.BlockSpec(memory_space=pl.ANY),
        scratch_shapes=[
            pltpu.VMEM((2, SORT_ROWS, D_MODEL), _BF16),
            pltpu.VMEM((EXPERT_TILE // 2, D_MODEL), _BF16),
            pltpu.SemaphoreType.DMA((2,)),
            pltpu.SemaphoreType.DMA(()),
        ],
    )
    return pl.pallas_call(
        _dispatch_kernel,
        grid_spec=grid_spec,
        out_shape=jax.ShapeDtypeStruct((n_rows, D_MODEL), _BF16),
        compiler_params=pltpu.CompilerParams(
            dimension_semantics=("arbitrary",), vmem_limit_bytes=VMEM_LIMIT),
        name="dispatch",
    )(*[plan[k] for k in PIECE_LIST_KEYS], plan["rows"], plan["pad_start"], plan["pad_len"], h2, info_t)


def _experts_kernel(te_ref, nt_ref, order_ref, next_ref, xs_ref, wg_hbm, wu_hbm, wd_hbm, ys_ref,
                    wg_f, wu_f, wd_f, wg_s, wu_s, wd_s, sems):
    i = pl.program_id(0)
    in_use = i < nt_ref[0]
    slot = order_ref[i] % 2
    prev = te_ref[jnp.maximum(i - 1, 0)]
    changed = jnp.logical_or(i == 0, te_ref[i] != prev)

    def fetch(expert, s):
        return [pltpu.make_async_copy(hbm.at[expert], stage.at[s], sems.at[s, j])
                for j, (hbm, stage) in enumerate(((wg_hbm, wg_f), (wu_hbm, wu_f), (wd_hbm, wd_f)))]

    @pl.when(i == 0)
    def _():
        for copy in fetch(te_ref[0], 0):
            copy.start()

    @pl.when(jnp.logical_and(in_use, changed))
    def _():
        for copy in fetch(te_ref[i], slot):
            copy.wait()
        wg_s[...] = wg_f[slot].astype(_BF16)
        wu_s[...] = wu_f[slot].astype(_BF16)
        wd_s[...] = wd_f[slot].astype(_BF16)

        @pl.when(next_ref[i] != te_ref[i])
        def _():
            for copy in fetch(next_ref[i], 1 - slot):
                copy.start()

    @pl.when(in_use)
    def _():
        subs = [slice(s * EXPERT_SUB, (s + 1) * EXPERT_SUB) for s in range(EXPERT_TILE // EXPERT_SUB)]
        gate = [_dot(xs_ref[rows, :], wg_s[...]) for rows in subs]
        up = [_dot(xs_ref[rows, :], wu_s[...]) for rows in subs]
        act = [((g * _sigmoid(g)) * u).astype(_BF16) for g, u in zip(gate, up)]
        for rows, a in zip(subs, act):
            ys_ref[rows, :] = _dot(a, wd_s[...]).astype(_BF16)


def _experts(plan, xs, w_gate, w_up, w_down):
    n_rows = xs.shape[0]
    last_used = lambda i, nt: jnp.minimum(i, nt[0] - 1)
    grid_spec = pltpu.PrefetchScalarGridSpec(
        num_scalar_prefetch=4,
        grid=(n_rows // EXPERT_TILE,),
        in_specs=[
            pl.BlockSpec((EXPERT_TILE, D_MODEL), lambda i, te, nt, *_: (last_used(i, nt), 0)),
            pl.BlockSpec(memory_space=pl.ANY),
            pl.BlockSpec(memory_space=pl.ANY),
            pl.BlockSpec(memory_space=pl.ANY),
        ],
        out_specs=pl.BlockSpec((EXPERT_TILE, D_MODEL), lambda i, te, nt, *_: (last_used(i, nt), 0)),
        scratch_shapes=[
            pltpu.VMEM((2, D_MODEL, MOE_D_FF), _F32),
            pltpu.VMEM((2, D_MODEL, MOE_D_FF), _F32),
            pltpu.VMEM((2, MOE_D_FF, D_MODEL), _F32),
            pltpu.VMEM((D_MODEL, MOE_D_FF), _BF16),
            pltpu.VMEM((D_MODEL, MOE_D_FF), _BF16),
            pltpu.VMEM((MOE_D_FF, D_MODEL), _BF16),
            pltpu.SemaphoreType.DMA((2, 3)),
        ],
    )
    return pl.pallas_call(
        _experts_kernel,
        grid_spec=grid_spec,
        out_shape=jax.ShapeDtypeStruct((n_rows, D_MODEL), _BF16),
        input_output_aliases={4: 0},
        compiler_params=pltpu.CompilerParams(
            dimension_semantics=("arbitrary",), vmem_limit_bytes=VMEM_LIMIT),
        name="experts",
    )(plan["tile_expert"], plan["tiles_used"], plan["tile_order"], plan["tile_next"], xs, w_gate, w_up, w_down)


def _combine_kernel(*refs):
    pieces = refs[:len(PIECE_LIST_KEYS)]
    rows_ref, x1_ref, info_ref, ys_hbm, out_ref, block_ref, sems = refs[len(PIECE_LIST_KEYS):]
    tile = pl.program_id(0)
    last = pl.num_programs(0) - 1
    slot = tile % 2
    tm = ROW_TILE

    def copy_maker(t):
        s = t % 2

        def make_copy(off, dst, piece):
            return pltpu.make_async_copy(
                ys_hbm.at[pl.ds(dst, piece)], block_ref.at[s, pl.ds(off, piece)], sems.at[s])

        return make_copy

    @pl.when(tile == 0)
    def _():
        block_ref[...] = jnp.zeros_like(block_ref)
        _start_run_copies(pieces, tile, copy_maker(tile))

    @pl.when(tile < last)
    def _():
        _start_run_copies(pieces, tile + 1, copy_maker(tile + 1))

    _wait_rows(rows_ref[tile], copy_maker(tile))

    info = info_ref[...]

    def weights_of(lo, hi):
        place = (_iota((tm, hi - lo), 1) + lo).astype(_F32)
        return (jnp.where(place == info[:, 4:5], info[:, 2:3], 0.0)
                + jnp.where(place == info[:, 5:6], info[:, 3:4], 0.0)).astype(_BF16)

    weights = weights_of(0, SORT_COMMON)
    for c in range(0, D_MODEL, 256):
        out_ref[:, c:c + 256] = x1_ref[:, c:c + 256] + _dot(weights, block_ref[slot, 0:SORT_COMMON, c:c + 256])

    @pl.when(rows_ref[tile] > SORT_COMMON)
    def _():
        tail = weights_of(SORT_COMMON, SORT_ROWS)
        for c in range(0, D_MODEL, 256):
            out_ref[:, c:c + 256] += _dot(tail, block_ref[slot, SORT_COMMON:SORT_ROWS, c:c + 256])


def _combine(plan, x1, info, ys):
    n = x1.shape[0]
    grid_spec = pltpu.PrefetchScalarGridSpec(
        num_scalar_prefetch=len(PIECE_LIST_KEYS) + 1,
        grid=(n // ROW_TILE,),
        in_specs=[
            pl.BlockSpec((ROW_TILE, D_MODEL), lambda i, *_: (i, 0)),
            pl.BlockSpec((ROW_TILE, LANES), lambda i, *_: (i, 0)),
            pl.BlockSpec(memory_space=pl.ANY),
        ],
        out_specs=pl.BlockSpec((ROW_TILE, D_MODEL), lambda i, *_: (i, 0)),
        scratch_shapes=[pltpu.VMEM((2, SORT_ROWS, D_MODEL), _BF16), pltpu.SemaphoreType.DMA((2,))],
    )
    return pl.pallas_call(
        _combine_kernel,
        grid_spec=grid_spec,
        out_shape=jax.ShapeDtypeStruct((n, D_MODEL), _F32),
        compiler_params=pltpu.CompilerParams(
            dimension_semantics=("arbitrary",), vmem_limit_bytes=VMEM_LIMIT),
        name="combine",
    )(*[plan[k] for k in PIECE_LIST_KEYS], plan["rows"], x1, info, ys)


def _rotary_tables(seq):
    half = RET_QK_DIM // 2
    inv = ROPE_BASE ** (-jnp.arange(half, dtype=_F32) / half)
    hi = (jnp.arange(seq // RET_CHUNK, dtype=jnp.int32) * RET_CHUNK).astype(_F32)[:, None] * inv[None, :]
    lo = jnp.arange(RET_CHUNK, dtype=jnp.int32).astype(_F32)[:, None] * inv[None, :]
    cos_hi, sin_hi = jnp.cos(hi)[:, None, :], jnp.sin(hi)[:, None, :]
    cos_lo, sin_lo = jnp.cos(lo)[None, :, :], jnp.sin(lo)[None, :, :]
    cos = (cos_hi * cos_lo - sin_hi * sin_lo).reshape(seq, half)
    sin = (sin_hi * cos_lo + cos_hi * sin_lo).reshape(seq, half)
    return jnp.concatenate([cos, cos], axis=1), jnp.concatenate([-sin, sin], axis=1)


def _expert_row_bound(n_tokens):
    rows = (2 * n_tokens + (n_tokens // ROW_TILE) * MOE_N_EXPERTS * (RUN_ALIGN - 1)
            + MOE_N_EXPERTS * (EXPERT_TILE - RUN_ALIGN))
    return -(-rows // EXPERT_TILE) * EXPERT_TILE


def _routing_plan(counts_out, n_rows):
    te = EXPERT_TILE
    n_tok_tiles = counts_out.shape[0] // SUBLANES
    counts = counts_out.reshape(n_tok_tiles, SUBLANES, LANES)[:, 0, :MOE_N_EXPERTS].astype(jnp.int32)
    run = (counts + RUN_ALIGN - 1) // RUN_ALIGN * RUN_ALIGN
    off = jnp.cumsum(run, axis=1) - run
    total = jnp.sum(run, axis=0)
    padded = (total + te - 1) // te * te
    pends = jnp.cumsum(padded)
    pstarts = pends - padded
    dst = pstarts[None, :] + jnp.cumsum(run, axis=0) - run
    tile_start = jnp.arange(n_rows // te, dtype=jnp.int32) * te
    tile_expert = jnp.sum((tile_start[:, None] >= pends[None, :]).astype(jnp.int32), axis=1)
    tile_expert = jnp.minimum(tile_expert, MOE_N_EXPERTS - 1)
    i32 = lambda v: v.astype(jnp.int32)

    experts = jnp.arange(MOE_N_EXPERTS, dtype=jnp.int32)
    has_rows = total > 0
    rank = jnp.cumsum(has_rows.astype(jnp.int32)) - 1
    later = jnp.where((experts[None, :] > experts[:, None]) & has_rows[None, :], experts[None, :], MOE_N_EXPERTS)
    following = jnp.min(later, axis=1)
    following = jnp.where(following == MOE_N_EXPERTS, experts, following)
    of_tile = tile_expert[:, None] == experts[None, :]
    tile_order = jnp.sum(jnp.where(of_tile, rank[None, :], 0), axis=1)
    tile_next = jnp.sum(jnp.where(of_tile, following[None, :], 0), axis=1)

    n_big = run // RUN_PIECE
    big_first = jnp.cumsum(n_big, axis=1) - n_big
    k_big = jnp.arange(BIG_CAP, dtype=jnp.int32)[None, :, None]
    owns_big = (k_big >= big_first[:, None, :]) & (k_big < (big_first + n_big)[:, None, :])
    pick_big = lambda v: jnp.sum(jnp.where(owns_big, (v - RUN_PIECE * big_first)[:, None, :], 0), axis=-1)
    has_small = (run // RUN_ALIGN) % 2
    small_index = jnp.cumsum(has_small, axis=1) - has_small
    k_small = jnp.arange(SMALL_CAP, dtype=jnp.int32)[None, :, None]
    owns_small = (has_small[:, None, :] == 1) & (small_index[:, None, :] == k_small)
    pick_small = lambda v: jnp.sum(jnp.where(owns_small, (v + RUN_PIECE * n_big)[:, None, :], 0), axis=-1)
    return {
        "n_big": i32(jnp.sum(n_big, axis=1)),
        "big_off": i32((pick_big(off) + RUN_PIECE * k_big[..., 0]).reshape(-1)),
        "big_dst": i32((pick_big(dst) + RUN_PIECE * k_big[..., 0]).reshape(-1)),
        "n_small": i32(jnp.sum(has_small, axis=1)),
        "small_off": i32(pick_small(off).reshape(-1)),
        "small_dst": i32(pick_small(dst).reshape(-1)),
        "rows": i32(jnp.sum(run, axis=1)),
        "pad_start": i32(jnp.concatenate([pstarts + total, pends[-1:]])), "pad_len": i32(padded - total),
        "tile_expert": i32(tile_expert), "tiles_used": i32(pends[-1:] // te),
        "tile_order": i32(tile_order), "tile_next": i32(tile_next),
    }


def kernel(x, norm_mix_g, w_in, ret_gn_g, w_ret_o, q_norm_g, k_norm_g, sinks, w_swa_o, w_out, norm_ffn_g,
           w_router_group, b_router_group, w_router_expert, b_router_expert, w_gate, w_up, w_down):
    batch, seq, d = x.shape
    n = batch * seq
    assert d == D_MODEL and seq % SEQ_TILE == 0 and n % ROW_TILE == 0
    cos_tab, sin_tab = _rotary_tables(seq)
    n_rows = _expert_row_bound(n)
    for l in range(w_in.shape[0]):
        x2d = x.reshape(n, d)
        ret_proj, swa_proj, gates = _inproj(x2d, norm_mix_g[l][None, :], w_in[l].astype(_BF16), cos_tab, sin_tab,
                                            ret_gn_g[l].reshape(1, RET_V_W))
        retg, attn = _mixers(ret_proj, swa_proj, sinks[l], jnp.tile(q_norm_g[l], 2)[None, :],
                             jnp.tile(k_norm_g[l], 2)[None, :], batch, seq)
        pad = LANES - MOE_N_EXPERTS - MOE_GROUPS
        w_router_t = jnp.concatenate(
            [w_router_expert[l].T, w_router_group[l].T, jnp.zeros((pad, d), _F32)], axis=0).astype(_BF16)
        b_router = jnp.concatenate([b_router_expert[l], b_router_group[l], jnp.zeros((pad,), _F32)])
        b_router_t = jnp.broadcast_to(b_router[:, None], (LANES, LANES))
        x1, h2, info, info_t, counts = _outproj(
            x2d, retg, attn, gates, w_ret_o[l].astype(_BF16), w_swa_o[l].astype(_BF16), w_out[l].astype(_BF16),
            norm_ffn_g[l][None, :], w_router_t, b_router_t)
        plan = _routing_plan(counts, n_rows)
        xs = _dispatch(plan, h2, info_t, n_rows)
        ys = _experts(plan, xs, w_gate[l], w_up[l], w_down[l])
        x = _combine(plan, x1, info, ys).reshape(batch, seq, d)
    return x
```

```python
import math

import numpy as np
import jax
import jax.numpy as jnp
from jax import lax
from jax.experimental import pallas as pl
from jax.experimental.pallas import tpu as pltpu

D_MODEL = 1024
RET_HEADS = 4
RET_QK_DIM = 128
RET_V_DIM = 256
RET_CHUNK = 128
ROPE_BASE = 10000.0
SWA_Q_HEADS = 8
SWA_KV_HEADS = 2
SWA_HEAD_DIM = 64
SWA_WINDOW = 128
MOE_GROUPS = 4
MOE_EXPERTS_PER_GROUP = 8
MOE_D_FF = 512
NORM_EPS = 1e-6

RET_QK_W = RET_HEADS * RET_QK_DIM
RET_V_W = RET_HEADS * RET_V_DIM
SWA_Q_W = SWA_Q_HEADS * SWA_HEAD_DIM
SWA_KV_W = SWA_KV_HEADS * SWA_HEAD_DIM
SWA_GROUP = SWA_Q_HEADS // SWA_KV_HEADS
MOE_N_EXPERTS = MOE_GROUPS * MOE_EXPERTS_PER_GROUP

RET_COLS = 2 * RET_QK_W + 2 * RET_V_W
SWA_COLS = SWA_Q_W + 2 * SWA_KV_W
GATE_COLS = 2 * D_MODEL
IN_WIDTH = RET_COLS + SWA_COLS + GATE_COLS

LANES = 128
SUBLANES = 8
ROUTER_GROUP_LANE = 32
NEG_BIG = -1e30

ROW_TILE = 512
SEQ_TILE = 512
EXPERT_TILE = 512
OUTPROJ_SUB = 128
INPROJ_SUB = 256
EXPERT_SUB = 256
RUN_ALIGN = 16
SORT_ROWS = 2 * ROW_TILE + MOE_N_EXPERTS * RUN_ALIGN
SORT_ROWS_POW2 = 1 << SORT_ROWS.bit_length()
SORT_COMMON = 2 * ROW_TILE + MOE_N_EXPERTS * RUN_ALIGN * 9 // 16
VMEM_LIMIT = 56 * 1024 * 1024

RET_LOG_DECAY = tuple(float(np.log1p(-np.exp2(-5.0 - h))) for h in range(RET_HEADS))

_BF16 = jnp.bfloat16
_F32 = jnp.float32


def _dot(a, b):
    return jnp.dot(a, b, preferred_element_type=_F32)


def _dot_nt(a, b):
    return lax.dot_general(a, b, (((1,), (1,)), ((), ())), preferred_element_type=_F32)


def _sigmoid(v):
    return 1.0 / (1.0 + jnp.exp(-v))


def _split_bf16(v):
    hi = v.astype(_BF16)
    return hi, (v - hi.astype(_F32)).astype(_BF16)


def _iota(shape, dim):
    return lax.broadcasted_iota(jnp.int32, shape, dim)


def _inproj_kernel(x_ref, g_ref, w_ref, cos_ref, sin_ref, gn_ref, ret_ref, swa_ref, gate_ref):
    subs = [slice(s * INPROJ_SUB, (s + 1) * INPROJ_SUB) for s in range(ROW_TILE // INPROJ_SUB)]

    def normed(rows):
        x = x_ref[rows, :]
        r = lax.rsqrt(jnp.mean(x * x, axis=-1, keepdims=True) + NORM_EPS)
        return ((x * r) * g_ref[...]).astype(_BF16)
    chunk_pos = (_iota((INPROJ_SUB, RET_QK_DIM), 0) % RET_CHUNK + 1).astype(_F32)

    def rotary_decay(base, sign, scale):
        def epilogue(rows, raw):
            cos = cos_ref[rows, :]
            sin = sin_ref[rows, :]
            for hd in range(RET_HEADS):
                v = raw[:, hd * RET_QK_DIM:(hd + 1) * RET_QK_DIM]
                rot = v * cos + pltpu.roll(v, RET_QK_DIM // 2, 1) * sin
                decay = jnp.exp((sign * RET_LOG_DECAY[hd]) * chunk_pos) * scale
                ret_ref[rows, base + hd * RET_QK_DIM:base + (hd + 1) * RET_QK_DIM] = (rot * decay).astype(_BF16)
        return epilogue

    def store(ref, lo, fn):
        def epilogue(rows, raw):
            ref[rows, lo:lo + raw.shape[1]] = fn(raw).astype(_BF16)
        return epilogue

    v_base = 2 * RET_QK_W
    g_base = v_base + RET_V_W
    gate_base = RET_COLS + SWA_COLS
    jobs = [(0, RET_QK_W, rotary_decay(0, 1.0, 1.0)),
            (RET_QK_W, RET_QK_W, rotary_decay(RET_QK_W, -1.0, RET_QK_DIM ** -0.5))]
    jobs += [(v_base + c, 512, store(ret_ref, v_base + c, lambda raw: raw)) for c in range(0, RET_V_W, 512)]
    jobs += [(g_base + c, 512, store(ret_ref, g_base + c,
                                     lambda raw, c=c: (raw * _sigmoid(raw)) * gn_ref[:, c:c + 512]))
             for c in range(0, RET_V_W, 512)]
    jobs += [(RET_COLS + c, 256, store(swa_ref, c, lambda raw: raw)) for c in range(0, SWA_COLS, 256)]
    jobs += [(gate_base + c, 512, store(gate_ref, c, _sigmoid)) for c in range(0, GATE_COLS, 512)]
    hs = [normed(rows) for rows in subs]
    for lo, width, epilogue in jobs:
        raws = [_dot(h, w_ref[:, lo:lo + width]) for h in hs]
        for rows, raw in zip(subs, raws):
            epilogue(rows, raw)


def _inproj(x2d, g, w_in_bf16, cos_tab, sin_tab, gn_row):
    n = x2d.shape[0]
    grid = (n // ROW_TILE,)
    seq_tiles = cos_tab.shape[0] // ROW_TILE
    return pl.pallas_call(
        _inproj_kernel,
        grid=grid,
        in_specs=[
            pl.BlockSpec((ROW_TILE, D_MODEL), lambda i: (i, 0)),
            pl.BlockSpec((1, D_MODEL), lambda i: (0, 0)),
            pl.BlockSpec((D_MODEL, IN_WIDTH), lambda i: (0, 0), pipeline_mode=pl.Buffered(1)),
            pl.BlockSpec((ROW_TILE, RET_QK_DIM), lambda i: (i % seq_tiles, 0)),
            pl.BlockSpec((ROW_TILE, RET_QK_DIM), lambda i: (i % seq_tiles, 0)),
            pl.BlockSpec((1, RET_V_W), lambda i: (0, 0)),
        ],
        out_specs=[
            pl.BlockSpec((ROW_TILE, RET_COLS), lambda i: (i, 0)),
            pl.BlockSpec((ROW_TILE, SWA_COLS), lambda i: (i, 0)),
            pl.BlockSpec((ROW_TILE, GATE_COLS), lambda i: (i, 0)),
        ],
        out_shape=[
            jax.ShapeDtypeStruct((n, RET_COLS), _BF16),
            jax.ShapeDtypeStruct((n, SWA_COLS), _BF16),
            jax.ShapeDtypeStruct((n, GATE_COLS), _BF16),
        ],
        compiler_params=pltpu.CompilerParams(
            dimension_semantics=("arbitrary",), vmem_limit_bytes=VMEM_LIMIT),
        name="inproj",
    )(x2d, g, w_in_bf16, cos_tab, sin_tab, gn_row)


def _retention_stages(q_ref, k_ref, v_ref, g_ref, out_ref, state_ref):
    c = RET_CHUNK
    causal = _iota((c, c), 0) >= _iota((c, c), 1)

    for h in range(RET_HEADS):
        chunk_decay = math.exp(RET_LOG_DECAY[h] * c)
        qs = slice(h * RET_QK_DIM, (h + 1) * RET_QK_DIM)
        vs = slice(h * RET_V_DIM, (h + 1) * RET_V_DIM)
        chunks = [slice(ci * c, (ci + 1) * c) for ci in range(SEQ_TILE // c)]
        scores = [jnp.where(causal, _dot_nt(q_ref[rows, qs], k_ref[rows, qs]), 0.0).astype(_BF16) for rows in chunks]
        kvs = [_dot(k_ref[rows, qs].astype(_F32).T.astype(_BF16), v_ref[rows, vs]) for rows in chunks]
        yield
        state = state_ref[h]
        states = []
        for kv in kvs:
            states.append(state.astype(_BF16))
            state = chunk_decay * (state + kv)
        state_ref[h] = state
        rets = [_dot(jnp.concatenate([s, q_ref[rows, qs]], axis=1), jnp.concatenate([v_ref[rows, vs], st], axis=0))
                for rows, s, st in zip(chunks, scores, states)]
        yield
        for rows, ret in zip(chunks, rets):
            mu = jnp.mean(ret, axis=-1, keepdims=True)
            dev = ret - mu
            var = jnp.mean(dev * dev, axis=-1, keepdims=True)
            out_ref[rows, vs] = ((dev * lax.rsqrt(var + NORM_EPS)) * g_ref[rows, vs].astype(_F32)).astype(_BF16)
        yield


def _swa_stages(first, sinks_ref, q_ref, kc_ref, vc_ref, kp_ref, vp_ref, qg_ref, kg_ref, out_ref):
    w = SWA_WINDOW
    d = SWA_HEAD_DIM
    n_blk = SEQ_TILE // w
    pair = 2 * d

    k_both = jnp.concatenate([kp_ref[...], kc_ref[...]], axis=0).astype(_F32)
    same_half = (_iota((pair, pair), 0) // d == _iota((pair, pair), 1) // d).astype(_BF16)
    k_hi, k_lo = _split_bf16(k_both * k_both)
    k_ssq = _dot(k_hi, same_half) + _dot(k_lo, same_half)
    k_n = k_both * lax.rsqrt(k_ssq * (1.0 / d) + NORM_EPS) * (kg_ref[...] * d ** -0.5)
    v_both = jnp.concatenate([vp_ref[...], vc_ref[...]], axis=0).astype(_F32)
    v_t = [v_both[b * w:(b + 1) * w].T for b in range(n_blk + 1)]

    key = _iota((2 * w, SWA_GROUP * w), 0)
    qry = _iota((2 * w, SWA_GROUP * w), 1) % w
    band = (key > qry) & (key <= qry + w)
    band_first = band & jnp.logical_or(key >= w, jnp.logical_not(first))
    head_of_lane = _iota((1, SWA_GROUP * w), 1) // w
    half_of_lane = _iota((1, pair), 1) // d
    half_rows = (_iota((SUBLANES, pair), 0) == _iota((SUBLANES, pair), 1) // d).astype(_BF16)
    ones_rows = jnp.ones((2 * SUBLANES, 2 * w), _BF16)
    q_gain = qg_ref[...]
    yield

    for kk in range(SWA_KV_HEADS):
        k_native = jnp.where(half_of_lane == kk, k_n, 0.0)
        k_moved = pltpu.roll(k_native, d, 1)
        k_even, k_odd = (k_native, k_moved) if kk == 0 else (k_moved, k_native)
        k_even = k_even.astype(_BF16)
        k_odd = k_odd.astype(_BF16)
        sink_row = jnp.zeros((1, SWA_GROUP * w), _F32)
        for g in range(SWA_GROUP):
            sink_row = jnp.where(head_of_lane == g, sinks_ref[kk * SWA_GROUP + g], sink_row)
        blocks = range(n_blk)
        pairs = range(SWA_GROUP // 2)
        q_pairs = [[q_ref[j * w:(j + 1) * w, (kk * SWA_GROUP + 2 * p) * d:(kk * SWA_GROUP + 2 * p + 2) * d].astype(_F32)
                    for p in pairs] for j in blocks]
        q_splits = [[_split_bf16(q * q) for q in qs] for qs in q_pairs]
        q_ssq = [[_dot_nt(half_rows, hi) + _dot_nt(half_rows, lo) for hi, lo in sp] for sp in q_splits]
        q_rs = [[lax.rsqrt(s * (1.0 / d) + NORM_EPS) for s in ss] for ss in q_ssq]
        q_gs = [[(q * q_gain).astype(_BF16) for q in qs] for qs in q_pairs]
        yield
        raw_even = [[_dot_nt(k_even[j * w:(j + 2) * w], q_gs[j][p]) for p in pairs] for j in blocks]
        raw_odd = [[_dot_nt(k_odd[j * w:(j + 2) * w], q_gs[j][p]) for p in pairs] for j in blocks]
        yield
        score_blocks = []
        for j in blocks:
            raws = []
            for p in pairs:
                raws.append(raw_even[j][p] * q_rs[j][p][0:1, :])
                raws.append(raw_odd[j][p] * q_rs[j][p][1:2, :])
            s_t = jnp.concatenate(raws, axis=1)
            score_blocks.append(jnp.where(band_first if j == 0 else band, s_t, NEG_BIG))
        yield
        maxes = [jnp.maximum(jnp.max(s_t, axis=0, keepdims=True), sink_row) for s_t in score_blocks]
        probs = [jnp.exp(s_t - m).astype(_BF16) for s_t, m in zip(score_blocks, maxes)]
        yield
        outs = []
        for j in blocks:
            v_ext = jnp.concatenate([v_t[j][kk * d:(kk + 1) * d], v_t[j + 1][kk * d:(kk + 1) * d]], axis=1)
            outs.append(_dot(jnp.concatenate([v_ext.astype(_BF16), ones_rows], axis=0), probs[j]))
        yield
        for j in blocks:
            rows = slice(j * w, (j + 1) * w)
            denom = outs[j][d:d + 1, :] + jnp.exp(sink_row - maxes[j])
            o_t = outs[j][0:d, :] * (1.0 / denom)
            for p in range(SWA_GROUP // 2):
                c0 = (kk * SWA_GROUP + 2 * p) * d
                both = jnp.concatenate([o_t[:, 2 * p * w:(2 * p + 1) * w], o_t[:, (2 * p + 1) * w:(2 * p + 2) * w]], axis=0)
                out_ref[rows, c0:c0 + pair] = both.T.astype(_BF16)
        yield


def _mixers_kernel(sinks_ref, rq_ref, rk_ref, rv_ref, rg_ref, sq_ref, kc_ref, vc_ref, kp_ref, vp_ref, qg_ref, kg_ref,
                   retg_ref, attn_ref, state_ref):
    first = pl.program_id(1) == 0

    @pl.when(first)
    def _():
        state_ref[...] = jnp.zeros_like(state_ref)

    streams = [_retention_stages(rq_ref, rk_ref, rv_ref, rg_ref, retg_ref, state_ref),
               _swa_stages(first, sinks_ref, sq_ref, kc_ref, vc_ref, kp_ref, vp_ref, qg_ref, kg_ref, attn_ref)]
    while streams:
        streams = [s for s in streams if next(s, StopIteration) is not StopIteration]


def _mixers(ret_proj, swa_proj, sinks, q_gain2, k_gain2, batch, seq):
    n = ret_proj.shape[0]
    steps = seq // SEQ_TILE
    blocks_per_step = SEQ_TILE // SWA_WINDOW
    rowmap = lambda b, s: b * steps + s
    prevmap = lambda b, s: jnp.maximum((b * steps + s) * blocks_per_step - 1, 0)
    k_col = SWA_Q_W // SWA_KV_W
    return pl.pallas_call(
        _mixers_kernel,
        grid=(batch, steps),
        in_specs=[
            pl.BlockSpec(memory_space=pltpu.SMEM),
            pl.BlockSpec((SEQ_TILE, RET_QK_W), lambda b, s: (rowmap(b, s), 0)),
            pl.BlockSpec((SEQ_TILE, RET_QK_W), lambda b, s: (rowmap(b, s), 1)),
            pl.BlockSpec((SEQ_TILE, RET_V_W), lambda b, s: (rowmap(b, s), 1)),
            pl.BlockSpec((SEQ_TILE, RET_V_W), lambda b, s: (rowmap(b, s), 2)),
            pl.BlockSpec((SEQ_TILE, SWA_Q_W), lambda b, s: (rowmap(b, s), 0)),
            pl.BlockSpec((SEQ_TILE, SWA_KV_W), lambda b, s: (rowmap(b, s), k_col)),
            pl.BlockSpec((SEQ_TILE, SWA_KV_W), lambda b, s: (rowmap(b, s), k_col + 1)),
            pl.BlockSpec((SWA_WINDOW, SWA_KV_W), lambda b, s: (prevmap(b, s), k_col)),
            pl.BlockSpec((SWA_WINDOW, SWA_KV_W), lambda b, s: (prevmap(b, s), k_col + 1)),
            pl.BlockSpec((1, 2 * SWA_HEAD_DIM), lambda b, s: (0, 0)),
            pl.BlockSpec((1, 2 * SWA_HEAD_DIM), lambda b, s: (0, 0)),
        ],
        out_specs=[
            pl.BlockSpec((SEQ_TILE, RET_V_W), lambda b, s: (rowmap(b, s), 0)),
            pl.BlockSpec((SEQ_TILE, SWA_Q_W), lambda b, s: (rowmap(b, s), 0)),
        ],
        out_shape=[
            jax.ShapeDtypeStruct((n, RET_V_W), _BF16),
            jax.ShapeDtypeStruct((n, SWA_Q_W), _BF16),
        ],
        scratch_shapes=[pltpu.VMEM((RET_HEADS, RET_QK_DIM, RET_V_DIM), _F32)],
        compiler_params=pltpu.CompilerParams(
            dimension_semantics=("arbitrary", "arbitrary"), vmem_limit_bytes=VMEM_LIMIT),
        name="mixers",
    )(sinks, ret_proj, ret_proj, ret_proj, ret_proj, swa_proj, swa_proj, swa_proj, swa_proj, swa_proj,
      q_gain2, k_gain2)


def _outproj_kernel(x_ref, retg_ref, attn_ref, ga_ref, gb_ref, wro_ref, wso_ref, wout_ref, g2_ref,
                    wrt_ref, brt_ref, x1_ref, h2_ref, info_ref, info_t_ref, counts_ref, logits_ref):
    tm = ROW_TILE
    step = pl.program_id(0)
    cur = step % 2

    @pl.when(step == 0)
    def _():
        logits_ref[...] = jnp.zeros_like(logits_ref)

    logits = logits_ref[1 - cur]
    row = _iota((LANES, tm), 0)
    row_f = row.astype(_F32)
    is_group = (row >= ROUTER_GROUP_LANE) & (row < ROUTER_GROUP_LANE + MOE_GROUPS)

    subs = [slice(s * OUTPROJ_SUB, (s + 1) * OUTPROJ_SUB) for s in range(tm // OUTPROJ_SUB)]
    y_a = [_dot(retg_ref[rows, :], wro_ref[...]) for rows in subs]

    gl = jnp.where(is_group, logits, NEG_BIG)
    g_max = jnp.max(gl, axis=0, keepdims=True)
    g_prob = 1.0 / jnp.sum(jnp.exp(gl - g_max), axis=0, keepdims=True)
    g_idx = jnp.min(jnp.where(gl == g_max, row_f - ROUTER_GROUP_LANE, float(LANES)), axis=0, keepdims=True)

    y_b = [_dot(attn_ref[rows, :], wso_ref[...]) for rows in subs]

    e_lo = g_idx * MOE_EXPERTS_PER_GROUP
    in_group = (row_f >= e_lo) & (row_f < e_lo + MOE_EXPERTS_PER_GROUP)
    el = jnp.where(in_group, logits, NEG_BIG)
    t1 = jnp.max(el, axis=0, keepdims=True)
    i1 = jnp.min(jnp.where(el == t1, row_f, float(LANES)), axis=0, keepdims=True)

    merged = [(ga_ref[rows, :].astype(_F32) * a + gb_ref[rows, :].astype(_F32) * b).astype(_BF16)
              for rows, a, b in zip(subs, y_a, y_b)]
    mixed = [_dot(m, wout_ref[...]) for m in merged]
    x1 = [x_ref[rows, :] + m for rows, m in zip(subs, mixed)]
    for rows, v in zip(subs, x1):
        x1_ref[rows, :] = v

    el2 = jnp.where(row_f == i1, NEG_BIG, el)
    t2 = jnp.max(el2, axis=0, keepdims=True)
    i2 = jnp.min(jnp.where(el2 == t2, row_f, float(LANES)), axis=0, keepdims=True)
    e21 = jnp.exp(t2 - t1)
    w1 = g_prob / (1.0 + e21)
    w2 = g_prob * e21 / (1.0 + e21)

    h2 = [((v * lax.rsqrt(jnp.mean(v * v, axis=-1, keepdims=True) + NORM_EPS)) * g2_ref[...]).astype(_BF16)
          for v in x1]
    for rows, v in zip(subs, h2):
        h2_ref[rows, :] = v

    sel1 = row_f == i1
    sel2 = row_f == i2
    onehot = jnp.where(sel1 | sel2, 1.0, 0.0).astype(_BF16)
    earlier = (_iota((tm, tm), 0) < _iota((tm, tm), 1)).astype(_BF16)
    before = _dot(onehot, earlier)
    counts = _dot(onehot, jnp.ones((tm, LANES), _BF16))

    for rows, v in zip(subs, h2):
        logits_ref[cur, :, rows] = _dot_nt(wrt_ref[...], v) + brt_ref[:, 0:1]

    run = jnp.floor((counts + (RUN_ALIGN - 1.0)) * (1.0 / RUN_ALIGN)) * RUN_ALIGN
    lower_experts = (_iota((LANES, LANES), 0) > _iota((LANES, LANES), 1)).astype(_BF16)
    run_off = _dot(lower_experts, run.astype(_BF16))
    place = before + jnp.concatenate([run_off] * (tm // LANES), axis=1)
    pos1 = jnp.sum(jnp.where(sel1, place, 0.0), axis=0, keepdims=True)
    pos2 = jnp.sum(jnp.where(sel2, place, 0.0), axis=0, keepdims=True)
    counts_ref[...] = counts.T[0:SUBLANES, :]

    info_t = jnp.concatenate([i1, i2, w1, w2, pos1, pos2, jnp.zeros((2, tm), _F32)], axis=0)
    info_t_ref[...] = info_t
    info_ref[...] = jnp.concatenate([info_t, jnp.zeros((LANES - SUBLANES, tm), _F32)], axis=0).T


def _outproj(x2d, retg, attn, gates, w_ret_o, w_swa_o, w_out, g2, w_router_t, b_router_t):
    n = x2d.shape[0]
    tm = ROW_TILE
    n_tiles = n // tm
    const = lambda s: (0, 0)
    proj = lambda s: jnp.minimum(s, n_tiles - 1)
    routed = lambda s: jnp.maximum(s - 1, 0)
    return pl.pallas_call(
        _outproj_kernel,
        grid=(n_tiles + 1,),
        in_specs=[
            pl.BlockSpec((tm, D_MODEL), lambda s: (proj(s), 0)),
            pl.BlockSpec((tm, RET_V_W), lambda s: (proj(s), 0)),
            pl.BlockSpec((tm, SWA_Q_W), lambda s: (proj(s), 0)),
            pl.BlockSpec((tm, D_MODEL), lambda s: (proj(s), 0)),
            pl.BlockSpec((tm, D_MODEL), lambda s: (proj(s), 1)),
            pl.BlockSpec((RET_V_W, D_MODEL), const),
            pl.BlockSpec((SWA_Q_W, D_MODEL), const),
            pl.BlockSpec((D_MODEL, D_MODEL), const),
            pl.BlockSpec((1, D_MODEL), const),
            pl.BlockSpec((LANES, D_MODEL), const),
            pl.BlockSpec((LANES, LANES), const),
        ],
        out_specs=[
            pl.BlockSpec((tm, D_MODEL), lambda s: (proj(s), 0)),
            pl.BlockSpec((tm, D_MODEL), lambda s: (proj(s), 0)),
            pl.BlockSpec((tm, LANES), lambda s: (routed(s), 0)),
            pl.BlockSpec((SUBLANES, tm), lambda s: (0, routed(s))),
            pl.BlockSpec((SUBLANES, LANES), lambda s: (routed(s), 0)),
        ],
        scratch_shapes=[pltpu.VMEM((2, LANES, tm), _F32)],
        out_shape=[
            jax.ShapeDtypeStruct((n, D_MODEL), _F32),
            jax.ShapeDtypeStruct((n, D_MODEL), _BF16),
            jax.ShapeDtypeStruct((n, LANES), _F32),
            jax.ShapeDtypeStruct((SUBLANES, n), _F32),
            jax.ShapeDtypeStruct((n // tm * SUBLANES, LANES), _F32),
        ],
        compiler_params=pltpu.CompilerParams(
            dimension_semantics=("arbitrary",), vmem_limit_bytes=VMEM_LIMIT),
        name="outproj",
    )(x2d, retg, attn, gates, gates, w_ret_o, w_swa_o, w_out, g2, w_router_t, b_router_t)


def _piece_sizes(largest):
    piece = largest
    while piece >= RUN_ALIGN:
        yield piece
        piece //= 2


PIECE_CLASSES = tuple(k * RUN_ALIGN for k in (4, 3, 2, 1))
PIECE_LIST_KEYS = tuple(f"{field}{size}" for size in PIECE_CLASSES for field in ("n", "off", "dst"))


def _piece_cap(size):
    return SORT_ROWS // size if size == PIECE_CLASSES[0] else MOE_N_EXPERTS


def _start_run_copies(pieces, tile, make_copy):
    for k, size in enumerate(PIECE_CLASSES):
        n_ref, off_ref, dst_ref = pieces[3 * k:3 * k + 3]

        def start(j, carry, off_ref=off_ref, dst_ref=dst_ref, size=size):
            i = tile * _piece_cap(size) + j
            make_copy(pl.multiple_of(off_ref[i], RUN_ALIGN), pl.multiple_of(dst_ref[i], RUN_ALIGN), size).start()
            return carry

        lax.fori_loop(0, n_ref[tile], start, 0)


def _wait_rows(rows, make_copy):
    for piece in _piece_sizes(SORT_ROWS_POW2 // 2):
        pl.when((rows & piece) != 0)(make_copy(0, 0, piece).wait)


def _zero_fill_padding(pad_start_ref, pad_len_ref, zeros_ref, xs_hbm, sem, wait):
    def per_expert(e, carry):
        pos = pad_start_ref[e]
        length = pad_len_ref[e]
        for piece in _piece_sizes(EXPERT_TILE // 2):
            copy = pltpu.make_async_copy(
                zeros_ref.at[pl.ds(0, piece)], xs_hbm.at[pl.ds(pl.multiple_of(pos, RUN_ALIGN), piece)], sem)
            pl.when((length & piece) != 0)(copy.wait if wait else copy.start)
            pos = pos + (length & piece)
        return carry

    lax.fori_loop(0, MOE_N_EXPERTS, per_expert, 0)

    tail_start = pad_start_ref[MOE_N_EXPERTS]
    n_rows = xs_hbm.shape[0]
    half = EXPERT_TILE // 2

    def per_half_tile(j, carry):
        pos = tail_start + j * half
        copy = pltpu.make_async_copy(
            zeros_ref, xs_hbm.at[pl.ds(pl.multiple_of(jnp.minimum(pos, n_rows - half), RUN_ALIGN), half)], sem)
        pl.when(pos < n_rows)(copy.wait if wait else copy.start)
        return carry

    lax.fori_loop(0, n_rows // half, per_half_tile, 0)


def _dispatch_kernel(*refs):
    pieces = refs[:len(PIECE_LIST_KEYS)]
    rows_ref, pad_start_ref, pad_len_ref, h2_ref, info_t_ref, xs_hbm, sorted_ref, zeros_ref, sems, zero_sem = (
        refs[len(PIECE_LIST_KEYS):])
    tile = pl.program_id(0)
    last = pl.num_programs(0) - 1
    slot = tile % 2
    tm = ROW_TILE

    @pl.when(tile == 0)
    def _():
        zeros_ref[...] = jnp.zeros_like(zeros_ref)
        _zero_fill_padding(pad_start_ref, pad_len_ref, zeros_ref, xs_hbm, zero_sem, wait=False)
        _zero_fill_padding(pad_start_ref, pad_len_ref, zeros_ref, xs_hbm, zero_sem, wait=True)

    pos = info_t_ref[4:6, :].astype(jnp.int32)

    def sort_rows(lo, hi):
        place = _iota((hi - lo, tm), 0) + lo
        onehot = ((place == pos[0:1, :]) | (place == pos[1:2, :])).astype(_BF16)
        for c in range(0, D_MODEL, 256):
            sorted_ref[slot, lo:hi, c:c + 256] = _dot(onehot, h2_ref[:, c:c + 256]).astype(_BF16)

    sort_rows(0, SORT_COMMON)
    pl.when(rows_ref[tile] > SORT_COMMON)(lambda: sort_rows(SORT_COMMON, SORT_ROWS))

    def copy_maker(t):
        s = t % 2

        def make_copy(off, dst, piece):
            return pltpu.make_async_copy(
                sorted_ref.at[s, pl.ds(off, piece)], xs_hbm.at[pl.ds(dst, piece)], sems.at[s])

        return make_copy

    _start_run_copies(pieces, tile, copy_maker(tile))

    @pl.when(tile > 0)
    def _():
        _wait_rows(rows_ref[tile - 1], copy_maker(tile - 1))

    @pl.when(tile == last)
    def _():
        _wait_rows(rows_ref[tile], copy_maker(tile))


def _dispatch(plan, h2, info_t, n_rows):
    n = h2.shape[0]
    grid_spec = pltpu.PrefetchScalarGridSpec(
        num_scalar_prefetch=len(PIECE_LIST_KEYS) + 3,
        grid=(n // ROW_TILE,),
        in_specs=[
            pl.BlockSpec((ROW_TILE, D_MODEL), lambda i, *_: (i, 0)),
            pl.BlockSpec((SUBLANES, ROW_TILE), lambda i, *_: (0, i)),
        ],
        out_specs=pl.BlockSpec(memory_space=pl.ANY),
        scratch_shapes=[
            pltpu.VMEM((2, SORT_ROWS, D_MODEL), _BF16),
            pltpu.VMEM((EXPERT_TILE // 2, D_MODEL), _BF16),
            pltpu.SemaphoreType.DMA((2,)),
            pltpu.SemaphoreType.DMA(()),
        ],
    )
    return pl.pallas_call(
        _dispatch_kernel,
        grid_spec=grid_spec,
        out_shape=jax.ShapeDtypeStruct((n_rows, D_MODEL), _BF16),
        compiler_params=pltpu.CompilerParams(
            dimension_semantics=("arbitrary",), vmem_limit_bytes=VMEM_LIMIT),
        name="dispatch",
    )(*[plan[k] for k in PIECE_LIST_KEYS], plan["rows"], plan["pad_start"], plan["pad_len"], h2, info_t)


def _experts_kernel(te_ref, nt_ref, order_ref, next_ref, xs_ref, wg_hbm, wu_hbm, wd_hbm, ys_ref,
                    wg_f, wu_f, wd_f, wg_s, wu_s, wd_s, sems):
    i = pl.program_id(0)
    in_use = i < nt_ref[0]
    slot = order_ref[i] % 2
    prev = te_ref[jnp.maximum(i - 1, 0)]
    changed = jnp.logical_or(i == 0, te_ref[i] != prev)

    def fetch(expert, s):
        return [pltpu.make_async_copy(hbm.at[expert], stage.at[s], sems.at[s, j])
                for j, (hbm, stage) in enumerate(((wg_hbm, wg_f), (wu_hbm, wu_f), (wd_hbm, wd_f)))]

    @pl.when(i == 0)
    def _():
        for copy in fetch(te_ref[0], 0):
            copy.start()

    @pl.when(jnp.logical_and(in_use, changed))
    def _():
        for copy in fetch(te_ref[i], slot):
            copy.wait()
        wg_s[...] = wg_f[slot].astype(_BF16)
        wu_s[...] = wu_f[slot].astype(_BF16)
        wd_s[...] = wd_f[slot].astype(_BF16)

        @pl.when(next_ref[i] != te_ref[i])
        def _():
            for copy in fetch(next_ref[i], 1 - slot):
                copy.start()

    @pl.when(in_use)
    def _():
        subs = [slice(s * EXPERT_SUB, (s + 1) * EXPERT_SUB) for s in range(EXPERT_TILE // EXPERT_SUB)]
        gate = [_dot(xs_ref[rows, :], wg_s[...]) for rows in subs]
        up = [_dot(xs_ref[rows, :], wu_s[...]) for rows in subs]
        act = [((g * _sigmoid(g)) * u).astype(_BF16) for g, u in zip(gate, up)]
        for rows, a in zip(subs, act):
            ys_ref[rows, :] = _dot(a, wd_s[...]).astype(_BF16)


def _experts(plan, xs, w_gate, w_up, w_down):
    n_rows = xs.shape[0]
    last_used = lambda i, nt: jnp.minimum(i, nt[0] - 1)
    grid_spec = pltpu.PrefetchScalarGridSpec(
        num_scalar_prefetch=4,
        grid=(n_rows // EXPERT_TILE,),
        in_specs=[
            pl.BlockSpec((EXPERT_TILE, D_MODEL), lambda i, te, nt, *_: (last_used(i, nt), 0)),
            pl.BlockSpec(memory_space=pl.ANY),
            pl.BlockSpec(memory_space=pl.ANY),
            pl.BlockSpec(memory_space=pl.ANY),
        ],
        out_specs=pl.BlockSpec((EXPERT_TILE, D_MODEL), lambda i, te, nt, *_: (last_used(i, nt), 0)),
        scratch_shapes=[
            pltpu.VMEM((2, D_MODEL, MOE_D_FF), _F32),
            pltpu.VMEM((2, D_MODEL, MOE_D_FF), _F32),
            pltpu.VMEM((2, MOE_D_FF, D_MODEL), _F32),
            pltpu.VMEM((D_MODEL, MOE_D_FF), _BF16),
            pltpu.VMEM((D_MODEL, MOE_D_FF), _BF16),
            pltpu.VMEM((MOE_D_FF, D_MODEL), _BF16),
            pltpu.SemaphoreType.DMA((2, 3)),
        ],
    )
    return pl.pallas_call(
        _experts_kernel,
        grid_spec=grid_spec,
        out_shape=jax.ShapeDtypeStruct((n_rows, D_MODEL), _BF16),
        input_output_aliases={4: 0},
        compiler_params=pltpu.CompilerParams(
            dimension_semantics=("arbitrary",), vmem_limit_bytes=VMEM_LIMIT),
        name="experts",
    )(plan["tile_expert"], plan["tiles_used"], plan["tile_order"], plan["tile_next"], xs, w_gate, w_up, w_down)


def _combine_kernel(*refs):
    pieces = refs[:len(PIECE_LIST_KEYS)]
    rows_ref, x1_ref, info_ref, ys_hbm, out_ref, block_ref, sems = refs[len(PIECE_LIST_KEYS):]
    tile = pl.program_id(0)
    last = pl.num_programs(0) - 1
    slot = tile % 2
    tm = ROW_TILE

    def copy_maker(t):
        s = t % 2

        def make_copy(off, dst, piece):
            return pltpu.make_async_copy(
                ys_hbm.at[pl.ds(dst, piece)], block_ref.at[s, pl.ds(off, piece)], sems.at[s])

        return make_copy

    @pl.when(tile == 0)
    def _():
        block_ref[...] = jnp.zeros_like(block_ref)
        _start_run_copies(pieces, tile, copy_maker(tile))

    @pl.when(tile < last)
    def _():
        _start_run_copies(pieces, tile + 1, copy_maker(tile + 1))

    _wait_rows(rows_ref[tile], copy_maker(tile))

    info = info_ref[...]

    def weights_of(lo, hi):
        place = (_iota((tm, hi - lo), 1) + lo).astype(_F32)
        return (jnp.where(place == info[:, 4:5], info[:, 2:3], 0.0)
                + jnp.where(place == info[:, 5:6], info[:, 3:4], 0.0)).astype(_BF16)

    weights = weights_of(0, SORT_COMMON)
    for c in range(0, D_MODEL, 256):
        out_ref[:, c:c + 256] = x1_ref[:, c:c + 256] + _dot(weights, block_ref[slot, 0:SORT_COMMON, c:c + 256])

    @pl.when(rows_ref[tile] > SORT_COMMON)
    def _():
        tail = weights_of(SORT_COMMON, SORT_ROWS)
        for c in range(0, D_MODEL, 256):
            out_ref[:, c:c + 256] += _dot(tail, block_ref[slot, SORT_COMMON:SORT_ROWS, c:c + 256])


def _combine(plan, x1, info, ys):
    n = x1.shape[0]
    grid_spec = pltpu.PrefetchScalarGridSpec(
        num_scalar_prefetch=len(PIECE_LIST_KEYS) + 1,
        grid=(n // ROW_TILE,),
        in_specs=[
            pl.BlockSpec((ROW_TILE, D_MODEL), lambda i, *_: (i, 0)),
            pl.BlockSpec((ROW_TILE, LANES), lambda i, *_: (i, 0)),
            pl.BlockSpec(memory_space=pl.ANY),
        ],
        out_specs=pl.BlockSpec((ROW_TILE, D_MODEL), lambda i, *_: (i, 0)),
        scratch_shapes=[pltpu.VMEM((2, SORT_ROWS, D_MODEL), _BF16), pltpu.SemaphoreType.DMA((2,))],
    )
    return pl.pallas_call(
        _combine_kernel,
        grid_spec=grid_spec,
        out_shape=jax.ShapeDtypeStruct((n, D_MODEL), _F32),
        compiler_params=pltpu.CompilerParams(
            dimension_semantics=("arbitrary",), vmem_limit_bytes=VMEM_LIMIT),
        name="combine",
    )(*[plan[k] for k in PIECE_LIST_KEYS], plan["rows"], x1, info, ys)


def _rotary_tables(seq):
    half = RET_QK_DIM // 2
    inv = ROPE_BASE ** (-jnp.arange(half, dtype=_F32) / half)
    hi = (jnp.arange(seq // RET_CHUNK, dtype=jnp.int32) * RET_CHUNK).astype(_F32)[:, None] * inv[None, :]
    lo = jnp.arange(RET_CHUNK, dtype=jnp.int32).astype(_F32)[:, None] * inv[None, :]
    cos_hi, sin_hi = jnp.cos(hi)[:, None, :], jnp.sin(hi)[:, None, :]
    cos_lo, sin_lo = jnp.cos(lo)[None, :, :], jnp.sin(lo)[None, :, :]
    cos = (cos_hi * cos_lo - sin_hi * sin_lo).reshape(seq, half)
    sin = (sin_hi * cos_lo + cos_hi * sin_lo).reshape(seq, half)
    return jnp.concatenate([cos, cos], axis=1), jnp.concatenate([-sin, sin], axis=1)


def _expert_row_bound(n_tokens):
    rows = (2 * n_tokens + (n_tokens // ROW_TILE) * MOE_N_EXPERTS * (RUN_ALIGN - 1)
            + MOE_N_EXPERTS * (EXPERT_TILE - RUN_ALIGN))
    return -(-rows // EXPERT_TILE) * EXPERT_TILE


def _routing_plan(counts_out, n_rows):
    te = EXPERT_TILE
    n_tok_tiles = counts_out.shape[0] // SUBLANES
    counts = counts_out.reshape(n_tok_tiles, SUBLANES, LANES)[:, 0, :MOE_N_EXPERTS].astype(jnp.int32)
    run = (counts + RUN_ALIGN - 1) // RUN_ALIGN * RUN_ALIGN
    off = jnp.cumsum(run, axis=1) - run
    total = jnp.sum(run, axis=0)
    padded = (total + te - 1) // te * te
    pends = jnp.cumsum(padded)
    pstarts = pends - padded
    dst = pstarts[None, :] + jnp.cumsum(run, axis=0) - run
    tile_start = jnp.arange(n_rows // te, dtype=jnp.int32) * te
    tile_expert = jnp.sum((tile_start[:, None] >= pends[None, :]).astype(jnp.int32), axis=1)
    tile_expert = jnp.minimum(tile_expert, MOE_N_EXPERTS - 1)
    i32 = lambda v: v.astype(jnp.int32)

    experts = jnp.arange(MOE_N_EXPERTS, dtype=jnp.int32)
    has_rows = total > 0
    rank = jnp.cumsum(has_rows.astype(jnp.int32)) - 1
    later = jnp.where((experts[None, :] > experts[:, None]) & has_rows[None, :], experts[None, :], MOE_N_EXPERTS)
    following = jnp.min(later, axis=1)
    following = jnp.where(following == MOE_N_EXPERTS, experts, following)
    of_tile = tile_expert[:, None] == experts[None, :]
    tile_order = jnp.sum(jnp.where(of_tile, rank[None, :], 0), axis=1)
    tile_next = jnp.sum(jnp.where(of_tile, following[None, :], 0), axis=1)

    largest = PIECE_CLASSES[0]
    n_big = run // largest
    rest = run - largest * n_big
    big_first = jnp.cumsum(n_big, axis=1) - n_big
    k_big = jnp.arange(_piece_cap(largest), dtype=jnp.int32)[None, :, None]
    owns_big = (k_big >= big_first[:, None, :]) & (k_big < (big_first + n_big)[:, None, :])
    pick_big = lambda v: jnp.sum(jnp.where(owns_big, (v - largest * big_first)[:, None, :], 0), axis=-1)
    lists = {f"n{largest}": i32(jnp.sum(n_big, axis=1)),
             f"off{largest}": i32((pick_big(off) + largest * k_big[..., 0]).reshape(-1)),
             f"dst{largest}": i32((pick_big(dst) + largest * k_big[..., 0]).reshape(-1))}
    k_small = jnp.arange(MOE_N_EXPERTS, dtype=jnp.int32)[None, :, None]
    for size in PIECE_CLASSES[1:]:
        has = (rest == size).astype(jnp.int32)
        index = jnp.cumsum(has, axis=1) - has
        owns = (has[:, None, :] == 1) & (index[:, None, :] == k_small)
        pick = lambda v, owns=owns: jnp.sum(jnp.where(owns, (v + largest * n_big)[:, None, :], 0), axis=-1)
        lists.update({f"n{size}": i32(jnp.sum(has, axis=1)),
                      f"off{size}": i32(pick(off).reshape(-1)), f"dst{size}": i32(pick(dst).reshape(-1))})
    return {
        **lists,
        "rows": i32(jnp.sum(run, axis=1)),
        "pad_start": i32(jnp.concatenate([pstarts + total, pends[-1:]])), "pad_len": i32(padded - total),
        "tile_expert": i32(tile_expert), "tiles_used": i32(pends[-1:] // te),
        "tile_order": i32(tile_order), "tile_next": i32(tile_next),
    }


def kernel(x, norm_mix_g, w_in, ret_gn_g, w_ret_o, q_norm_g, k_norm_g, sinks, w_swa_o, w_out, norm_ffn_g,
           w_router_group, b_router_group, w_router_expert, b_router_expert, w_gate, w_up, w_down):
    batch, seq, d = x.shape
    n = batch * seq
    assert d == D_MODEL and seq % SEQ_TILE == 0 and n % ROW_TILE == 0
    cos_tab, sin_tab = _rotary_tables(seq)
    n_rows = _expert_row_bound(n)
    for l in range(w_in.shape[0]):
        x2d = x.reshape(n, d)
        ret_proj, swa_proj, gates = _inproj(x2d, norm_mix_g[l][None, :], w_in[l].astype(_BF16), cos_tab, sin_tab,
                                            ret_gn_g[l].reshape(1, RET_V_W))
        retg, attn = _mixers(ret_proj, swa_proj, sinks[l], jnp.tile(q_norm_g[l], 2)[None, :],
                             jnp.tile(k_norm_g[l], 2)[None, :], batch, seq)
        pad = LANES - MOE_N_EXPERTS - MOE_GROUPS
        w_router_t = jnp.concatenate(
            [w_router_expert[l].T, w_router_group[l].T, jnp.zeros((pad, d), _F32)], axis=0).astype(_BF16)
        b_router = jnp.concatenate([b_router_expert[l], b_router_group[l], jnp.zeros((pad,), _F32)])
        b_router_t = jnp.broadcast_to(b_router[:, None], (LANES, LANES))
        x1, h2, info, info_t, counts = _outproj(
            x2d, retg, attn, gates, w_ret_o[l].astype(_BF16), w_swa_o[l].astype(_BF16), w_out[l].astype(_BF16),
            norm_ffn_g[l][None, :], w_router_t, b_router_t)
        plan = _routing_plan(counts, n_rows)
        xs = _dispatch(plan, h2, info_t, n_rows)
        ys = _experts(plan, xs, w_gate[l], w_up[l], w_down[l])
        x = _combine(plan, x1, info, ys).reshape(batch, seq, d)
    return x
```

```python
import math

import numpy as np
import jax
import jax.numpy as jnp
from jax import lax
from jax.experimental import pallas as pl
from jax.experimental.pallas import tpu as pltpu

D_MODEL = 1024
RET_HEADS = 4
RET_QK_DIM = 128
RET_V_DIM = 256
RET_CHUNK = 128
ROPE_BASE = 10000.0
SWA_Q_HEADS = 8
SWA_KV_HEADS = 2
SWA_HEAD_DIM = 64
SWA_WINDOW = 128
MOE_GROUPS = 4
MOE_EXPERTS_PER_GROUP = 8
MOE_D_FF = 512
NORM_EPS = 1e-6

RET_QK_W = RET_HEADS * RET_QK_DIM
RET_V_W = RET_HEADS * RET_V_DIM
SWA_Q_W = SWA_Q_HEADS * SWA_HEAD_DIM
SWA_KV_W = SWA_KV_HEADS * SWA_HEAD_DIM
SWA_GROUP = SWA_Q_HEADS // SWA_KV_HEADS
MOE_N_EXPERTS = MOE_GROUPS * MOE_EXPERTS_PER_GROUP

RET_COLS = 2 * RET_QK_W + 2 * RET_V_W
SWA_COLS = SWA_Q_W + 2 * SWA_KV_W
GATE_COLS = 2 * D_MODEL
IN_WIDTH = RET_COLS + SWA_COLS + GATE_COLS

LANES = 128
SUBLANES = 8
ROUTER_GROUP_LANE = 32
NEG_BIG = -1e30

ROW_TILE = 512
SEQ_TILE = 512
EXPERT_TILE = 512
OUTPROJ_SUB = 128
INPROJ_SUB = 256
EXPERT_SUB = 256
RUN_ALIGN = 16
SORT_ROWS = 2 * ROW_TILE + MOE_N_EXPERTS * RUN_ALIGN
SORT_ROWS_POW2 = 1 << SORT_ROWS.bit_length()
SORT_COMMON = 2 * ROW_TILE + MOE_N_EXPERTS * RUN_ALIGN * 9 // 16
RING = 3
VMEM_LIMIT = 56 * 1024 * 1024

RET_LOG_DECAY = tuple(float(np.log1p(-np.exp2(-5.0 - h))) for h in range(RET_HEADS))

_BF16 = jnp.bfloat16
_F32 = jnp.float32


def _dot(a, b):
    return jnp.dot(a, b, preferred_element_type=_F32)


def _dot_nt(a, b):
    return lax.dot_general(a, b, (((1,), (1,)), ((), ())), preferred_element_type=_F32)


def _sigmoid(v):
    return 1.0 / (1.0 + jnp.exp(-v))


def _split_bf16(v):
    hi = v.astype(_BF16)
    return hi, (v - hi.astype(_F32)).astype(_BF16)


def _iota(shape, dim):
    return lax.broadcasted_iota(jnp.int32, shape, dim)


def _inproj_kernel(x_ref, g_ref, w_ref, cos_ref, sin_ref, gn_ref, ret_ref, swa_ref, gate_ref):
    subs = [slice(s * INPROJ_SUB, (s + 1) * INPROJ_SUB) for s in range(ROW_TILE // INPROJ_SUB)]

    def normed(rows):
        x = x_ref[rows, :]
        r = lax.rsqrt(jnp.mean(x * x, axis=-1, keepdims=True) + NORM_EPS)
        return ((x * r) * g_ref[...]).astype(_BF16)
    chunk_pos = (_iota((INPROJ_SUB, RET_QK_DIM), 0) % RET_CHUNK + 1).astype(_F32)

    def rotary_decay(base, sign, scale):
        def epilogue(rows, raw):
            cos = cos_ref[rows, :]
            sin = sin_ref[rows, :]
            for hd in range(RET_HEADS):
                v = raw[:, hd * RET_QK_DIM:(hd + 1) * RET_QK_DIM]
                rot = v * cos + pltpu.roll(v, RET_QK_DIM // 2, 1) * sin
                decay = jnp.exp((sign * RET_LOG_DECAY[hd]) * chunk_pos) * scale
                ret_ref[rows, base + hd * RET_QK_DIM:base + (hd + 1) * RET_QK_DIM] = (rot * decay).astype(_BF16)
        return epilogue

    def store(ref, lo, fn):
        def epilogue(rows, raw):
            ref[rows, lo:lo + raw.shape[1]] = fn(raw).astype(_BF16)
        return epilogue

    v_base = 2 * RET_QK_W
    g_base = v_base + RET_V_W
    gate_base = RET_COLS + SWA_COLS
    jobs = [(0, RET_QK_W, rotary_decay(0, 1.0, 1.0)),
            (RET_QK_W, RET_QK_W, rotary_decay(RET_QK_W, -1.0, RET_QK_DIM ** -0.5))]
    jobs += [(v_base + c, 512, store(ret_ref, v_base + c, lambda raw: raw)) for c in range(0, RET_V_W, 512)]
    jobs += [(g_base + c, 512, store(ret_ref, g_base + c,
                                     lambda raw, c=c: (raw * _sigmoid(raw)) * gn_ref[:, c:c + 512]))
             for c in range(0, RET_V_W, 512)]
    jobs += [(RET_COLS + c, 256, store(swa_ref, c, lambda raw: raw)) for c in range(0, SWA_COLS, 256)]
    jobs += [(gate_base + c, 512, store(gate_ref, c, _sigmoid)) for c in range(0, GATE_COLS, 512)]
    hs = [normed(rows) for rows in subs]
    for lo, width, epilogue in jobs:
        raws = [_dot(h, w_ref[:, lo:lo + width]) for h in hs]
        for rows, raw in zip(subs, raws):
            epilogue(rows, raw)


def _inproj(x2d, g, w_in_bf16, cos_tab, sin_tab, gn_row):
    n = x2d.shape[0]
    grid = (n // ROW_TILE,)
    seq_tiles = cos_tab.shape[0] // ROW_TILE
    return pl.pallas_call(
        _inproj_kernel,
        grid=grid,
        in_specs=[
            pl.BlockSpec((ROW_TILE, D_MODEL), lambda i: (i, 0)),
            pl.BlockSpec((1, D_MODEL), lambda i: (0, 0)),
            pl.BlockSpec((D_MODEL, IN_WIDTH), lambda i: (0, 0), pipeline_mode=pl.Buffered(1)),
            pl.BlockSpec((ROW_TILE, RET_QK_DIM), lambda i: (i % seq_tiles, 0)),
            pl.BlockSpec((ROW_TILE, RET_QK_DIM), lambda i: (i % seq_tiles, 0)),
            pl.BlockSpec((1, RET_V_W), lambda i: (0, 0)),
        ],
        out_specs=[
            pl.BlockSpec((ROW_TILE, RET_COLS), lambda i: (i, 0)),
            pl.BlockSpec((ROW_TILE, SWA_COLS), lambda i: (i, 0)),
            pl.BlockSpec((ROW_TILE, GATE_COLS), lambda i: (i, 0)),
        ],
        out_shape=[
            jax.ShapeDtypeStruct((n, RET_COLS), _BF16),
            jax.ShapeDtypeStruct((n, SWA_COLS), _BF16),
            jax.ShapeDtypeStruct((n, GATE_COLS), _BF16),
        ],
        compiler_params=pltpu.CompilerParams(
            dimension_semantics=("arbitrary",), vmem_limit_bytes=VMEM_LIMIT),
        name="inproj",
    )(x2d, g, w_in_bf16, cos_tab, sin_tab, gn_row)


def _retention_stages(q_ref, k_ref, v_ref, g_ref, out_ref, state_ref):
    c = RET_CHUNK
    causal = _iota((c, c), 0) >= _iota((c, c), 1)

    for h in range(RET_HEADS):
        chunk_decay = math.exp(RET_LOG_DECAY[h] * c)
        qs = slice(h * RET_QK_DIM, (h + 1) * RET_QK_DIM)
        vs = slice(h * RET_V_DIM, (h + 1) * RET_V_DIM)
        chunks = [slice(ci * c, (ci + 1) * c) for ci in range(SEQ_TILE // c)]
        scores = [jnp.where(causal, _dot_nt(q_ref[rows, qs], k_ref[rows, qs]), 0.0).astype(_BF16) for rows in chunks]
        kvs = [_dot(k_ref[rows, qs].astype(_F32).T.astype(_BF16), v_ref[rows, vs]) for rows in chunks]
        yield
        state = state_ref[h]
        states = []
        for kv in kvs:
            states.append(state.astype(_BF16))
            state = chunk_decay * (state + kv)
        state_ref[h] = state
        rets = [_dot(jnp.concatenate([s, q_ref[rows, qs]], axis=1), jnp.concatenate([v_ref[rows, vs], st], axis=0))
                for rows, s, st in zip(chunks, scores, states)]
        yield
        for rows, ret in zip(chunks, rets):
            mu = jnp.mean(ret, axis=-1, keepdims=True)
            dev = ret - mu
            var = jnp.mean(dev * dev, axis=-1, keepdims=True)
            out_ref[rows, vs] = ((dev * lax.rsqrt(var + NORM_EPS)) * g_ref[rows, vs].astype(_F32)).astype(_BF16)
        yield


def _swa_stages(first, sinks_ref, q_ref, kc_ref, vc_ref, kp_ref, vp_ref, qg_ref, kg_ref, out_ref):
    w = SWA_WINDOW
    d = SWA_HEAD_DIM
    n_blk = SEQ_TILE // w
    pair = 2 * d

    k_both = jnp.concatenate([kp_ref[...], kc_ref[...]], axis=0).astype(_F32)
    same_half = (_iota((pair, pair), 0) // d == _iota((pair, pair), 1) // d).astype(_BF16)
    k_hi, k_lo = _split_bf16(k_both * k_both)
    k_ssq = _dot(k_hi, same_half) + _dot(k_lo, same_half)
    k_n = k_both * lax.rsqrt(k_ssq * (1.0 / d) + NORM_EPS) * (kg_ref[...] * d ** -0.5)
    v_both = jnp.concatenate([vp_ref[...], vc_ref[...]], axis=0).astype(_F32)
    v_t = [v_both[b * w:(b + 1) * w].T for b in range(n_blk + 1)]

    key = _iota((2 * w, SWA_GROUP * w), 0)
    qry = _iota((2 * w, SWA_GROUP * w), 1) % w
    band = (key > qry) & (key <= qry + w)
    band_first = band & jnp.logical_or(key >= w, jnp.logical_not(first))
    head_of_lane = _iota((1, SWA_GROUP * w), 1) // w
    half_of_lane = _iota((1, pair), 1) // d
    half_rows = (_iota((SUBLANES, pair), 0) == _iota((SUBLANES, pair), 1) // d).astype(_BF16)
    ones_rows = jnp.ones((2 * SUBLANES, 2 * w), _BF16)
    q_gain = qg_ref[...]
    yield

    for kk in range(SWA_KV_HEADS):
        k_native = jnp.where(half_of_lane == kk, k_n, 0.0)
        k_moved = pltpu.roll(k_native, d, 1)
        k_even, k_odd = (k_native, k_moved) if kk == 0 else (k_moved, k_native)
        k_even = k_even.astype(_BF16)
        k_odd = k_odd.astype(_BF16)
        sink_row = jnp.zeros((1, SWA_GROUP * w), _F32)
        for g in range(SWA_GROUP):
            sink_row = jnp.where(head_of_lane == g, sinks_ref[kk * SWA_GROUP + g], sink_row)
        blocks = range(n_blk)
        pairs = range(SWA_GROUP // 2)
        q_pairs = [[q_ref[j * w:(j + 1) * w, (kk * SWA_GROUP + 2 * p) * d:(kk * SWA_GROUP + 2 * p + 2) * d].astype(_F32)
                    for p in pairs] for j in blocks]
        q_splits = [[_split_bf16(q * q) for q in qs] for qs in q_pairs]
        q_ssq = [[_dot_nt(half_rows, hi) + _dot_nt(half_rows, lo) for hi, lo in sp] for sp in q_splits]
        q_rs = [[lax.rsqrt(s * (1.0 / d) + NORM_EPS) for s in ss] for ss in q_ssq]
        q_gs = [[(q * q_gain).astype(_BF16) for q in qs] for qs in q_pairs]
        yield
        raw_even = [[_dot_nt(k_even[j * w:(j + 2) * w], q_gs[j][p]) for p in pairs] for j in blocks]
        raw_odd = [[_dot_nt(k_odd[j * w:(j + 2) * w], q_gs[j][p]) for p in pairs] for j in blocks]
        yield
        score_blocks = []
        for j in blocks:
            raws = []
            for p in pairs:
                raws.append(raw_even[j][p] * q_rs[j][p][0:1, :])
                raws.append(raw_odd[j][p] * q_rs[j][p][1:2, :])
            s_t = jnp.concatenate(raws, axis=1)
            score_blocks.append(jnp.where(band_first if j == 0 else band, s_t, NEG_BIG))
        yield
        maxes = [jnp.maximum(jnp.max(s_t, axis=0, keepdims=True), sink_row) for s_t in score_blocks]
        probs = [jnp.exp(s_t - m).astype(_BF16) for s_t, m in zip(score_blocks, maxes)]
        yield
        outs = []
        for j in blocks:
            v_ext = jnp.concatenate([v_t[j][kk * d:(kk + 1) * d], v_t[j + 1][kk * d:(kk + 1) * d]], axis=1)
            outs.append(_dot(jnp.concatenate([v_ext.astype(_BF16), ones_rows], axis=0), probs[j]))
        yield
        for j in blocks:
            rows = slice(j * w, (j + 1) * w)
            denom = outs[j][d:d + 1, :] + jnp.exp(sink_row - maxes[j])
            o_t = outs[j][0:d, :] * (1.0 / denom)
            for p in range(SWA_GROUP // 2):
                c0 = (kk * SWA_GROUP + 2 * p) * d
                both = jnp.concatenate([o_t[:, 2 * p * w:(2 * p + 1) * w], o_t[:, (2 * p + 1) * w:(2 * p + 2) * w]], axis=0)
                out_ref[rows, c0:c0 + pair] = both.T.astype(_BF16)
        yield


def _mixers_kernel(sinks_ref, rq_ref, rk_ref, rv_ref, rg_ref, sq_ref, kc_ref, vc_ref, kp_ref, vp_ref, qg_ref, kg_ref,
                   retg_ref, attn_ref, state_ref):
    first = pl.program_id(1) == 0

    @pl.when(first)
    def _():
        state_ref[...] = jnp.zeros_like(state_ref)

    streams = [_retention_stages(rq_ref, rk_ref, rv_ref, rg_ref, retg_ref, state_ref),
               _swa_stages(first, sinks_ref, sq_ref, kc_ref, vc_ref, kp_ref, vp_ref, qg_ref, kg_ref, attn_ref)]
    while streams:
        streams = [s for s in streams if next(s, StopIteration) is not StopIteration]


def _mixers(ret_proj, swa_proj, sinks, q_gain2, k_gain2, batch, seq):
    n = ret_proj.shape[0]
    steps = seq // SEQ_TILE
    blocks_per_step = SEQ_TILE // SWA_WINDOW
    rowmap = lambda b, s: b * steps + s
    prevmap = lambda b, s: jnp.maximum((b * steps + s) * blocks_per_step - 1, 0)
    k_col = SWA_Q_W // SWA_KV_W
    return pl.pallas_call(
        _mixers_kernel,
        grid=(batch, steps),
        in_specs=[
            pl.BlockSpec(memory_space=pltpu.SMEM),
            pl.BlockSpec((SEQ_TILE, RET_QK_W), lambda b, s: (rowmap(b, s), 0)),
            pl.BlockSpec((SEQ_TILE, RET_QK_W), lambda b, s: (rowmap(b, s), 1)),
            pl.BlockSpec((SEQ_TILE, RET_V_W), lambda b, s: (rowmap(b, s), 1)),
            pl.BlockSpec((SEQ_TILE, RET_V_W), lambda b, s: (rowmap(b, s), 2)),
            pl.BlockSpec((SEQ_TILE, SWA_Q_W), lambda b, s: (rowmap(b, s), 0)),
            pl.BlockSpec((SEQ_TILE, SWA_KV_W), lambda b, s: (rowmap(b, s), k_col)),
            pl.BlockSpec((SEQ_TILE, SWA_KV_W), lambda b, s: (rowmap(b, s), k_col + 1)),
            pl.BlockSpec((SWA_WINDOW, SWA_KV_W), lambda b, s: (prevmap(b, s), k_col)),
            pl.BlockSpec((SWA_WINDOW, SWA_KV_W), lambda b, s: (prevmap(b, s), k_col + 1)),
            pl.BlockSpec((1, 2 * SWA_HEAD_DIM), lambda b, s: (0, 0)),
            pl.BlockSpec((1, 2 * SWA_HEAD_DIM), lambda b, s: (0, 0)),
        ],
        out_specs=[
            pl.BlockSpec((SEQ_TILE, RET_V_W), lambda b, s: (rowmap(b, s), 0)),
            pl.BlockSpec((SEQ_TILE, SWA_Q_W), lambda b, s: (rowmap(b, s), 0)),
        ],
        out_shape=[
            jax.ShapeDtypeStruct((n, RET_V_W), _BF16),
            jax.ShapeDtypeStruct((n, SWA_Q_W), _BF16),
        ],
        scratch_shapes=[pltpu.VMEM((RET_HEADS, RET_QK_DIM, RET_V_DIM), _F32)],
        compiler_params=pltpu.CompilerParams(
            dimension_semantics=("arbitrary", "arbitrary"), vmem_limit_bytes=VMEM_LIMIT),
        name="mixers",
    )(sinks, ret_proj, ret_proj, ret_proj, ret_proj, swa_proj, swa_proj, swa_proj, swa_proj, swa_proj,
      q_gain2, k_gain2)


def _outproj_kernel(x_ref, retg_ref, attn_ref, ga_ref, gb_ref, wro_ref, wso_ref, wout_ref, g2_ref,
                    wrt_ref, brt_ref, x1_ref, h2_ref, info_ref, info_t_ref, counts_ref, logits_ref):
    tm = ROW_TILE
    step = pl.program_id(0)
    cur = step % 2

    @pl.when(step == 0)
    def _():
        logits_ref[...] = jnp.zeros_like(logits_ref)

    logits = logits_ref[1 - cur]
    row = _iota((LANES, tm), 0)
    row_f = row.astype(_F32)
    is_group = (row >= ROUTER_GROUP_LANE) & (row < ROUTER_GROUP_LANE + MOE_GROUPS)

    subs = [slice(s * OUTPROJ_SUB, (s + 1) * OUTPROJ_SUB) for s in range(tm // OUTPROJ_SUB)]
    y_a = [_dot(retg_ref[rows, :], wro_ref[...]) for rows in subs]

    gl = jnp.where(is_group, logits, NEG_BIG)
    g_max = jnp.max(gl, axis=0, keepdims=True)
    g_prob = 1.0 / jnp.sum(jnp.exp(gl - g_max), axis=0, keepdims=True)
    g_idx = jnp.min(jnp.where(gl == g_max, row_f - ROUTER_GROUP_LANE, float(LANES)), axis=0, keepdims=True)

    y_b = [_dot(attn_ref[rows, :], wso_ref[...]) for rows in subs]

    e_lo = g_idx * MOE_EXPERTS_PER_GROUP
    in_group = (row_f >= e_lo) & (row_f < e_lo + MOE_EXPERTS_PER_GROUP)
    el = jnp.where(in_group, logits, NEG_BIG)
    t1 = jnp.max(el, axis=0, keepdims=True)
    i1 = jnp.min(jnp.where(el == t1, row_f, float(LANES)), axis=0, keepdims=True)

    merged = [(ga_ref[rows, :].astype(_F32) * a + gb_ref[rows, :].astype(_F32) * b).astype(_BF16)
              for rows, a, b in zip(subs, y_a, y_b)]
    mixed = [_dot(m, wout_ref[...]) for m in merged]
    x1 = [x_ref[rows, :] + m for rows, m in zip(subs, mixed)]
    for rows, v in zip(subs, x1):
        x1_ref[rows, :] = v

    el2 = jnp.where(row_f == i1, NEG_BIG, el)
    t2 = jnp.max(el2, axis=0, keepdims=True)
    i2 = jnp.min(jnp.where(el2 == t2, row_f, float(LANES)), axis=0, keepdims=True)
    e21 = jnp.exp(t2 - t1)
    w1 = g_prob / (1.0 + e21)
    w2 = g_prob * e21 / (1.0 + e21)

    h2 = [((v * lax.rsqrt(jnp.mean(v * v, axis=-1, keepdims=True) + NORM_EPS)) * g2_ref[...]).astype(_BF16)
          for v in x1]
    for rows, v in zip(subs, h2):
        h2_ref[rows, :] = v

    sel1 = row_f == i1
    sel2 = row_f == i2
    onehot = jnp.where(sel1 | sel2, 1.0, 0.0).astype(_BF16)
    earlier = (_iota((tm, tm), 0) < _iota((tm, tm), 1)).astype(_BF16)
    before = _dot(onehot, earlier)
    counts = _dot(onehot, jnp.ones((tm, LANES), _BF16))

    for rows, v in zip(subs, h2):
        logits_ref[cur, :, rows] = _dot_nt(wrt_ref[...], v) + brt_ref[:, 0:1]

    run = jnp.floor((counts + (RUN_ALIGN - 1.0)) * (1.0 / RUN_ALIGN)) * RUN_ALIGN
    lower_experts = (_iota((LANES, LANES), 0) > _iota((LANES, LANES), 1)).astype(_BF16)
    run_off = _dot(lower_experts, run.astype(_BF16))
    place = before + jnp.concatenate([run_off] * (tm // LANES), axis=1)
    pos1 = jnp.sum(jnp.where(sel1, place, 0.0), axis=0, keepdims=True)
    pos2 = jnp.sum(jnp.where(sel2, place, 0.0), axis=0, keepdims=True)
    counts_ref[...] = counts.T[0:SUBLANES, :]

    info_t = jnp.concatenate([i1, i2, w1, w2, pos1, pos2, jnp.zeros((2, tm), _F32)], axis=0)
    info_t_ref[...] = info_t
    info_ref[...] = jnp.concatenate([info_t, jnp.zeros((LANES - SUBLANES, tm), _F32)], axis=0).T


def _outproj(x2d, retg, attn, gates, w_ret_o, w_swa_o, w_out, g2, w_router_t, b_router_t):
    n = x2d.shape[0]
    tm = ROW_TILE
    n_tiles = n // tm
    const = lambda s: (0, 0)
    proj = lambda s: jnp.minimum(s, n_tiles - 1)
    routed = lambda s: jnp.maximum(s - 1, 0)
    return pl.pallas_call(
        _outproj_kernel,
        grid=(n_tiles + 1,),
        in_specs=[
            pl.BlockSpec((tm, D_MODEL), lambda s: (proj(s), 0)),
            pl.BlockSpec((tm, RET_V_W), lambda s: (proj(s), 0)),
            pl.BlockSpec((tm, SWA_Q_W), lambda s: (proj(s), 0)),
            pl.BlockSpec((tm, D_MODEL), lambda s: (proj(s), 0)),
            pl.BlockSpec((tm, D_MODEL), lambda s: (proj(s), 1)),
            pl.BlockSpec((RET_V_W, D_MODEL), const),
            pl.BlockSpec((SWA_Q_W, D_MODEL), const),
            pl.BlockSpec((D_MODEL, D_MODEL), const),
            pl.BlockSpec((1, D_MODEL), const),
            pl.BlockSpec((LANES, D_MODEL), const),
            pl.BlockSpec((LANES, LANES), const),
        ],
        out_specs=[
            pl.BlockSpec((tm, D_MODEL), lambda s: (proj(s), 0)),
            pl.BlockSpec((tm, D_MODEL), lambda s: (proj(s), 0)),
            pl.BlockSpec((tm, LANES), lambda s: (routed(s), 0)),
            pl.BlockSpec((SUBLANES, tm), lambda s: (0, routed(s))),
            pl.BlockSpec((SUBLANES, LANES), lambda s: (routed(s), 0)),
        ],
        scratch_shapes=[pltpu.VMEM((2, LANES, tm), _F32)],
        out_shape=[
            jax.ShapeDtypeStruct((n, D_MODEL), _F32),
            jax.ShapeDtypeStruct((n, D_MODEL), _BF16),
            jax.ShapeDtypeStruct((n, LANES), _F32),
            jax.ShapeDtypeStruct((SUBLANES, n), _F32),
            jax.ShapeDtypeStruct((n // tm * SUBLANES, LANES), _F32),
        ],
        compiler_params=pltpu.CompilerParams(
            dimension_semantics=("arbitrary",), vmem_limit_bytes=VMEM_LIMIT),
        name="outproj",
    )(x2d, retg, attn, gates, gates, w_ret_o, w_swa_o, w_out, g2, w_router_t, b_router_t)


def _piece_sizes(largest):
    piece = largest
    while piece >= RUN_ALIGN:
        yield piece
        piece //= 2


PIECE_CLASSES = tuple(k * RUN_ALIGN for k in (4, 3, 2, 1))
PIECE_LIST_KEYS = tuple(f"{field}{size}" for size in PIECE_CLASSES for field in ("n", "off", "dst"))


def _piece_cap(size):
    return SORT_ROWS // size if size == PIECE_CLASSES[0] else MOE_N_EXPERTS


def _start_run_copies(pieces, tile, make_copy):
    for k, size in enumerate(PIECE_CLASSES):
        n_ref, off_ref, dst_ref = pieces[3 * k:3 * k + 3]

        def start(j, carry, off_ref=off_ref, dst_ref=dst_ref, size=size):
            i = tile * _piece_cap(size) + j
            make_copy(pl.multiple_of(off_ref[i], RUN_ALIGN), pl.multiple_of(dst_ref[i], RUN_ALIGN), size).start()
            return carry

        lax.fori_loop(0, n_ref[tile], start, 0)


def _wait_rows(rows, make_copy):
    for piece in _piece_sizes(SORT_ROWS_POW2 // 2):
        pl.when((rows & piece) != 0)(make_copy(0, 0, piece).wait)


def _zero_fill_padding(pad_start_ref, pad_len_ref, zeros_ref, xs_hbm, sem, wait):
    def per_expert(e, carry):
        pos = pad_start_ref[e]
        length = pad_len_ref[e]
        for piece in _piece_sizes(EXPERT_TILE // 2):
            copy = pltpu.make_async_copy(
                zeros_ref.at[pl.ds(0, piece)], xs_hbm.at[pl.ds(pl.multiple_of(pos, RUN_ALIGN), piece)], sem)
            pl.when((length & piece) != 0)(copy.wait if wait else copy.start)
            pos = pos + (length & piece)
        return carry

    lax.fori_loop(0, MOE_N_EXPERTS, per_expert, 0)

    tail_start = pad_start_ref[MOE_N_EXPERTS]
    n_rows = xs_hbm.shape[0]
    half = EXPERT_TILE // 2

    def per_half_tile(j, carry):
        pos = tail_start + j * half
        copy = pltpu.make_async_copy(
            zeros_ref, xs_hbm.at[pl.ds(pl.multiple_of(jnp.minimum(pos, n_rows - half), RUN_ALIGN), half)], sem)
        pl.when(pos < n_rows)(copy.wait if wait else copy.start)
        return carry

    lax.fori_loop(0, n_rows // half, per_half_tile, 0)


def _dispatch_kernel(*refs):
    pieces = refs[:len(PIECE_LIST_KEYS)]
    rows_ref, pad_start_ref, pad_len_ref, h2_ref, info_t_ref, xs_hbm, sorted_ref, zeros_ref, sems, zero_sem = (
        refs[len(PIECE_LIST_KEYS):])
    tile = pl.program_id(0)
    last = pl.num_programs(0) - 1
    slot = tile % RING
    tm = ROW_TILE

    @pl.when(tile == 0)
    def _():
        zeros_ref[...] = jnp.zeros_like(zeros_ref)
        _zero_fill_padding(pad_start_ref, pad_len_ref, zeros_ref, xs_hbm, zero_sem, wait=False)
        _zero_fill_padding(pad_start_ref, pad_len_ref, zeros_ref, xs_hbm, zero_sem, wait=True)

    pos = info_t_ref[4:6, :].astype(jnp.int32)

    def sort_rows(lo, hi):
        place = _iota((hi - lo, tm), 0) + lo
        onehot = ((place == pos[0:1, :]) | (place == pos[1:2, :])).astype(_BF16)
        for c in range(0, D_MODEL, 256):
            sorted_ref[slot, lo:hi, c:c + 256] = _dot(onehot, h2_ref[:, c:c + 256]).astype(_BF16)

    sort_rows(0, SORT_COMMON)
    pl.when(rows_ref[tile] > SORT_COMMON)(lambda: sort_rows(SORT_COMMON, SORT_ROWS))

    def copy_maker(t):
        s = t % RING

        def make_copy(off, dst, piece):
            return pltpu.make_async_copy(
                sorted_ref.at[s, pl.ds(off, piece)], xs_hbm.at[pl.ds(dst, piece)], sems.at[s])

        return make_copy

    _start_run_copies(pieces, tile, copy_maker(tile))

    @pl.when(tile >= RING - 1)
    def _():
        _wait_rows(rows_ref[tile - (RING - 1)], copy_maker(tile - (RING - 1)))

    for back in range(RING - 1):
        @pl.when(jnp.logical_and(tile == last, tile >= back))
        def _(back=back):
            _wait_rows(rows_ref[tile - back], copy_maker(tile - back))


def _dispatch(plan, h2, info_t, n_rows):
    n = h2.shape[0]
    grid_spec = pltpu.PrefetchScalarGridSpec(
        num_scalar_prefetch=len(PIECE_LIST_KEYS) + 3,
        grid=(n // ROW_TILE,),
        in_specs=[
            pl.BlockSpec((ROW_TILE, D_MODEL), lambda i, *_: (i, 0)),
            pl.BlockSpec((SUBLANES, ROW_TILE), lambda i, *_: (0, i)),
        ],
        out_specs=pl.BlockSpec(memory_space=pl.ANY),
        scratch_shapes=[
            pltpu.VMEM((RING, SORT_ROWS, D_MODEL), _BF16),
            pltpu.VMEM((EXPERT_TILE // 2, D_MODEL), _BF16),
            pltpu.SemaphoreType.DMA((RING,)),
            pltpu.SemaphoreType.DMA(()),
        ],
    )
    return pl.pallas_call(
        _dispatch_kernel,
        grid_spec=grid_spec,
        out_shape=jax.ShapeDtypeStruct((n_rows, D_MODEL), _BF16),
        compiler_params=pltpu.CompilerParams(
            dimension_semantics=("arbitrary",), vmem_limit_bytes=VMEM_LIMIT),
        name="dispatch",
    )(*[plan[k] for k in PIECE_LIST_KEYS], plan["rows"], plan["pad_start"], plan["pad_len"], h2, info_t)


def _experts_kernel(te_ref, nt_ref, order_ref, next_ref, xs_ref, wg_hbm, wu_hbm, wd_hbm, ys_ref,
                    wg_f, wu_f, wd_f, wg_s, wu_s, wd_s, sems):
    i = pl.program_id(0)
    in_use = i < nt_ref[0]
    slot = order_ref[i] % 2
    prev = te_ref[jnp.maximum(i - 1, 0)]
    changed = jnp.logical_or(i == 0, te_ref[i] != prev)

    def fetch(expert, s):
        return [pltpu.make_async_copy(hbm.at[expert], stage.at[s], sems.at[s, j])
                for j, (hbm, stage) in enumerate(((wg_hbm, wg_f), (wu_hbm, wu_f), (wd_hbm, wd_f)))]

    @pl.when(i == 0)
    def _():
        for copy in fetch(te_ref[0], 0):
            copy.start()

    @pl.when(jnp.logical_and(in_use, changed))
    def _():
        for copy in fetch(te_ref[i], slot):
            copy.wait()
        wg_s[...] = wg_f[slot].astype(_BF16)
        wu_s[...] = wu_f[slot].astype(_BF16)
        wd_s[...] = wd_f[slot].astype(_BF16)

        @pl.when(next_ref[i] != te_ref[i])
        def _():
            for copy in fetch(next_ref[i], 1 - slot):
                copy.start()

    @pl.when(in_use)
    def _():
        subs = [slice(s * EXPERT_SUB, (s + 1) * EXPERT_SUB) for s in range(EXPERT_TILE // EXPERT_SUB)]
        gate = [_dot(xs_ref[rows, :], wg_s[...]) for rows in subs]
        up = [_dot(xs_ref[rows, :], wu_s[...]) for rows in subs]
        act = [((g * _sigmoid(g)) * u).astype(_BF16) for g, u in zip(gate, up)]
        for rows, a in zip(subs, act):
            ys_ref[rows, :] = _dot(a, wd_s[...]).astype(_BF16)


def _experts(plan, xs, w_gate, w_up, w_down):
    n_rows = xs.shape[0]
    last_used = lambda i, nt: jnp.minimum(i, nt[0] - 1)
    grid_spec = pltpu.PrefetchScalarGridSpec(
        num_scalar_prefetch=4,
        grid=(n_rows // EXPERT_TILE,),
        in_specs=[
            pl.BlockSpec((EXPERT_TILE, D_MODEL), lambda i, te, nt, *_: (last_used(i, nt), 0)),
            pl.BlockSpec(memory_space=pl.ANY),
            pl.BlockSpec(memory_space=pl.ANY),
            pl.BlockSpec(memory_space=pl.ANY),
        ],
        out_specs=pl.BlockSpec((EXPERT_TILE, D_MODEL), lambda i, te, nt, *_: (last_used(i, nt), 0)),
        scratch_shapes=[
            pltpu.VMEM((2, D_MODEL, MOE_D_FF), _F32),
            pltpu.VMEM((2, D_MODEL, MOE_D_FF), _F32),
            pltpu.VMEM((2, MOE_D_FF, D_MODEL), _F32),
            pltpu.VMEM((D_MODEL, MOE_D_FF), _BF16),
            pltpu.VMEM((D_MODEL, MOE_D_FF), _BF16),
            pltpu.VMEM((MOE_D_FF, D_MODEL), _BF16),
            pltpu.SemaphoreType.DMA((2, 3)),
        ],
    )
    return pl.pallas_call(
        _experts_kernel,
        grid_spec=grid_spec,
        out_shape=jax.ShapeDtypeStruct((n_rows, D_MODEL), _BF16),
        input_output_aliases={4: 0},
        compiler_params=pltpu.CompilerParams(
            dimension_semantics=("arbitrary",), vmem_limit_bytes=VMEM_LIMIT),
        name="experts",
    )(plan["tile_expert"], plan["tiles_used"], plan["tile_order"], plan["tile_next"], xs, w_gate, w_up, w_down)


def _combine_kernel(*refs):
    pieces = refs[:len(PIECE_LIST_KEYS)]
    rows_ref, x1_ref, info_ref, ys_hbm, out_ref, block_ref, sems = refs[len(PIECE_LIST_KEYS):]
    tile = pl.program_id(0)
    last = pl.num_programs(0) - 1
    slot = tile % RING
    tm = ROW_TILE

    def copy_maker(t):
        s = t % RING

        def make_copy(off, dst, piece):
            return pltpu.make_async_copy(
                ys_hbm.at[pl.ds(dst, piece)], block_ref.at[s, pl.ds(off, piece)], sems.at[s])

        return make_copy

    @pl.when(tile == 0)
    def _():
        block_ref[...] = jnp.zeros_like(block_ref)
        for ahead in range(RING - 1):
            pl.when(ahead <= last)(lambda ahead=ahead: _start_run_copies(pieces, ahead, copy_maker(ahead)))

    @pl.when(tile + (RING - 1) <= last)
    def _():
        _start_run_copies(pieces, tile + (RING - 1), copy_maker(tile + (RING - 1)))

    _wait_rows(rows_ref[tile], copy_maker(tile))

    info = info_ref[...]

    def weights_of(lo, hi):
        place = (_iota((tm, hi - lo), 1) + lo).astype(_F32)
        return (jnp.where(place == info[:, 4:5], info[:, 2:3], 0.0)
                + jnp.where(place == info[:, 5:6], info[:, 3:4], 0.0)).astype(_BF16)

    weights = weights_of(0, SORT_COMMON)
    for c in range(0, D_MODEL, 256):
        out_ref[:, c:c + 256] = x1_ref[:, c:c + 256] + _dot(weights, block_ref[slot, 0:SORT_COMMON, c:c + 256])

    @pl.when(rows_ref[tile] > SORT_COMMON)
    def _():
        tail = weights_of(SORT_COMMON, SORT_ROWS)
        for c in range(0, D_MODEL, 256):
            out_ref[:, c:c + 256] += _dot(tail, block_ref[slot, SORT_COMMON:SORT_ROWS, c:c + 256])


def _combine(plan, x1, info, ys):
    n = x1.shape[0]
    grid_spec = pltpu.PrefetchScalarGridSpec(
        num_scalar_prefetch=len(PIECE_LIST_KEYS) + 1,
        grid=(n // ROW_TILE,),
        in_specs=[
            pl.BlockSpec((ROW_TILE, D_MODEL), lambda i, *_: (i, 0)),
            pl.BlockSpec((ROW_TILE, LANES), lambda i, *_: (i, 0)),
            pl.BlockSpec(memory_space=pl.ANY),
        ],
        out_specs=pl.BlockSpec((ROW_TILE, D_MODEL), lambda i, *_: (i, 0)),
        scratch_shapes=[pltpu.VMEM((RING, SORT_ROWS, D_MODEL), _BF16), pltpu.SemaphoreType.DMA((RING,))],
    )
    return pl.pallas_call(
        _combine_kernel,
        grid_spec=grid_spec,
        out_shape=jax.ShapeDtypeStruct((n, D_MODEL), _F32),
        compiler_params=pltpu.CompilerParams(
            dimension_semantics=("arbitrary",), vmem_limit_bytes=VMEM_LIMIT),
        name="combine",
    )(*[plan[k] for k in PIECE_LIST_KEYS], plan["rows"], x1, info, ys)


def _rotary_tables(seq):
    half = RET_QK_DIM // 2
    inv = ROPE_BASE ** (-jnp.arange(half, dtype=_F32) / half)
    hi = (jnp.arange(seq // RET_CHUNK, dtype=jnp.int32) * RET_CHUNK).astype(_F32)[:, None] * inv[None, :]
    lo = jnp.arange(RET_CHUNK, dtype=jnp.int32).astype(_F32)[:, None] * inv[None, :]
    cos_hi, sin_hi = jnp.cos(hi)[:, None, :], jnp.sin(hi)[:, None, :]
    cos_lo, sin_lo = jnp.cos(lo)[None, :, :], jnp.sin(lo)[None, :, :]
    cos = (cos_hi * cos_lo - sin_hi * sin_lo).reshape(seq, half)
    sin = (sin_hi * cos_lo + cos_hi * sin_lo).reshape(seq, half)
    return jnp.concatenate([cos, cos], axis=1), jnp.concatenate([-sin, sin], axis=1)


def _expert_row_bound(n_tokens):
    rows = (2 * n_tokens + (n_tokens // ROW_TILE) * MOE_N_EXPERTS * (RUN_ALIGN - 1)
            + MOE_N_EXPERTS * (EXPERT_TILE - RUN_ALIGN))
    return -(-rows // EXPERT_TILE) * EXPERT_TILE


def _routing_plan(counts_out, n_rows):
    te = EXPERT_TILE
    n_tok_tiles = counts_out.shape[0] // SUBLANES
    counts = counts_out.reshape(n_tok_tiles, SUBLANES, LANES)[:, 0, :MOE_N_EXPERTS].astype(jnp.int32)
    run = (counts + RUN_ALIGN - 1) // RUN_ALIGN * RUN_ALIGN
    off = jnp.cumsum(run, axis=1) - run
    total = jnp.sum(run, axis=0)
    padded = (total + te - 1) // te * te
    pends = jnp.cumsum(padded)
    pstarts = pends - padded
    dst = pstarts[None, :] + jnp.cumsum(run, axis=0) - run
    tile_start = jnp.arange(n_rows // te, dtype=jnp.int32) * te
    tile_expert = jnp.sum((tile_start[:, None] >= pends[None, :]).astype(jnp.int32), axis=1)
    tile_expert = jnp.minimum(tile_expert, MOE_N_EXPERTS - 1)
    i32 = lambda v: v.astype(jnp.int32)

    experts = jnp.arange(MOE_N_EXPERTS, dtype=jnp.int32)
    has_rows = total > 0
    rank = jnp.cumsum(has_rows.astype(jnp.int32)) - 1
    later = jnp.where((experts[None, :] > experts[:, None]) & has_rows[None, :], experts[None, :], MOE_N_EXPERTS)
    following = jnp.min(later, axis=1)
    following = jnp.where(following == MOE_N_EXPERTS, experts, following)
    of_tile = tile_expert[:, None] == experts[None, :]
    tile_order = jnp.sum(jnp.where(of_tile, rank[None, :], 0), axis=1)
    tile_next = jnp.sum(jnp.where(of_tile, following[None, :], 0), axis=1)

    largest = PIECE_CLASSES[0]
    n_big = run // largest
    rest = run - largest * n_big
    big_first = jnp.cumsum(n_big, axis=1) - n_big
    k_big = jnp.arange(_piece_cap(largest), dtype=jnp.int32)[None, :, None]
    owns_big = (k_big >= big_first[:, None, :]) & (k_big < (big_first + n_big)[:, None, :])
    pick_big = lambda v: jnp.sum(jnp.where(owns_big, (v - largest * big_first)[:, None, :], 0), axis=-1)
    lists = {f"n{largest}": i32(jnp.sum(n_big, axis=1)),
             f"off{largest}": i32((pick_big(off) + largest * k_big[..., 0]).reshape(-1)),
             f"dst{largest}": i32((pick_big(dst) + largest * k_big[..., 0]).reshape(-1))}
    k_small = jnp.arange(MOE_N_EXPERTS, dtype=jnp.int32)[None, :, None]
    for size in PIECE_CLASSES[1:]:
        has = (rest == size).astype(jnp.int32)
        index = jnp.cumsum(has, axis=1) - has
        owns = (has[:, None, :] == 1) & (index[:, None, :] == k_small)
        pick = lambda v, owns=owns: jnp.sum(jnp.where(owns, (v + largest * n_big)[:, None, :], 0), axis=-1)
        lists.update({f"n{size}": i32(jnp.sum(has, axis=1)),
                      f"off{size}": i32(pick(off).reshape(-1)), f"dst{size}": i32(pick(dst).reshape(-1))})
    return {
        **lists,
        "rows": i32(jnp.sum(run, axis=1)),
        "pad_start": i32(jnp.concatenate([pstarts + total, pends[-1:]])), "pad_len": i32(padded - total),
        "tile_expert": i32(tile_expert), "tiles_used": i32(pends[-1:] // te),
        "tile_order": i32(tile_order), "tile_next": i32(tile_next),
    }


def kernel(x, norm_mix_g, w_in, ret_gn_g, w_ret_o, q_norm_g, k_norm_g, sinks, w_swa_o, w_out, norm_ffn_g,
           w_router_group, b_router_group, w_router_expert, b_router_expert, w_gate, w_up, w_down):
    batch, seq, d = x.shape
    n = batch * seq
    assert d == D_MODEL and seq % SEQ_TILE == 0 and n % ROW_TILE == 0
    cos_tab, sin_tab = _rotary_tables(seq)
    n_rows = _expert_row_bound(n)
    for l in range(w_in.shape[0]):
        x2d = x.reshape(n, d)
        ret_proj, swa_proj, gates = _inproj(x2d, norm_mix_g[l][None, :], w_in[l].astype(_BF16), cos_tab, sin_tab,
                                            ret_gn_g[l].reshape(1, RET_V_W))
        retg, attn = _mixers(ret_proj, swa_proj, sinks[l], jnp.tile(q_norm_g[l], 2)[None, :],
                             jnp.tile(k_norm_g[l], 2)[None, :], batch, seq)
        pad = LANES - MOE_N_EXPERTS - MOE_GROUPS
        w_router_t = jnp.concatenate(
            [w_router_expert[l].T, w_router_group[l].T, jnp.zeros((pad, d), _F32)], axis=0).astype(_BF16)
        b_router = jnp.concatenate([b_router_expert[l], b_router_group[l], jnp.zeros((pad,), _F32)])
        b_router_t = jnp.broadcast_to(b_router[:, None], (LANES, LANES))
        x1, h2, info, info_t, counts = _outproj(
            x2d, retg, attn, gates, w_ret_o[l].astype(_BF16), w_swa_o[l].astype(_BF16), w_out[l].astype(_BF16),
            norm_ffn_g[l][None, :], w_router_t, b_router_t)
        plan = _routing_plan(counts, n_rows)
        xs = _dispatch(plan, h2, info_t, n_rows)
        ys = _experts(plan, xs, w_gate[l], w_up[l], w_down[l])
        x = _combine(plan, x1, info, ys).reshape(batch, seq, d)
    return x
```

```python
import math

import numpy as np
import jax
import jax.numpy as jnp
from jax import lax
from jax.experimental import pallas as pl
from jax.experimental.pallas import tpu as pltpu

D_MODEL = 1024
RET_HEADS = 4
RET_QK_DIM = 128
RET_V_DIM = 256
RET_CHUNK = 128
ROPE_BASE = 10000.0
SWA_Q_HEADS = 8
SWA_KV_HEADS = 2
SWA_HEAD_DIM = 64
SWA_WINDOW = 128
MOE_GROUPS = 4
MOE_EXPERTS_PER_GROUP = 8
MOE_D_FF = 512
NORM_EPS = 1e-6

RET_QK_W = RET_HEADS * RET_QK_DIM
RET_V_W = RET_HEADS * RET_V_DIM
SWA_Q_W = SWA_Q_HEADS * SWA_HEAD_DIM
SWA_KV_W = SWA_KV_HEADS * SWA_HEAD_DIM
SWA_GROUP = SWA_Q_HEADS // SWA_KV_HEADS
MOE_N_EXPERTS = MOE_GROUPS * MOE_EXPERTS_PER_GROUP

RET_COLS = 2 * RET_QK_W + 2 * RET_V_W
SWA_COLS = SWA_Q_W + 2 * SWA_KV_W
GATE_COLS = 2 * D_MODEL
IN_WIDTH = RET_COLS + SWA_COLS + GATE_COLS

LANES = 128
SUBLANES = 8
ROUTER_GROUP_LANE = 32
NEG_BIG = -1e30

ROW_TILE = 512
SEQ_TILE = 512
EXPERT_TILE = 512
OUTPROJ_SUB = 128
INPROJ_SUB = 256
EXPERT_SUB = 256
RUN_ALIGN = 16
SORT_ROWS = 2 * ROW_TILE + MOE_N_EXPERTS * RUN_ALIGN
SORT_ROWS_POW2 = 1 << SORT_ROWS.bit_length()
SORT_COMMON = 2 * ROW_TILE + MOE_N_EXPERTS * RUN_ALIGN * 9 // 16
RING = 3
VMEM_LIMIT = 56 * 1024 * 1024

RET_LOG_DECAY = tuple(float(np.log1p(-np.exp2(-5.0 - h))) for h in range(RET_HEADS))

_BF16 = jnp.bfloat16
_F32 = jnp.float32


def _dot(a, b):
    return jnp.dot(a, b, preferred_element_type=_F32)


def _dot_nt(a, b):
    return lax.dot_general(a, b, (((1,), (1,)), ((), ())), preferred_element_type=_F32)


def _sigmoid(v):
    return 1.0 / (1.0 + jnp.exp(-v))


def _split_bf16(v):
    hi = v.astype(_BF16)
    return hi, (v - hi.astype(_F32)).astype(_BF16)


def _iota(shape, dim):
    return lax.broadcasted_iota(jnp.int32, shape, dim)


def _inproj_kernel(x_ref, g_ref, w_ref, cos_ref, sin_ref, gn_ref, ret_ref, swa_ref, gate_ref):
    subs = [slice(s * INPROJ_SUB, (s + 1) * INPROJ_SUB) for s in range(ROW_TILE // INPROJ_SUB)]

    def normed(rows):
        x = x_ref[rows, :]
        r = lax.rsqrt(jnp.mean(x * x, axis=-1, keepdims=True) + NORM_EPS)
        return ((x * r) * g_ref[...]).astype(_BF16)
    chunk_pos = (_iota((INPROJ_SUB, RET_QK_DIM), 0) % RET_CHUNK + 1).astype(_F32)

    def rotary_decay(base, sign, scale):
        def epilogue(rows, raw):
            cos = cos_ref[rows, :]
            sin = sin_ref[rows, :]
            for hd in range(RET_HEADS):
                v = raw[:, hd * RET_QK_DIM:(hd + 1) * RET_QK_DIM]
                rot = v * cos + pltpu.roll(v, RET_QK_DIM // 2, 1) * sin
                decay = jnp.exp((sign * RET_LOG_DECAY[hd]) * chunk_pos) * scale
                ret_ref[rows, base + hd * RET_QK_DIM:base + (hd + 1) * RET_QK_DIM] = (rot * decay).astype(_BF16)
        return epilogue

    def store(ref, lo, fn):
        def epilogue(rows, raw):
            ref[rows, lo:lo + raw.shape[1]] = fn(raw).astype(_BF16)
        return epilogue

    v_base = 2 * RET_QK_W
    g_base = v_base + RET_V_W
    gate_base = RET_COLS + SWA_COLS
    jobs = [(0, RET_QK_W, rotary_decay(0, 1.0, 1.0)),
            (RET_QK_W, RET_QK_W, rotary_decay(RET_QK_W, -1.0, RET_QK_DIM ** -0.5))]
    jobs += [(v_base + c, 512, store(ret_ref, v_base + c, lambda raw: raw)) for c in range(0, RET_V_W, 512)]
    jobs += [(g_base + c, 512, store(ret_ref, g_base + c,
                                     lambda raw, c=c: (raw * _sigmoid(raw)) * gn_ref[:, c:c + 512]))
             for c in range(0, RET_V_W, 512)]
    jobs += [(RET_COLS + c, 256, store(swa_ref, c, lambda raw: raw)) for c in range(0, SWA_COLS, 256)]
    jobs += [(gate_base + c, 512, store(gate_ref, c, _sigmoid)) for c in range(0, GATE_COLS, 512)]
    hs = [normed(rows) for rows in subs]
    for lo, width, epilogue in jobs:
        raws = [_dot(h, w_ref[:, lo:lo + width]) for h in hs]
        for rows, raw in zip(subs, raws):
            epilogue(rows, raw)


def _inproj(x2d, g, w_in_bf16, cos_tab, sin_tab, gn_row):
    n = x2d.shape[0]
    grid = (n // ROW_TILE,)
    seq_tiles = cos_tab.shape[0] // ROW_TILE
    return pl.pallas_call(
        _inproj_kernel,
        grid=grid,
        in_specs=[
            pl.BlockSpec((ROW_TILE, D_MODEL), lambda i: (i, 0)),
            pl.BlockSpec((1, D_MODEL), lambda i: (0, 0)),
            pl.BlockSpec((D_MODEL, IN_WIDTH), lambda i: (0, 0), pipeline_mode=pl.Buffered(1)),
            pl.BlockSpec((ROW_TILE, RET_QK_DIM), lambda i: (i % seq_tiles, 0)),
            pl.BlockSpec((ROW_TILE, RET_QK_DIM), lambda i: (i % seq_tiles, 0)),
            pl.BlockSpec((1, RET_V_W), lambda i: (0, 0)),
        ],
        out_specs=[
            pl.BlockSpec((ROW_TILE, RET_COLS), lambda i: (i, 0)),
            pl.BlockSpec((ROW_TILE, SWA_COLS), lambda i: (i, 0)),
            pl.BlockSpec((ROW_TILE, GATE_COLS), lambda i: (i, 0)),
        ],
        out_shape=[
            jax.ShapeDtypeStruct((n, RET_COLS), _BF16),
            jax.ShapeDtypeStruct((n, SWA_COLS), _BF16),
            jax.ShapeDtypeStruct((n, GATE_COLS), _BF16),
        ],
        compiler_params=pltpu.CompilerParams(
            dimension_semantics=("arbitrary",), vmem_limit_bytes=VMEM_LIMIT),
        name="inproj",
    )(x2d, g, w_in_bf16, cos_tab, sin_tab, gn_row)


def _retention_stages(q_ref, k_ref, v_ref, g_ref, out_ref, state_ref):
    c = RET_CHUNK
    causal = _iota((c, c), 0) >= _iota((c, c), 1)

    for h in range(RET_HEADS):
        chunk_decay = math.exp(RET_LOG_DECAY[h] * c)
        qs = slice(h * RET_QK_DIM, (h + 1) * RET_QK_DIM)
        vs = slice(h * RET_V_DIM, (h + 1) * RET_V_DIM)
        chunks = [slice(ci * c, (ci + 1) * c) for ci in range(SEQ_TILE // c)]
        scores = [jnp.where(causal, _dot_nt(q_ref[rows, qs], k_ref[rows, qs]), 0.0).astype(_BF16) for rows in chunks]
        kvs = [_dot(k_ref[rows, qs].astype(_F32).T.astype(_BF16), v_ref[rows, vs]) for rows in chunks]
        yield
        state = state_ref[h]
        states = []
        for kv in kvs:
            states.append(state.astype(_BF16))
            state = chunk_decay * (state + kv)
        state_ref[h] = state
        rets = [_dot(jnp.concatenate([s, q_ref[rows, qs]], axis=1), jnp.concatenate([v_ref[rows, vs], st], axis=0))
                for rows, s, st in zip(chunks, scores, states)]
        yield
        for rows, ret in zip(chunks, rets):
            mu = jnp.mean(ret, axis=-1, keepdims=True)
            dev = ret - mu
            var = jnp.mean(dev * dev, axis=-1, keepdims=True)
            out_ref[rows, vs] = ((dev * lax.rsqrt(var + NORM_EPS)) * g_ref[rows, vs].astype(_F32)).astype(_BF16)
        yield


def _swa_stages(first, sinks_ref, q_ref, kc_ref, vc_ref, kp_ref, vp_ref, qg_ref, kg_ref, out_ref):
    w = SWA_WINDOW
    d = SWA_HEAD_DIM
    n_blk = SEQ_TILE // w
    pair = 2 * d

    k_both = jnp.concatenate([kp_ref[...], kc_ref[...]], axis=0).astype(_F32)
    same_half = (_iota((pair, pair), 0) // d == _iota((pair, pair), 1) // d).astype(_BF16)
    k_hi, k_lo = _split_bf16(k_both * k_both)
    k_ssq = _dot(k_hi, same_half) + _dot(k_lo, same_half)
    k_n = k_both * lax.rsqrt(k_ssq * (1.0 / d) + NORM_EPS) * (kg_ref[...] * d ** -0.5)
    v_both = jnp.concatenate([vp_ref[...], vc_ref[...]], axis=0).astype(_F32)
    v_t = [v_both[b * w:(b + 1) * w].T for b in range(n_blk + 1)]

    key = _iota((2 * w, SWA_GROUP * w), 0)
    qry = _iota((2 * w, SWA_GROUP * w), 1) % w
    band = (key > qry) & (key <= qry + w)
    band_first = band & jnp.logical_or(key >= w, jnp.logical_not(first))
    head_of_lane = _iota((1, SWA_GROUP * w), 1) // w
    half_of_lane = _iota((1, pair), 1) // d
    half_rows = (_iota((SUBLANES, pair), 0) == _iota((SUBLANES, pair), 1) // d).astype(_BF16)
    ones_rows = jnp.ones((2 * SUBLANES, 2 * w), _BF16)
    q_gain = qg_ref[...]
    yield

    for kk in range(SWA_KV_HEADS):
        k_native = jnp.where(half_of_lane == kk, k_n, 0.0)
        k_moved = pltpu.roll(k_native, d, 1)
        k_even, k_odd = (k_native, k_moved) if kk == 0 else (k_moved, k_native)
        k_even = k_even.astype(_BF16)
        k_odd = k_odd.astype(_BF16)
        sink_row = jnp.zeros((1, SWA_GROUP * w), _F32)
        for g in range(SWA_GROUP):
            sink_row = jnp.where(head_of_lane == g, sinks_ref[kk * SWA_GROUP + g], sink_row)
        blocks = range(n_blk)
        pairs = range(SWA_GROUP // 2)
        q_pairs = [[q_ref[j * w:(j + 1) * w, (kk * SWA_GROUP + 2 * p) * d:(kk * SWA_GROUP + 2 * p + 2) * d].astype(_F32)
                    for p in pairs] for j in blocks]
        q_splits = [[_split_bf16(q * q) for q in qs] for qs in q_pairs]
        q_ssq = [[_dot_nt(half_rows, hi) + _dot_nt(half_rows, lo) for hi, lo in sp] for sp in q_splits]
        q_rs = [[lax.rsqrt(s * (1.0 / d) + NORM_EPS) for s in ss] for ss in q_ssq]
        q_gs = [[(q * q_gain).astype(_BF16) for q in qs] for qs in q_pairs]
        yield
        raw_even = [[_dot_nt(k_even[j * w:(j + 2) * w], q_gs[j][p]) for p in pairs] for j in blocks]
        raw_odd = [[_dot_nt(k_odd[j * w:(j + 2) * w], q_gs[j][p]) for p in pairs] for j in blocks]
        yield
        score_blocks = []
        for j in blocks:
            raws = []
            for p in pairs:
                raws.append(raw_even[j][p] * q_rs[j][p][0:1, :])
                raws.append(raw_odd[j][p] * q_rs[j][p][1:2, :])
            s_t = jnp.concatenate(raws, axis=1)
            score_blocks.append(jnp.where(band_first if j == 0 else band, s_t, NEG_BIG))
        yield
        maxes = [jnp.maximum(jnp.max(s_t, axis=0, keepdims=True), sink_row) for s_t in score_blocks]
        probs = [jnp.exp(s_t - m).astype(_BF16) for s_t, m in zip(score_blocks, maxes)]
        yield
        outs = []
        for j in blocks:
            v_ext = jnp.concatenate([v_t[j][kk * d:(kk + 1) * d], v_t[j + 1][kk * d:(kk + 1) * d]], axis=1)
            outs.append(_dot(jnp.concatenate([v_ext.astype(_BF16), ones_rows], axis=0), probs[j]))
        yield
        for j in blocks:
            rows = slice(j * w, (j + 1) * w)
            denom = outs[j][d:d + 1, :] + jnp.exp(sink_row - maxes[j])
            o_t = outs[j][0:d, :] * (1.0 / denom)
            for p in range(SWA_GROUP // 2):
                c0 = (kk * SWA_GROUP + 2 * p) * d
                both = jnp.concatenate([o_t[:, 2 * p * w:(2 * p + 1) * w], o_t[:, (2 * p + 1) * w:(2 * p + 2) * w]], axis=0)
                out_ref[rows, c0:c0 + pair] = both.T.astype(_BF16)
        yield


def _mixers_kernel(sinks_ref, rq_ref, rk_ref, rv_ref, rg_ref, sq_ref, kc_ref, vc_ref, kp_ref, vp_ref, qg_ref, kg_ref,
                   retg_ref, attn_ref, state_ref):
    first = pl.program_id(1) == 0

    @pl.when(first)
    def _():
        state_ref[...] = jnp.zeros_like(state_ref)

    streams = [_retention_stages(rq_ref, rk_ref, rv_ref, rg_ref, retg_ref, state_ref),
               _swa_stages(first, sinks_ref, sq_ref, kc_ref, vc_ref, kp_ref, vp_ref, qg_ref, kg_ref, attn_ref)]
    while streams:
        streams = [s for s in streams if next(s, StopIteration) is not StopIteration]


def _mixers(ret_proj, swa_proj, sinks, q_gain2, k_gain2, batch, seq):
    n = ret_proj.shape[0]
    steps = seq // SEQ_TILE
    blocks_per_step = SEQ_TILE // SWA_WINDOW
    rowmap = lambda b, s: b * steps + s
    prevmap = lambda b, s: jnp.maximum((b * steps + s) * blocks_per_step - 1, 0)
    k_col = SWA_Q_W // SWA_KV_W
    return pl.pallas_call(
        _mixers_kernel,
        grid=(batch, steps),
        in_specs=[
            pl.BlockSpec(memory_space=pltpu.SMEM),
            pl.BlockSpec((SEQ_TILE, RET_QK_W), lambda b, s: (rowmap(b, s), 0)),
            pl.BlockSpec((SEQ_TILE, RET_QK_W), lambda b, s: (rowmap(b, s), 1)),
            pl.BlockSpec((SEQ_TILE, RET_V_W), lambda b, s: (rowmap(b, s), 1)),
            pl.BlockSpec((SEQ_TILE, RET_V_W), lambda b, s: (rowmap(b, s), 2)),
            pl.BlockSpec((SEQ_TILE, SWA_Q_W), lambda b, s: (rowmap(b, s), 0)),
            pl.BlockSpec((SEQ_TILE, SWA_KV_W), lambda b, s: (rowmap(b, s), k_col)),
            pl.BlockSpec((SEQ_TILE, SWA_KV_W), lambda b, s: (rowmap(b, s), k_col + 1)),
            pl.BlockSpec((SWA_WINDOW, SWA_KV_W), lambda b, s: (prevmap(b, s), k_col)),
            pl.BlockSpec((SWA_WINDOW, SWA_KV_W), lambda b, s: (prevmap(b, s), k_col + 1)),
            pl.BlockSpec((1, 2 * SWA_HEAD_DIM), lambda b, s: (0, 0)),
            pl.BlockSpec((1, 2 * SWA_HEAD_DIM), lambda b, s: (0, 0)),
        ],
        out_specs=[
            pl.BlockSpec((SEQ_TILE, RET_V_W), lambda b, s: (rowmap(b, s), 0)),
            pl.BlockSpec((SEQ_TILE, SWA_Q_W), lambda b, s: (rowmap(b, s), 0)),
        ],
        out_shape=[
            jax.ShapeDtypeStruct((n, RET_V_W), _BF16),
            jax.ShapeDtypeStruct((n, SWA_Q_W), _BF16),
        ],
        scratch_shapes=[pltpu.VMEM((RET_HEADS, RET_QK_DIM, RET_V_DIM), _F32)],
        compiler_params=pltpu.CompilerParams(
            dimension_semantics=("arbitrary", "arbitrary"), vmem_limit_bytes=VMEM_LIMIT),
        name="mixers",
    )(sinks, ret_proj, ret_proj, ret_proj, ret_proj, swa_proj, swa_proj, swa_proj, swa_proj, swa_proj,
      q_gain2, k_gain2)


def _outproj_kernel(x_ref, retg_ref, attn_ref, ga_ref, gb_ref, wro_ref, wso_ref, wout_ref, g2_ref,
                    wrt_ref, brt_ref, x1_ref, h2_ref, info_ref, info_t_ref, counts_ref, logits_ref):
    tm = ROW_TILE
    step = pl.program_id(0)
    cur = step % 2

    @pl.when(step == 0)
    def _():
        logits_ref[...] = jnp.zeros_like(logits_ref)

    logits = logits_ref[1 - cur]
    row = _iota((LANES, tm), 0)
    row_f = row.astype(_F32)
    is_group = (row >= ROUTER_GROUP_LANE) & (row < ROUTER_GROUP_LANE + MOE_GROUPS)

    subs = [slice(s * OUTPROJ_SUB, (s + 1) * OUTPROJ_SUB) for s in range(tm // OUTPROJ_SUB)]
    y_a = [_dot(retg_ref[rows, :], wro_ref[...]) for rows in subs]

    gl = jnp.where(is_group, logits, NEG_BIG)
    g_max = jnp.max(gl, axis=0, keepdims=True)
    g_prob = 1.0 / jnp.sum(jnp.exp(gl - g_max), axis=0, keepdims=True)
    g_idx = jnp.min(jnp.where(gl == g_max, row_f - ROUTER_GROUP_LANE, float(LANES)), axis=0, keepdims=True)

    y_b = [_dot(attn_ref[rows, :], wso_ref[...]) for rows in subs]

    e_lo = g_idx * MOE_EXPERTS_PER_GROUP
    in_group = (row_f >= e_lo) & (row_f < e_lo + MOE_EXPERTS_PER_GROUP)
    el = jnp.where(in_group, logits, NEG_BIG)
    t1 = jnp.max(el, axis=0, keepdims=True)
    i1 = jnp.min(jnp.where(el == t1, row_f, float(LANES)), axis=0, keepdims=True)

    merged = [(ga_ref[rows, :].astype(_F32) * a + gb_ref[rows, :].astype(_F32) * b).astype(_BF16)
              for rows, a, b in zip(subs, y_a, y_b)]
    mixed = [_dot(m, wout_ref[...]) for m in merged]
    x1 = [x_ref[rows, :] + m for rows, m in zip(subs, mixed)]
    for rows, v in zip(subs, x1):
        x1_ref[rows, :] = v

    el2 = jnp.where(row_f == i1, NEG_BIG, el)
    t2 = jnp.max(el2, axis=0, keepdims=True)
    i2 = jnp.min(jnp.where(el2 == t2, row_f, float(LANES)), axis=0, keepdims=True)
    e21 = jnp.exp(t2 - t1)
    w1 = g_prob / (1.0 + e21)
    w2 = g_prob * e21 / (1.0 + e21)

    h2 = [((v * lax.rsqrt(jnp.mean(v * v, axis=-1, keepdims=True) + NORM_EPS)) * g2_ref[...]).astype(_BF16)
          for v in x1]
    for rows, v in zip(subs, h2):
        h2_ref[rows, :] = v

    sel1 = row_f == i1
    sel2 = row_f == i2
    onehot = jnp.where(sel1 | sel2, 1.0, 0.0).astype(_BF16)
    earlier = (_iota((tm, tm), 0) < _iota((tm, tm), 1)).astype(_BF16)
    before = _dot(onehot, earlier)
    counts = _dot(onehot, jnp.ones((tm, LANES), _BF16))

    for rows, v in zip(subs, h2):
        logits_ref[cur, :, rows] = _dot_nt(wrt_ref[...], v) + brt_ref[:, 0:1]

    run = jnp.floor((counts + (RUN_ALIGN - 1.0)) * (1.0 / RUN_ALIGN)) * RUN_ALIGN
    lower_experts = (_iota((LANES, LANES), 0) > _iota((LANES, LANES), 1)).astype(_BF16)
    run_off = _dot(lower_experts, run.astype(_BF16))
    place = before + jnp.concatenate([run_off] * (tm // LANES), axis=1)
    pos1 = jnp.sum(jnp.where(sel1, place, 0.0), axis=0, keepdims=True)
    pos2 = jnp.sum(jnp.where(sel2, place, 0.0), axis=0, keepdims=True)
    counts_ref[...] = counts.T[0:SUBLANES, :]

    info_t = jnp.concatenate([i1, i2, w1, w2, pos1, pos2, jnp.zeros((2, tm), _F32)], axis=0)
    info_t_ref[...] = info_t
    info_ref[...] = jnp.concatenate([info_t, jnp.zeros((LANES - SUBLANES, tm), _F32)], axis=0).T


def _outproj(x2d, retg, attn, gates, w_ret_o, w_swa_o, w_out, g2, w_router_t, b_router_t):
    n = x2d.shape[0]
    tm = ROW_TILE
    n_tiles = n // tm
    const = lambda s: (0, 0)
    proj = lambda s: jnp.minimum(s, n_tiles - 1)
    routed = lambda s: jnp.maximum(s - 1, 0)
    return pl.pallas_call(
        _outproj_kernel,
        grid=(n_tiles + 1,),
        in_specs=[
            pl.BlockSpec((tm, D_MODEL), lambda s: (proj(s), 0)),
            pl.BlockSpec((tm, RET_V_W), lambda s: (proj(s), 0)),
            pl.BlockSpec((tm, SWA_Q_W), lambda s: (proj(s), 0)),
            pl.BlockSpec((tm, D_MODEL), lambda s: (proj(s), 0)),
            pl.BlockSpec((tm, D_MODEL), lambda s: (proj(s), 1)),
            pl.BlockSpec((RET_V_W, D_MODEL), const),
            pl.BlockSpec((SWA_Q_W, D_MODEL), const),
            pl.BlockSpec((D_MODEL, D_MODEL), const),
            pl.BlockSpec((1, D_MODEL), const),
            pl.BlockSpec((LANES, D_MODEL), const),
            pl.BlockSpec((LANES, LANES), const),
        ],
        out_specs=[
            pl.BlockSpec((tm, D_MODEL), lambda s: (proj(s), 0)),
            pl.BlockSpec((tm, D_MODEL), lambda s: (proj(s), 0)),
            pl.BlockSpec((tm, LANES), lambda s: (routed(s), 0)),
            pl.BlockSpec((SUBLANES, tm), lambda s: (0, routed(s))),
            pl.BlockSpec((SUBLANES, LANES), lambda s: (routed(s), 0)),
        ],
        scratch_shapes=[pltpu.VMEM((2, LANES, tm), _F32)],
        out_shape=[
            jax.ShapeDtypeStruct((n, D_MODEL), _F32),
            jax.ShapeDtypeStruct((n, D_MODEL), _BF16),
            jax.ShapeDtypeStruct((n, LANES), _F32),
            jax.ShapeDtypeStruct((SUBLANES, n), _F32),
            jax.ShapeDtypeStruct((n // tm * SUBLANES, LANES), _F32),
        ],
        compiler_params=pltpu.CompilerParams(
            dimension_semantics=("arbitrary",), vmem_limit_bytes=VMEM_LIMIT),
        name="outproj",
    )(x2d, retg, attn, gates, gates, w_ret_o, w_swa_o, w_out, g2, w_router_t, b_router_t)


def _piece_sizes(largest):
    piece = largest
    while piece >= RUN_ALIGN:
        yield piece
        piece //= 2


PIECE_CLASSES = tuple(k * RUN_ALIGN for k in (4, 3, 2, 1))
PIECE_LIST_KEYS = tuple(f"{field}{size}" for size in PIECE_CLASSES for field in ("n", "off", "dst"))


def _piece_cap(size):
    return SORT_ROWS // size if size == PIECE_CLASSES[0] else MOE_N_EXPERTS


def _start_run_copies(pieces, tile, make_copy):
    for k, size in enumerate(PIECE_CLASSES):
        n_ref, off_ref, dst_ref = pieces[3 * k:3 * k + 3]

        def start(j, carry, off_ref=off_ref, dst_ref=dst_ref, size=size):
            i = tile * _piece_cap(size) + j
            make_copy(pl.multiple_of(off_ref[i], RUN_ALIGN), pl.multiple_of(dst_ref[i], RUN_ALIGN), size).start()
            return carry

        lax.fori_loop(0, n_ref[tile], start, 0)


def _wait_rows(rows, make_copy):
    for piece in _piece_sizes(SORT_ROWS_POW2 // 2):
        pl.when((rows & piece) != 0)(make_copy(0, 0, piece).wait)


def _zero_fill_padding(pad_start_ref, pad_len_ref, zeros_ref, xs_hbm, sem, wait):
    def per_expert(e, carry):
        pos = pad_start_ref[e]
        length = pad_len_ref[e]
        for piece in _piece_sizes(EXPERT_TILE // 2):
            copy = pltpu.make_async_copy(
                zeros_ref.at[pl.ds(0, piece)], xs_hbm.at[pl.ds(pl.multiple_of(pos, RUN_ALIGN), piece)], sem)
            pl.when((length & piece) != 0)(copy.wait if wait else copy.start)
            pos = pos + (length & piece)
        return carry

    lax.fori_loop(0, MOE_N_EXPERTS, per_expert, 0)

    tail_start = pad_start_ref[MOE_N_EXPERTS]
    n_rows = xs_hbm.shape[0]
    half = EXPERT_TILE // 2

    def per_half_tile(j, carry):
        pos = tail_start + j * half
        copy = pltpu.make_async_copy(
            zeros_ref, xs_hbm.at[pl.ds(pl.multiple_of(jnp.minimum(pos, n_rows - half), RUN_ALIGN), half)], sem)
        pl.when(pos < n_rows)(copy.wait if wait else copy.start)
        return carry

    lax.fori_loop(0, n_rows // half, per_half_tile, 0)


def _dispatch_kernel(*refs):
    pieces = refs[:len(PIECE_LIST_KEYS)]
    rows_ref, pad_start_ref, pad_len_ref, h2_ref, info_t_ref, xs_hbm, sorted_ref, zeros_ref, sems, zero_sem = (
        refs[len(PIECE_LIST_KEYS):])
    tile = pl.program_id(0)
    last = pl.num_programs(0) - 1
    slot = tile % RING
    tm = ROW_TILE

    @pl.when(tile == 0)
    def _():
        zeros_ref[...] = jnp.zeros_like(zeros_ref)
        _zero_fill_padding(pad_start_ref, pad_len_ref, zeros_ref, xs_hbm, zero_sem, wait=False)

    pos = info_t_ref[4:6, :].astype(jnp.int32)

    def sort_rows(lo, hi):
        place = _iota((hi - lo, tm), 0) + lo
        onehot = ((place == pos[0:1, :]) | (place == pos[1:2, :])).astype(_BF16)
        for c in range(0, D_MODEL, 256):
            sorted_ref[slot, lo:hi, c:c + 256] = _dot(onehot, h2_ref[:, c:c + 256]).astype(_BF16)

    sort_rows(0, SORT_COMMON)
    pl.when(rows_ref[tile] > SORT_COMMON)(lambda: sort_rows(SORT_COMMON, SORT_ROWS))

    def copy_maker(t):
        s = t % RING

        def make_copy(off, dst, piece):
            return pltpu.make_async_copy(
                sorted_ref.at[s, pl.ds(off, piece)], xs_hbm.at[pl.ds(dst, piece)], sems.at[s])

        return make_copy

    _start_run_copies(pieces, tile, copy_maker(tile))

    @pl.when(tile >= RING - 1)
    def _():
        _wait_rows(rows_ref[tile - (RING - 1)], copy_maker(tile - (RING - 1)))

    for back in range(RING - 1):
        @pl.when(jnp.logical_and(tile == last, tile >= back))
        def _(back=back):
            _wait_rows(rows_ref[tile - back], copy_maker(tile - back))

    @pl.when(tile == last)
    def _():
        _zero_fill_padding(pad_start_ref, pad_len_ref, zeros_ref, xs_hbm, zero_sem, wait=True)


def _dispatch(plan, h2, info_t, n_rows):
    n = h2.shape[0]
    grid_spec = pltpu.PrefetchScalarGridSpec(
        num_scalar_prefetch=len(PIECE_LIST_KEYS) + 3,
        grid=(n // ROW_TILE,),
        in_specs=[
            pl.BlockSpec((ROW_TILE, D_MODEL), lambda i, *_: (i, 0)),
            pl.BlockSpec((SUBLANES, ROW_TILE), lambda i, *_: (0, i)),
        ],
        out_specs=pl.BlockSpec(memory_space=pl.ANY),
        scratch_shapes=[
            pltpu.VMEM((RING, SORT_ROWS, D_MODEL), _BF16),
            pltpu.VMEM((EXPERT_TILE // 2, D_MODEL), _BF16),
            pltpu.SemaphoreType.DMA((RING,)),
            pltpu.SemaphoreType.DMA(()),
        ],
    )
    return pl.pallas_call(
        _dispatch_kernel,
        grid_spec=grid_spec,
        out_shape=jax.ShapeDtypeStruct((n_rows, D_MODEL), _BF16),
        compiler_params=pltpu.CompilerParams(
            dimension_semantics=("arbitrary",), vmem_limit_bytes=VMEM_LIMIT),
        name="dispatch",
    )(*[plan[k] for k in PIECE_LIST_KEYS], plan["rows"], plan["pad_start"], plan["pad_len"], h2, info_t)


def _experts_kernel(te_ref, nt_ref, order_ref, next_ref, xs_ref, wg_hbm, wu_hbm, wd_hbm, ys_ref,
                    wg_f, wu_f, wd_f, wg_s, wu_s, wd_s, sems):
    i = pl.program_id(0)
    in_use = i < nt_ref[0]
    slot = order_ref[i] % 2
    prev = te_ref[jnp.maximum(i - 1, 0)]
    changed = jnp.logical_or(i == 0, te_ref[i] != prev)

    def fetch(expert, s):
        return [pltpu.make_async_copy(hbm.at[expert], stage.at[s], sems.at[s, j])
                for j, (hbm, stage) in enumerate(((wg_hbm, wg_f), (wu_hbm, wu_f), (wd_hbm, wd_f)))]

    @pl.when(i == 0)
    def _():
        for copy in fetch(te_ref[0], 0):
            copy.start()

    @pl.when(jnp.logical_and(in_use, changed))
    def _():
        for copy in fetch(te_ref[i], slot):
            copy.wait()
        wg_s[...] = wg_f[slot].astype(_BF16)
        wu_s[...] = wu_f[slot].astype(_BF16)
        wd_s[...] = wd_f[slot].astype(_BF16)

        @pl.when(next_ref[i] != te_ref[i])
        def _():
            for copy in fetch(next_ref[i], 1 - slot):
                copy.start()

    @pl.when(in_use)
    def _():
        subs = [slice(s * EXPERT_SUB, (s + 1) * EXPERT_SUB) for s in range(EXPERT_TILE // EXPERT_SUB)]
        gate = [_dot(xs_ref[rows, :], wg_s[...]) for rows in subs]
        up = [_dot(xs_ref[rows, :], wu_s[...]) for rows in subs]
        act = [((g * _sigmoid(g)) * u).astype(_BF16) for g, u in zip(gate, up)]
        for rows, a in zip(subs, act):
            ys_ref[rows, :] = _dot(a, wd_s[...]).astype(_BF16)


def _experts(plan, xs, w_gate, w_up, w_down):
    n_rows = xs.shape[0]
    last_used = lambda i, nt: jnp.minimum(i, nt[0] - 1)
    grid_spec = pltpu.PrefetchScalarGridSpec(
        num_scalar_prefetch=4,
        grid=(n_rows // EXPERT_TILE,),
        in_specs=[
            pl.BlockSpec((EXPERT_TILE, D_MODEL), lambda i, te, nt, *_: (last_used(i, nt), 0)),
            pl.BlockSpec(memory_space=pl.ANY),
            pl.BlockSpec(memory_space=pl.ANY),
            pl.BlockSpec(memory_space=pl.ANY),
        ],
        out_specs=pl.BlockSpec((EXPERT_TILE, D_MODEL), lambda i, te, nt, *_: (last_used(i, nt), 0)),
        scratch_shapes=[
            pltpu.VMEM((2, D_MODEL, MOE_D_FF), _F32),
            pltpu.VMEM((2, D_MODEL, MOE_D_FF), _F32),
            pltpu.VMEM((2, MOE_D_FF, D_MODEL), _F32),
            pltpu.VMEM((D_MODEL, MOE_D_FF), _BF16),
            pltpu.VMEM((D_MODEL, MOE_D_FF), _BF16),
            pltpu.VMEM((MOE_D_FF, D_MODEL), _BF16),
            pltpu.SemaphoreType.DMA((2, 3)),
        ],
    )
    return pl.pallas_call(
        _experts_kernel,
        grid_spec=grid_spec,
        out_shape=jax.ShapeDtypeStruct((n_rows, D_MODEL), _BF16),
        input_output_aliases={4: 0},
        compiler_params=pltpu.CompilerParams(
            dimension_semantics=("arbitrary",), vmem_limit_bytes=VMEM_LIMIT),
        name="experts",
    )(plan["tile_expert"], plan["tiles_used"], plan["tile_order"], plan["tile_next"], xs, w_gate, w_up, w_down)


def _combine_kernel(*refs):
    pieces = refs[:len(PIECE_LIST_KEYS)]
    rows_ref, x1_ref, info_ref, ys_hbm, out_ref, block_ref, sems = refs[len(PIECE_LIST_KEYS):]
    tile = pl.program_id(0)
    last = pl.num_programs(0) - 1
    slot = tile % RING
    tm = ROW_TILE

    def copy_maker(t):
        s = t % RING

        def make_copy(off, dst, piece):
            return pltpu.make_async_copy(
                ys_hbm.at[pl.ds(dst, piece)], block_ref.at[s, pl.ds(off, piece)], sems.at[s])

        return make_copy

    @pl.when(tile == 0)
    def _():
        block_ref[...] = jnp.zeros_like(block_ref)
        for ahead in range(RING - 1):
            pl.when(ahead <= last)(lambda ahead=ahead: _start_run_copies(pieces, ahead, copy_maker(ahead)))

    @pl.when(tile + (RING - 1) <= last)
    def _():
        _start_run_copies(pieces, tile + (RING - 1), copy_maker(tile + (RING - 1)))

    _wait_rows(rows_ref[tile], copy_maker(tile))

    info = info_ref[...]

    def weights_of(lo, hi):
        place = (_iota((tm, hi - lo), 1) + lo).astype(_F32)
        return (jnp.where(place == info[:, 4:5], info[:, 2:3], 0.0)
                + jnp.where(place == info[:, 5:6], info[:, 3:4], 0.0)).astype(_BF16)

    weights = weights_of(0, SORT_COMMON)
    for c in range(0, D_MODEL, 256):
        out_ref[:, c:c + 256] = x1_ref[:, c:c + 256] + _dot(weights, block_ref[slot, 0:SORT_COMMON, c:c + 256])

    @pl.when(rows_ref[tile] > SORT_COMMON)
    def _():
        tail = weights_of(SORT_COMMON, SORT_ROWS)
        for c in range(0, D_MODEL, 256):
            out_ref[:, c:c + 256] += _dot(tail, block_ref[slot, SORT_COMMON:SORT_ROWS, c:c + 256])


def _combine(plan, x1, info, ys):
    n = x1.shape[0]
    grid_spec = pltpu.PrefetchScalarGridSpec(
        num_scalar_prefetch=len(PIECE_LIST_KEYS) + 1,
        grid=(n // ROW_TILE,),
        in_specs=[
            pl.BlockSpec((ROW_TILE, D_MODEL), lambda i, *_: (i, 0)),
            pl.BlockSpec((ROW_TILE, LANES), lambda i, *_: (i, 0)),
            pl.BlockSpec(memory_space=pl.ANY),
        ],
        out_specs=pl.BlockSpec((ROW_TILE, D_MODEL), lambda i, *_: (i, 0)),
        scratch_shapes=[pltpu.VMEM((RING, SORT_ROWS, D_MODEL), _BF16), pltpu.SemaphoreType.DMA((RING,))],
    )
    return pl.pallas_call(
        _combine_kernel,
        grid_spec=grid_spec,
        out_shape=jax.ShapeDtypeStruct((n, D_MODEL), _F32),
        compiler_params=pltpu.CompilerParams(
            dimension_semantics=("arbitrary",), vmem_limit_bytes=VMEM_LIMIT),
        name="combine",
    )(*[plan[k] for k in PIECE_LIST_KEYS], plan["rows"], x1, info, ys)


def _rotary_tables(seq):
    half = RET_QK_DIM // 2
    inv = ROPE_BASE ** (-jnp.arange(half, dtype=_F32) / half)
    hi = (jnp.arange(seq // RET_CHUNK, dtype=jnp.int32) * RET_CHUNK).astype(_F32)[:, None] * inv[None, :]
    lo = jnp.arange(RET_CHUNK, dtype=jnp.int32).astype(_F32)[:, None] * inv[None, :]
    cos_hi, sin_hi = jnp.cos(hi)[:, None, :], jnp.sin(hi)[:, None, :]
    cos_lo, sin_lo = jnp.cos(lo)[None, :, :], jnp.sin(lo)[None, :, :]
    cos = (cos_hi * cos_lo - sin_hi * sin_lo).reshape(seq, half)
    sin = (sin_hi * cos_lo + cos_hi * sin_lo).reshape(seq, half)
    return jnp.concatenate([cos, cos], axis=1), jnp.concatenate([-sin, sin], axis=1)


def _expert_row_bound(n_tokens):
    rows = (2 * n_tokens + (n_tokens // ROW_TILE) * MOE_N_EXPERTS * (RUN_ALIGN - 1)
            + MOE_N_EXPERTS * (EXPERT_TILE - RUN_ALIGN))
    return -(-rows // EXPERT_TILE) * EXPERT_TILE


def _routing_plan(counts_out, n_rows):
    te = EXPERT_TILE
    n_tok_tiles = counts_out.shape[0] // SUBLANES
    counts = counts_out.reshape(n_tok_tiles, SUBLANES, LANES)[:, 0, :MOE_N_EXPERTS].astype(jnp.int32)
    run = (counts + RUN_ALIGN - 1) // RUN_ALIGN * RUN_ALIGN
    off = jnp.cumsum(run, axis=1) - run
    total = jnp.sum(run, axis=0)
    padded = (total + te - 1) // te * te
    pends = jnp.cumsum(padded)
    pstarts = pends - padded
    dst = pstarts[None, :] + jnp.cumsum(run, axis=0) - run
    tile_start = jnp.arange(n_rows // te, dtype=jnp.int32) * te
    tile_expert = jnp.sum((tile_start[:, None] >= pends[None, :]).astype(jnp.int32), axis=1)
    tile_expert = jnp.minimum(tile_expert, MOE_N_EXPERTS - 1)
    i32 = lambda v: v.astype(jnp.int32)

    experts = jnp.arange(MOE_N_EXPERTS, dtype=jnp.int32)
    has_rows = total > 0
    rank = jnp.cumsum(has_rows.astype(jnp.int32)) - 1
    later = jnp.where((experts[None, :] > experts[:, None]) & has_rows[None, :], experts[None, :], MOE_N_EXPERTS)
    following = jnp.min(later, axis=1)
    following = jnp.where(following == MOE_N_EXPERTS, experts, following)
    of_tile = tile_expert[:, None] == experts[None, :]
    tile_order = jnp.sum(jnp.where(of_tile, rank[None, :], 0), axis=1)
    tile_next = jnp.sum(jnp.where(of_tile, following[None, :], 0), axis=1)

    largest = PIECE_CLASSES[0]
    n_big = run // largest
    rest = run - largest * n_big
    big_first = jnp.cumsum(n_big, axis=1) - n_big
    k_big = jnp.arange(_piece_cap(largest), dtype=jnp.int32)[None, :, None]
    owns_big = (k_big >= big_first[:, None, :]) & (k_big < (big_first + n_big)[:, None, :])
    pick_big = lambda v: jnp.sum(jnp.where(owns_big, (v - largest * big_first)[:, None, :], 0), axis=-1)
    lists = {f"n{largest}": i32(jnp.sum(n_big, axis=1)),
             f"off{largest}": i32((pick_big(off) + largest * k_big[..., 0]).reshape(-1)),
             f"dst{largest}": i32((pick_big(dst) + largest * k_big[..., 0]).reshape(-1))}
    k_small = jnp.arange(MOE_N_EXPERTS, dtype=jnp.int32)[None, :, None]
    for size in PIECE_CLASSES[1:]:
        has = (rest == size).astype(jnp.int32)
        index = jnp.cumsum(has, axis=1) - has
        owns = (has[:, None, :] == 1) & (index[:, None, :] == k_small)
        pick = lambda v, owns=owns: jnp.sum(jnp.where(owns, (v + largest * n_big)[:, None, :], 0), axis=-1)
        lists.update({f"n{size}": i32(jnp.sum(has, axis=1)),
                      f"off{size}": i32(pick(off).reshape(-1)), f"dst{size}": i32(pick(dst).reshape(-1))})
    return {
        **lists,
        "rows": i32(jnp.sum(run, axis=1)),
        "pad_start": i32(jnp.concatenate([pstarts + total, pends[-1:]])), "pad_len": i32(padded - total),
        "tile_expert": i32(tile_expert), "tiles_used": i32(pends[-1:] // te),
        "tile_order": i32(tile_order), "tile_next": i32(tile_next),
    }


def kernel(x, norm_mix_g, w_in, ret_gn_g, w_ret_o, q_norm_g, k_norm_g, sinks, w_swa_o, w_out, norm_ffn_g,
           w_router_group, b_router_group, w_router_expert, b_router_expert, w_gate, w_up, w_down):
    batch, seq, d = x.shape
    n = batch * seq
    assert d == D_MODEL and seq % SEQ_TILE == 0 and n % ROW_TILE == 0
    cos_tab, sin_tab = _rotary_tables(seq)
    n_rows = _expert_row_bound(n)
    for l in range(w_in.shape[0]):
        x2d = x.reshape(n, d)
        ret_proj, swa_proj, gates = _inproj(x2d, norm_mix_g[l][None, :], w_in[l].astype(_BF16), cos_tab, sin_tab,
                                            ret_gn_g[l].reshape(1, RET_V_W))
        retg, attn = _mixers(ret_proj, swa_proj, sinks[l], jnp.tile(q_norm_g[l], 2)[None, :],
                             jnp.tile(k_norm_g[l], 2)[None, :], batch, seq)
        pad = LANES - MOE_N_EXPERTS - MOE_GROUPS
        w_router_t = jnp.concatenate(
            [w_router_expert[l].T, w_router_group[l].T, jnp.zeros((pad, d), _F32)], axis=0).astype(_BF16)
        b_router = jnp.concatenate([b_router_expert[l], b_router_group[l], jnp.zeros((pad,), _F32)])
        b_router_t = jnp.broadcast_to(b_router[:, None], (LANES, LANES))
        x1, h2, info, info_t, counts = _outproj(
            x2d, retg, attn, gates, w_ret_o[l].astype(_BF16), w_swa_o[l].astype(_BF16), w_out[l].astype(_BF16),
            norm_ffn_g[l][None, :], w_router_t, b_router_t)
        plan = _routing_plan(counts, n_rows)
        xs = _dispatch(plan, h2, info_t, n_rows)
        ys = _experts(plan, xs, w_gate[l], w_up[l], w_down[l])
        x = _combine(plan, x1, info, ys).reshape(batch, seq, d)
    return x
```

```python
import math

import numpy as np
import jax
import jax.numpy as jnp
from jax import lax
from jax.experimental import pallas as pl
from jax.experimental.pallas import tpu as pltpu

D_MODEL = 1024
RET_HEADS = 4
RET_QK_DIM = 128
RET_V_DIM = 256
RET_CHUNK = 128
ROPE_BASE = 10000.0
SWA_Q_HEADS = 8
SWA_KV_HEADS = 2
SWA_HEAD_DIM = 64
SWA_WINDOW = 128
MOE_GROUPS = 4
MOE_EXPERTS_PER_GROUP = 8
MOE_D_FF = 512
NORM_EPS = 1e-6

RET_QK_W = RET_HEADS * RET_QK_DIM
RET_V_W = RET_HEADS * RET_V_DIM
SWA_Q_W = SWA_Q_HEADS * SWA_HEAD_DIM
SWA_KV_W = SWA_KV_HEADS * SWA_HEAD_DIM
SWA_GROUP = SWA_Q_HEADS // SWA_KV_HEADS
MOE_N_EXPERTS = MOE_GROUPS * MOE_EXPERTS_PER_GROUP

RET_COLS = 2 * RET_QK_W + 2 * RET_V_W
SWA_COLS = SWA_Q_W + 2 * SWA_KV_W
GATE_COLS = 2 * D_MODEL
IN_WIDTH = RET_COLS + SWA_COLS + GATE_COLS

LANES = 128
SUBLANES = 8
ROUTER_GROUP_LANE = 32
NEG_BIG = -1e30
LOG2_E = math.log2(math.e)

ROW_TILE = 512
SEQ_TILE = 512
EXPERT_TILE = 512
OUTPROJ_SUB = 128
INPROJ_SUB = 256
EXPERT_SUB = 256
RUN_ALIGN = 16
SORT_ROWS = 2 * ROW_TILE + MOE_N_EXPERTS * RUN_ALIGN
SORT_ROWS_POW2 = 1 << SORT_ROWS.bit_length()
SORT_COMMON = 2 * ROW_TILE + MOE_N_EXPERTS * RUN_ALIGN * 9 // 16
RING = 3
VMEM_LIMIT = 56 * 1024 * 1024

RET_LOG_DECAY = tuple(float(np.log1p(-np.exp2(-5.0 - h))) for h in range(RET_HEADS))

_BF16 = jnp.bfloat16
_F32 = jnp.float32


def _dot(a, b):
    return jnp.dot(a, b, preferred_element_type=_F32)


def _dot_nt(a, b):
    return lax.dot_general(a, b, (((1,), (1,)), ((), ())), preferred_element_type=_F32)


def _sigmoid(v):
    return 1.0 / (1.0 + jnp.exp(-v))


def _split_bf16(v):
    hi = v.astype(_BF16)
    return hi, (v - hi.astype(_F32)).astype(_BF16)


def _iota(shape, dim):
    return lax.broadcasted_iota(jnp.int32, shape, dim)


def _inproj_kernel(x_ref, g_ref, w_ref, cos_ref, sin_ref, gn_ref, ret_ref, swa_ref, gate_ref):
    subs = [slice(s * INPROJ_SUB, (s + 1) * INPROJ_SUB) for s in range(ROW_TILE // INPROJ_SUB)]

    def normed(rows):
        x = x_ref[rows, :]
        r = lax.rsqrt(jnp.mean(x * x, axis=-1, keepdims=True) + NORM_EPS)
        return ((x * r) * g_ref[...]).astype(_BF16)
    chunk_pos = (_iota((INPROJ_SUB, RET_QK_DIM), 0) % RET_CHUNK + 1).astype(_F32)

    def rotary_decay(base, sign, scale):
        def epilogue(rows, raw):
            cos = cos_ref[rows, :]
            sin = sin_ref[rows, :]
            for hd in range(RET_HEADS):
                v = raw[:, hd * RET_QK_DIM:(hd + 1) * RET_QK_DIM]
                rot = v * cos + pltpu.roll(v, RET_QK_DIM // 2, 1) * sin
                decay = jnp.exp((sign * RET_LOG_DECAY[hd]) * chunk_pos) * scale
                ret_ref[rows, base + hd * RET_QK_DIM:base + (hd + 1) * RET_QK_DIM] = (rot * decay).astype(_BF16)
        return epilogue

    def store(ref, lo, fn):
        def epilogue(rows, raw):
            ref[rows, lo:lo + raw.shape[1]] = fn(raw).astype(_BF16)
        return epilogue

    v_base = 2 * RET_QK_W
    g_base = v_base + RET_V_W
    gate_base = RET_COLS + SWA_COLS
    jobs = [(0, RET_QK_W, rotary_decay(0, 1.0, 1.0)),
            (RET_QK_W, RET_QK_W, rotary_decay(RET_QK_W, -1.0, RET_QK_DIM ** -0.5))]
    jobs += [(v_base + c, 512, store(ret_ref, v_base + c, lambda raw: raw)) for c in range(0, RET_V_W, 512)]
    jobs += [(g_base + c, 512, store(ret_ref, g_base + c,
                                     lambda raw, c=c: (raw * _sigmoid(raw)) * gn_ref[:, c:c + 512]))
             for c in range(0, RET_V_W, 512)]
    jobs += [(RET_COLS + c, 256, store(swa_ref, c, lambda raw: raw)) for c in range(0, SWA_COLS, 256)]
    jobs += [(gate_base + c, 512, store(gate_ref, c, _sigmoid)) for c in range(0, GATE_COLS, 512)]
    hs = [normed(rows) for rows in subs]
    for lo, width, epilogue in jobs:
        raws = [_dot(h, w_ref[:, lo:lo + width]) for h in hs]
        for rows, raw in zip(subs, raws):
            epilogue(rows, raw)


def _inproj(x2d, g, w_in_bf16, cos_tab, sin_tab, gn_row):
    n = x2d.shape[0]
    grid = (n // ROW_TILE,)
    seq_tiles = cos_tab.shape[0] // ROW_TILE
    return pl.pallas_call(
        _inproj_kernel,
        grid=grid,
        in_specs=[
            pl.BlockSpec((ROW_TILE, D_MODEL), lambda i: (i, 0)),
            pl.BlockSpec((1, D_MODEL), lambda i: (0, 0)),
            pl.BlockSpec((D_MODEL, IN_WIDTH), lambda i: (0, 0), pipeline_mode=pl.Buffered(1)),
            pl.BlockSpec((ROW_TILE, RET_QK_DIM), lambda i: (i % seq_tiles, 0)),
            pl.BlockSpec((ROW_TILE, RET_QK_DIM), lambda i: (i % seq_tiles, 0)),
            pl.BlockSpec((1, RET_V_W), lambda i: (0, 0)),
        ],
        out_specs=[
            pl.BlockSpec((ROW_TILE, RET_COLS), lambda i: (i, 0)),
            pl.BlockSpec((ROW_TILE, SWA_COLS), lambda i: (i, 0)),
            pl.BlockSpec((ROW_TILE, GATE_COLS), lambda i: (i, 0)),
        ],
        out_shape=[
            jax.ShapeDtypeStruct((n, RET_COLS), _BF16),
            jax.ShapeDtypeStruct((n, SWA_COLS), _BF16),
            jax.ShapeDtypeStruct((n, GATE_COLS), _BF16),
        ],
        compiler_params=pltpu.CompilerParams(
            dimension_semantics=("arbitrary",), vmem_limit_bytes=VMEM_LIMIT),
        name="inproj",
    )(x2d, g, w_in_bf16, cos_tab, sin_tab, gn_row)


def _retention_stages(q_ref, k_ref, v_ref, g_ref, out_ref, state_ref):
    c = RET_CHUNK
    causal = _iota((c, c), 0) >= _iota((c, c), 1)

    for h in range(RET_HEADS):
        chunk_decay = math.exp(RET_LOG_DECAY[h] * c)
        qs = slice(h * RET_QK_DIM, (h + 1) * RET_QK_DIM)
        vs = slice(h * RET_V_DIM, (h + 1) * RET_V_DIM)
        chunks = [slice(ci * c, (ci + 1) * c) for ci in range(SEQ_TILE // c)]
        scores = [jnp.where(causal, _dot_nt(q_ref[rows, qs], k_ref[rows, qs]), 0.0).astype(_BF16) for rows in chunks]
        kvs = [_dot(k_ref[rows, qs].astype(_F32).T.astype(_BF16), v_ref[rows, vs]) for rows in chunks]
        yield
        state = state_ref[h]
        states = []
        for kv in kvs:
            states.append(state.astype(_BF16))
            state = chunk_decay * (state + kv)
        state_ref[h] = state
        rets = [_dot(jnp.concatenate([s, q_ref[rows, qs]], axis=1), jnp.concatenate([v_ref[rows, vs], st], axis=0))
                for rows, s, st in zip(chunks, scores, states)]
        yield
        for rows, ret in zip(chunks, rets):
            mu = jnp.mean(ret, axis=-1, keepdims=True)
            dev = ret - mu
            var = jnp.mean(dev * dev, axis=-1, keepdims=True)
            out_ref[rows, vs] = ((dev * lax.rsqrt(var + NORM_EPS)) * g_ref[rows, vs].astype(_F32)).astype(_BF16)
        yield


def _swa_stages(first, sinks_ref, q_ref, kc_ref, vc_ref, kp_ref, vp_ref, qg_ref, kg_ref, out_ref):
    w = SWA_WINDOW
    d = SWA_HEAD_DIM
    n_blk = SEQ_TILE // w
    pair = 2 * d

    k_both = jnp.concatenate([kp_ref[...], kc_ref[...]], axis=0).astype(_F32)
    same_half = (_iota((pair, pair), 0) // d == _iota((pair, pair), 1) // d).astype(_BF16)
    k_hi, k_lo = _split_bf16(k_both * k_both)
    k_ssq = _dot(k_hi, same_half) + _dot(k_lo, same_half)
    k_n = k_both * lax.rsqrt(k_ssq * (1.0 / d) + NORM_EPS) * (kg_ref[...] * (d ** -0.5 * LOG2_E))
    v_both = jnp.concatenate([vp_ref[...], vc_ref[...]], axis=0).astype(_F32)
    v_t = [v_both[b * w:(b + 1) * w].T for b in range(n_blk + 1)]

    key = _iota((2 * w, SWA_GROUP * w), 0)
    qry = _iota((2 * w, SWA_GROUP * w), 1) % w
    band = (key > qry) & (key <= qry + w)
    band_first = band & jnp.logical_or(key >= w, jnp.logical_not(first))
    head_of_lane = _iota((1, SWA_GROUP * w), 1) // w
    half_of_lane = _iota((1, pair), 1) // d
    ones_rows = jnp.ones((2 * SUBLANES, 2 * w), _BF16)
    q_gain = qg_ref[...]
    yield

    for kk in range(SWA_KV_HEADS):
        k_native = jnp.where(half_of_lane == kk, k_n, 0.0)
        k_moved = pltpu.roll(k_native, d, 1)
        k_even, k_odd = (k_native, k_moved) if kk == 0 else (k_moved, k_native)
        k_even = k_even.astype(_BF16)
        k_odd = k_odd.astype(_BF16)
        sink_row = jnp.zeros((1, SWA_GROUP * w), _F32)
        for g in range(SWA_GROUP):
            sink_row = jnp.where(head_of_lane == g, sinks_ref[kk * SWA_GROUP + g] * LOG2_E, sink_row)
        blocks = range(n_blk)
        pairs = range(SWA_GROUP // 2)
        q_pairs = [[q_ref[j * w:(j + 1) * w, (kk * SWA_GROUP + 2 * p) * d:(kk * SWA_GROUP + 2 * p + 2) * d].astype(_F32)
                    for p in pairs] for j in blocks]
        q_splits = [[_split_bf16(q * q) for q in qs] for qs in q_pairs]
        q_ssq = [[_dot(hi, same_half) + _dot(lo, same_half) for hi, lo in sp] for sp in q_splits]
        q_ns = [[(q * lax.rsqrt(s * (1.0 / d) + NORM_EPS) * q_gain).astype(_BF16) for q, s in zip(qs, ss)]
                for qs, ss in zip(q_pairs, q_ssq)]
        yield
        raw_even = [[_dot_nt(k_even[j * w:(j + 2) * w], q_ns[j][p]) for p in pairs] for j in blocks]
        raw_odd = [[_dot_nt(k_odd[j * w:(j + 2) * w], q_ns[j][p]) for p in pairs] for j in blocks]
        yield
        score_blocks = []
        for j in blocks:
            raws = []
            for p in pairs:
                raws.append(raw_even[j][p])
                raws.append(raw_odd[j][p])
            s_t = jnp.concatenate(raws, axis=1)
            score_blocks.append(jnp.where(band_first if j == 0 else band, s_t, NEG_BIG))
        yield
        maxes = [jnp.maximum(jnp.max(s_t, axis=0, keepdims=True), sink_row) for s_t in score_blocks]
        probs = [jnp.exp2(s_t - m).astype(_BF16) for s_t, m in zip(score_blocks, maxes)]
        yield
        outs = []
        for j in blocks:
            v_ext = jnp.concatenate([v_t[j][kk * d:(kk + 1) * d], v_t[j + 1][kk * d:(kk + 1) * d]], axis=1)
            outs.append(_dot(jnp.concatenate([v_ext.astype(_BF16), ones_rows], axis=0), probs[j]))
        yield
        for j in blocks:
            rows = slice(j * w, (j + 1) * w)
            denom = outs[j][d:d + 1, :] + jnp.exp2(sink_row - maxes[j])
            o_t = outs[j][0:d, :] * (1.0 / denom)
            for p in range(SWA_GROUP // 2):
                c0 = (kk * SWA_GROUP + 2 * p) * d
                both = jnp.concatenate([o_t[:, 2 * p * w:(2 * p + 1) * w], o_t[:, (2 * p + 1) * w:(2 * p + 2) * w]], axis=0)
                out_ref[rows, c0:c0 + pair] = both.T.astype(_BF16)
        yield


def _mixers_kernel(sinks_ref, rq_ref, rk_ref, rv_ref, rg_ref, sq_ref, kc_ref, vc_ref, kp_ref, vp_ref, qg_ref, kg_ref,
                   retg_ref, attn_ref, state_ref):
    first = pl.program_id(1) == 0

    @pl.when(first)
    def _():
        state_ref[...] = jnp.zeros_like(state_ref)

    streams = [_retention_stages(rq_ref, rk_ref, rv_ref, rg_ref, retg_ref, state_ref),
               _swa_stages(first, sinks_ref, sq_ref, kc_ref, vc_ref, kp_ref, vp_ref, qg_ref, kg_ref, attn_ref)]
    while streams:
        streams = [s for s in streams if next(s, StopIteration) is not StopIteration]


def _mixers(ret_proj, swa_proj, sinks, q_gain2, k_gain2, batch, seq):
    n = ret_proj.shape[0]
    steps = seq // SEQ_TILE
    blocks_per_step = SEQ_TILE // SWA_WINDOW
    rowmap = lambda b, s: b * steps + s
    prevmap = lambda b, s: jnp.maximum((b * steps + s) * blocks_per_step - 1, 0)
    k_col = SWA_Q_W // SWA_KV_W
    return pl.pallas_call(
        _mixers_kernel,
        grid=(batch, steps),
        in_specs=[
            pl.BlockSpec(memory_space=pltpu.SMEM),
            pl.BlockSpec((SEQ_TILE, RET_QK_W), lambda b, s: (rowmap(b, s), 0)),
            pl.BlockSpec((SEQ_TILE, RET_QK_W), lambda b, s: (rowmap(b, s), 1)),
            pl.BlockSpec((SEQ_TILE, RET_V_W), lambda b, s: (rowmap(b, s), 1)),
            pl.BlockSpec((SEQ_TILE, RET_V_W), lambda b, s: (rowmap(b, s), 2)),
            pl.BlockSpec((SEQ_TILE, SWA_Q_W), lambda b, s: (rowmap(b, s), 0)),
            pl.BlockSpec((SEQ_TILE, SWA_KV_W), lambda b, s: (rowmap(b, s), k_col)),
            pl.BlockSpec((SEQ_TILE, SWA_KV_W), lambda b, s: (rowmap(b, s), k_col + 1)),
            pl.BlockSpec((SWA_WINDOW, SWA_KV_W), lambda b, s: (prevmap(b, s), k_col)),
            pl.BlockSpec((SWA_WINDOW, SWA_KV_W), lambda b, s: (prevmap(b, s), k_col + 1)),
            pl.BlockSpec((1, 2 * SWA_HEAD_DIM), lambda b, s: (0, 0)),
            pl.BlockSpec((1, 2 * SWA_HEAD_DIM), lambda b, s: (0, 0)),
        ],
        out_specs=[
            pl.BlockSpec((SEQ_TILE, RET_V_W), lambda b, s: (rowmap(b, s), 0)),
            pl.BlockSpec((SEQ_TILE, SWA_Q_W), lambda b, s: (rowmap(b, s), 0)),
        ],
        out_shape=[
            jax.ShapeDtypeStruct((n, RET_V_W), _BF16),
            jax.ShapeDtypeStruct((n, SWA_Q_W), _BF16),
        ],
        scratch_shapes=[pltpu.VMEM((RET_HEADS, RET_QK_DIM, RET_V_DIM), _F32)],
        compiler_params=pltpu.CompilerParams(
            dimension_semantics=("arbitrary", "arbitrary"), vmem_limit_bytes=VMEM_LIMIT),
        name="mixers",
    )(sinks, ret_proj, ret_proj, ret_proj, ret_proj, swa_proj, swa_proj, swa_proj, swa_proj, swa_proj,
      q_gain2, k_gain2)


def _outproj_kernel(x_ref, retg_ref, attn_ref, ga_ref, gb_ref, wro_ref, wso_ref, wout_ref, g2_ref,
                    wrt_ref, brt_ref, x1_ref, h2_ref, info_ref, info_t_ref, counts_ref, logits_ref):
    tm = ROW_TILE
    step = pl.program_id(0)
    cur = step % 2

    @pl.when(step == 0)
    def _():
        logits_ref[...] = jnp.zeros_like(logits_ref)

    logits = logits_ref[1 - cur]
    row = _iota((LANES, tm), 0)
    row_f = row.astype(_F32)
    is_group = (row >= ROUTER_GROUP_LANE) & (row < ROUTER_GROUP_LANE + MOE_GROUPS)

    subs = [slice(s * OUTPROJ_SUB, (s + 1) * OUTPROJ_SUB) for s in range(tm // OUTPROJ_SUB)]
    y_a = [_dot(retg_ref[rows, :], wro_ref[...]) for rows in subs]

    gl = jnp.where(is_group, logits, NEG_BIG)
    g_max = jnp.max(gl, axis=0, keepdims=True)
    g_prob = 1.0 / jnp.sum(jnp.exp(gl - g_max), axis=0, keepdims=True)
    g_idx = jnp.min(jnp.where(gl == g_max, row_f - ROUTER_GROUP_LANE, float(LANES)), axis=0, keepdims=True)

    y_b = [_dot(attn_ref[rows, :], wso_ref[...]) for rows in subs]

    e_lo = g_idx * MOE_EXPERTS_PER_GROUP
    in_group = (row_f >= e_lo) & (row_f < e_lo + MOE_EXPERTS_PER_GROUP)
    el = jnp.where(in_group, logits, NEG_BIG)
    t1 = jnp.max(el, axis=0, keepdims=True)
    i1 = jnp.min(jnp.where(el == t1, row_f, float(LANES)), axis=0, keepdims=True)

    merged = [(ga_ref[rows, :].astype(_F32) * a + gb_ref[rows, :].astype(_F32) * b).astype(_BF16)
              for rows, a, b in zip(subs, y_a, y_b)]
    mixed = [_dot(m, wout_ref[...]) for m in merged]
    x1 = [x_ref[rows, :] + m for rows, m in zip(subs, mixed)]
    for rows, v in zip(subs, x1):
        x1_ref[rows, :] = v

    el2 = jnp.where(row_f == i1, NEG_BIG, el)
    t2 = jnp.max(el2, axis=0, keepdims=True)
    i2 = jnp.min(jnp.where(el2 == t2, row_f, float(LANES)), axis=0, keepdims=True)
    e21 = jnp.exp(t2 - t1)
    w1 = g_prob / (1.0 + e21)
    w2 = g_prob * e21 / (1.0 + e21)

    h2 = [((v * lax.rsqrt(jnp.mean(v * v, axis=-1, keepdims=True) + NORM_EPS)) * g2_ref[...]).astype(_BF16)
          for v in x1]
    for rows, v in zip(subs, h2):
        h2_ref[rows, :] = v

    sel1 = row_f == i1
    sel2 = row_f == i2
    onehot = jnp.where(sel1 | sel2, 1.0, 0.0).astype(_BF16)
    earlier = (_iota((tm, tm), 0) < _iota((tm, tm), 1)).astype(_BF16)
    before = _dot(onehot, earlier)
    counts = _dot(onehot, jnp.ones((tm, LANES), _BF16))

    for rows, v in zip(subs, h2):
        logits_ref[cur, :, rows] = _dot_nt(wrt_ref[...], v) + brt_ref[:, 0:1]

    run = jnp.floor((counts + (RUN_ALIGN - 1.0)) * (1.0 / RUN_ALIGN)) * RUN_ALIGN
    lower_experts = (_iota((LANES, LANES), 0) > _iota((LANES, LANES), 1)).astype(_BF16)
    run_off = _dot(lower_experts, run.astype(_BF16))
    place = before + jnp.concatenate([run_off] * (tm // LANES), axis=1)
    pos1 = jnp.sum(jnp.where(sel1, place, 0.0), axis=0, keepdims=True)
    pos2 = jnp.sum(jnp.where(sel2, place, 0.0), axis=0, keepdims=True)
    counts_ref[...] = counts.T[0:SUBLANES, :]

    info_t = jnp.concatenate([i1, i2, w1, w2, pos1, pos2, jnp.zeros((2, tm), _F32)], axis=0)
    info_t_ref[...] = info_t
    info_ref[...] = jnp.concatenate([info_t, jnp.zeros((LANES - SUBLANES, tm), _F32)], axis=0).T


def _outproj(x2d, retg, attn, gates, w_ret_o, w_swa_o, w_out, g2, w_router_t, b_router_t):
    n = x2d.shape[0]
    tm = ROW_TILE
    n_tiles = n // tm
    const = lambda s: (0, 0)
    proj = lambda s: jnp.minimum(s, n_tiles - 1)
    routed = lambda s: jnp.maximum(s - 1, 0)
    return pl.pallas_call(
        _outproj_kernel,
        grid=(n_tiles + 1,),
        in_specs=[
            pl.BlockSpec((tm, D_MODEL), lambda s: (proj(s), 0)),
            pl.BlockSpec((tm, RET_V_W), lambda s: (proj(s), 0)),
            pl.BlockSpec((tm, SWA_Q_W), lambda s: (proj(s), 0)),
            pl.BlockSpec((tm, D_MODEL), lambda s: (proj(s), 0)),
            pl.BlockSpec((tm, D_MODEL), lambda s: (proj(s), 1)),
            pl.BlockSpec((RET_V_W, D_MODEL), const),
            pl.BlockSpec((SWA_Q_W, D_MODEL), const),
            pl.BlockSpec((D_MODEL, D_MODEL), const),
            pl.BlockSpec((1, D_MODEL), const),
            pl.BlockSpec((LANES, D_MODEL), const),
            pl.BlockSpec((LANES, LANES), const),
        ],
        out_specs=[
            pl.BlockSpec((tm, D_MODEL), lambda s: (proj(s), 0)),
            pl.BlockSpec((tm, D_MODEL), lambda s: (proj(s), 0)),
            pl.BlockSpec((tm, LANES), lambda s: (routed(s), 0)),
            pl.BlockSpec((SUBLANES, tm), lambda s: (0, routed(s))),
            pl.BlockSpec((SUBLANES, LANES), lambda s: (routed(s), 0)),
        ],
        scratch_shapes=[pltpu.VMEM((2, LANES, tm), _F32)],
        out_shape=[
            jax.ShapeDtypeStruct((n, D_MODEL), _F32),
            jax.ShapeDtypeStruct((n, D_MODEL), _BF16),
            jax.ShapeDtypeStruct((n, LANES), _F32),
            jax.ShapeDtypeStruct((SUBLANES, n), _F32),
            jax.ShapeDtypeStruct((n // tm * SUBLANES, LANES), _F32),
        ],
        compiler_params=pltpu.CompilerParams(
            dimension_semantics=("arbitrary",), vmem_limit_bytes=VMEM_LIMIT),
        name="outproj",
    )(x2d, retg, attn, gates, gates, w_ret_o, w_swa_o, w_out, g2, w_router_t, b_router_t)


def _piece_sizes(largest):
    piece = largest
    while piece >= RUN_ALIGN:
        yield piece
        piece //= 2


PIECE_CLASSES = tuple(k * RUN_ALIGN for k in (4, 3, 2, 1))
PIECE_LIST_KEYS = tuple(f"{field}{size}" for size in PIECE_CLASSES for field in ("n", "off", "dst"))


def _piece_cap(size):
    return SORT_ROWS // size if size == PIECE_CLASSES[0] else MOE_N_EXPERTS


def _start_run_copies(pieces, tile, make_copy):
    for k, size in enumerate(PIECE_CLASSES):
        n_ref, off_ref, dst_ref = pieces[3 * k:3 * k + 3]

        def start(j, carry, off_ref=off_ref, dst_ref=dst_ref, size=size):
            i = tile * _piece_cap(size) + j
            make_copy(pl.multiple_of(off_ref[i], RUN_ALIGN), pl.multiple_of(dst_ref[i], RUN_ALIGN), size).start()
            return carry

        lax.fori_loop(0, n_ref[tile], start, 0)


def _wait_rows(rows, make_copy):
    for piece in _piece_sizes(SORT_ROWS_POW2 // 2):
        pl.when((rows & piece) != 0)(make_copy(0, 0, piece).wait)


def _zero_fill_padding(pad_start_ref, pad_len_ref, zeros_ref, xs_hbm, sem, wait):
    def per_expert(e, carry):
        pos = pad_start_ref[e]
        length = pad_len_ref[e]
        for piece in _piece_sizes(EXPERT_TILE // 2):
            copy = pltpu.make_async_copy(
                zeros_ref.at[pl.ds(0, piece)], xs_hbm.at[pl.ds(pl.multiple_of(pos, RUN_ALIGN), piece)], sem)
            pl.when((length & piece) != 0)(copy.wait if wait else copy.start)
            pos = pos + (length & piece)
        return carry

    lax.fori_loop(0, MOE_N_EXPERTS, per_expert, 0)

    tail_start = pad_start_ref[MOE_N_EXPERTS]
    n_rows = xs_hbm.shape[0]
    half = EXPERT_TILE // 2

    def per_half_tile(j, carry):
        pos = tail_start + j * half
        copy = pltpu.make_async_copy(
            zeros_ref, xs_hbm.at[pl.ds(pl.multiple_of(jnp.minimum(pos, n_rows - half), RUN_ALIGN), half)], sem)
        pl.when(pos < n_rows)(copy.wait if wait else copy.start)
        return carry

    lax.fori_loop(0, n_rows // half, per_half_tile, 0)


def _dispatch_kernel(*refs):
    pieces = refs[:len(PIECE_LIST_KEYS)]
    rows_ref, pad_start_ref, pad_len_ref, h2_ref, info_t_ref, xs_hbm, sorted_ref, zeros_ref, sems, zero_sem = (
        refs[len(PIECE_LIST_KEYS):])
    tile = pl.program_id(0)
    last = pl.num_programs(0) - 1
    slot = tile % RING
    tm = ROW_TILE

    @pl.when(tile == 0)
    def _():
        zeros_ref[...] = jnp.zeros_like(zeros_ref)
        _zero_fill_padding(pad_start_ref, pad_len_ref, zeros_ref, xs_hbm, zero_sem, wait=False)
        _zero_fill_padding(pad_start_ref, pad_len_ref, zeros_ref, xs_hbm, zero_sem, wait=True)

    pos = info_t_ref[4:6, :].astype(jnp.int32)

    def sort_rows(lo, hi):
        place = _iota((hi - lo, tm), 0) + lo
        onehot = ((place == pos[0:1, :]) | (place == pos[1:2, :])).astype(_BF16)
        for c in range(0, D_MODEL, 256):
            sorted_ref[slot, lo:hi, c:c + 256] = _dot(onehot, h2_ref[:, c:c + 256]).astype(_BF16)

    sort_rows(0, SORT_COMMON)
    pl.when(rows_ref[tile] > SORT_COMMON)(lambda: sort_rows(SORT_COMMON, SORT_ROWS))

    def copy_maker(t):
        s = t % RING

        def make_copy(off, dst, piece):
            return pltpu.make_async_copy(
                sorted_ref.at[s, pl.ds(off, piece)], xs_hbm.at[pl.ds(dst, piece)], sems.at[s])

        return make_copy

    _start_run_copies(pieces, tile, copy_maker(tile))

    @pl.when(tile >= RING - 1)
    def _():
        _wait_rows(rows_ref[tile - (RING - 1)], copy_maker(tile - (RING - 1)))

    for back in range(RING - 1):
        @pl.when(jnp.logical_and(tile == last, tile >= back))
        def _(back=back):
            _wait_rows(rows_ref[tile - back], copy_maker(tile - back))


def _dispatch(plan, h2, info_t, n_rows):
    n = h2.shape[0]
    grid_spec = pltpu.PrefetchScalarGridSpec(
        num_scalar_prefetch=len(PIECE_LIST_KEYS) + 3,
        grid=(n // ROW_TILE,),
        in_specs=[
            pl.BlockSpec((ROW_TILE, D_MODEL), lambda i, *_: (i, 0)),
            pl.BlockSpec((SUBLANES, ROW_TILE), lambda i, *_: (0, i)),
        ],
        out_specs=pl.BlockSpec(memory_space=pl.ANY),
        scratch_shapes=[
            pltpu.VMEM((RING, SORT_ROWS, D_MODEL), _BF16),
            pltpu.VMEM((EXPERT_TILE // 2, D_MODEL), _BF16),
            pltpu.SemaphoreType.DMA((RING,)),
            pltpu.SemaphoreType.DMA(()),
        ],
    )
    return pl.pallas_call(
        _dispatch_kernel,
        grid_spec=grid_spec,
        out_shape=jax.ShapeDtypeStruct((n_rows, D_MODEL), _BF16),
        compiler_params=pltpu.CompilerParams(
            dimension_semantics=("arbitrary",), vmem_limit_bytes=VMEM_LIMIT),
        name="dispatch",
    )(*[plan[k] for k in PIECE_LIST_KEYS], plan["rows"], plan["pad_start"], plan["pad_len"], h2, info_t)


def _experts_kernel(te_ref, nt_ref, order_ref, next_ref, xs_ref, wg_hbm, wu_hbm, wd_hbm, ys_ref,
                    wg_f, wu_f, wd_f, wg_s, wu_s, wd_s, sems):
    i = pl.program_id(0)
    in_use = i < nt_ref[0]
    slot = order_ref[i] % 2
    prev = te_ref[jnp.maximum(i - 1, 0)]
    changed = jnp.logical_or(i == 0, te_ref[i] != prev)

    def fetch(expert, s):
        return [pltpu.make_async_copy(hbm.at[expert], stage.at[s], sems.at[s, j])
                for j, (hbm, stage) in enumerate(((wg_hbm, wg_f), (wu_hbm, wu_f), (wd_hbm, wd_f)))]

    @pl.when(i == 0)
    def _():
        for copy in fetch(te_ref[0], 0):
            copy.start()

    @pl.when(jnp.logical_and(in_use, changed))
    def _():
        for copy in fetch(te_ref[i], slot):
            copy.wait()
        wg_s[...] = wg_f[slot].astype(_BF16)
        wu_s[...] = wu_f[slot].astype(_BF16)
        wd_s[...] = wd_f[slot].astype(_BF16)

        @pl.when(next_ref[i] != te_ref[i])
        def _():
            for copy in fetch(next_ref[i], 1 - slot):
                copy.start()

    @pl.when(in_use)
    def _():
        subs = [slice(s * EXPERT_SUB, (s + 1) * EXPERT_SUB) for s in range(EXPERT_TILE // EXPERT_SUB)]
        gate = [_dot(xs_ref[rows, :], wg_s[...]) for rows in subs]
        up = [_dot(xs_ref[rows, :], wu_s[...]) for rows in subs]
        act = [((g * _sigmoid(g)) * u).astype(_BF16) for g, u in zip(gate, up)]
        for rows, a in zip(subs, act):
            ys_ref[rows, :] = _dot(a, wd_s[...]).astype(_BF16)


def _experts(plan, xs, w_gate, w_up, w_down):
    n_rows = xs.shape[0]
    last_used = lambda i, nt: jnp.minimum(i, nt[0] - 1)
    grid_spec = pltpu.PrefetchScalarGridSpec(
        num_scalar_prefetch=4,
        grid=(n_rows // EXPERT_TILE,),
        in_specs=[
            pl.BlockSpec((EXPERT_TILE, D_MODEL), lambda i, te, nt, *_: (last_used(i, nt), 0)),
            pl.BlockSpec(memory_space=pl.ANY),
            pl.BlockSpec(memory_space=pl.ANY),
            pl.BlockSpec(memory_space=pl.ANY),
        ],
        out_specs=pl.BlockSpec((EXPERT_TILE, D_MODEL), lambda i, te, nt, *_: (last_used(i, nt), 0)),
        scratch_shapes=[
            pltpu.VMEM((2, D_MODEL, MOE_D_FF), _F32),
            pltpu.VMEM((2, D_MODEL, MOE_D_FF), _F32),
            pltpu.VMEM((2, MOE_D_FF, D_MODEL), _F32),
            pltpu.VMEM((D_MODEL, MOE_D_FF), _BF16),
            pltpu.VMEM((D_MODEL, MOE_D_FF), _BF16),
            pltpu.VMEM((MOE_D_FF, D_MODEL), _BF16),
            pltpu.SemaphoreType.DMA((2, 3)),
        ],
    )
    return pl.pallas_call(
        _experts_kernel,
        grid_spec=grid_spec,
        out_shape=jax.ShapeDtypeStruct((n_rows, D_MODEL), _BF16),
        input_output_aliases={4: 0},
        compiler_params=pltpu.CompilerParams(
            dimension_semantics=("arbitrary",), vmem_limit_bytes=VMEM_LIMIT),
        name="experts",
    )(plan["tile_expert"], plan["tiles_used"], plan["tile_order"], plan["tile_next"], xs, w_gate, w_up, w_down)


def _combine_kernel(*refs):
    pieces = refs[:len(PIECE_LIST_KEYS)]
    rows_ref, x1_ref, info_ref, ys_hbm, out_ref, block_ref, sems = refs[len(PIECE_LIST_KEYS):]
    tile = pl.program_id(0)
    last = pl.num_programs(0) - 1
    slot = tile % RING
    tm = ROW_TILE

    def copy_maker(t):
        s = t % RING

        def make_copy(off, dst, piece):
            return pltpu.make_async_copy(
                ys_hbm.at[pl.ds(dst, piece)], block_ref.at[s, pl.ds(off, piece)], sems.at[s])

        return make_copy

    @pl.when(tile == 0)
    def _():
        block_ref[...] = jnp.zeros_like(block_ref)
        for ahead in range(RING - 1):
            pl.when(ahead <= last)(lambda ahead=ahead: _start_run_copies(pieces, ahead, copy_maker(ahead)))

    @pl.when(tile + (RING - 1) <= last)
    def _():
        _start_run_copies(pieces, tile + (RING - 1), copy_maker(tile + (RING - 1)))

    _wait_rows(rows_ref[tile], copy_maker(tile))

    info = info_ref[...]

    def weights_of(lo, hi):
        place = (_iota((tm, hi - lo), 1) + lo).astype(_F32)
        return (jnp.where(place == info[:, 4:5], info[:, 2:3], 0.0)
                + jnp.where(place == info[:, 5:6], info[:, 3:4], 0.0)).astype(_BF16)

    weights = weights_of(0, SORT_COMMON)
    for c in range(0, D_MODEL, 256):
        out_ref[:, c:c + 256] = x1_ref[:, c:c + 256] + _dot(weights, block_ref[slot, 0:SORT_COMMON, c:c + 256])

    @pl.when(rows_ref[tile] > SORT_COMMON)
    def _():
        tail = weights_of(SORT_COMMON, SORT_ROWS)
        for c in range(0, D_MODEL, 256):
            out_ref[:, c:c + 256] += _dot(tail, block_ref[slot, SORT_COMMON:SORT_ROWS, c:c + 256])


def _combine(plan, x1, info, ys):
    n = x1.shape[0]
    grid_spec = pltpu.PrefetchScalarGridSpec(
        num_scalar_prefetch=len(PIECE_LIST_KEYS) + 1,
        grid=(n // ROW_TILE,),
        in_specs=[
            pl.BlockSpec((ROW_TILE, D_MODEL), lambda i, *_: (i, 0)),
            pl.BlockSpec((ROW_TILE, LANES), lambda i, *_: (i, 0)),
            pl.BlockSpec(memory_space=pl.ANY),
        ],
        out_specs=pl.BlockSpec((ROW_TILE, D_MODEL), lambda i, *_: (i, 0)),
        scratch_shapes=[pltpu.VMEM((RING, SORT_ROWS, D_MODEL), _BF16), pltpu.SemaphoreType.DMA((RING,))],
    )
    return pl.pallas_call(
        _combine_kernel,
        grid_spec=grid_spec,
        out_shape=jax.ShapeDtypeStruct((n, D_MODEL), _F32),
        compiler_params=pltpu.CompilerParams(
            dimension_semantics=("arbitrary",), vmem_limit_bytes=VMEM_LIMIT),
        name="combine",
    )(*[plan[k] for k in PIECE_LIST_KEYS], plan["rows"], x1, info, ys)


def _rotary_tables(seq):
    half = RET_QK_DIM // 2
    inv = ROPE_BASE ** (-jnp.arange(half, dtype=_F32) / half)
    hi = (jnp.arange(seq // RET_CHUNK, dtype=jnp.int32) * RET_CHUNK).astype(_F32)[:, None] * inv[None, :]
    lo = jnp.arange(RET_CHUNK, dtype=jnp.int32).astype(_F32)[:, None] * inv[None, :]
    cos_hi, sin_hi = jnp.cos(hi)[:, None, :], jnp.sin(hi)[:, None, :]
    cos_lo, sin_lo = jnp.cos(lo)[None, :, :], jnp.sin(lo)[None, :, :]
    cos = (cos_hi * cos_lo - sin_hi * sin_lo).reshape(seq, half)
    sin = (sin_hi * cos_lo + cos_hi * sin_lo).reshape(seq, half)
    return jnp.concatenate([cos, cos], axis=1), jnp.concatenate([-sin, sin], axis=1)


def _expert_row_bound(n_tokens):
    rows = (2 * n_tokens + (n_tokens // ROW_TILE) * MOE_N_EXPERTS * (RUN_ALIGN - 1)
            + MOE_N_EXPERTS * (EXPERT_TILE - RUN_ALIGN))
    return -(-rows // EXPERT_TILE) * EXPERT_TILE


def _routing_plan(counts_out, n_rows):
    te = EXPERT_TILE
    n_tok_tiles = counts_out.shape[0] // SUBLANES
    counts = counts_out.reshape(n_tok_tiles, SUBLANES, LANES)[:, 0, :MOE_N_EXPERTS].astype(jnp.int32)
    run = (counts + RUN_ALIGN - 1) // RUN_ALIGN * RUN_ALIGN
    off = jnp.cumsum(run, axis=1) - run
    total = jnp.sum(run, axis=0)
    padded = (total + te - 1) // te * te
    pends = jnp.cumsum(padded)
    pstarts = pends - padded
    dst = pstarts[None, :] + jnp.cumsum(run, axis=0) - run
    tile_start = jnp.arange(n_rows // te, dtype=jnp.int32) * te
    tile_expert = jnp.sum((tile_start[:, None] >= pends[None, :]).astype(jnp.int32), axis=1)
    tile_expert = jnp.minimum(tile_expert, MOE_N_EXPERTS - 1)
    i32 = lambda v: v.astype(jnp.int32)

    experts = jnp.arange(MOE_N_EXPERTS, dtype=jnp.int32)
    has_rows = total > 0
    rank = jnp.cumsum(has_rows.astype(jnp.int32)) - 1
    later = jnp.where((experts[None, :] > experts[:, None]) & has_rows[None, :], experts[None, :], MOE_N_EXPERTS)
    following = jnp.min(later, axis=1)
    following = jnp.where(following == MOE_N_EXPERTS, experts, following)
    of_tile = tile_expert[:, None] == experts[None, :]
    tile_order = jnp.sum(jnp.where(of_tile, rank[None, :], 0), axis=1)
    tile_next = jnp.sum(jnp.where(of_tile, following[None, :], 0), axis=1)

    largest = PIECE_CLASSES[0]
    n_big = run // largest
    rest = run - largest * n_big
    big_first = jnp.cumsum(n_big, axis=1) - n_big
    k_big = jnp.arange(_piece_cap(largest), dtype=jnp.int32)[None, :, None]
    owns_big = (k_big >= big_first[:, None, :]) & (k_big < (big_first + n_big)[:, None, :])
    pick_big = lambda v: jnp.sum(jnp.where(owns_big, (v - largest * big_first)[:, None, :], 0), axis=-1)
    lists = {f"n{largest}": i32(jnp.sum(n_big, axis=1)),
             f"off{largest}": i32((pick_big(off) + largest * k_big[..., 0]).reshape(-1)),
             f"dst{largest}": i32((pick_big(dst) + largest * k_big[..., 0]).reshape(-1))}
    k_small = jnp.arange(MOE_N_EXPERTS, dtype=jnp.int32)[None, :, None]
    for size in PIECE_CLASSES[1:]:
        has = (rest == size).astype(jnp.int32)
        index = jnp.cumsum(has, axis=1) - has
        owns = (has[:, None, :] == 1) & (index[:, None, :] == k_small)
        pick = lambda v, owns=owns: jnp.sum(jnp.where(owns, (v + largest * n_big)[:, None, :], 0), axis=-1)
        lists.update({f"n{size}": i32(jnp.sum(has, axis=1)),
                      f"off{size}": i32(pick(off).reshape(-1)), f"dst{size}": i32(pick(dst).reshape(-1))})
    return {
        **lists,
        "rows": i32(jnp.sum(run, axis=1)),
        "pad_start": i32(jnp.concatenate([pstarts + total, pends[-1:]])), "pad_len": i32(padded - total),
        "tile_expert": i32(tile_expert), "tiles_used": i32(pends[-1:] // te),
        "tile_order": i32(tile_order), "tile_next": i32(tile_next),
    }


def kernel(x, norm_mix_g, w_in, ret_gn_g, w_ret_o, q_norm_g, k_norm_g, sinks, w_swa_o, w_out, norm_ffn_g,
           w_router_group, b_router_group, w_router_expert, b_router_expert, w_gate, w_up, w_down):
    batch, seq, d = x.shape
    n = batch * seq
    assert d == D_MODEL and seq % SEQ_TILE == 0 and n % ROW_TILE == 0
    cos_tab, sin_tab = _rotary_tables(seq)
    n_rows = _expert_row_bound(n)
    for l in range(w_in.shape[0]):
        x2d = x.reshape(n, d)
        ret_proj, swa_proj, gates = _inproj(x2d, norm_mix_g[l][None, :], w_in[l].astype(_BF16), cos_tab, sin_tab,
                                            ret_gn_g[l].reshape(1, RET_V_W))
        retg, attn = _mixers(ret_proj, swa_proj, sinks[l], jnp.tile(q_norm_g[l], 2)[None, :],
                             jnp.tile(k_norm_g[l], 2)[None, :], batch, seq)
        pad = LANES - MOE_N_EXPERTS - MOE_GROUPS
        w_router_t = jnp.concatenate(
            [w_router_expert[l].T, w_router_group[l].T, jnp.zeros((pad, d), _F32)], axis=0).astype(_BF16)
        b_router = jnp.concatenate([b_router_expert[l], b_router_group[l], jnp.zeros((pad,), _F32)])
        b_router_t = jnp.broadcast_to(b_router[:, None], (LANES, LANES))
        x1, h2, info, info_t, counts = _outproj(
            x2d, retg, attn, gates, w_ret_o[l].astype(_BF16), w_swa_o[l].astype(_BF16), w_out[l].astype(_BF16),
            norm_ffn_g[l][None, :], w_router_t, b_router_t)
        plan = _routing_plan(counts, n_rows)
        xs = _dispatch(plan, h2, info_t, n_rows)
        ys = _experts(plan, xs, w_gate[l], w_up[l], w_down[l])
        x = _combine(plan, x1, info, ys).reshape(batch, seq, d)
    return x
```

```python
import math

import numpy as np
import jax
import jax.numpy as jnp
from jax import lax
from jax.experimental import pallas as pl
from jax.experimental.pallas import tpu as pltpu

D_MODEL = 1024
RET_HEADS = 4
RET_QK_DIM = 128
RET_V_DIM = 256
RET_CHUNK = 128
ROPE_BASE = 10000.0
SWA_Q_HEADS = 8
SWA_KV_HEADS = 2
SWA_HEAD_DIM = 64
SWA_WINDOW = 128
MOE_GROUPS = 4
MOE_EXPERTS_PER_GROUP = 8
MOE_D_FF = 512
NORM_EPS = 1e-6

RET_QK_W = RET_HEADS * RET_QK_DIM
RET_V_W = RET_HEADS * RET_V_DIM
SWA_Q_W = SWA_Q_HEADS * SWA_HEAD_DIM
SWA_KV_W = SWA_KV_HEADS * SWA_HEAD_DIM
SWA_GROUP = SWA_Q_HEADS // SWA_KV_HEADS
MOE_N_EXPERTS = MOE_GROUPS * MOE_EXPERTS_PER_GROUP

RET_COLS = 2 * RET_QK_W + 2 * RET_V_W
SWA_COLS = SWA_Q_W + 2 * SWA_KV_W
GATE_COLS = 2 * D_MODEL
IN_WIDTH = RET_COLS + SWA_COLS + GATE_COLS

LANES = 128
SUBLANES = 8
ROUTER_GROUP_LANE = 32
NEG_BIG = -1e30
LOG2_E = math.log2(math.e)

ROW_TILE = 512
SEQ_TILE = 512
EXPERT_TILE = 512
OUTPROJ_SUB = 128
INPROJ_SUB = 256
EXPERT_SUB = 256
RUN_ALIGN = 16
SORT_ROWS = 2 * ROW_TILE + MOE_N_EXPERTS * RUN_ALIGN
SORT_ROWS_POW2 = 1 << SORT_ROWS.bit_length()
SORT_COMMON = 2 * ROW_TILE + MOE_N_EXPERTS * RUN_ALIGN * 9 // 16
RING = 3
VMEM_LIMIT = 56 * 1024 * 1024

RET_LOG_DECAY = tuple(float(np.log1p(-np.exp2(-5.0 - h))) for h in range(RET_HEADS))

_BF16 = jnp.bfloat16
_F32 = jnp.float32


def _dot(a, b):
    return jnp.dot(a, b, preferred_element_type=_F32)


def _dot_nt(a, b):
    return lax.dot_general(a, b, (((1,), (1,)), ((), ())), preferred_element_type=_F32)


def _sigmoid(v):
    return 1.0 / (1.0 + jnp.exp(-v))


def _split_bf16(v):
    hi = v.astype(_BF16)
    return hi, (v - hi.astype(_F32)).astype(_BF16)


def _iota(shape, dim):
    return lax.broadcasted_iota(jnp.int32, shape, dim)


def _inproj_kernel(x_ref, g_ref, w_ref, cos_ref, sin_ref, gn_ref, ret_ref, swa_ref, gate_ref):
    subs = [slice(s * INPROJ_SUB, (s + 1) * INPROJ_SUB) for s in range(ROW_TILE // INPROJ_SUB)]

    def normed(rows):
        x = x_ref[rows, :]
        r = lax.rsqrt(jnp.mean(x * x, axis=-1, keepdims=True) + NORM_EPS)
        return ((x * r) * g_ref[...]).astype(_BF16)
    chunk_pos = (_iota((INPROJ_SUB, RET_QK_DIM), 0) % RET_CHUNK + 1).astype(_F32)

    def rotary_decay(base, sign, scale):
        def epilogue(rows, raw):
            cos = cos_ref[rows, :]
            sin = sin_ref[rows, :]
            for hd in range(RET_HEADS):
                v = raw[:, hd * RET_QK_DIM:(hd + 1) * RET_QK_DIM]
                rot = v * cos + pltpu.roll(v, RET_QK_DIM // 2, 1) * sin
                decay = jnp.exp((sign * RET_LOG_DECAY[hd]) * chunk_pos) * scale
                ret_ref[rows, base + hd * RET_QK_DIM:base + (hd + 1) * RET_QK_DIM] = (rot * decay).astype(_BF16)
        return epilogue

    def store(ref, lo, fn):
        def epilogue(rows, raw):
            ref[rows, lo:lo + raw.shape[1]] = fn(raw).astype(_BF16)
        return epilogue

    v_base = 2 * RET_QK_W
    g_base = v_base + RET_V_W
    gate_base = RET_COLS + SWA_COLS
    jobs = [(0, RET_QK_W, rotary_decay(0, 1.0, 1.0)),
            (RET_QK_W, RET_QK_W, rotary_decay(RET_QK_W, -1.0, RET_QK_DIM ** -0.5))]
    jobs += [(v_base + c, 512, store(ret_ref, v_base + c, lambda raw: raw)) for c in range(0, RET_V_W, 512)]
    jobs += [(g_base + c, 512, store(ret_ref, g_base + c,
                                     lambda raw, c=c: (raw * _sigmoid(raw)) * gn_ref[:, c:c + 512]))
             for c in range(0, RET_V_W, 512)]
    jobs += [(RET_COLS + c, 256, store(swa_ref, c, lambda raw: raw)) for c in range(0, SWA_COLS, 256)]
    jobs += [(gate_base + c, 512, store(gate_ref, c, _sigmoid)) for c in range(0, GATE_COLS, 512)]
    hs = [normed(rows) for rows in subs]
    for lo, width, epilogue in jobs:
        raws = [_dot(h, w_ref[:, lo:lo + width]) for h in hs]
        for rows, raw in zip(subs, raws):
            epilogue(rows, raw)


def _inproj(x2d, g, w_in_bf16, cos_tab, sin_tab, gn_row):
    n = x2d.shape[0]
    grid = (n // ROW_TILE,)
    seq_tiles = cos_tab.shape[0] // ROW_TILE
    return pl.pallas_call(
        _inproj_kernel,
        grid=grid,
        in_specs=[
            pl.BlockSpec((ROW_TILE, D_MODEL), lambda i: (i, 0)),
            pl.BlockSpec((1, D_MODEL), lambda i: (0, 0)),
            pl.BlockSpec((D_MODEL, IN_WIDTH), lambda i: (0, 0), pipeline_mode=pl.Buffered(1)),
            pl.BlockSpec((ROW_TILE, RET_QK_DIM), lambda i: (i % seq_tiles, 0)),
            pl.BlockSpec((ROW_TILE, RET_QK_DIM), lambda i: (i % seq_tiles, 0)),
            pl.BlockSpec((1, RET_V_W), lambda i: (0, 0)),
        ],
        out_specs=[
            pl.BlockSpec((ROW_TILE, RET_COLS), lambda i: (i, 0)),
            pl.BlockSpec((ROW_TILE, SWA_COLS), lambda i: (i, 0)),
            pl.BlockSpec((ROW_TILE, GATE_COLS), lambda i: (i, 0)),
        ],
        out_shape=[
            jax.ShapeDtypeStruct((n, RET_COLS), _BF16),
            jax.ShapeDtypeStruct((n, SWA_COLS), _BF16),
            jax.ShapeDtypeStruct((n, GATE_COLS), _BF16),
        ],
        compiler_params=pltpu.CompilerParams(
            dimension_semantics=("arbitrary",), vmem_limit_bytes=VMEM_LIMIT),
        name="inproj",
    )(x2d, g, w_in_bf16, cos_tab, sin_tab, gn_row)


def _retention_stages(q_ref, k_ref, v_ref, g_ref, out_ref, state_ref):
    c = RET_CHUNK
    causal = _iota((c, c), 0) >= _iota((c, c), 1)

    for h in range(RET_HEADS):
        chunk_decay = math.exp(RET_LOG_DECAY[h] * c)
        qs = slice(h * RET_QK_DIM, (h + 1) * RET_QK_DIM)
        vs = slice(h * RET_V_DIM, (h + 1) * RET_V_DIM)
        chunks = [slice(ci * c, (ci + 1) * c) for ci in range(SEQ_TILE // c)]
        scores = [jnp.where(causal, _dot_nt(q_ref[rows, qs], k_ref[rows, qs]), 0.0).astype(_BF16) for rows in chunks]
        kvs = [_dot(k_ref[rows, qs].T, v_ref[rows, vs]) for rows in chunks]
        yield
        state = state_ref[h]
        states = []
        for kv in kvs:
            states.append(state.astype(_BF16))
            state = chunk_decay * (state + kv)
        state_ref[h] = state
        rets = [_dot(jnp.concatenate([s, q_ref[rows, qs]], axis=1), jnp.concatenate([v_ref[rows, vs], st], axis=0))
                for rows, s, st in zip(chunks, scores, states)]
        yield
        for rows, ret in zip(chunks, rets):
            mu = jnp.mean(ret, axis=-1, keepdims=True)
            dev = ret - mu
            var = jnp.mean(dev * dev, axis=-1, keepdims=True)
            out_ref[rows, vs] = ((dev * lax.rsqrt(var + NORM_EPS)) * g_ref[rows, vs].astype(_F32)).astype(_BF16)
        yield


def _swa_stages(first, sinks_ref, q_ref, kc_ref, vc_ref, kp_ref, vp_ref, qg_ref, kg_ref, out_ref):
    w = SWA_WINDOW
    d = SWA_HEAD_DIM
    n_blk = SEQ_TILE // w
    pair = 2 * d

    k_both = jnp.concatenate([kp_ref[...], kc_ref[...]], axis=0).astype(_F32)
    same_half = (_iota((pair, pair), 0) // d == _iota((pair, pair), 1) // d).astype(_BF16)
    k_hi, k_lo = _split_bf16(k_both * k_both)
    k_ssq = _dot(k_hi, same_half) + _dot(k_lo, same_half)
    k_n = k_both * lax.rsqrt(k_ssq * (1.0 / d) + NORM_EPS) * (kg_ref[...] * (d ** -0.5 * LOG2_E))
    v_both = jnp.concatenate([vp_ref[...], vc_ref[...]], axis=0)
    v_t = [v_both[b * w:(b + 1) * w].T for b in range(n_blk + 1)]

    key = _iota((2 * w, SWA_GROUP * w), 0)
    qry = _iota((2 * w, SWA_GROUP * w), 1) % w
    band = (key > qry) & (key <= qry + w)
    band_first = band & jnp.logical_or(key >= w, jnp.logical_not(first))
    head_of_lane = _iota((1, SWA_GROUP * w), 1) // w
    half_of_lane = _iota((1, pair), 1) // d
    ones_rows = jnp.ones((2 * SUBLANES, 2 * w), _BF16)
    q_gain = qg_ref[...]
    yield

    for kk in range(SWA_KV_HEADS):
        k_native = jnp.where(half_of_lane == kk, k_n, 0.0)
        k_moved = pltpu.roll(k_native, d, 1)
        k_even, k_odd = (k_native, k_moved) if kk == 0 else (k_moved, k_native)
        k_even = k_even.astype(_BF16)
        k_odd = k_odd.astype(_BF16)
        sink_row = jnp.zeros((1, SWA_GROUP * w), _F32)
        for g in range(SWA_GROUP):
            sink_row = jnp.where(head_of_lane == g, sinks_ref[kk * SWA_GROUP + g] * LOG2_E, sink_row)
        blocks = range(n_blk)
        pairs = range(SWA_GROUP // 2)
        q_pairs = [[q_ref[j * w:(j + 1) * w, (kk * SWA_GROUP + 2 * p) * d:(kk * SWA_GROUP + 2 * p + 2) * d].astype(_F32)
                    for p in pairs] for j in blocks]
        q_splits = [[_split_bf16(q * q) for q in qs] for qs in q_pairs]
        q_ssq = [[_dot(hi, same_half) + _dot(lo, same_half) for hi, lo in sp] for sp in q_splits]
        q_ns = [[(q * lax.rsqrt(s * (1.0 / d) + NORM_EPS) * q_gain).astype(_BF16) for q, s in zip(qs, ss)]
                for qs, ss in zip(q_pairs, q_ssq)]
        yield
        raw_even = [[_dot_nt(k_even[j * w:(j + 2) * w], q_ns[j][p]) for p in pairs] for j in blocks]
        raw_odd = [[_dot_nt(k_odd[j * w:(j + 2) * w], q_ns[j][p]) for p in pairs] for j in blocks]
        yield
        score_blocks = []
        for j in blocks:
            raws = []
            for p in pairs:
                raws.append(raw_even[j][p])
                raws.append(raw_odd[j][p])
            s_t = jnp.concatenate(raws, axis=1)
            score_blocks.append(jnp.where(band_first if j == 0 else band, s_t, NEG_BIG))
        yield
        maxes = [jnp.maximum(jnp.max(s_t, axis=0, keepdims=True), sink_row) for s_t in score_blocks]
        probs = [jnp.exp2(s_t - m).astype(_BF16) for s_t, m in zip(score_blocks, maxes)]
        yield
        outs = []
        for j in blocks:
            v_ext = jnp.concatenate([v_t[j][kk * d:(kk + 1) * d], v_t[j + 1][kk * d:(kk + 1) * d]], axis=1)
            outs.append(_dot(jnp.concatenate([v_ext, ones_rows], axis=0), probs[j]))
        yield
        for j in blocks:
            rows = slice(j * w, (j + 1) * w)
            denom = outs[j][d:d + 1, :] + jnp.exp2(sink_row - maxes[j])
            o_t = outs[j][0:d, :] * (1.0 / denom)
            for p in range(SWA_GROUP // 2):
                c0 = (kk * SWA_GROUP + 2 * p) * d
                both = jnp.concatenate([o_t[:, 2 * p * w:(2 * p + 1) * w], o_t[:, (2 * p + 1) * w:(2 * p + 2) * w]], axis=0)
                out_ref[rows, c0:c0 + pair] = both.astype(_BF16).T
        yield


def _mixers_kernel(sinks_ref, rq_ref, rk_ref, rv_ref, rg_ref, sq_ref, kc_ref, vc_ref, kp_ref, vp_ref, qg_ref, kg_ref,
                   retg_ref, attn_ref, state_ref):
    first = pl.program_id(1) == 0

    @pl.when(first)
    def _():
        state_ref[...] = jnp.zeros_like(state_ref)

    streams = [_retention_stages(rq_ref, rk_ref, rv_ref, rg_ref, retg_ref, state_ref),
               _swa_stages(first, sinks_ref, sq_ref, kc_ref, vc_ref, kp_ref, vp_ref, qg_ref, kg_ref, attn_ref)]
    while streams:
        streams = [s for s in streams if next(s, StopIteration) is not StopIteration]


def _mixers(ret_proj, swa_proj, sinks, q_gain2, k_gain2, batch, seq):
    n = ret_proj.shape[0]
    steps = seq // SEQ_TILE
    blocks_per_step = SEQ_TILE // SWA_WINDOW
    rowmap = lambda b, s: b * steps + s
    prevmap = lambda b, s: jnp.maximum((b * steps + s) * blocks_per_step - 1, 0)
    k_col = SWA_Q_W // SWA_KV_W
    return pl.pallas_call(
        _mixers_kernel,
        grid=(batch, steps),
        in_specs=[
            pl.BlockSpec(memory_space=pltpu.SMEM),
            pl.BlockSpec((SEQ_TILE, RET_QK_W), lambda b, s: (rowmap(b, s), 0)),
            pl.BlockSpec((SEQ_TILE, RET_QK_W), lambda b, s: (rowmap(b, s), 1)),
            pl.BlockSpec((SEQ_TILE, RET_V_W), lambda b, s: (rowmap(b, s), 1)),
            pl.BlockSpec((SEQ_TILE, RET_V_W), lambda b, s: (rowmap(b, s), 2)),
            pl.BlockSpec((SEQ_TILE, SWA_Q_W), lambda b, s: (rowmap(b, s), 0)),
            pl.BlockSpec((SEQ_TILE, SWA_KV_W), lambda b, s: (rowmap(b, s), k_col)),
            pl.BlockSpec((SEQ_TILE, SWA_KV_W), lambda b, s: (rowmap(b, s), k_col + 1)),
            pl.BlockSpec((SWA_WINDOW, SWA_KV_W), lambda b, s: (prevmap(b, s), k_col)),
            pl.BlockSpec((SWA_WINDOW, SWA_KV_W), lambda b, s: (prevmap(b, s), k_col + 1)),
            pl.BlockSpec((1, 2 * SWA_HEAD_DIM), lambda b, s: (0, 0)),
            pl.BlockSpec((1, 2 * SWA_HEAD_DIM), lambda b, s: (0, 0)),
        ],
        out_specs=[
            pl.BlockSpec((SEQ_TILE, RET_V_W), lambda b, s: (rowmap(b, s), 0)),
            pl.BlockSpec((SEQ_TILE, SWA_Q_W), lambda b, s: (rowmap(b, s), 0)),
        ],
        out_shape=[
            jax.ShapeDtypeStruct((n, RET_V_W), _BF16),
            jax.ShapeDtypeStruct((n, SWA_Q_W), _BF16),
        ],
        scratch_shapes=[pltpu.VMEM((RET_HEADS, RET_QK_DIM, RET_V_DIM), _F32)],
        compiler_params=pltpu.CompilerParams(
            dimension_semantics=("arbitrary", "arbitrary"), vmem_limit_bytes=VMEM_LIMIT),
        name="mixers",
    )(sinks, ret_proj, ret_proj, ret_proj, ret_proj, swa_proj, swa_proj, swa_proj, swa_proj, swa_proj,
      q_gain2, k_gain2)


def _outproj_kernel(x_ref, retg_ref, attn_ref, ga_ref, gb_ref, wro_ref, wso_ref, wout_ref, g2_ref,
                    wrt_ref, brt_ref, x1_ref, h2_ref, info_ref, info_t_ref, counts_ref, logits_ref):
    tm = ROW_TILE
    step = pl.program_id(0)
    cur = step % 2

    @pl.when(step == 0)
    def _():
        logits_ref[...] = jnp.zeros_like(logits_ref)

    logits = logits_ref[1 - cur]
    row = _iota((LANES, tm), 0)
    row_f = row.astype(_F32)
    is_group = (row >= ROUTER_GROUP_LANE) & (row < ROUTER_GROUP_LANE + MOE_GROUPS)

    subs = [slice(s * OUTPROJ_SUB, (s + 1) * OUTPROJ_SUB) for s in range(tm // OUTPROJ_SUB)]
    y_a = [_dot(retg_ref[rows, :], wro_ref[...]) for rows in subs]

    gl = jnp.where(is_group, logits, NEG_BIG)
    g_max = jnp.max(gl, axis=0, keepdims=True)
    g_prob = 1.0 / jnp.sum(jnp.exp(gl - g_max), axis=0, keepdims=True)
    g_idx = jnp.min(jnp.where(gl == g_max, row_f - ROUTER_GROUP_LANE, float(LANES)), axis=0, keepdims=True)

    y_b = [_dot(attn_ref[rows, :], wso_ref[...]) for rows in subs]

    e_lo = g_idx * MOE_EXPERTS_PER_GROUP
    in_group = (row_f >= e_lo) & (row_f < e_lo + MOE_EXPERTS_PER_GROUP)
    el = jnp.where(in_group, logits, NEG_BIG)
    t1 = jnp.max(el, axis=0, keepdims=True)
    i1 = jnp.min(jnp.where(el == t1, row_f, float(LANES)), axis=0, keepdims=True)

    merged = [(ga_ref[rows, :].astype(_F32) * a + gb_ref[rows, :].astype(_F32) * b).astype(_BF16)
              for rows, a, b in zip(subs, y_a, y_b)]
    mixed = [_dot(m, wout_ref[...]) for m in merged]
    x1 = [x_ref[rows, :] + m for rows, m in zip(subs, mixed)]
    for rows, v in zip(subs, x1):
        x1_ref[rows, :] = v

    el2 = jnp.where(row_f == i1, NEG_BIG, el)
    t2 = jnp.max(el2, axis=0, keepdims=True)
    i2 = jnp.min(jnp.where(el2 == t2, row_f, float(LANES)), axis=0, keepdims=True)
    e21 = jnp.exp(t2 - t1)
    w1 = g_prob / (1.0 + e21)
    w2 = g_prob * e21 / (1.0 + e21)

    h2 = [((v * lax.rsqrt(jnp.mean(v * v, axis=-1, keepdims=True) + NORM_EPS)) * g2_ref[...]).astype(_BF16)
          for v in x1]
    for rows, v in zip(subs, h2):
        h2_ref[rows, :] = v

    sel1 = row_f == i1
    sel2 = row_f == i2
    onehot = jnp.where(sel1 | sel2, 1.0, 0.0).astype(_BF16)
    earlier = (_iota((tm, tm), 0) < _iota((tm, tm), 1)).astype(_BF16)
    before = _dot(onehot, earlier)
    counts = _dot(onehot, jnp.ones((tm, LANES), _BF16))

    for rows, v in zip(subs, h2):
        logits_ref[cur, :, rows] = _dot_nt(wrt_ref[...], v) + brt_ref[:, 0:1]

    run = jnp.floor((counts + (RUN_ALIGN - 1.0)) * (1.0 / RUN_ALIGN)) * RUN_ALIGN
    lower_experts = (_iota((LANES, LANES), 0) > _iota((LANES, LANES), 1)).astype(_BF16)
    run_off = _dot(lower_experts, run.astype(_BF16))
    place = before + jnp.concatenate([run_off] * (tm // LANES), axis=1)
    pos1 = jnp.sum(jnp.where(sel1, place, 0.0), axis=0, keepdims=True)
    pos2 = jnp.sum(jnp.where(sel2, place, 0.0), axis=0, keepdims=True)
    counts_ref[...] = counts.T[0:SUBLANES, :]

    info_t = jnp.concatenate([i1, i2, w1, w2, pos1, pos2, jnp.zeros((2, tm), _F32)], axis=0)
    info_t_ref[...] = info_t
    info_ref[...] = jnp.concatenate([info_t, jnp.zeros((LANES - SUBLANES, tm), _F32)], axis=0).T


def _outproj(x2d, retg, attn, gates, w_ret_o, w_swa_o, w_out, g2, w_router_t, b_router_t):
    n = x2d.shape[0]
    tm = ROW_TILE
    n_tiles = n // tm
    const = lambda s: (0, 0)
    proj = lambda s: jnp.minimum(s, n_tiles - 1)
    routed = lambda s: jnp.maximum(s - 1, 0)
    return pl.pallas_call(
        _outproj_kernel,
        grid=(n_tiles + 1,),
        in_specs=[
            pl.BlockSpec((tm, D_MODEL), lambda s: (proj(s), 0)),
            pl.BlockSpec((tm, RET_V_W), lambda s: (proj(s), 0)),
            pl.BlockSpec((tm, SWA_Q_W), lambda s: (proj(s), 0)),
            pl.BlockSpec((tm, D_MODEL), lambda s: (proj(s), 0)),
            pl.BlockSpec((tm, D_MODEL), lambda s: (proj(s), 1)),
            pl.BlockSpec((RET_V_W, D_MODEL), const),
            pl.BlockSpec((SWA_Q_W, D_MODEL), const),
            pl.BlockSpec((D_MODEL, D_MODEL), const),
            pl.BlockSpec((1, D_MODEL), const),
            pl.BlockSpec((LANES, D_MODEL), const),
            pl.BlockSpec((LANES, LANES), const),
        ],
        out_specs=[
            pl.BlockSpec((tm, D_MODEL), lambda s: (proj(s), 0)),
            pl.BlockSpec((tm, D_MODEL), lambda s: (proj(s), 0)),
            pl.BlockSpec((tm, LANES), lambda s: (routed(s), 0)),
            pl.BlockSpec((SUBLANES, tm), lambda s: (0, routed(s))),
            pl.BlockSpec((SUBLANES, LANES), lambda s: (routed(s), 0)),
        ],
        scratch_shapes=[pltpu.VMEM((2, LANES, tm), _F32)],
        out_shape=[
            jax.ShapeDtypeStruct((n, D_MODEL), _F32),
            jax.ShapeDtypeStruct((n, D_MODEL), _BF16),
            jax.ShapeDtypeStruct((n, LANES), _F32),
            jax.ShapeDtypeStruct((SUBLANES, n), _F32),
            jax.ShapeDtypeStruct((n // tm * SUBLANES, LANES), _F32),
        ],
        compiler_params=pltpu.CompilerParams(
            dimension_semantics=("arbitrary",), vmem_limit_bytes=VMEM_LIMIT),
        name="outproj",
    )(x2d, retg, attn, gates, gates, w_ret_o, w_swa_o, w_out, g2, w_router_t, b_router_t)


def _piece_sizes(largest):
    piece = largest
    while piece >= RUN_ALIGN:
        yield piece
        piece //= 2


PIECE_CLASSES = tuple(k * RUN_ALIGN for k in (4, 3, 2, 1))
PIECE_LIST_KEYS = tuple(f"{field}{size}" for size in PIECE_CLASSES for field in ("n", "off", "dst"))


def _piece_cap(size):
    return SORT_ROWS // size if size == PIECE_CLASSES[0] else MOE_N_EXPERTS


def _start_run_copies(pieces, tile, make_copy):
    for k, size in enumerate(PIECE_CLASSES):
        n_ref, off_ref, dst_ref = pieces[3 * k:3 * k + 3]

        def start(j, carry, off_ref=off_ref, dst_ref=dst_ref, size=size):
            i = tile * _piece_cap(size) + j
            make_copy(pl.multiple_of(off_ref[i], RUN_ALIGN), pl.multiple_of(dst_ref[i], RUN_ALIGN), size).start()
            return carry

        lax.fori_loop(0, n_ref[tile], start, 0)


def _wait_rows(rows, make_copy):
    for piece in _piece_sizes(SORT_ROWS_POW2 // 2):
        pl.when((rows & piece) != 0)(make_copy(0, 0, piece).wait)


def _zero_fill_padding(pad_start_ref, pad_len_ref, zeros_ref, xs_hbm, sem, wait):
    def per_expert(e, carry):
        pos = pad_start_ref[e]
        length = pad_len_ref[e]
        for piece in _piece_sizes(EXPERT_TILE // 2):
            copy = pltpu.make_async_copy(
                zeros_ref.at[pl.ds(0, piece)], xs_hbm.at[pl.ds(pl.multiple_of(pos, RUN_ALIGN), piece)], sem)
            pl.when((length & piece) != 0)(copy.wait if wait else copy.start)
            pos = pos + (length & piece)
        return carry

    lax.fori_loop(0, MOE_N_EXPERTS, per_expert, 0)

    tail_start = pad_start_ref[MOE_N_EXPERTS]
    n_rows = xs_hbm.shape[0]
    half = EXPERT_TILE // 2

    def per_half_tile(j, carry):
        pos = tail_start + j * half
        copy = pltpu.make_async_copy(
            zeros_ref, xs_hbm.at[pl.ds(pl.multiple_of(jnp.minimum(pos, n_rows - half), RUN_ALIGN), half)], sem)
        pl.when(pos < n_rows)(copy.wait if wait else copy.start)
        return carry

    lax.fori_loop(0, n_rows // half, per_half_tile, 0)


def _dispatch_kernel(*refs):
    pieces = refs[:len(PIECE_LIST_KEYS)]
    rows_ref, pad_start_ref, pad_len_ref, h2_ref, info_t_ref, xs_hbm, sorted_ref, zeros_ref, sems, zero_sem = (
        refs[len(PIECE_LIST_KEYS):])
    tile = pl.program_id(0)
    last = pl.num_programs(0) - 1
    slot = tile % RING
    tm = ROW_TILE

    @pl.when(tile == 0)
    def _():
        zeros_ref[...] = jnp.zeros_like(zeros_ref)
        _zero_fill_padding(pad_start_ref, pad_len_ref, zeros_ref, xs_hbm, zero_sem, wait=False)
        _zero_fill_padding(pad_start_ref, pad_len_ref, zeros_ref, xs_hbm, zero_sem, wait=True)

    pos = info_t_ref[4:6, :].astype(jnp.int32)

    def sort_rows(lo, hi):
        place = _iota((hi - lo, tm), 0) + lo
        onehot = ((place == pos[0:1, :]) | (place == pos[1:2, :])).astype(_BF16)
        for c in range(0, D_MODEL, 256):
            sorted_ref[slot, lo:hi, c:c + 256] = _dot(onehot, h2_ref[:, c:c + 256]).astype(_BF16)

    sort_rows(0, SORT_COMMON)
    pl.when(rows_ref[tile] > SORT_COMMON)(lambda: sort_rows(SORT_COMMON, SORT_ROWS))

    def copy_maker(t):
        s = t % RING

        def make_copy(off, dst, piece):
            return pltpu.make_async_copy(
                sorted_ref.at[s, pl.ds(off, piece)], xs_hbm.at[pl.ds(dst, piece)], sems.at[s])

        return make_copy

    _start_run_copies(pieces, tile, copy_maker(tile))

    @pl.when(tile >= RING - 1)
    def _():
        _wait_rows(rows_ref[tile - (RING - 1)], copy_maker(tile - (RING - 1)))

    for back in range(RING - 1):
        @pl.when(jnp.logical_and(tile == last, tile >= back))
        def _(back=back):
            _wait_rows(rows_ref[tile - back], copy_maker(tile - back))


def _dispatch(plan, h2, info_t, n_rows):
    n = h2.shape[0]
    grid_spec = pltpu.PrefetchScalarGridSpec(
        num_scalar_prefetch=len(PIECE_LIST_KEYS) + 3,
        grid=(n // ROW_TILE,),
        in_specs=[
            pl.BlockSpec((ROW_TILE, D_MODEL), lambda i, *_: (i, 0)),
            pl.BlockSpec((SUBLANES, ROW_TILE), lambda i, *_: (0, i)),
        ],
        out_specs=pl.BlockSpec(memory_space=pl.ANY),
        scratch_shapes=[
            pltpu.VMEM((RING, SORT_ROWS, D_MODEL), _BF16),
            pltpu.VMEM((EXPERT_TILE // 2, D_MODEL), _BF16),
            pltpu.SemaphoreType.DMA((RING,)),
            pltpu.SemaphoreType.DMA(()),
        ],
    )
    return pl.pallas_call(
        _dispatch_kernel,
        grid_spec=grid_spec,
        out_shape=jax.ShapeDtypeStruct((n_rows, D_MODEL), _BF16),
        compiler_params=pltpu.CompilerParams(
            dimension_semantics=("arbitrary",), vmem_limit_bytes=VMEM_LIMIT),
        name="dispatch",
    )(*[plan[k] for k in PIECE_LIST_KEYS], plan["rows"], plan["pad_start"], plan["pad_len"], h2, info_t)


def _experts_kernel(te_ref, nt_ref, order_ref, next_ref, xs_ref, wg_hbm, wu_hbm, wd_hbm, ys_ref,
                    wg_f, wu_f, wd_f, wg_s, wu_s, wd_s, sems):
    i = pl.program_id(0)
    in_use = i < nt_ref[0]
    slot = order_ref[i] % 2
    prev = te_ref[jnp.maximum(i - 1, 0)]
    changed = jnp.logical_or(i == 0, te_ref[i] != prev)

    def fetch(expert, s):
        return [pltpu.make_async_copy(hbm.at[expert], stage.at[s], sems.at[s, j])
                for j, (hbm, stage) in enumerate(((wg_hbm, wg_f), (wu_hbm, wu_f), (wd_hbm, wd_f)))]

    @pl.when(i == 0)
    def _():
        for copy in fetch(te_ref[0], 0):
            copy.start()

    @pl.when(jnp.logical_and(in_use, changed))
    def _():
        for copy in fetch(te_ref[i], slot):
            copy.wait()
        wg_s[...] = wg_f[slot].astype(_BF16)
        wu_s[...] = wu_f[slot].astype(_BF16)
        wd_s[...] = wd_f[slot].astype(_BF16)

        @pl.when(next_ref[i] != te_ref[i])
        def _():
            for copy in fetch(next_ref[i], 1 - slot):
                copy.start()

    @pl.when(in_use)
    def _():
        subs = [slice(s * EXPERT_SUB, (s + 1) * EXPERT_SUB) for s in range(EXPERT_TILE // EXPERT_SUB)]
        gate = [_dot(xs_ref[rows, :], wg_s[...]) for rows in subs]
        up = [_dot(xs_ref[rows, :], wu_s[...]) for rows in subs]
        act = [((g * _sigmoid(g)) * u).astype(_BF16) for g, u in zip(gate, up)]
        for rows, a in zip(subs, act):
            ys_ref[rows, :] = _dot(a, wd_s[...]).astype(_BF16)


def _experts(plan, xs, w_gate, w_up, w_down):
    n_rows = xs.shape[0]
    last_used = lambda i, nt: jnp.minimum(i, nt[0] - 1)
    grid_spec = pltpu.PrefetchScalarGridSpec(
        num_scalar_prefetch=4,
        grid=(n_rows // EXPERT_TILE,),
        in_specs=[
            pl.BlockSpec((EXPERT_TILE, D_MODEL), lambda i, te, nt, *_: (last_used(i, nt), 0)),
            pl.BlockSpec(memory_space=pl.ANY),
            pl.BlockSpec(memory_space=pl.ANY),
            pl.BlockSpec(memory_space=pl.ANY),
        ],
        out_specs=pl.BlockSpec((EXPERT_TILE, D_MODEL), lambda i, te, nt, *_: (last_used(i, nt), 0)),
        scratch_shapes=[
            pltpu.VMEM((2, D_MODEL, MOE_D_FF), _F32),
            pltpu.VMEM((2, D_MODEL, MOE_D_FF), _F32),
            pltpu.VMEM((2, MOE_D_FF, D_MODEL), _F32),
            pltpu.VMEM((D_MODEL, MOE_D_FF), _BF16),
            pltpu.VMEM((D_MODEL, MOE_D_FF), _BF16),
            pltpu.VMEM((MOE_D_FF, D_MODEL), _BF16),
            pltpu.SemaphoreType.DMA((2, 3)),
        ],
    )
    return pl.pallas_call(
        _experts_kernel,
        grid_spec=grid_spec,
        out_shape=jax.ShapeDtypeStruct((n_rows, D_MODEL), _BF16),
        input_output_aliases={4: 0},
        compiler_params=pltpu.CompilerParams(
            dimension_semantics=("arbitrary",), vmem_limit_bytes=VMEM_LIMIT),
        name="experts",
    )(plan["tile_expert"], plan["tiles_used"], plan["tile_order"], plan["tile_next"], xs, w_gate, w_up, w_down)


def _combine_kernel(*refs):
    pieces = refs[:len(PIECE_LIST_KEYS)]
    rows_ref, x1_ref, info_ref, ys_hbm, out_ref, block_ref, sems = refs[len(PIECE_LIST_KEYS):]
    tile = pl.program_id(0)
    last = pl.num_programs(0) - 1
    slot = tile % RING
    tm = ROW_TILE

    def copy_maker(t):
        s = t % RING

        def make_copy(off, dst, piece):
            return pltpu.make_async_copy(
                ys_hbm.at[pl.ds(dst, piece)], block_ref.at[s, pl.ds(off, piece)], sems.at[s])

        return make_copy

    @pl.when(tile == 0)
    def _():
        block_ref[...] = jnp.zeros_like(block_ref)
        for ahead in range(RING - 1):
            pl.when(ahead <= last)(lambda ahead=ahead: _start_run_copies(pieces, ahead, copy_maker(ahead)))

    @pl.when(tile + (RING - 1) <= last)
    def _():
        _start_run_copies(pieces, tile + (RING - 1), copy_maker(tile + (RING - 1)))

    _wait_rows(rows_ref[tile], copy_maker(tile))

    info = info_ref[...]

    def weights_of(lo, hi):
        place = (_iota((tm, hi - lo), 1) + lo).astype(_F32)
        return (jnp.where(place == info[:, 4:5], info[:, 2:3], 0.0)
                + jnp.where(place == info[:, 5:6], info[:, 3:4], 0.0)).astype(_BF16)

    weights = weights_of(0, SORT_COMMON)
    for c in range(0, D_MODEL, 256):
        out_ref[:, c:c + 256] = x1_ref[:, c:c + 256] + _dot(weights, block_ref[slot, 0:SORT_COMMON, c:c + 256])

    @pl.when(rows_ref[tile] > SORT_COMMON)
    def _():
        tail = weights_of(SORT_COMMON, SORT_ROWS)
        for c in range(0, D_MODEL, 256):
            out_ref[:, c:c + 256] += _dot(tail, block_ref[slot, SORT_COMMON:SORT_ROWS, c:c + 256])


def _combine(plan, x1, info, ys):
    n = x1.shape[0]
    grid_spec = pltpu.PrefetchScalarGridSpec(
        num_scalar_prefetch=len(PIECE_LIST_KEYS) + 1,
        grid=(n // ROW_TILE,),
        in_specs=[
            pl.BlockSpec((ROW_TILE, D_MODEL), lambda i, *_: (i, 0)),
            pl.BlockSpec((ROW_TILE, LANES), lambda i, *_: (i, 0)),
            pl.BlockSpec(memory_space=pl.ANY),
        ],
        out_specs=pl.BlockSpec((ROW_TILE, D_MODEL), lambda i, *_: (i, 0)),
        scratch_shapes=[pltpu.VMEM((RING, SORT_ROWS, D_MODEL), _BF16), pltpu.SemaphoreType.DMA((RING,))],
    )
    return pl.pallas_call(
        _combine_kernel,
        grid_spec=grid_spec,
        out_shape=jax.ShapeDtypeStruct((n, D_MODEL), _F32),
        compiler_params=pltpu.CompilerParams(
            dimension_semantics=("arbitrary",), vmem_limit_bytes=VMEM_LIMIT),
        name="combine",
    )(*[plan[k] for k in PIECE_LIST_KEYS], plan["rows"], x1, info, ys)


def _rotary_tables(seq):
    half = RET_QK_DIM // 2
    inv = ROPE_BASE ** (-jnp.arange(half, dtype=_F32) / half)
    hi = (jnp.arange(seq // RET_CHUNK, dtype=jnp.int32) * RET_CHUNK).astype(_F32)[:, None] * inv[None, :]
    lo = jnp.arange(RET_CHUNK, dtype=jnp.int32).astype(_F32)[:, None] * inv[None, :]
    cos_hi, sin_hi = jnp.cos(hi)[:, None, :], jnp.sin(hi)[:, None, :]
    cos_lo, sin_lo = jnp.cos(lo)[None, :, :], jnp.sin(lo)[None, :, :]
    cos = (cos_hi * cos_lo - sin_hi * sin_lo).reshape(seq, half)
    sin = (sin_hi * cos_lo + cos_hi * sin_lo).reshape(seq, half)
    return jnp.concatenate([cos, cos], axis=1), jnp.concatenate([-sin, sin], axis=1)


def _expert_row_bound(n_tokens):
    rows = (2 * n_tokens + (n_tokens // ROW_TILE) * MOE_N_EXPERTS * (RUN_ALIGN - 1)
            + MOE_N_EXPERTS * (EXPERT_TILE - RUN_ALIGN))
    return -(-rows // EXPERT_TILE) * EXPERT_TILE


def _routing_plan(counts_out, n_rows):
    te = EXPERT_TILE
    n_tok_tiles = counts_out.shape[0] // SUBLANES
    counts = counts_out.reshape(n_tok_tiles, SUBLANES, LANES)[:, 0, :MOE_N_EXPERTS].astype(jnp.int32)
    run = (counts + RUN_ALIGN - 1) // RUN_ALIGN * RUN_ALIGN
    off = jnp.cumsum(run, axis=1) - run
    total = jnp.sum(run, axis=0)
    padded = (total + te - 1) // te * te
    pends = jnp.cumsum(padded)
    pstarts = pends - padded
    dst = pstarts[None, :] + jnp.cumsum(run, axis=0) - run
    tile_start = jnp.arange(n_rows // te, dtype=jnp.int32) * te
    tile_expert = jnp.sum((tile_start[:, None] >= pends[None, :]).astype(jnp.int32), axis=1)
    tile_expert = jnp.minimum(tile_expert, MOE_N_EXPERTS - 1)
    i32 = lambda v: v.astype(jnp.int32)

    experts = jnp.arange(MOE_N_EXPERTS, dtype=jnp.int32)
    has_rows = total > 0
    rank = jnp.cumsum(has_rows.astype(jnp.int32)) - 1
    later = jnp.where((experts[None, :] > experts[:, None]) & has_rows[None, :], experts[None, :], MOE_N_EXPERTS)
    following = jnp.min(later, axis=1)
    following = jnp.where(following == MOE_N_EXPERTS, experts, following)
    of_tile = tile_expert[:, None] == experts[None, :]
    tile_order = jnp.sum(jnp.where(of_tile, rank[None, :], 0), axis=1)
    tile_next = jnp.sum(jnp.where(of_tile, following[None, :], 0), axis=1)

    largest = PIECE_CLASSES[0]
    n_big = run // largest
    rest = run - largest * n_big
    big_first = jnp.cumsum(n_big, axis=1) - n_big
    k_big = jnp.arange(_piece_cap(largest), dtype=jnp.int32)[None, :, None]
    owns_big = (k_big >= big_first[:, None, :]) & (k_big < (big_first + n_big)[:, None, :])
    pick_big = lambda v: jnp.sum(jnp.where(owns_big, (v - largest * big_first)[:, None, :], 0), axis=-1)
    lists = {f"n{largest}": i32(jnp.sum(n_big, axis=1)),
             f"off{largest}": i32((pick_big(off) + largest * k_big[..., 0]).reshape(-1)),
             f"dst{largest}": i32((pick_big(dst) + largest * k_big[..., 0]).reshape(-1))}
    k_small = jnp.arange(MOE_N_EXPERTS, dtype=jnp.int32)[None, :, None]
    for size in PIECE_CLASSES[1:]:
        has = (rest == size).astype(jnp.int32)
        index = jnp.cumsum(has, axis=1) - has
        owns = (has[:, None, :] == 1) & (index[:, None, :] == k_small)
        pick = lambda v, owns=owns: jnp.sum(jnp.where(owns, (v + largest * n_big)[:, None, :], 0), axis=-1)
        lists.update({f"n{size}": i32(jnp.sum(has, axis=1)),
                      f"off{size}": i32(pick(off).reshape(-1)), f"dst{size}": i32(pick(dst).reshape(-1))})
    return {
        **lists,
        "rows": i32(jnp.sum(run, axis=1)),
        "pad_start": i32(jnp.concatenate([pstarts + total, pends[-1:]])), "pad_len": i32(padded - total),
        "tile_expert": i32(tile_expert), "tiles_used": i32(pends[-1:] // te),
        "tile_order": i32(tile_order), "tile_next": i32(tile_next),
    }


def kernel(x, norm_mix_g, w_in, ret_gn_g, w_ret_o, q_norm_g, k_norm_g, sinks, w_swa_o, w_out, norm_ffn_g,
           w_router_group, b_router_group, w_router_expert, b_router_expert, w_gate, w_up, w_down):
    batch, seq, d = x.shape
    n = batch * seq
    assert d == D_MODEL and seq % SEQ_TILE == 0 and n % ROW_TILE == 0
    cos_tab, sin_tab = _rotary_tables(seq)
    n_rows = _expert_row_bound(n)
    for l in range(w_in.shape[0]):
        x2d = x.reshape(n, d)
        ret_proj, swa_proj, gates = _inproj(x2d, norm_mix_g[l][None, :], w_in[l].astype(_BF16), cos_tab, sin_tab,
                                            ret_gn_g[l].reshape(1, RET_V_W))
        retg, attn = _mixers(ret_proj, swa_proj, sinks[l], jnp.tile(q_norm_g[l], 2)[None, :],
                             jnp.tile(k_norm_g[l], 2)[None, :], batch, seq)
        pad = LANES - MOE_N_EXPERTS - MOE_GROUPS
        w_router_t = jnp.concatenate(
            [w_router_expert[l].T, w_router_group[l].T, jnp.zeros((pad, d), _F32)], axis=0).astype(_BF16)
        b_router = jnp.concatenate([b_router_expert[l], b_router_group[l], jnp.zeros((pad,), _F32)])
        b_router_t = jnp.broadcast_to(b_router[:, None], (LANES, LANES))
        x1, h2, info, info_t, counts = _outproj(
            x2d, retg, attn, gates, w_ret_o[l].astype(_BF16), w_swa_o[l].astype(_BF16), w_out[l].astype(_BF16),
            norm_ffn_g[l][None, :], w_router_t, b_router_t)
        plan = _routing_plan(counts, n_rows)
        xs = _dispatch(plan, h2, info_t, n_rows)
        ys = _experts(plan, xs, w_gate[l], w_up[l], w_down[l])
        x = _combine(plan, x1, info, ys).reshape(batch, seq, d)
    return x
```

```python
import math

import numpy as np
import jax
import jax.numpy as jnp
from jax import lax
from jax.experimental import pallas as pl
from jax.experimental.pallas import tpu as pltpu

D_MODEL = 1024
RET_HEADS = 4
RET_QK_DIM = 128
RET_V_DIM = 256
RET_CHUNK = 128
ROPE_BASE = 10000.0
SWA_Q_HEADS = 8
SWA_KV_HEADS = 2
SWA_HEAD_DIM = 64
SWA_WINDOW = 128
MOE_GROUPS = 4
MOE_EXPERTS_PER_GROUP = 8
MOE_D_FF = 512
NORM_EPS = 1e-6

RET_QK_W = RET_HEADS * RET_QK_DIM
RET_V_W = RET_HEADS * RET_V_DIM
SWA_Q_W = SWA_Q_HEADS * SWA_HEAD_DIM
SWA_KV_W = SWA_KV_HEADS * SWA_HEAD_DIM
SWA_GROUP = SWA_Q_HEADS // SWA_KV_HEADS
MOE_N_EXPERTS = MOE_GROUPS * MOE_EXPERTS_PER_GROUP

RET_COLS = 2 * RET_QK_W + 2 * RET_V_W
SWA_COLS = SWA_Q_W + 2 * SWA_KV_W
GATE_COLS = 2 * D_MODEL
IN_WIDTH = RET_COLS + SWA_COLS + GATE_COLS

LANES = 128
SUBLANES = 8
ROUTER_GROUP_LANE = 32
NEG_BIG = -1e30
LOG2_E = math.log2(math.e)

ROW_TILE = 512
SEQ_TILE = 512
EXPERT_TILE = 512
OUTPROJ_SUB = 128
INPROJ_SUB = 256
EXPERT_SUB = 256
RUN_ALIGN = 16
SORT_ROWS = 2 * ROW_TILE + MOE_N_EXPERTS * RUN_ALIGN
SORT_ROWS_POW2 = 1 << SORT_ROWS.bit_length()
SORT_COMMON = 2 * ROW_TILE + MOE_N_EXPERTS * RUN_ALIGN * 9 // 16
RING = 3
VMEM_LIMIT = 56 * 1024 * 1024

RET_LOG_DECAY = tuple(float(np.log1p(-np.exp2(-5.0 - h))) for h in range(RET_HEADS))

_BF16 = jnp.bfloat16
_F32 = jnp.float32


def _dot(a, b):
    return jnp.dot(a, b, preferred_element_type=_F32)


def _dot_nt(a, b):
    return lax.dot_general(a, b, (((1,), (1,)), ((), ())), preferred_element_type=_F32)


def _sigmoid(v):
    return 1.0 / (1.0 + jnp.exp(-v))


def _split_bf16(v):
    hi = v.astype(_BF16)
    return hi, (v - hi.astype(_F32)).astype(_BF16)


def _iota(shape, dim):
    return lax.broadcasted_iota(jnp.int32, shape, dim)


def _inproj_kernel(x_ref, g_ref, w_ref, cos_ref, sin_ref, gn_ref, ret_ref, swa_ref, gate_ref):
    subs = [slice(s * INPROJ_SUB, (s + 1) * INPROJ_SUB) for s in range(ROW_TILE // INPROJ_SUB)]

    def normed(rows):
        x = x_ref[rows, :]
        r = lax.rsqrt(jnp.mean(x * x, axis=-1, keepdims=True) + NORM_EPS)
        return ((x * r) * g_ref[...]).astype(_BF16)
    chunk_pos = (_iota((INPROJ_SUB, RET_QK_DIM), 0) % RET_CHUNK + 1).astype(_F32)

    def rotary_decay(base, sign, scale):
        def epilogue(rows, raw):
            cos = cos_ref[rows, :]
            sin = sin_ref[rows, :]
            for hd in range(RET_HEADS):
                v = raw[:, hd * RET_QK_DIM:(hd + 1) * RET_QK_DIM]
                rot = v * cos + pltpu.roll(v, RET_QK_DIM // 2, 1) * sin
                decay = jnp.exp((sign * RET_LOG_DECAY[hd]) * chunk_pos) * scale
                ret_ref[rows, base + hd * RET_QK_DIM:base + (hd + 1) * RET_QK_DIM] = (rot * decay).astype(_BF16)
        return epilogue

    def store(ref, lo, fn):
        def epilogue(rows, raw):
            ref[rows, lo:lo + raw.shape[1]] = fn(raw).astype(_BF16)
        return epilogue

    v_base = 2 * RET_QK_W
    g_base = v_base + RET_V_W
    gate_base = RET_COLS + SWA_COLS
    jobs = [(0, RET_QK_W, rotary_decay(0, 1.0, 1.0)),
            (RET_QK_W, RET_QK_W, rotary_decay(RET_QK_W, -1.0, RET_QK_DIM ** -0.5))]
    jobs += [(v_base + c, 512, store(ret_ref, v_base + c, lambda raw: raw)) for c in range(0, RET_V_W, 512)]
    jobs += [(g_base + c, 512, store(ret_ref, g_base + c,
                                     lambda raw, c=c: (raw * _sigmoid(raw)) * gn_ref[:, c:c + 512]))
             for c in range(0, RET_V_W, 512)]
    jobs += [(RET_COLS + c, 256, store(swa_ref, c, lambda raw: raw)) for c in range(0, SWA_COLS, 256)]
    jobs += [(gate_base + c, 512, store(gate_ref, c, _sigmoid)) for c in range(0, GATE_COLS, 512)]
    hs = [normed(rows) for rows in subs]
    for lo, width, epilogue in jobs:
        raws = [_dot(h, w_ref[:, lo:lo + width]) for h in hs]
        for rows, raw in zip(subs, raws):
            epilogue(rows, raw)


def _inproj(x2d, g, w_in_bf16, cos_tab, sin_tab, gn_row):
    n = x2d.shape[0]
    grid = (n // ROW_TILE,)
    seq_tiles = cos_tab.shape[0] // ROW_TILE
    whole_in_vmem = pl.BlockSpec(memory_space=pltpu.VMEM)
    in_hbm = pl.BlockSpec(memory_space=pl.ANY)

    def streamed(x_hbm, g_ref, w_ref, cos_hbm, sin_hbm, gn_ref, ret_hbm, swa_hbm, gate_hbm):
        def step(x_ref, cos_ref, sin_ref, ret_ref, swa_ref, gate_ref):
            _inproj_kernel(x_ref, g_ref, w_ref, cos_ref, sin_ref, gn_ref, ret_ref, swa_ref, gate_ref)

        pltpu.emit_pipeline(
            step,
            grid=grid,
            in_specs=[
                pl.BlockSpec((ROW_TILE, D_MODEL), lambda i: (i, 0)),
                pl.BlockSpec((ROW_TILE, RET_QK_DIM), lambda i: (i % seq_tiles, 0)),
                pl.BlockSpec((ROW_TILE, RET_QK_DIM), lambda i: (i % seq_tiles, 0)),
            ],
            out_specs=[
                pl.BlockSpec((ROW_TILE, RET_COLS), lambda i: (i, 0)),
                pl.BlockSpec((ROW_TILE, SWA_COLS), lambda i: (i, 0)),
                pl.BlockSpec((ROW_TILE, GATE_COLS), lambda i: (i, 0)),
            ],
        )(x_hbm, cos_hbm, sin_hbm, ret_hbm, swa_hbm, gate_hbm)

    return pl.pallas_call(
        streamed,
        in_specs=[in_hbm, whole_in_vmem, whole_in_vmem, in_hbm, in_hbm, whole_in_vmem],
        out_specs=[in_hbm, in_hbm, in_hbm],
        out_shape=[
            jax.ShapeDtypeStruct((n, RET_COLS), _BF16),
            jax.ShapeDtypeStruct((n, SWA_COLS), _BF16),
            jax.ShapeDtypeStruct((n, GATE_COLS), _BF16),
        ],
        compiler_params=pltpu.CompilerParams(vmem_limit_bytes=VMEM_LIMIT),
        name="inproj",
    )(x2d, g, w_in_bf16, cos_tab, sin_tab, gn_row)


def _retention_stages(q_ref, k_ref, v_ref, g_ref, out_ref, state_ref):
    c = RET_CHUNK
    causal = _iota((c, c), 0) >= _iota((c, c), 1)

    for h in range(RET_HEADS):
        chunk_decay = math.exp(RET_LOG_DECAY[h] * c)
        qs = slice(h * RET_QK_DIM, (h + 1) * RET_QK_DIM)
        vs = slice(h * RET_V_DIM, (h + 1) * RET_V_DIM)
        chunks = [slice(ci * c, (ci + 1) * c) for ci in range(SEQ_TILE // c)]
        scores = [jnp.where(causal, _dot_nt(q_ref[rows, qs], k_ref[rows, qs]), 0.0).astype(_BF16) for rows in chunks]
        kvs = [_dot(k_ref[rows, qs].T, v_ref[rows, vs]) for rows in chunks]
        yield
        state = state_ref[h]
        states = []
        for kv in kvs:
            states.append(state.astype(_BF16))
            state = chunk_decay * (state + kv)
        state_ref[h] = state
        rets = [_dot(jnp.concatenate([s, q_ref[rows, qs]], axis=1), jnp.concatenate([v_ref[rows, vs], st], axis=0))
                for rows, s, st in zip(chunks, scores, states)]
        yield
        for rows, ret in zip(chunks, rets):
            mu = jnp.mean(ret, axis=-1, keepdims=True)
            dev = ret - mu
            var = jnp.mean(dev * dev, axis=-1, keepdims=True)
            out_ref[rows, vs] = ((dev * lax.rsqrt(var + NORM_EPS)) * g_ref[rows, vs].astype(_F32)).astype(_BF16)
        yield


def _swa_stages(first, sinks_ref, q_ref, kc_ref, vc_ref, kp_ref, vp_ref, qg_ref, kg_ref, out_ref):
    w = SWA_WINDOW
    d = SWA_HEAD_DIM
    n_blk = SEQ_TILE // w
    pair = 2 * d

    k_both = jnp.concatenate([kp_ref[...], kc_ref[...]], axis=0).astype(_F32)
    same_half = (_iota((pair, pair), 0) // d == _iota((pair, pair), 1) // d).astype(_BF16)
    k_hi, k_lo = _split_bf16(k_both * k_both)
    k_ssq = _dot(k_hi, same_half) + _dot(k_lo, same_half)
    k_n = k_both * lax.rsqrt(k_ssq * (1.0 / d) + NORM_EPS) * (kg_ref[...] * (d ** -0.5 * LOG2_E))
    v_both = jnp.concatenate([vp_ref[...], vc_ref[...]], axis=0)
    v_t = [v_both[b * w:(b + 1) * w].T for b in range(n_blk + 1)]

    key = _iota((2 * w, SWA_GROUP * w), 0)
    qry = _iota((2 * w, SWA_GROUP * w), 1) % w
    band = (key > qry) & (key <= qry + w)
    band_first = band & jnp.logical_or(key >= w, jnp.logical_not(first))
    head_of_lane = _iota((1, SWA_GROUP * w), 1) // w
    half_of_lane = _iota((1, pair), 1) // d
    ones_rows = jnp.ones((2 * SUBLANES, 2 * w), _BF16)
    q_gain = qg_ref[...]
    yield

    for kk in range(SWA_KV_HEADS):
        k_native = jnp.where(half_of_lane == kk, k_n, 0.0)
        k_moved = pltpu.roll(k_native, d, 1)
        k_even, k_odd = (k_native, k_moved) if kk == 0 else (k_moved, k_native)
        k_even = k_even.astype(_BF16)
        k_odd = k_odd.astype(_BF16)
        sink_row = jnp.zeros((1, SWA_GROUP * w), _F32)
        for g in range(SWA_GROUP):
            sink_row = jnp.where(head_of_lane == g, sinks_ref[kk * SWA_GROUP + g] * LOG2_E, sink_row)
        blocks = range(n_blk)
        pairs = range(SWA_GROUP // 2)
        q_pairs = [[q_ref[j * w:(j + 1) * w, (kk * SWA_GROUP + 2 * p) * d:(kk * SWA_GROUP + 2 * p + 2) * d].astype(_F32)
                    for p in pairs] for j in blocks]
        q_splits = [[_split_bf16(q * q) for q in qs] for qs in q_pairs]
        q_ssq = [[_dot(hi, same_half) + _dot(lo, same_half) for hi, lo in sp] for sp in q_splits]
        q_ns = [[(q * lax.rsqrt(s * (1.0 / d) + NORM_EPS) * q_gain).astype(_BF16) for q, s in zip(qs, ss)]
                for qs, ss in zip(q_pairs, q_ssq)]
        yield
        raw_even = [[_dot_nt(k_even[j * w:(j + 2) * w], q_ns[j][p]) for p in pairs] for j in blocks]
        raw_odd = [[_dot_nt(k_odd[j * w:(j + 2) * w], q_ns[j][p]) for p in pairs] for j in blocks]
        yield
        score_blocks = []
        for j in blocks:
            raws = []
            for p in pairs:
                raws.append(raw_even[j][p])
                raws.append(raw_odd[j][p])
            s_t = jnp.concatenate(raws, axis=1)
            score_blocks.append(jnp.where(band_first if j == 0 else band, s_t, NEG_BIG))
        yield
        maxes = [jnp.maximum(jnp.max(s_t, axis=0, keepdims=True), sink_row) for s_t in score_blocks]
        probs = [jnp.exp2(s_t - m).astype(_BF16) for s_t, m in zip(score_blocks, maxes)]
        yield
        outs = []
        for j in blocks:
            v_ext = jnp.concatenate([v_t[j][kk * d:(kk + 1) * d], v_t[j + 1][kk * d:(kk + 1) * d]], axis=1)
            outs.append(_dot(jnp.concatenate([v_ext, ones_rows], axis=0), probs[j]))
        yield
        for j in blocks:
            rows = slice(j * w, (j + 1) * w)
            denom = outs[j][d:d + 1, :] + jnp.exp2(sink_row - maxes[j])
            o_t = outs[j][0:d, :] * (1.0 / denom)
            for p in range(SWA_GROUP // 2):
                c0 = (kk * SWA_GROUP + 2 * p) * d
                both = jnp.concatenate([o_t[:, 2 * p * w:(2 * p + 1) * w], o_t[:, (2 * p + 1) * w:(2 * p + 2) * w]], axis=0)
                out_ref[rows, c0:c0 + pair] = both.astype(_BF16).T
        yield


def _mixers_kernel(sinks_ref, rq_ref, rk_ref, rv_ref, rg_ref, sq_ref, kc_ref, vc_ref, kp_ref, vp_ref, qg_ref, kg_ref,
                   retg_ref, attn_ref, state_ref):
    first = pl.program_id(1) == 0

    @pl.when(first)
    def _():
        state_ref[...] = jnp.zeros_like(state_ref)

    streams = [_retention_stages(rq_ref, rk_ref, rv_ref, rg_ref, retg_ref, state_ref),
               _swa_stages(first, sinks_ref, sq_ref, kc_ref, vc_ref, kp_ref, vp_ref, qg_ref, kg_ref, attn_ref)]
    while streams:
        streams = [s for s in streams if next(s, StopIteration) is not StopIteration]


def _mixers(ret_proj, swa_proj, sinks, q_gain2, k_gain2, batch, seq):
    n = ret_proj.shape[0]
    steps = seq // SEQ_TILE
    blocks_per_step = SEQ_TILE // SWA_WINDOW
    rowmap = lambda b, s: b * steps + s
    prevmap = lambda b, s: jnp.maximum((b * steps + s) * blocks_per_step - 1, 0)
    k_col = SWA_Q_W // SWA_KV_W
    return pl.pallas_call(
        _mixers_kernel,
        grid=(batch, steps),
        in_specs=[
            pl.BlockSpec(memory_space=pltpu.SMEM),
            pl.BlockSpec((SEQ_TILE, RET_QK_W), lambda b, s: (rowmap(b, s), 0)),
            pl.BlockSpec((SEQ_TILE, RET_QK_W), lambda b, s: (rowmap(b, s), 1)),
            pl.BlockSpec((SEQ_TILE, RET_V_W), lambda b, s: (rowmap(b, s), 1)),
            pl.BlockSpec((SEQ_TILE, RET_V_W), lambda b, s: (rowmap(b, s), 2)),
            pl.BlockSpec((SEQ_TILE, SWA_Q_W), lambda b, s: (rowmap(b, s), 0)),
            pl.BlockSpec((SEQ_TILE, SWA_KV_W), lambda b, s: (rowmap(b, s), k_col)),
            pl.BlockSpec((SEQ_TILE, SWA_KV_W), lambda b, s: (rowmap(b, s), k_col + 1)),
            pl.BlockSpec((SWA_WINDOW, SWA_KV_W), lambda b, s: (prevmap(b, s), k_col)),
            pl.BlockSpec((SWA_WINDOW, SWA_KV_W), lambda b, s: (prevmap(b, s), k_col + 1)),
            pl.BlockSpec((1, 2 * SWA_HEAD_DIM), lambda b, s: (0, 0)),
            pl.BlockSpec((1, 2 * SWA_HEAD_DIM), lambda b, s: (0, 0)),
        ],
        out_specs=[
            pl.BlockSpec((SEQ_TILE, RET_V_W), lambda b, s: (rowmap(b, s), 0)),
            pl.BlockSpec((SEQ_TILE, SWA_Q_W), lambda b, s: (rowmap(b, s), 0)),
        ],
        out_shape=[
            jax.ShapeDtypeStruct((n, RET_V_W), _BF16),
            jax.ShapeDtypeStruct((n, SWA_Q_W), _BF16),
        ],
        scratch_shapes=[pltpu.VMEM((RET_HEADS, RET_QK_DIM, RET_V_DIM), _F32)],
        compiler_params=pltpu.CompilerParams(
            dimension_semantics=("arbitrary", "arbitrary"), vmem_limit_bytes=VMEM_LIMIT),
        name="mixers",
    )(sinks, ret_proj, ret_proj, ret_proj, ret_proj, swa_proj, swa_proj, swa_proj, swa_proj, swa_proj,
      q_gain2, k_gain2)


def _outproj_kernel(x_ref, retg_ref, attn_ref, ga_ref, gb_ref, wro_ref, wso_ref, wout_ref, g2_ref,
                    wrt_ref, brt_ref, x1_ref, h2_ref, info_ref, info_t_ref, counts_ref, logits_ref):
    tm = ROW_TILE
    step = pl.program_id(0)
    cur = step % 2

    @pl.when(step == 0)
    def _():
        logits_ref[...] = jnp.zeros_like(logits_ref)

    logits = logits_ref[1 - cur]
    row = _iota((LANES, tm), 0)
    row_f = row.astype(_F32)
    is_group = (row >= ROUTER_GROUP_LANE) & (row < ROUTER_GROUP_LANE + MOE_GROUPS)

    subs = [slice(s * OUTPROJ_SUB, (s + 1) * OUTPROJ_SUB) for s in range(tm // OUTPROJ_SUB)]
    y_a = [_dot(retg_ref[rows, :], wro_ref[...]) for rows in subs]

    gl = jnp.where(is_group, logits, NEG_BIG)
    g_max = jnp.max(gl, axis=0, keepdims=True)
    g_prob = 1.0 / jnp.sum(jnp.exp(gl - g_max), axis=0, keepdims=True)
    g_idx = jnp.min(jnp.where(gl == g_max, row_f - ROUTER_GROUP_LANE, float(LANES)), axis=0, keepdims=True)

    y_b = [_dot(attn_ref[rows, :], wso_ref[...]) for rows in subs]

    e_lo = g_idx * MOE_EXPERTS_PER_GROUP
    in_group = (row_f >= e_lo) & (row_f < e_lo + MOE_EXPERTS_PER_GROUP)
    el = jnp.where(in_group, logits, NEG_BIG)
    t1 = jnp.max(el, axis=0, keepdims=True)
    i1 = jnp.min(jnp.where(el == t1, row_f, float(LANES)), axis=0, keepdims=True)

    merged = [(ga_ref[rows, :].astype(_F32) * a + gb_ref[rows, :].astype(_F32) * b).astype(_BF16)
              for rows, a, b in zip(subs, y_a, y_b)]
    mixed = [_dot(m, wout_ref[...]) for m in merged]
    x1 = [x_ref[rows, :] + m for rows, m in zip(subs, mixed)]
    for rows, v in zip(subs, x1):
        x1_ref[rows, :] = v

    el2 = jnp.where(row_f == i1, NEG_BIG, el)
    t2 = jnp.max(el2, axis=0, keepdims=True)
    i2 = jnp.min(jnp.where(el2 == t2, row_f, float(LANES)), axis=0, keepdims=True)
    e21 = jnp.exp(t2 - t1)
    w1 = g_prob / (1.0 + e21)
    w2 = g_prob * e21 / (1.0 + e21)

    h2 = [((v * lax.rsqrt(jnp.mean(v * v, axis=-1, keepdims=True) + NORM_EPS)) * g2_ref[...]).astype(_BF16)
          for v in x1]
    for rows, v in zip(subs, h2):
        h2_ref[rows, :] = v

    sel1 = row_f == i1
    sel2 = row_f == i2
    onehot = jnp.where(sel1 | sel2, 1.0, 0.0).astype(_BF16)
    earlier = (_iota((tm, tm), 0) < _iota((tm, tm), 1)).astype(_BF16)
    before = _dot(onehot, earlier)
    counts = _dot(onehot, jnp.ones((tm, LANES), _BF16))

    for rows, v in zip(subs, h2):
        logits_ref[cur, :, rows] = _dot_nt(wrt_ref[...], v) + brt_ref[:, 0:1]

    run = jnp.floor((counts + (RUN_ALIGN - 1.0)) * (1.0 / RUN_ALIGN)) * RUN_ALIGN
    lower_experts = (_iota((LANES, LANES), 0) > _iota((LANES, LANES), 1)).astype(_BF16)
    run_off = _dot(lower_experts, run.astype(_BF16))
    place = before + jnp.concatenate([run_off] * (tm // LANES), axis=1)
    pos1 = jnp.sum(jnp.where(sel1, place, 0.0), axis=0, keepdims=True)
    pos2 = jnp.sum(jnp.where(sel2, place, 0.0), axis=0, keepdims=True)
    counts_ref[...] = counts.T[0:SUBLANES, :]

    info_t = jnp.concatenate([i1, i2, w1, w2, pos1, pos2, jnp.zeros((2, tm), _F32)], axis=0)
    info_t_ref[...] = info_t
    info_ref[...] = jnp.concatenate([info_t, jnp.zeros((LANES - SUBLANES, tm), _F32)], axis=0).T


def _outproj(x2d, retg, attn, gates, w_ret_o, w_swa_o, w_out, g2, w_router_t, b_router_t):
    n = x2d.shape[0]
    tm = ROW_TILE
    n_tiles = n // tm
    const = lambda s: (0, 0)
    proj = lambda s: jnp.minimum(s, n_tiles - 1)
    routed = lambda s: jnp.maximum(s - 1, 0)
    return pl.pallas_call(
        _outproj_kernel,
        grid=(n_tiles + 1,),
        in_specs=[
            pl.BlockSpec((tm, D_MODEL), lambda s: (proj(s), 0)),
            pl.BlockSpec((tm, RET_V_W), lambda s: (proj(s), 0)),
            pl.BlockSpec((tm, SWA_Q_W), lambda s: (proj(s), 0)),
            pl.BlockSpec((tm, D_MODEL), lambda s: (proj(s), 0)),
            pl.BlockSpec((tm, D_MODEL), lambda s: (proj(s), 1)),
            pl.BlockSpec((RET_V_W, D_MODEL), const),
            pl.BlockSpec((SWA_Q_W, D_MODEL), const),
            pl.BlockSpec((D_MODEL, D_MODEL), const),
            pl.BlockSpec((1, D_MODEL), const),
            pl.BlockSpec((LANES, D_MODEL), const),
            pl.BlockSpec((LANES, LANES), const),
        ],
        out_specs=[
            pl.BlockSpec((tm, D_MODEL), lambda s: (proj(s), 0)),
            pl.BlockSpec((tm, D_MODEL), lambda s: (proj(s), 0)),
            pl.BlockSpec((tm, LANES), lambda s: (routed(s), 0)),
            pl.BlockSpec((SUBLANES, tm), lambda s: (0, routed(s))),
            pl.BlockSpec((SUBLANES, LANES), lambda s: (routed(s), 0)),
        ],
        scratch_shapes=[pltpu.VMEM((2, LANES, tm), _F32)],
        out_shape=[
            jax.ShapeDtypeStruct((n, D_MODEL), _F32),
            jax.ShapeDtypeStruct((n, D_MODEL), _BF16),
            jax.ShapeDtypeStruct((n, LANES), _F32),
            jax.ShapeDtypeStruct((SUBLANES, n), _F32),
            jax.ShapeDtypeStruct((n // tm * SUBLANES, LANES), _F32),
        ],
        compiler_params=pltpu.CompilerParams(
            dimension_semantics=("arbitrary",), vmem_limit_bytes=VMEM_LIMIT),
        name="outproj",
    )(x2d, retg, attn, gates, gates, w_ret_o, w_swa_o, w_out, g2, w_router_t, b_router_t)


def _piece_sizes(largest):
    piece = largest
    while piece >= RUN_ALIGN:
        yield piece
        piece //= 2


PIECE_CLASSES = tuple(k * RUN_ALIGN for k in (4, 3, 2, 1))
PIECE_LIST_KEYS = tuple(f"{field}{size}" for size in PIECE_CLASSES for field in ("n", "off", "dst"))


def _piece_cap(size):
    return SORT_ROWS // size if size == PIECE_CLASSES[0] else MOE_N_EXPERTS


def _start_run_copies(pieces, tile, make_copy):
    for k, size in enumerate(PIECE_CLASSES):
        n_ref, off_ref, dst_ref = pieces[3 * k:3 * k + 3]

        def start(j, carry, off_ref=off_ref, dst_ref=dst_ref, size=size):
            i = tile * _piece_cap(size) + j
            make_copy(pl.multiple_of(off_ref[i], RUN_ALIGN), pl.multiple_of(dst_ref[i], RUN_ALIGN), size).start()
            return carry

        lax.fori_loop(0, n_ref[tile], start, 0)


def _wait_rows(rows, make_copy):
    for piece in _piece_sizes(SORT_ROWS_POW2 // 2):
        pl.when((rows & piece) != 0)(make_copy(0, 0, piece).wait)


def _zero_fill_padding(pad_start_ref, pad_len_ref, zeros_ref, xs_hbm, sem, wait):
    def per_expert(e, carry):
        pos = pad_start_ref[e]
        length = pad_len_ref[e]
        for piece in _piece_sizes(EXPERT_TILE // 2):
            copy = pltpu.make_async_copy(
                zeros_ref.at[pl.ds(0, piece)], xs_hbm.at[pl.ds(pl.multiple_of(pos, RUN_ALIGN), piece)], sem)
            pl.when((length & piece) != 0)(copy.wait if wait else copy.start)
            pos = pos + (length & piece)
        return carry

    lax.fori_loop(0, MOE_N_EXPERTS, per_expert, 0)

    tail_start = pad_start_ref[MOE_N_EXPERTS]
    n_rows = xs_hbm.shape[0]
    half = EXPERT_TILE // 2

    def per_half_tile(j, carry):
        pos = tail_start + j * half
        copy = pltpu.make_async_copy(
            zeros_ref, xs_hbm.at[pl.ds(pl.multiple_of(jnp.minimum(pos, n_rows - half), RUN_ALIGN), half)], sem)
        pl.when(pos < n_rows)(copy.wait if wait else copy.start)
        return carry

    lax.fori_loop(0, n_rows // half, per_half_tile, 0)


def _dispatch_kernel(*refs):
    pieces = refs[:len(PIECE_LIST_KEYS)]
    rows_ref, pad_start_ref, pad_len_ref, h2_ref, info_t_ref, xs_hbm, sorted_ref, zeros_ref, sems, zero_sem = (
        refs[len(PIECE_LIST_KEYS):])
    tile = pl.program_id(0)
    last = pl.num_programs(0) - 1
    slot = tile % RING
    tm = ROW_TILE

    @pl.when(tile == 0)
    def _():
        zeros_ref[...] = jnp.zeros_like(zeros_ref)
        _zero_fill_padding(pad_start_ref, pad_len_ref, zeros_ref, xs_hbm, zero_sem, wait=False)
        _zero_fill_padding(pad_start_ref, pad_len_ref, zeros_ref, xs_hbm, zero_sem, wait=True)

    pos = info_t_ref[4:6, :].astype(jnp.int32)

    def sort_rows(lo, hi):
        place = _iota((hi - lo, tm), 0) + lo
        onehot = ((place == pos[0:1, :]) | (place == pos[1:2, :])).astype(_BF16)
        for c in range(0, D_MODEL, 256):
            sorted_ref[slot, lo:hi, c:c + 256] = _dot(onehot, h2_ref[:, c:c + 256]).astype(_BF16)

    sort_rows(0, SORT_COMMON)
    pl.when(rows_ref[tile] > SORT_COMMON)(lambda: sort_rows(SORT_COMMON, SORT_ROWS))

    def copy_maker(t):
        s = t % RING

        def make_copy(off, dst, piece):
            return pltpu.make_async_copy(
                sorted_ref.at[s, pl.ds(off, piece)], xs_hbm.at[pl.ds(dst, piece)], sems.at[s])

        return make_copy

    _start_run_copies(pieces, tile, copy_maker(tile))

    @pl.when(tile >= RING - 1)
    def _():
        _wait_rows(rows_ref[tile - (RING - 1)], copy_maker(tile - (RING - 1)))

    for back in range(RING - 1):
        @pl.when(jnp.logical_and(tile == last, tile >= back))
        def _(back=back):
            _wait_rows(rows_ref[tile - back], copy_maker(tile - back))


def _dispatch(plan, h2, info_t, n_rows):
    n = h2.shape[0]
    grid_spec = pltpu.PrefetchScalarGridSpec(
        num_scalar_prefetch=len(PIECE_LIST_KEYS) + 3,
        grid=(n // ROW_TILE,),
        in_specs=[
            pl.BlockSpec((ROW_TILE, D_MODEL), lambda i, *_: (i, 0)),
            pl.BlockSpec((SUBLANES, ROW_TILE), lambda i, *_: (0, i)),
        ],
        out_specs=pl.BlockSpec(memory_space=pl.ANY),
        scratch_shapes=[
            pltpu.VMEM((RING, SORT_ROWS, D_MODEL), _BF16),
            pltpu.VMEM((EXPERT_TILE // 2, D_MODEL), _BF16),
            pltpu.SemaphoreType.DMA((RING,)),
            pltpu.SemaphoreType.DMA(()),
        ],
    )
    return pl.pallas_call(
        _dispatch_kernel,
        grid_spec=grid_spec,
        out_shape=jax.ShapeDtypeStruct((n_rows, D_MODEL), _BF16),
        compiler_params=pltpu.CompilerParams(
            dimension_semantics=("arbitrary",), vmem_limit_bytes=VMEM_LIMIT),
        name="dispatch",
    )(*[plan[k] for k in PIECE_LIST_KEYS], plan["rows"], plan["pad_start"], plan["pad_len"], h2, info_t)


def _experts_kernel(te_ref, nt_ref, order_ref, next_ref, xs_ref, wg_hbm, wu_hbm, wd_hbm, ys_ref,
                    wg_f, wu_f, wd_f, wg_s, wu_s, wd_s, sems):
    i = pl.program_id(0)
    in_use = i < nt_ref[0]
    slot = order_ref[i] % 2
    prev = te_ref[jnp.maximum(i - 1, 0)]
    changed = jnp.logical_or(i == 0, te_ref[i] != prev)

    def fetch(expert, s):
        return [pltpu.make_async_copy(hbm.at[expert], stage.at[s], sems.at[s, j])
                for j, (hbm, stage) in enumerate(((wg_hbm, wg_f), (wu_hbm, wu_f), (wd_hbm, wd_f)))]

    @pl.when(i == 0)
    def _():
        for copy in fetch(te_ref[0], 0):
            copy.start()

    @pl.when(jnp.logical_and(in_use, changed))
    def _():
        for copy in fetch(te_ref[i], slot):
            copy.wait()
        wg_s[...] = wg_f[slot].astype(_BF16)
        wu_s[...] = wu_f[slot].astype(_BF16)
        wd_s[...] = wd_f[slot].astype(_BF16)

        @pl.when(next_ref[i] != te_ref[i])
        def _():
            for copy in fetch(next_ref[i], 1 - slot):
                copy.start()

    @pl.when(in_use)
    def _():
        subs = [slice(s * EXPERT_SUB, (s + 1) * EXPERT_SUB) for s in range(EXPERT_TILE // EXPERT_SUB)]
        gate = [_dot(xs_ref[rows, :], wg_s[...]) for rows in subs]
        up = [_dot(xs_ref[rows, :], wu_s[...]) for rows in subs]
        act = [((g * _sigmoid(g)) * u).astype(_BF16) for g, u in zip(gate, up)]
        for rows, a in zip(subs, act):
            ys_ref[rows, :] = _dot(a, wd_s[...]).astype(_BF16)


def _experts(plan, xs, w_gate, w_up, w_down):
    n_rows = xs.shape[0]
    last_used = lambda i, nt: jnp.minimum(i, nt[0] - 1)
    grid_spec = pltpu.PrefetchScalarGridSpec(
        num_scalar_prefetch=4,
        grid=(n_rows // EXPERT_TILE,),
        in_specs=[
            pl.BlockSpec((EXPERT_TILE, D_MODEL), lambda i, te, nt, *_: (last_used(i, nt), 0)),
            pl.BlockSpec(memory_space=pl.ANY),
            pl.BlockSpec(memory_space=pl.ANY),
            pl.BlockSpec(memory_space=pl.ANY),
        ],
        out_specs=pl.BlockSpec((EXPERT_TILE, D_MODEL), lambda i, te, nt, *_: (last_used(i, nt), 0)),
        scratch_shapes=[
            pltpu.VMEM((2, D_MODEL, MOE_D_FF), _F32),
            pltpu.VMEM((2, D_MODEL, MOE_D_FF), _F32),
            pltpu.VMEM((2, MOE_D_FF, D_MODEL), _F32),
            pltpu.VMEM((D_MODEL, MOE_D_FF), _BF16),
            pltpu.VMEM((D_MODEL, MOE_D_FF), _BF16),
            pltpu.VMEM((MOE_D_FF, D_MODEL), _BF16),
            pltpu.SemaphoreType.DMA((2, 3)),
        ],
    )
    return pl.pallas_call(
        _experts_kernel,
        grid_spec=grid_spec,
        out_shape=jax.ShapeDtypeStruct((n_rows, D_MODEL), _BF16),
        input_output_aliases={4: 0},
        compiler_params=pltpu.CompilerParams(
            dimension_semantics=("arbitrary",), vmem_limit_bytes=VMEM_LIMIT),
        name="experts",
    )(plan["tile_expert"], plan["tiles_used"], plan["tile_order"], plan["tile_next"], xs, w_gate, w_up, w_down)


def _combine_kernel(*refs):
    pieces = refs[:len(PIECE_LIST_KEYS)]
    rows_ref, x1_ref, info_ref, ys_hbm, out_ref, block_ref, sems = refs[len(PIECE_LIST_KEYS):]
    tile = pl.program_id(0)
    last = pl.num_programs(0) - 1
    slot = tile % RING
    tm = ROW_TILE

    def copy_maker(t):
        s = t % RING

        def make_copy(off, dst, piece):
            return pltpu.make_async_copy(
                ys_hbm.at[pl.ds(dst, piece)], block_ref.at[s, pl.ds(off, piece)], sems.at[s])

        return make_copy

    @pl.when(tile == 0)
    def _():
        block_ref[...] = jnp.zeros_like(block_ref)
        for ahead in range(RING - 1):
            pl.when(ahead <= last)(lambda ahead=ahead: _start_run_copies(pieces, ahead, copy_maker(ahead)))

    @pl.when(tile + (RING - 1) <= last)
    def _():
        _start_run_copies(pieces, tile + (RING - 1), copy_maker(tile + (RING - 1)))

    _wait_rows(rows_ref[tile], copy_maker(tile))

    info = info_ref[...]

    def weights_of(lo, hi):
        place = (_iota((tm, hi - lo), 1) + lo).astype(_F32)
        return (jnp.where(place == info[:, 4:5], info[:, 2:3], 0.0)
                + jnp.where(place == info[:, 5:6], info[:, 3:4], 0.0)).astype(_BF16)

    weights = weights_of(0, SORT_COMMON)
    for c in range(0, D_MODEL, 256):
        out_ref[:, c:c + 256] = x1_ref[:, c:c + 256] + _dot(weights, block_ref[slot, 0:SORT_COMMON, c:c + 256])

    @pl.when(rows_ref[tile] > SORT_COMMON)
    def _():
        tail = weights_of(SORT_COMMON, SORT_ROWS)
        for c in range(0, D_MODEL, 256):
            out_ref[:, c:c + 256] += _dot(tail, block_ref[slot, SORT_COMMON:SORT_ROWS, c:c + 256])


def _combine(plan, x1, info, ys):
    n = x1.shape[0]
    grid_spec = pltpu.PrefetchScalarGridSpec(
        num_scalar_prefetch=len(PIECE_LIST_KEYS) + 1,
        grid=(n // ROW_TILE,),
        in_specs=[
            pl.BlockSpec((ROW_TILE, D_MODEL), lambda i, *_: (i, 0)),
            pl.BlockSpec((ROW_TILE, LANES), lambda i, *_: (i, 0)),
            pl.BlockSpec(memory_space=pl.ANY),
        ],
        out_specs=pl.BlockSpec((ROW_TILE, D_MODEL), lambda i, *_: (i, 0)),
        scratch_shapes=[pltpu.VMEM((RING, SORT_ROWS, D_MODEL), _BF16), pltpu.SemaphoreType.DMA((RING,))],
    )
    return pl.pallas_call(
        _combine_kernel,
        grid_spec=grid_spec,
        out_shape=jax.ShapeDtypeStruct((n, D_MODEL), _F32),
        compiler_params=pltpu.CompilerParams(
            dimension_semantics=("arbitrary",), vmem_limit_bytes=VMEM_LIMIT),
        name="combine",
    )(*[plan[k] for k in PIECE_LIST_KEYS], plan["rows"], x1, info, ys)


def _rotary_tables(seq):
    half = RET_QK_DIM // 2
    inv = ROPE_BASE ** (-jnp.arange(half, dtype=_F32) / half)
    hi = (jnp.arange(seq // RET_CHUNK, dtype=jnp.int32) * RET_CHUNK).astype(_F32)[:, None] * inv[None, :]
    lo = jnp.arange(RET_CHUNK, dtype=jnp.int32).astype(_F32)[:, None] * inv[None, :]
    cos_hi, sin_hi = jnp.cos(hi)[:, None, :], jnp.sin(hi)[:, None, :]
    cos_lo, sin_lo = jnp.cos(lo)[None, :, :], jnp.sin(lo)[None, :, :]
    cos = (cos_hi * cos_lo - sin_hi * sin_lo).reshape(seq, half)
    sin = (sin_hi * cos_lo + cos_hi * sin_lo).reshape(seq, half)
    return jnp.concatenate([cos, cos], axis=1), jnp.concatenate([-sin, sin], axis=1)


def _expert_row_bound(n_tokens):
    rows = (2 * n_tokens + (n_tokens // ROW_TILE) * MOE_N_EXPERTS * (RUN_ALIGN - 1)
            + MOE_N_EXPERTS * (EXPERT_TILE - RUN_ALIGN))
    return -(-rows // EXPERT_TILE) * EXPERT_TILE


def _routing_plan(counts_out, n_rows):
    te = EXPERT_TILE
    n_tok_tiles = counts_out.shape[0] // SUBLANES
    counts = counts_out.reshape(n_tok_tiles, SUBLANES, LANES)[:, 0, :MOE_N_EXPERTS].astype(jnp.int32)
    run = (counts + RUN_ALIGN - 1) // RUN_ALIGN * RUN_ALIGN
    off = jnp.cumsum(run, axis=1) - run
    total = jnp.sum(run, axis=0)
    padded = (total + te - 1) // te * te
    pends = jnp.cumsum(padded)
    pstarts = pends - padded
    dst = pstarts[None, :] + jnp.cumsum(run, axis=0) - run
    tile_start = jnp.arange(n_rows // te, dtype=jnp.int32) * te
    tile_expert = jnp.sum((tile_start[:, None] >= pends[None, :]).astype(jnp.int32), axis=1)
    tile_expert = jnp.minimum(tile_expert, MOE_N_EXPERTS - 1)
    i32 = lambda v: v.astype(jnp.int32)

    experts = jnp.arange(MOE_N_EXPERTS, dtype=jnp.int32)
    has_rows = total > 0
    rank = jnp.cumsum(has_rows.astype(jnp.int32)) - 1
    later = jnp.where((experts[None, :] > experts[:, None]) & has_rows[None, :], experts[None, :], MOE_N_EXPERTS)
    following = jnp.min(later, axis=1)
    following = jnp.where(following == MOE_N_EXPERTS, experts, following)
    of_tile = tile_expert[:, None] == experts[None, :]
    tile_order = jnp.sum(jnp.where(of_tile, rank[None, :], 0), axis=1)
    tile_next = jnp.sum(jnp.where(of_tile, following[None, :], 0), axis=1)

    largest = PIECE_CLASSES[0]
    n_big = run // largest
    rest = run - largest * n_big
    big_first = jnp.cumsum(n_big, axis=1) - n_big
    k_big = jnp.arange(_piece_cap(largest), dtype=jnp.int32)[None, :, None]
    owns_big = (k_big >= big_first[:, None, :]) & (k_big < (big_first + n_big)[:, None, :])
    pick_big = lambda v: jnp.sum(jnp.where(owns_big, (v - largest * big_first)[:, None, :], 0), axis=-1)
    lists = {f"n{largest}": i32(jnp.sum(n_big, axis=1)),
             f"off{largest}": i32((pick_big(off) + largest * k_big[..., 0]).reshape(-1)),
             f"dst{largest}": i32((pick_big(dst) + largest * k_big[..., 0]).reshape(-1))}
    k_small = jnp.arange(MOE_N_EXPERTS, dtype=jnp.int32)[None, :, None]
    for size in PIECE_CLASSES[1:]:
        has = (rest == size).astype(jnp.int32)
        index = jnp.cumsum(has, axis=1) - has
        owns = (has[:, None, :] == 1) & (index[:, None, :] == k_small)
        pick = lambda v, owns=owns: jnp.sum(jnp.where(owns, (v + largest * n_big)[:, None, :], 0), axis=-1)
        lists.update({f"n{size}": i32(jnp.sum(has, axis=1)),
                      f"off{size}": i32(pick(off).reshape(-1)), f"dst{size}": i32(pick(dst).reshape(-1))})
    return {
        **lists,
        "rows": i32(jnp.sum(run, axis=1)),
        "pad_start": i32(jnp.concatenate([pstarts + total, pends[-1:]])), "pad_len": i32(padded - total),
        "tile_expert": i32(tile_expert), "tiles_used": i32(pends[-1:] // te),
        "tile_order": i32(tile_order), "tile_next": i32(tile_next),
    }


def kernel(x, norm_mix_g, w_in, ret_gn_g, w_ret_o, q_norm_g, k_norm_g, sinks, w_swa_o, w_out, norm_ffn_g,
           w_router_group, b_router_group, w_router_expert, b_router_expert, w_gate, w_up, w_down):
    batch, seq, d = x.shape
    n = batch * seq
    assert d == D_MODEL and seq % SEQ_TILE == 0 and n % ROW_TILE == 0
    cos_tab, sin_tab = _rotary_tables(seq)
    n_rows = _expert_row_bound(n)
    for l in range(w_in.shape[0]):
        x2d = x.reshape(n, d)
        ret_proj, swa_proj, gates = _inproj(x2d, norm_mix_g[l][None, :], w_in[l].astype(_BF16), cos_tab, sin_tab,
                                            ret_gn_g[l].reshape(1, RET_V_W))
        retg, attn = _mixers(ret_proj, swa_proj, sinks[l], jnp.tile(q_norm_g[l], 2)[None, :],
                             jnp.tile(k_norm_g[l], 2)[None, :], batch, seq)
        pad = LANES - MOE_N_EXPERTS - MOE_GROUPS
        w_router_t = jnp.concatenate(
            [w_router_expert[l].T, w_router_group[l].T, jnp.zeros((pad, d), _F32)], axis=0).astype(_BF16)
        b_router = jnp.concatenate([b_router_expert[l], b_router_group[l], jnp.zeros((pad,), _F32)])
        b_router_t = jnp.broadcast_to(b_router[:, None], (LANES, LANES))
        x1, h2, info, info_t, counts = _outproj(
            x2d, retg, attn, gates, w_ret_o[l].astype(_BF16), w_swa_o[l].astype(_BF16), w_out[l].astype(_BF16),
            norm_ffn_g[l][None, :], w_router_t, b_router_t)
        plan = _routing_plan(counts, n_rows)
        xs = _dispatch(plan, h2, info_t, n_rows)
        ys = _experts(plan, xs, w_gate[l], w_up[l], w_down[l])
        x = _combine(plan, x1, info, ys).reshape(batch, seq, d)
    return x
```
